```python
import math
import jax, jax.numpy as jnp
from jax import lax
import numpy as np

D_MODEL = 1024
BATCH = 4
SEQ = 4096
DEPTH = 2

N_EVEN = (DEPTH + 1) // 2
N_ODD = DEPTH // 2
HEAD_DIM = 64
NORM_EPS = 1e-6

RWKV_HEADS = 8
RWKV_DIM = RWKV_HEADS * HEAD_DIM
DECAY_LORA = 64
ICLR_LORA = 64
GATE_LORA = 128
RWKV_GN_EPS = 64e-5
RWKV_COLS = 3 * RWKV_DIM + DECAY_LORA + ICLR_LORA + GATE_LORA
RWKV_SPLITS = (RWKV_DIM, 2 * RWKV_DIM, 3 * RWKV_DIM, 3 * RWKV_DIM + DECAY_LORA,
               3 * RWKV_DIM + DECAY_LORA + ICLR_LORA)

SSD_HEADS = 8
SSD_DIM = SSD_HEADS * HEAD_DIM
SSD_GROUPS = 2
SSD_STATE = 128
SSD_CONV = 4
SSD_CHUNK = 128
SSD_XBC = SSD_DIM + 2 * SSD_GROUPS * SSD_STATE
SSD_COLS = SSD_DIM + SSD_XBC + SSD_HEADS
IN0_COLS = RWKV_COLS + SSD_COLS
MIX0_DIM = RWKV_DIM + SSD_DIM

ATTN_Q_HEADS = 16
ATTN_KV_HEADS = 4
GQA_GROUP = ATTN_Q_HEADS // ATTN_KV_HEADS
Q_DIM = ATTN_Q_HEADS * HEAD_DIM
KV_DIM = ATTN_KV_HEADS * HEAD_DIM
QKV_COLS = Q_DIM + 2 * KV_DIM
WINDOW = 128
ROPE_THETA = 500000.0
ROPE_DIM = HEAD_DIM // 4

FFN_DENSE = 2816
N_EXPERTS = 8
TOP_K = 2
FFN_EXPERT = 1408

kernel_name = "hybrid_rwkv7_ssd_swa_moe_block"


def rmsnorm(x, g, eps=NORM_EPS):
    xf = x.astype(jnp.float32)
    y = xf * lax.rsqrt(jnp.mean(xf * xf, axis=-1, keepdims=True) + eps)
    return (y * g.astype(jnp.float32)).astype(x.dtype)


def token_shift(u):
    return jnp.pad(u[:, :-1], ((0, 0), (1, 0), (0, 0)))


def causal_depthwise_conv(u, w, b):
    c = u.shape[-1]
    y = lax.conv_general_dilated(u, w[:, None, :], window_strides=(1,),
                                 padding=[(SSD_CONV - 1, 0)],
                                 dimension_numbers=('NWC', 'WIO', 'NWC'),
                                 feature_group_count=c)
    return y + b


def rwkv7_scan(r, decay, k, v, a_vec, b_vec):
    out_dtype = v.dtype
    bsz, _, nh, n = r.shape

    def seq_major(t):
        return jnp.moveaxis(t.astype(jnp.float32), 1, 0)

    def step(s, inp):
        r_t, w_t, k_t, v_t, a_t, bb_t = inp
        sa = jnp.einsum('bhvk,bhk->bhv', s, a_t)
        s = s * w_t[:, :, None, :] + sa[..., :, None] * bb_t[..., None, :] + v_t[..., :, None] * k_t[..., None, :]
        return s, jnp.einsum('bhvk,bhk->bhv', s, r_t)

    s0 = jnp.zeros((bsz, nh, n, n), jnp.float32)
    _, y = lax.scan(step, s0, tuple(map(seq_major, (r, decay, k, v, a_vec, b_vec))))
    return jnp.moveaxis(y, 0, 1).astype(out_dtype)


def ssd_chunked(x, dt, a, bm, cm):
    bsz, seq, nh, p = x.shape
    g, n = bm.shape[2], bm.shape[3]
    hpg = nh // g
    nc, q = seq // SSD_CHUNK, SSD_CHUNK
    f32 = jnp.float32
    xc = x.astype(f32).reshape(bsz, nc, q, g, hpg, p)
    dtc = dt.reshape(bsz, nc, q, g, hpg)
    bc = bm.astype(f32).reshape(bsz, nc, q, g, n)
    cc = cm.astype(f32).reshape(bsz, nc, q, g, n)
    cum = jnp.cumsum(dtc * a.reshape(g, hpg), axis=2)
    seg = cum[:, :, :, None] - cum[:, :, None]
    causal = jnp.tril(jnp.ones((q, q), bool))[None, None, :, :, None, None]
    ldec = jnp.exp(jnp.where(causal, seg, -jnp.inf))
    cb = jnp.einsum('bclgn,bcsgn->bclsg', cc, bc)
    wts = cb[..., None] * ldec * dtc[:, :, None]
    y_diag = jnp.einsum('bclsgh,bcsghp->bclghp', wts, xc)
    to_end = jnp.exp(cum[:, :, -1:] - cum) * dtc
    states = jnp.einsum('bcsgn,bcsghp->bcghpn', bc, xc * to_end[..., None])
    chunk_decay = jnp.exp(cum[:, :, -1])

    def step(hs, inp):
        st, dec = inp
        return hs * dec[..., None, None] + st, hs

    h0 = jnp.zeros((bsz, g, hpg, p, n), f32)
    _, h_prev = lax.scan(step, h0, (jnp.moveaxis(states, 1, 0), jnp.moveaxis(chunk_decay, 1, 0)))
    h_prev = jnp.moveaxis(h_prev, 0, 1)
    y_off = jnp.einsum('bclgn,bcghpn->bclghp', cc, h_prev) * jnp.exp(cum)[..., None]
    return (y_diag + y_off).reshape(bsz, seq, nh, p)


def rwkv_ssd_mixer(h, w_in, mu_shift, w0, w_decay_up, a0, w_iclr_up, w_gate_up, k_k, k_a, r_k,
                   gn_w, gn_b, conv_w, conv_b, dt_bias, a_log, d_skip, ssd_norm, w_out):
    bsz, seq, _ = h.shape
    f32 = jnp.float32
    proj = h @ w_in
    pa, pb = proj[..., :RWKV_COLS], proj[..., RWKV_COLS:]

    pa = pa + (token_shift(pa) - pa) * mu_shift
    r, k, v, wl, al, gl = jnp.split(pa, RWKV_SPLITS, axis=-1)
    w_raw = (w0 + jnp.tanh(wl) @ w_decay_up).astype(f32)
    decay = jnp.exp(-jnp.exp(-jax.nn.softplus(-w_raw) - 0.5))
    iclr = jax.nn.sigmoid(a0 + al @ w_iclr_up)
    g_out = jax.nn.sigmoid(gl) @ w_gate_up

    def heads(t):
        return t.reshape(bsz, seq, RWKV_HEADS, HEAD_DIM)

    kk = heads(k * k_k).astype(f32)
    kk = kk * lax.rsqrt(jnp.sum(kk * kk, axis=-1, keepdims=True) + 1e-12)
    k = k * (1.0 + (iclr - 1.0) * k_a)
    r_h, k_h, v_h, a_h = heads(r), heads(k), heads(v), heads(iclr)
    y = rwkv7_scan(r_h, heads(decay), k_h, v_h, -kk, kk * a_h).astype(f32)
    mu = jnp.mean(y, axis=-1, keepdims=True)
    var = jnp.mean(jnp.square(y - mu), axis=-1, keepdims=True)
    y = (y - mu) * lax.rsqrt(var + RWKV_GN_EPS) * gn_w.reshape(RWKV_HEADS, HEAD_DIM) + gn_b.reshape(RWKV_HEADS, HEAD_DIM)
    bonus = jnp.sum((r_h * k_h * r_k).astype(f32), axis=-1, keepdims=True) * v_h
    ya = ((y + bonus).reshape(bsz, seq, RWKV_DIM) * g_out).astype(h.dtype)

    z, xbc, dt = jnp.split(pb, (SSD_DIM, SSD_DIM + SSD_XBC), axis=-1)
    xbc = jax.nn.silu(causal_depthwise_conv(xbc, conv_w, conv_b))
    xs, bm, cm = jnp.split(xbc, (SSD_DIM, SSD_DIM + SSD_GROUPS * SSD_STATE), axis=-1)
    dt = jax.nn.softplus((dt + dt_bias).astype(f32))
    a = -jnp.exp(a_log.astype(f32))
    xs_h = xs.reshape(bsz, seq, SSD_HEADS, HEAD_DIM)
    ys = ssd_chunked(xs_h, dt, a,
                     bm.reshape(bsz, seq, SSD_GROUPS, SSD_STATE),
                     cm.reshape(bsz, seq, SSD_GROUPS, SSD_STATE))
    ys = ys + d_skip[:, None] * xs_h
    ys = ys.reshape(bsz, seq, SSD_DIM) * jax.nn.silu(z)
    ys = rmsnorm(ys.reshape(bsz, seq, SSD_GROUPS, SSD_DIM // SSD_GROUPS),
                 ssd_norm.reshape(SSD_GROUPS, SSD_DIM // SSD_GROUPS)).reshape(bsz, seq, SSD_DIM)

    return jnp.concatenate([ya, ys.astype(h.dtype)], axis=-1) @ w_out


def partial_rope(t, cos, sin):
    rot, rest = t[..., :ROPE_DIM], t[..., ROPE_DIM:]
    x1, x2 = rot[..., :ROPE_DIM // 2], rot[..., ROPE_DIM // 2:]
    shape = cos.shape[:2] + (1,) * (t.ndim - 3) + (cos.shape[-1],)
    c, s = cos.reshape(shape), sin.reshape(shape)
    return jnp.concatenate([x1 * c - x2 * s, x2 * c + x1 * s, rest], axis=-1)


def swa_sink_attention(h, cos, sin, w_qkv, b_qkv, q_norm, k_norm, sinks, w_o, b_o):
    bsz, seq, _ = h.shape
    nb = seq // WINDOW
    qkv = h @ w_qkv + b_qkv
    q, k, v = jnp.split(qkv, (Q_DIM, Q_DIM + KV_DIM), axis=-1)
    q = q.reshape(bsz, seq, ATTN_KV_HEADS, GQA_GROUP, HEAD_DIM)
    k = k.reshape(bsz, seq, ATTN_KV_HEADS, HEAD_DIM)
    v = v.reshape(bsz, seq, ATTN_KV_HEADS, HEAD_DIM)
    q = partial_rope(rmsnorm(q, q_norm), cos, sin)
    k = partial_rope(rmsnorm(k, k_norm), cos, sin)
    qb = q.reshape(bsz, nb, WINDOW, ATTN_KV_HEADS, GQA_GROUP, HEAD_DIM)
    kb = k.reshape(bsz, nb, WINDOW, ATTN_KV_HEADS, HEAD_DIM)
    vb = v.reshape(bsz, nb, WINDOW, ATTN_KV_HEADS, HEAD_DIM)

    def with_prev_block(t):
        prev = jnp.pad(t[:, :-1], ((0, 0), (1, 0), (0, 0), (0, 0), (0, 0)))
        return jnp.concatenate([prev, t], axis=2)

    kw, vw = with_prev_block(kb), with_prev_block(vb)
    qi = jnp.arange(WINDOW)[:, None]
    kj = jnp.arange(2 * WINDOW)[None, :]
    diff = WINDOW + qi - kj
    band = (diff >= 0) & (diff < WINDOW)
    blk = jnp.arange(nb)[:, None, None]
    valid = band[None] & ((blk > 0) | (kj[None] >= WINDOW))
    scores = jnp.einsum('bnqkgd,bnskd->bkgnqs', qb, kw).astype(jnp.float32) * (HEAD_DIM ** -0.5)
    scores = jnp.where(valid[None, None, None], scores, -jnp.inf)
    sink = jnp.broadcast_to(sinks.astype(jnp.float32).reshape(1, ATTN_KV_HEADS, GQA_GROUP, 1, 1, 1),
                            scores.shape[:-1] + (1,))
    probs = jax.nn.softmax(jnp.concatenate([scores, sink], axis=-1), axis=-1)[..., :-1]
    out = jnp.einsum('bkgnqs,bnskd->bnqkgd', probs.astype(vw.dtype), vw)
    return out.reshape(bsz, seq, Q_DIM) @ w_o + b_o


def swiglu(h, w_gate, w_up, w_down):
    return (jax.nn.silu(h @ w_gate) * (h @ w_up)) @ w_down


def moe_swiglu(h, w_router, w_gate, w_up, w_down):
    bsz, seq, d = h.shape
    t = h.reshape(-1, d)
    logits = (t @ w_router).astype(jnp.float32)
    top_vals, top_idx = lax.top_k(logits, TOP_K)
    top_w = jax.nn.softmax(top_vals, axis=-1)
    combine = jnp.sum(jax.nn.one_hot(top_idx, N_EXPERTS, dtype=jnp.float32) * top_w[..., None], axis=1)
    combine = combine.astype(t.dtype)
    out = jnp.zeros_like(t)
    for e in range(N_EXPERTS):
        out = out + combine[:, e:e + 1] * swiglu(t, w_gate[e], w_up[e], w_down[e])
    return out.reshape(bsz, seq, d)


def setup_inputs(seed: int = 0) -> dict:
    key = jax.random.key(seed)
    keys = jax.random.split(key, 48)
    ctr = [0]
    f32 = jnp.float32

    def nxt():
        kk = keys[ctr[0]]
        ctr[0] += 1
        return kk

    def nrm(shape, scale):
        return jax.random.normal(nxt(), shape, f32) * scale

    def gain(shape):
        return 1.0 + nrm(shape, 0.02)

    ne, no = N_EVEN, N_ODD
    x = nrm((BATCH, SEQ, D_MODEL), 1.0)
    positions = jnp.broadcast_to(jnp.arange(SEQ, dtype=jnp.int32), (BATCH, SEQ))
    ev_norm_mix = gain((ne, D_MODEL))
    ev_w_in = nrm((ne, D_MODEL, IN0_COLS), D_MODEL ** -0.5)
    ev_mu_shift = jax.random.uniform(nxt(), (ne, RWKV_COLS), f32)
    ev_w0 = nrm((ne, RWKV_DIM), 0.5)
    ev_w_decay_up = nrm((ne, DECAY_LORA, RWKV_DIM), 0.5 * DECAY_LORA ** -0.5)
    ev_a0 = nrm((ne, RWKV_DIM), 0.1)
    ev_w_iclr_up = nrm((ne, ICLR_LORA, RWKV_DIM), ICLR_LORA ** -0.5)
    ev_w_gate_up = nrm((ne, GATE_LORA, RWKV_DIM), GATE_LORA ** -0.5)
    ev_k_k = 0.85 + nrm((ne, RWKV_DIM), 0.05)
    ev_k_a = 1.0 + nrm((ne, RWKV_DIM), 0.05)
    ev_r_k = nrm((ne, RWKV_HEADS, HEAD_DIM), 0.1)
    ev_gn_w = gain((ne, RWKV_DIM))
    ev_gn_b = nrm((ne, RWKV_DIM), 0.02)
    ev_conv_w = nrm((ne, SSD_CONV, SSD_XBC), SSD_CONV ** -0.5)
    ev_conv_b = nrm((ne, SSD_XBC), 0.02)
    dt_init = jnp.exp(jax.random.uniform(nxt(), (ne, SSD_HEADS), f32, math.log(1e-3), math.log(1e-1)))
    ev_dt_bias = dt_init + jnp.log(-jnp.expm1(-dt_init))
    ev_a_log = jnp.log(jax.random.uniform(nxt(), (ne, SSD_HEADS), f32, 1.0, 16.0))
    ev_d_skip = 1.0 + nrm((ne, SSD_HEADS), 0.1)
    ev_ssd_norm = gain((ne, SSD_DIM))
    ev_w_out = nrm((ne, MIX0_DIM, D_MODEL), MIX0_DIM ** -0.5)
    ev_norm_ffn = gain((ne, D_MODEL))
    ev_ffn_gate = nrm((ne, D_MODEL, FFN_DENSE), D_MODEL ** -0.5)
    ev_ffn_up = nrm((ne, D_MODEL, FFN_DENSE), D_MODEL ** -0.5)
    ev_ffn_down = nrm((ne, FFN_DENSE, D_MODEL), FFN_DENSE ** -0.5)
    od_norm_mix = gain((no, D_MODEL))
    od_w_qkv = nrm((no, D_MODEL, QKV_COLS), D_MODEL ** -0.5)
    od_b_qkv = nrm((no, QKV_COLS), 0.02)
    od_q_norm = gain((no, HEAD_DIM))
    od_k_norm = gain((no, HEAD_DIM))
    od_sinks = nrm((no, ATTN_Q_HEADS), 1.0)
    od_w_o = nrm((no, Q_DIM, D_MODEL), Q_DIM ** -0.5)
    od_b_o = nrm((no, D_MODEL), 0.02)
    od_norm_ffn = gain((no, D_MODEL))
    od_router = nrm((no, D_MODEL, N_EXPERTS), D_MODEL ** -0.5)
    od_exp_gate = nrm((no, N_EXPERTS, D_MODEL, FFN_EXPERT), D_MODEL ** -0.5)
    od_exp_up = nrm((no, N_EXPERTS, D_MODEL, FFN_EXPERT), D_MODEL ** -0.5)
    od_exp_down = nrm((no, N_EXPERTS, FFN_EXPERT, D_MODEL), FFN_EXPERT ** -0.5)
    return {
        "x": x, "positions": positions,
        "ev_norm_mix": ev_norm_mix, "ev_w_in": ev_w_in, "ev_mu_shift": ev_mu_shift, "ev_w0": ev_w0,
        "ev_w_decay_up": ev_w_decay_up, "ev_a0": ev_a0, "ev_w_iclr_up": ev_w_iclr_up,
        "ev_w_gate_up": ev_w_gate_up, "ev_k_k": ev_k_k, "ev_k_a": ev_k_a, "ev_r_k": ev_r_k,
        "ev_gn_w": ev_gn_w, "ev_gn_b": ev_gn_b, "ev_conv_w": ev_conv_w, "ev_conv_b": ev_conv_b,
        "ev_dt_bias": ev_dt_bias, "ev_a_log": ev_a_log, "ev_d_skip": ev_d_skip, "ev_ssd_norm": ev_ssd_norm,
        "ev_w_out": ev_w_out, "ev_norm_ffn": ev_norm_ffn, "ev_ffn_gate": ev_ffn_gate,
        "ev_ffn_up": ev_ffn_up, "ev_ffn_down": ev_ffn_down,
        "od_norm_mix": od_norm_mix, "od_w_qkv": od_w_qkv, "od_b_qkv": od_b_qkv, "od_q_norm": od_q_norm,
        "od_k_norm": od_k_norm, "od_sinks": od_sinks, "od_w_o": od_w_o, "od_b_o": od_b_o,
        "od_norm_ffn": od_norm_ffn, "od_router": od_router, "od_exp_gate": od_exp_gate,
        "od_exp_up": od_exp_up, "od_exp_down": od_exp_down,
    }


def reference(x, positions,
              ev_norm_mix, ev_w_in, ev_mu_shift, ev_w0, ev_w_decay_up, ev_a0, ev_w_iclr_up,
              ev_w_gate_up, ev_k_k, ev_k_a, ev_r_k, ev_gn_w, ev_gn_b, ev_conv_w, ev_conv_b,
              ev_dt_bias, ev_a_log, ev_d_skip, ev_ssd_norm, ev_w_out, ev_norm_ffn, ev_ffn_gate,
              ev_ffn_up, ev_ffn_down,
              od_norm_mix, od_w_qkv, od_b_qkv, od_q_norm, od_k_norm, od_sinks, od_w_o, od_b_o,
              od_norm_ffn, od_router, od_exp_gate, od_exp_up, od_exp_down):
    inv_freq = ROPE_THETA ** (-jnp.arange(0, ROPE_DIM, 2, dtype=jnp.float32) / ROPE_DIM)
    ang = positions.astype(jnp.float32)[..., None] * inv_freq
    cos, sin = jnp.cos(ang).astype(x.dtype), jnp.sin(ang).astype(x.dtype)
    h = x
    for layer in range(DEPTH):
        i = layer // 2
        if layer % 2 == 0:
            h = h + rwkv_ssd_mixer(rmsnorm(h, ev_norm_mix[i]), ev_w_in[i], ev_mu_shift[i], ev_w0[i],
                                   ev_w_decay_up[i], ev_a0[i], ev_w_iclr_up[i], ev_w_gate_up[i],
                                   ev_k_k[i], ev_k_a[i], ev_r_k[i], ev_gn_w[i], ev_gn_b[i],
                                   ev_conv_w[i], ev_conv_b[i], ev_dt_bias[i], ev_a_log[i],
                                   ev_d_skip[i], ev_ssd_norm[i], ev_w_out[i])
            h = h + swiglu(rmsnorm(h, ev_norm_ffn[i]), ev_ffn_gate[i], ev_ffn_up[i], ev_ffn_down[i])
        else:
            h = h + swa_sink_attention(rmsnorm(h, od_norm_mix[i]), cos, sin, od_w_qkv[i], od_b_qkv[i],
                                       od_q_norm[i], od_k_norm[i], od_sinks[i], od_w_o[i], od_b_o[i])
            h = h + moe_swiglu(rmsnorm(h, od_norm_ffn[i]), od_router[i], od_exp_gate[i],
                               od_exp_up[i], od_exp_down[i])
    return h
```

```python
import functools
import math

import jax
import jax.numpy as jnp
from jax import lax
from jax.experimental import pallas as pl
from jax.experimental.pallas import tpu as pltpu

F32 = jnp.float32
BF16 = jnp.bfloat16

D_MODEL = 1024
HEAD_DIM = 64
NORM_EPS = 1e-6

RWKV_HEADS = 8
RWKV_DIM = 512
RWKV_COLS = 1792
RWKV_GN_EPS = 64e-5
RWKV_CHUNK = 64

SSD_HEADS = 8
SSD_DIM = 512
SSD_GROUPS = 2
SSD_STATE = 128
SSD_CONV = 4
SSD_CHUNK = 128
SSD_XBC = 1024
SSD_COLS = 1544
SSD_COLS_PAD = 1664

Q_HEADS = 16
KV_HEADS = 4
GQA_GROUP = 4
Q_DIM = 1024
KV_DIM = 256
WINDOW = 128
ROPE_THETA = 500000.0
ROPE_DIM = 16

FFN_DENSE = 2816
N_EXPERTS = 8
FFN_EXPERT = 1408

LANES = 128
VMEM_LIMIT_BYTES = 56 * 1024 * 1024


def _params(*sem):
    return pltpu.CompilerParams(dimension_semantics=sem, vmem_limit_bytes=VMEM_LIMIT_BYTES)


def _bdot(a, b):
    return jnp.dot(a.astype(BF16), b.astype(BF16), preferred_element_type=F32)


def _bdot_nt(a, b):
    return lax.dot_general(a.astype(BF16), b.astype(BF16), (((1,), (1,)), ((), ())),
                           preferred_element_type=F32)


def _bdot_tn(a, b):
    return lax.dot_general(a.astype(BF16), b.astype(BF16), (((0,), (0,)), ((), ())),
                           preferred_element_type=F32)


def _split(x):
    hi = x.astype(BF16)
    lo = (x - hi.astype(F32)).astype(BF16)
    return hi, lo


def _dot_exact_rhs(x, m):
    hi, lo = _split(x)
    return (jnp.dot(hi, m, preferred_element_type=F32) + jnp.dot(lo, m, preferred_element_type=F32))


def _dot_exact_lhs(m, x):
    hi, lo = _split(x)
    return (jnp.dot(m, hi, preferred_element_type=F32) + jnp.dot(m, lo, preferred_element_type=F32))


def _sigmoid(x):
    return 1.0 / (1.0 + jnp.exp(-x))


def _silu(x):
    return x * _sigmoid(x)


def _softplus(x):
    return jnp.maximum(x, 0.0) + jnp.log(1.0 + jnp.exp(-jnp.abs(x)))


def _rms(x, g):
    ms = jnp.mean(x * x, axis=-1, keepdims=True)
    return x * lax.rsqrt(ms + NORM_EPS) * g


def _tile(n, pref):
    t = min(n, pref)
    while n % t:
        t //= 2
    return t


def _full(shape):
    nd = len(shape)
    return pl.BlockSpec(shape, lambda *_: (0,) * nd)


def _in_proj_kernel(x_ref, g_ref, wa_ref, wb_ref, pa_ref, pb_ref):
    xn = _rms(x_ref[...], g_ref[...]).astype(BF16)
    pa_ref[...] = jnp.dot(xn, wa_ref[...], preferred_element_type=F32)
    pb_ref[...] = jnp.dot(xn, wb_ref[...], preferred_element_type=F32)


def _in_proj(x2, g, wa, wb):
    t = x2.shape[0]
    tm = _tile(t, 256)
    return pl.pallas_call(
        _in_proj_kernel,
        grid=(t // tm,),
        in_specs=[pl.BlockSpec((tm, D_MODEL), lambda i: (i, 0)), _full((1, D_MODEL)),
                  _full(wa.shape), _full(wb.shape)],
        out_specs=[pl.BlockSpec((tm, RWKV_COLS), lambda i: (i, 0)),
                   pl.BlockSpec((tm, SSD_COLS_PAD), lambda i: (i, 0))],
        out_shape=[jax.ShapeDtypeStruct((t, RWKV_COLS), F32),
                   jax.ShapeDtypeStruct((t, SSD_COLS_PAD), F32)],
        compiler_params=_params("parallel"),
        name="in_proj",
    )(x2, g, wa, wb)


def _rwkv_prep_kernel(pa_ref, mu_ref, w0_ref, wd_ref, a0_ref, wi_ref, wg_ref, kk_ref, ka_ref, rk_ref,
                      seg_ref, r_ref, k_ref, v_ref, lw_ref, kkn_ref, b_ref, g_ref, bonus_ref, carry_ref):
    @pl.when(pl.program_id(1) == 0)
    def _():
        carry_ref[...] = jnp.zeros_like(carry_ref)

    pa = pa_ref[...]
    tm = pa.shape[0]
    row = lax.broadcasted_iota(jnp.int32, pa.shape, 0)
    prev = jnp.where(row == 0, carry_ref[...], pltpu.roll(pa, 1, axis=0))
    carry_ref[...] = pa[tm - 1:tm, :]
    x = pa + (prev - pa) * mu_ref[...]

    r = x[:, 0:RWKV_DIM]
    k = x[:, RWKV_DIM:2 * RWKV_DIM]
    v = x[:, 2 * RWKV_DIM:3 * RWKV_DIM]
    lora = x[:, 3 * RWKV_DIM:3 * RWKV_DIM + LANES]
    gl = x[:, 3 * RWKV_DIM + LANES:]
    seg = seg_ref[...]

    w_raw = w0_ref[...] + _bdot(jnp.tanh(lora), wd_ref[...])
    lw_ref[...] = (-math.exp(-0.5)) * _sigmoid(w_raw)
    iclr = _sigmoid(a0_ref[...] + _bdot(lora, wi_ref[...]))
    g_ref[...] = _bdot(_sigmoid(gl), wg_ref[...])

    kk = k * kk_ref[...]
    kkn = kk * lax.rsqrt(_dot_exact_rhs(kk * kk, seg) + 1e-12)
    k2 = k * (1.0 + (iclr - 1.0) * ka_ref[...])
    r_ref[...] = r
    k_ref[...] = k2
    v_ref[...] = v
    kkn_ref[...] = kkn
    b_ref[...] = kkn * iclr
    bonus_ref[...] = _dot_exact_rhs(r * k2 * rk_ref[...], seg) * v


def _rwkv_prep(pa, bsz, seq, mu, w0, wd, a0, wi, wg, k_k, k_a, r_k, seg):
    t = pa.shape[0]
    tm = _tile(seq, 256)
    nt = seq // tm
    row = lambda b, i: (b * nt + i, 0)
    out = jax.ShapeDtypeStruct((t, RWKV_DIM), F32)
    return pl.pallas_call(
        _rwkv_prep_kernel,
        grid=(bsz, nt),
        in_specs=[pl.BlockSpec((tm, RWKV_COLS), row), _full(mu.shape), _full(w0.shape), _full(wd.shape),
                  _full(a0.shape), _full(wi.shape), _full(wg.shape), _full(k_k.shape), _full(k_a.shape),
                  _full(r_k.shape), _full(seg.shape)],
        out_specs=[pl.BlockSpec((tm, RWKV_DIM), row)] * 8,
        out_shape=[out] * 8,
        scratch_shapes=[pltpu.VMEM((1, RWKV_COLS), F32)],
        compiler_params=_params("arbitrary", "arbitrary"),
        name="rwkv_prep",
    )(pa, mu, w0, wd, a0, wi, wg, k_k, k_a, r_k, seg)


def _rwkv_chunk_kernel(r_ref, k_ref, v_ref, lw_ref, kk_ref, b_ref, y_ref, z_ref, *, nchunk):
    c_len = RWKV_CHUNK
    n = HEAD_DIM

    @pl.when(pl.program_id(2) == 0)
    def _():
        z_ref[...] = jnp.zeros_like(z_ref)

    ri = lax.broadcasted_iota(jnp.int32, (c_len, c_len), 0)
    ci = lax.broadcasted_iota(jnp.int32, (c_len, c_len), 1)
    tri_incl = (ri >= ci).astype(BF16)
    eye = ri == ci
    ri2 = lax.broadcasted_iota(jnp.int32, (2 * c_len, c_len), 0)
    ci2 = lax.broadcasted_iota(jnp.int32, (2 * c_len, c_len), 1)
    mask2 = ci2 <= jnp.where(ri2 < c_len, ri2 - 1, ri2 - c_len)

    for c in range(nchunk):
        sl = pl.ds(c * c_len, c_len)
        lw = lw_ref[sl, :]
        g_in = _dot_exact_lhs(tri_incl, lw)
        g_ex = g_in - lw
        g_end = g_in[c_len - 1:c_len, :]
        e_end = jnp.exp(g_end - g_in)
        en = jnp.exp(-g_in)
        r = r_ref[sl, :]
        k = k_ref[sl, :]
        v = v_ref[sl, :]
        b = b_ref[sl, :]
        rt = r * jnp.exp(g_in)
        at = -kk_ref[sl, :] * jnp.exp(g_ex)
        kt = k * en
        bt = b * en
        bend = b * e_end
        kend = k * e_end
        pend = jnp.exp(g_end)
        ys = []
        for h in range(2):
            ls = slice(h * n, (h + 1) * n)
            at_h = at[:, ls]
            rt_h = rt[:, ls]
            v_h = v[:, ls]
            la = jnp.concatenate([at_h, rt_h], axis=0)
            xb = jnp.where(mask2, _bdot_nt(la, bt[:, ls]), 0.0)
            xk = jnp.where(mask2, _bdot_nt(la, kt[:, ls]), 0.0)
            xkv = _bdot(xk, v_h)
            p = xb[0:c_len]
            arb = xb[c_len:]
            x = jnp.concatenate([at_h, xkv[0:c_len]], axis=1)
            for i in range(6):
                x = x + _bdot(p, x)
                if i < 5:
                    p = _bdot(p, p)
            mn = _bdot_tn(bend[:, ls], x)
            kv = _bdot_tn(kend[:, ls], v_h)
            m_mat = mn[:, 0:n] + jnp.where(eye, pend[:, ls], 0.0)
            n_mat = mn[:, n:] + kv
            yy = _bdot(arb, x)
            rq = yy[:, 0:n] + rt_h
            y0 = yy[:, n:] + xkv[c_len:]
            z = z_ref[h]
            ys.append(_bdot(rq, z) + y0)
            z_ref[h] = _bdot(m_mat, z) + n_mat
        y_ref[sl, :] = jnp.concatenate(ys, axis=1)


def _rwkv_chunk(r, k, v, lw, kkn, b, bsz, seq):
    t = r.shape[0]
    lt = _tile(seq, 256)
    nt = seq // lt
    spec = pl.BlockSpec((lt, LANES), lambda bi, hp, i: (bi * nt + i, hp))
    return pl.pallas_call(
        functools.partial(_rwkv_chunk_kernel, nchunk=lt // RWKV_CHUNK),
        grid=(bsz, RWKV_DIM // LANES, nt),
        in_specs=[spec] * 6,
        out_specs=spec,
        out_shape=jax.ShapeDtypeStruct((t, RWKV_DIM), F32),
        scratch_shapes=[pltpu.VMEM((2, HEAD_DIM, HEAD_DIM), F32)],
        compiler_params=_params("arbitrary", "arbitrary", "arbitrary"),
        name="rwkv_chunk",
    )(r, k, v, lw, kkn, b)


def _ssd_kernel(pb_ref, cw_ref, cb_ref, dtb_ref, alog_ref, dskip_ref, nrm_ref, ys_ref, ext_ref, st_ref):
    q = SSD_CHUNK
    p = HEAD_DIM
    hpg = SSD_HEADS // SSD_GROUPS

    @pl.when(pl.program_id(1) == 0)
    def _():
        ext_ref[0:8, :] = jnp.zeros((8, SSD_XBC), F32)
        st_ref[...] = jnp.zeros_like(st_ref)

    z = pb_ref[:, 0:SSD_DIM]
    u = pb_ref[:, SSD_DIM:SSD_DIM + SSD_XBC]
    dt_raw = pb_ref[:, SSD_DIM + SSD_XBC:]

    ext_ref[8:8 + q, :] = u
    conv = cb_ref[...] + cw_ref[SSD_CONV - 1:SSD_CONV, :] * u
    for j in range(SSD_CONV - 1):
        off = 8 - (SSD_CONV - 1) + j
        conv = conv + cw_ref[j:j + 1, :] * ext_ref[off:off + q, :]
    ext_ref[0:8, :] = u[q - 8:q, :]
    xbc = _silu(conv)
    xs = xbc[:, 0:SSD_DIM]
    bm = xbc[:, SSD_DIM:SSD_DIM + SSD_GROUPS * SSD_STATE]
    cm = xbc[:, SSD_DIM + SSD_GROUPS * SSD_STATE:]

    dt = _softplus(dt_raw + dtb_ref[...])
    a = -jnp.exp(alog_ref[...])
    ri = lax.broadcasted_iota(jnp.int32, (q, q), 0)
    ci = lax.broadcasted_iota(jnp.int32, (q, q), 1)
    causal = ri >= ci
    cum = _dot_exact_lhs(causal.astype(BF16), dt * a)
    cum_t = cum.T
    dt_t = dt.T
    cum_end = cum[q - 1:q, :]
    to_end = jnp.exp(cum_end - cum) * dt
    ecum = jnp.exp(cum)
    edec = jnp.exp(cum_end)

    ys = []
    for g in range(SSD_GROUPS):
        bm_g = bm[:, g * SSD_STATE:(g + 1) * SSD_STATE]
        cm_g = cm[:, g * SSD_STATE:(g + 1) * SSD_STATE]
        cb = _bdot_nt(cm_g, bm_g)
        bm_t = bm_g.T
        for hh in range(hpg):
            h = g * hpg + hh
            x_h = xs[:, h * p:(h + 1) * p]
            seg = cum[:, h:h + 1] - cum_t[h:h + 1, :]
            ldec = jnp.exp(jnp.where(causal, seg, -jnp.inf))
            wts = cb * ldec * dt_t[h:h + 1, :]
            y = _bdot(wts, x_h)
            h_prev = st_ref[h]
            y = y + _bdot(cm_g, h_prev) * ecum[:, h:h + 1]
            st_ref[h] = h_prev * edec[:, h:h + 1] + _bdot(bm_t, x_h * to_end[:, h:h + 1])
            ys.append(y + dskip_ref[:, h:h + 1] * x_h)
    yall = jnp.concatenate(ys, axis=1) * _silu(z)
    gw = SSD_DIM // SSD_GROUPS
    outs = []
    for g in range(SSD_GROUPS):
        yg = yall[:, g * gw:(g + 1) * gw]
        ms = jnp.mean(yg * yg, axis=-1, keepdims=True)
        outs.append(yg * lax.rsqrt(ms + NORM_EPS) * nrm_ref[:, g * gw:(g + 1) * gw])
    ys_ref[...] = jnp.concatenate(outs, axis=1)


def _ssd(pb, bsz, seq, conv_w, conv_b, dt_bias, a_log, d_skip, ssd_norm):
    t = pb.shape[0]
    nc = seq // SSD_CHUNK
    row = lambda b, i: (b * nc + i, 0)
    return pl.pallas_call(
        _ssd_kernel,
        grid=(bsz, nc),
        in_specs=[pl.BlockSpec((SSD_CHUNK, SSD_COLS_PAD), row), _full(conv_w.shape), _full(conv_b.shape),
                  _full(dt_bias.shape), _full(a_log.shape), _full(d_skip.shape), _full(ssd_norm.shape)],
        out_specs=pl.BlockSpec((SSD_CHUNK, SSD_DIM), row),
        out_shape=jax.ShapeDtypeStruct((t, SSD_DIM), F32),
        scratch_shapes=[pltpu.VMEM((8 + SSD_CHUNK, SSD_XBC), F32),
                        pltpu.VMEM((SSD_HEADS, SSD_STATE, HEAD_DIM), F32)],
        compiler_params=_params("arbitrary", "arbitrary"),
        name="ssd",
    )(pb, conv_w, conv_b, dt_bias, a_log, d_skip, ssd_norm)


def _mix_out_kernel(h_ref, y_ref, bonus_ref, g_ref, ys_ref, gnw_ref, gnb_ref, seg_ref, wa_ref, wb_ref, o_ref):
    y = y_ref[...]
    seg = seg_ref[...]
    mu = _dot_exact_rhs(y, seg) * (1.0 / HEAD_DIM)
    yc = y - mu
    var = _dot_exact_rhs(yc * yc, seg) * (1.0 / HEAD_DIM)
    yn = yc * lax.rsqrt(var + RWKV_GN_EPS) * gnw_ref[...] + gnb_ref[...]
    ya = (yn + bonus_ref[...]) * g_ref[...]
    o_ref[...] = h_ref[...] + _bdot(ya, wa_ref[...]) + _bdot(ys_ref[...], wb_ref[...])


def _mix_out(h2, y, bonus, g, ys, gn_w, gn_b, seg, wa, wb):
    t = h2.shape[0]
    tm = _tile(t, 512)
    row = lambda i: (i, 0)
    half = pl.BlockSpec((tm, RWKV_DIM), row)
    return pl.pallas_call(
        _mix_out_kernel,
        grid=(t // tm,),
        in_specs=[pl.BlockSpec((tm, D_MODEL), row), half, half, half, half, _full(gn_w.shape),
                  _full(gn_b.shape), _full(seg.shape), _full(wa.shape), _full(wb.shape)],
        out_specs=pl.BlockSpec((tm, D_MODEL), row),
        out_shape=jax.ShapeDtypeStruct((t, D_MODEL), F32),
        compiler_params=_params("parallel"),
        name="mix_out",
    )(h2, y, bonus, g, ys, gn_w, gn_b, seg, wa, wb)


def _ffn_kernel(h_ref, g_ref, wg_ref, wu_ref, wd_ref, o_ref, xn_ref):
    j = pl.program_id(1)

    @pl.when(j == 0)
    def _():
        h = h_ref[...]
        xn_ref[...] = _rms(h, g_ref[...]).astype(BF16)
        o_ref[...] = h

    xn = xn_ref[...]
    act = _silu(jnp.dot(xn, wg_ref[...], preferred_element_type=F32)) * jnp.dot(
        xn, wu_ref[...], preferred_element_type=F32)
    o_ref[...] += jnp.dot(act.astype(BF16), wd_ref[...], preferred_element_type=F32)


def _ffn(h2, g, wg, wu, wd):
    t = h2.shape[0]
    f = wg.shape[1]
    tm = _tile(t, 512)
    tf = 256
    return pl.pallas_call(
        _ffn_kernel,
        grid=(t // tm, f // tf),
        in_specs=[pl.BlockSpec((tm, D_MODEL), lambda i, j: (i, 0)), _full((1, D_MODEL)),
                  pl.BlockSpec((D_MODEL, tf), lambda i, j: (0, j)),
                  pl.BlockSpec((D_MODEL, tf), lambda i, j: (0, j)),
                  pl.BlockSpec((tf, D_MODEL), lambda i, j: (j, 0))],
        out_specs=pl.BlockSpec((tm, D_MODEL), lambda i, j: (i, 0)),
        out_shape=jax.ShapeDtypeStruct((t, D_MODEL), F32),
        scratch_shapes=[pltpu.VMEM((tm, D_MODEL), BF16)],
        compiler_params=_params("parallel", "arbitrary"),
        name="ffn",
    )(h2, g, wg, wu, wd)


def _rope_table_kernel(pos_ref, freq_ref, cos_ref, sin_ref):
    ang = pos_ref[...] * freq_ref[...]
    cos_ref[...] = jnp.cos(ang)
    sin_ref[...] = jnp.sin(ang)


def _rope_tables(positions):
    t = positions.size
    half = ROPE_DIM // 2
    rows = t * half // LANES
    inv_freq = ROPE_THETA ** (-jnp.arange(0, ROPE_DIM, 2, dtype=F32) / ROPE_DIM)
    pos_rep = jnp.repeat(positions.reshape(-1).astype(F32), half).reshape(rows, LANES)
    freq = jnp.tile(inv_freq, LANES // half).reshape(1, LANES)
    tr = _tile(rows, 256)
    cos, sin = pl.pallas_call(
        _rope_table_kernel,
        grid=(rows // tr,),
        in_specs=[pl.BlockSpec((tr, LANES), lambda i: (i, 0)), _full((1, LANES))],
        out_specs=[pl.BlockSpec((tr, LANES), lambda i: (i, 0))] * 2,
        out_shape=[jax.ShapeDtypeStruct((rows, LANES), F32)] * 2,
        compiler_params=_params("parallel"),
        name="rope_tables",
    )(pos_rep, freq)
    cos = cos.reshape(t, half)
    sin = sin.reshape(t, half)
    pad = HEAD_DIM - ROPE_DIM
    one = jnp.ones((t, pad), F32)
    zero = jnp.zeros((t, pad), F32)
    zh = jnp.zeros((t, half), F32)
    cos_f = jnp.concatenate([cos, cos, one], axis=1)
    sin_a = jnp.concatenate([-sin, zh, zero], axis=1)
    sin_b = jnp.concatenate([zh, sin, zero], axis=1)
    return tuple(jnp.tile(m, (1, LANES // HEAD_DIM)) for m in (cos_f, sin_a, sin_b))


def _qkv_kernel(h_ref, g_ref, w_ref, b_ref, qn_ref, kn_ref, cos_ref, sa_ref, sb_ref, seg_ref,
                q_ref, k_ref, v_ref):
    xn = _rms(h_ref[...], g_ref[...]).astype(BF16)
    qkv = jnp.dot(xn, w_ref[...], preferred_element_type=F32) + b_ref[...]
    cos = cos_ref[...]
    sa = sa_ref[...]
    sb = sb_ref[...]
    seg = seg_ref[...]

    def norm_rope(x, gain):
        ms = _dot_exact_rhs(x * x, seg) * (1.0 / HEAD_DIM)
        xn_ = x * lax.rsqrt(ms + NORM_EPS) * gain
        return (xn_ * cos + pltpu.roll(xn_, LANES - ROPE_DIM // 2, axis=1) * sa
                + pltpu.roll(xn_, ROPE_DIM // 2, axis=1) * sb)

    for c in range(Q_DIM // LANES):
        x = qkv[:, c * LANES:(c + 1) * LANES]
        q_ref[:, c * LANES:(c + 1) * LANES] = (norm_rope(x, qn_ref[...]) * (HEAD_DIM ** -0.5)).astype(BF16)
    for c in range(KV_DIM // LANES):
        x = qkv[:, Q_DIM + c * LANES:Q_DIM + (c + 1) * LANES]
        k_ref[:, c * LANES:(c + 1) * LANES] = norm_rope(x, kn_ref[...]).astype(BF16)
    v_ref[...] = qkv[:, Q_DIM + KV_DIM:].astype(BF16)


def _qkv(h2, g, w, b, qn, kn, cos_f, sin_a, sin_b, seg2):
    t = h2.shape[0]
    tm = _tile(t, 256)
    row = lambda i: (i, 0)
    tab = pl.BlockSpec((tm, LANES), row)
    return pl.pallas_call(
        _qkv_kernel,
        grid=(t // tm,),
        in_specs=[pl.BlockSpec((tm, D_MODEL), row), _full((1, D_MODEL)), _full(w.shape), _full(b.shape),
                  _full(qn.shape), _full(kn.shape), tab, tab, tab, _full(seg2.shape)],
        out_specs=[pl.BlockSpec((tm, Q_DIM), row), pl.BlockSpec((tm, KV_DIM), row),
                   pl.BlockSpec((tm, KV_DIM), row)],
        out_shape=[jax.ShapeDtypeStruct((t, Q_DIM), BF16), jax.ShapeDtypeStruct((t, KV_DIM), BF16),
                   jax.ShapeDtypeStruct((t, KV_DIM), BF16)],
        compiler_params=_params("parallel"),
        name="qkv",
    )(h2, g, w, b, qn, kn, cos_f, sin_a, sin_b, seg2)


def _attn_kernel(h_ref, q_ref, kc_ref, kp_ref, vc_ref, vp_ref, sink_ref, wo_ref, bo_ref, o_ref):
    w = WINDOW
    n = HEAD_DIM
    kmin = jnp.where(pl.program_id(1) == 0, w, 0)
    rows = GQA_GROUP * w
    qi = lax.broadcasted_iota(jnp.int32, (rows, 2 * w), 0) % w
    kj = lax.broadcasted_iota(jnp.int32, (rows, 2 * w), 1)
    diff = w + qi - kj
    valid = (diff >= 0) & (diff < w) & (kj >= kmin)
    outs = [None] * Q_HEADS
    for g in range(KV_HEADS):
        ls = slice(g * n, (g + 1) * n)
        kk = jnp.concatenate([kp_ref[:, ls], kc_ref[:, ls]], axis=0)
        vv = jnp.concatenate([vp_ref[:, ls], vc_ref[:, ls]], axis=0)
        qs = jnp.concatenate(
            [q_ref[:, (g * GQA_GROUP + i) * n:(g * GQA_GROUP + i + 1) * n] for i in range(GQA_GROUP)], axis=0)
        s = lax.dot_general(qs, kk, (((1,), (1,)), ((), ())), preferred_element_type=F32)
        s = jnp.where(valid, s, -jnp.inf)
        sink = jnp.concatenate(
            [jnp.broadcast_to(sink_ref[:, g * GQA_GROUP + i:g * GQA_GROUP + i + 1], (w, 1))
             for i in range(GQA_GROUP)], axis=0)
        m = jnp.maximum(jnp.max(s, axis=-1, keepdims=True), sink)
        e = jnp.exp(s - m)
        den = jnp.sum(e, axis=-1, keepdims=True) + jnp.exp(sink - m)
        o = jnp.dot(e.astype(BF16), vv, preferred_element_type=F32) / den
        for i in range(GQA_GROUP):
            outs[g * GQA_GROUP + i] = o[i * w:(i + 1) * w]
    att = jnp.concatenate(outs, axis=1)
    o_ref[...] = h_ref[...] + _bdot(att, wo_ref[...]) + bo_ref[...]


def _attn(h2, q, k, v, sinks, wo, bo, bsz, seq):
    t = h2.shape[0]
    nb = seq // WINDOW
    cur = lambda b, i: (b * nb + i, 0)
    prv = lambda b, i: (b * nb + jnp.maximum(i - 1, 0), 0)
    return pl.pallas_call(
        _attn_kernel,
        grid=(bsz, nb),
        in_specs=[pl.BlockSpec((WINDOW, D_MODEL), cur), pl.BlockSpec((WINDOW, Q_DIM), cur),
                  pl.BlockSpec((WINDOW, KV_DIM), cur), pl.BlockSpec((WINDOW, KV_DIM), prv),
                  pl.BlockSpec((WINDOW, KV_DIM), cur), pl.BlockSpec((WINDOW, KV_DIM), prv),
                  _full(sinks.shape), _full(wo.shape), _full(bo.shape)],
        out_specs=pl.BlockSpec((WINDOW, D_MODEL), cur),
        out_shape=jax.ShapeDtypeStruct((t, D_MODEL), F32),
        compiler_params=_params("parallel", "parallel"),
        name="attn",
    )(h2, q, k, k, v, v, sinks, wo, bo)


def _router_kernel(h_ref, g_ref, wr_ref, xn_ref, comb_ref):
    xn = _rms(h_ref[...], g_ref[...])
    xn_ref[...] = xn.astype(BF16)
    hi, lo = _split(xn)
    whi = wr_ref[0]
    wlo = wr_ref[1]
    logits = (jnp.dot(hi, whi, preferred_element_type=F32) + jnp.dot(hi, wlo, preferred_element_type=F32)
              + jnp.dot(lo, whi, preferred_element_type=F32))
    lane = lax.broadcasted_iota(jnp.int32, logits.shape, 1)
    logits = jnp.where(lane < N_EXPERTS, logits, -jnp.inf)
    m1 = jnp.max(logits, axis=-1, keepdims=True)
    i1 = jnp.min(jnp.where(logits == m1, lane, LANES), axis=-1, keepdims=True)
    rest = jnp.where(lane == i1, -jnp.inf, logits)
    m2 = jnp.max(rest, axis=-1, keepdims=True)
    i2 = jnp.min(jnp.where(rest == m2, lane, LANES), axis=-1, keepdims=True)
    e2 = jnp.exp(m2 - m1)
    w1 = 1.0 / (1.0 + e2)
    w2 = e2 / (1.0 + e2)
    comb_ref[...] = jnp.where(lane == i1, w1, 0.0) + jnp.where(lane == i2, w2, 0.0)


def _router(h2, g, wr):
    t = h2.shape[0]
    tm = _tile(t, 512)
    row = lambda i: (i, 0)
    return pl.pallas_call(
        _router_kernel,
        grid=(t // tm,),
        in_specs=[pl.BlockSpec((tm, D_MODEL), row), _full((1, D_MODEL)), _full(wr.shape)],
        out_specs=[pl.BlockSpec((tm, D_MODEL), row), pl.BlockSpec((tm, LANES), row)],
        out_shape=[jax.ShapeDtypeStruct((t, D_MODEL), BF16), jax.ShapeDtypeStruct((t, LANES), F32)],
        compiler_params=_params("parallel"),
        name="router",
    )(h2, g, wr)


def _moe_kernel(h_ref, xn_ref, comb_ref, wg_ref, wu_ref, wd_ref, o_ref):
    e = pl.program_id(1)
    j = pl.program_id(2)

    @pl.when((e == 0) & (j == 0))
    def _():
        o_ref[...] = h_ref[...]

    xn = xn_ref[...]
    lane = lax.broadcasted_iota(jnp.int32, comb_ref.shape, 1)
    cw = jnp.sum(jnp.where(lane == e, comb_ref[...], 0.0), axis=-1, keepdims=True)
    act = _silu(jnp.dot(xn, wg_ref[0], preferred_element_type=F32)) * jnp.dot(
        xn, wu_ref[0], preferred_element_type=F32)
    o_ref[...] += jnp.dot((act * cw).astype(BF16), wd_ref[0], preferred_element_type=F32)


def _moe(h2, xn, comb, wg, wu, wd):
    t = h2.shape[0]
    tm = _tile(t, 512)
    tf = 128
    nf = FFN_EXPERT // tf
    return pl.pallas_call(
        _moe_kernel,
        grid=(t // tm, N_EXPERTS, nf),
        in_specs=[pl.BlockSpec((tm, D_MODEL), lambda i, e, j: (i, 0)),
                  pl.BlockSpec((tm, D_MODEL), lambda i, e, j: (i, 0)),
                  pl.BlockSpec((tm, LANES), lambda i, e, j: (i, 0)),
                  pl.BlockSpec((1, D_MODEL, tf), lambda i, e, j: (e, 0, j)),
                  pl.BlockSpec((1, D_MODEL, tf), lambda i, e, j: (e, 0, j)),
                  pl.BlockSpec((1, tf, D_MODEL), lambda i, e, j: (e, j, 0))],
        out_specs=pl.BlockSpec((tm, D_MODEL), lambda i, e, j: (i, 0)),
        out_shape=jax.ShapeDtypeStruct((t, D_MODEL), F32),
        compiler_params=_params("parallel", "arbitrary", "arbitrary"),
        name="moe",
    )(h2, xn, comb, wg, wu, wd)


def _segment_matrix(width):
    i = jnp.arange(width) // HEAD_DIM
    return (i[:, None] == i[None, :]).astype(BF16)


def _layer0(h2, bsz, seq, norm_mix, w_in, mu_shift, w0, w_decay_up, a0, w_iclr_up, w_gate_up, k_k, k_a, r_k,
            gn_w, gn_b, conv_w, conv_b, dt_bias, a_log, d_skip, ssd_norm, w_out, norm_ffn, ffn_gate, ffn_up,
            ffn_down):
    row = lambda a: a.reshape(1, -1)
    seg = _segment_matrix(RWKV_DIM)
    wa = w_in[:, :RWKV_COLS].astype(BF16)
    wb = jnp.pad(w_in[:, RWKV_COLS:], ((0, 0), (0, SSD_COLS_PAD - SSD_COLS))).astype(BF16)
    pa, pb = _in_proj(h2, row(norm_mix), wa, wb)

    lora = w_decay_up.shape[0]
    wd = jnp.concatenate([w_decay_up, jnp.zeros((LANES - lora, RWKV_DIM), F32)], axis=0).astype(BF16)
    wi = jnp.concatenate([jnp.zeros((LANES - lora, RWKV_DIM), F32), w_iclr_up], axis=0).astype(BF16)
    r, k, v, lw, kkn, b, g, bonus = _rwkv_prep(
        pa, bsz, seq, row(mu_shift), row(w0), wd, row(a0), wi, w_gate_up.astype(BF16), row(k_k), row(k_a),
        row(r_k), seg)
    y = _rwkv_chunk(r, k, v, lw, kkn, b, bsz, seq)

    lane_pad = lambda a: jnp.pad(row(a), ((0, 0), (0, LANES - a.size)))
    ys = _ssd(pb, bsz, seq, conv_w, row(conv_b), lane_pad(dt_bias), lane_pad(a_log), lane_pad(d_skip),
              row(ssd_norm))

    w_out = w_out.astype(BF16)
    h2 = _mix_out(h2, y, bonus, g, ys, row(gn_w), row(gn_b), seg, w_out[:RWKV_DIM], w_out[RWKV_DIM:])
    return _ffn(h2, row(norm_ffn), ffn_gate.astype(BF16), ffn_up.astype(BF16), ffn_down.astype(BF16))


def _layer1(h2, bsz, seq, tables, norm_mix, w_qkv, b_qkv, q_norm, k_norm, sinks, w_o, b_o, norm_ffn, router,
            exp_gate, exp_up, exp_down):
    row = lambda a: a.reshape(1, -1)
    two = lambda a: jnp.tile(a, LANES // HEAD_DIM).reshape(1, LANES)
    cos_f, sin_a, sin_b = tables
    q, k, v = _qkv(h2, row(norm_mix), w_qkv.astype(BF16), row(b_qkv), two(q_norm), two(k_norm), cos_f, sin_a,
                   sin_b, _segment_matrix(LANES))
    h2 = _attn(h2, q, k, v, row(sinks), w_o.astype(BF16), row(b_o), bsz, seq)

    wr = jnp.pad(router, ((0, 0), (0, LANES - N_EXPERTS)))
    wr_hi = wr.astype(BF16)
    wr_lo = (wr - wr_hi.astype(F32)).astype(BF16)
    xn, comb = _router(h2, row(norm_ffn), jnp.stack([wr_hi, wr_lo]))
    return _moe(h2, xn, comb, exp_gate.astype(BF16), exp_up.astype(BF16), exp_down.astype(BF16))


def kernel(x, positions, ev_norm_mix, ev_w_in, ev_mu_shift, ev_w0, ev_w_decay_up, ev_a0, ev_w_iclr_up, ev_w_gate_up, ev_k_k, ev_k_a, ev_r_k, ev_gn_w, ev_gn_b, ev_conv_w, ev_conv_b, ev_dt_bias, ev_a_log, ev_d_skip, ev_ssd_norm, ev_w_out, ev_norm_ffn, ev_ffn_gate, ev_ffn_up, ev_ffn_down, od_norm_mix, od_w_qkv, od_b_qkv, od_q_norm, od_k_norm, od_sinks, od_w_o, od_b_o, od_norm_ffn, od_router, od_exp_gate, od_exp_up, od_exp_down):
    bsz, seq, d = x.shape
    depth = ev_norm_mix.shape[0] + od_norm_mix.shape[0]
    tables = _rope_tables(positions)
    h2 = x.reshape(bsz * seq, d)
    for layer in range(depth):
        i = layer // 2
        if layer % 2 == 0:
            h2 = _layer0(h2, bsz, seq, ev_norm_mix[i], ev_w_in[i], ev_mu_shift[i], ev_w0[i], ev_w_decay_up[i],
                         ev_a0[i], ev_w_iclr_up[i], ev_w_gate_up[i], ev_k_k[i], ev_k_a[i], ev_r_k[i],
                         ev_gn_w[i], ev_gn_b[i], ev_conv_w[i], ev_conv_b[i], ev_dt_bias[i], ev_a_log[i],
                         ev_d_skip[i], ev_ssd_norm[i], ev_w_out[i], ev_norm_ffn[i], ev_ffn_gate[i],
                         ev_ffn_up[i], ev_ffn_down[i])
        else:
            h2 = _layer1(h2, bsz, seq, tables, od_norm_mix[i], od_w_qkv[i], od_b_qkv[i], od_q_norm[i],
                         od_k_norm[i], od_sinks[i], od_w_o[i], od_b_o[i], od_norm_ffn[i], od_router[i],
                         od_exp_gate[i], od_exp_up[i], od_exp_down[i])
    return h2.reshape(bsz, seq, d)
```

```python
import functools
import math

import jax
import jax.numpy as jnp
from jax import lax
from jax.experimental import pallas as pl
from jax.experimental.pallas import tpu as pltpu

F32 = jnp.float32
BF16 = jnp.bfloat16

D_MODEL = 1024
HEAD_DIM = 64
NORM_EPS = 1e-6

RWKV_HEADS = 8
RWKV_DIM = 512
RWKV_COLS = 1792
RWKV_GN_EPS = 64e-5
RWKV_CHUNK = 64
RWKV_GROUP = 256

SSD_HEADS = 8
SSD_DIM = 512
SSD_GROUPS = 2
SSD_STATE = 128
SSD_CONV = 4
SSD_CHUNK = 128
SSD_XBC = 1024
SSD_COLS = 1544
SSD_COLS_PAD = 1664

Q_HEADS = 16
KV_HEADS = 4
GQA_GROUP = 4
Q_DIM = 1024
KV_DIM = 256
WINDOW = 128
ROPE_THETA = 500000.0
ROPE_DIM = 16

FFN_DENSE = 2816
N_EXPERTS = 8
FFN_EXPERT = 1408
MOE_TM = 256
MOE_DMA_CHUNK = 256

LANES = 128
VMEM_LIMIT_BYTES = 56 * 1024 * 1024


def _params(*sem):
    return pltpu.CompilerParams(dimension_semantics=sem, vmem_limit_bytes=VMEM_LIMIT_BYTES)


def _bdot(a, b):
    return jnp.dot(a.astype(BF16), b.astype(BF16), preferred_element_type=F32)


def _bdot_nt(a, b):
    return lax.dot_general(a.astype(BF16), b.astype(BF16), (((1,), (1,)), ((), ())),
                           preferred_element_type=F32)


def _bdot_tn(a, b):
    return lax.dot_general(a.astype(BF16), b.astype(BF16), (((0,), (0,)), ((), ())),
                           preferred_element_type=F32)


def _split(x):
    hi = x.astype(BF16)
    lo = (x - hi.astype(F32)).astype(BF16)
    return hi, lo


def _dot_exact_rhs(x, m):
    hi, lo = _split(x)
    return (jnp.dot(hi, m, preferred_element_type=F32) + jnp.dot(lo, m, preferred_element_type=F32))


def _dot_exact_lhs(m, x):
    hi, lo = _split(x)
    return (jnp.dot(m, hi, preferred_element_type=F32) + jnp.dot(m, lo, preferred_element_type=F32))


def _sigmoid(x):
    return 1.0 / (1.0 + jnp.exp(-x))


def _silu(x):
    return x * _sigmoid(x)


def _softplus(x):
    return jnp.maximum(x, 0.0) + jnp.log(1.0 + jnp.exp(-jnp.abs(x)))


def _rms(x, g):
    ms = jnp.mean(x * x, axis=-1, keepdims=True)
    return x * lax.rsqrt(ms + NORM_EPS) * g


def _tile(n, pref):
    t = min(n, pref)
    while n % t:
        t //= 2
    return t


def _full(shape):
    nd = len(shape)
    return pl.BlockSpec(shape, lambda *_: (0,) * nd)


def _in_proj_kernel(x_ref, g_ref, wa_ref, wb_ref, pa_ref, pb_ref):
    xn = _rms(x_ref[...], g_ref[...]).astype(BF16)
    pa_ref[...] = jnp.dot(xn, wa_ref[...], preferred_element_type=F32)
    pb_ref[...] = jnp.dot(xn, wb_ref[...], preferred_element_type=F32)


def _in_proj(x2, g, wa, wb):
    t = x2.shape[0]
    tm = _tile(t, 256)
    return pl.pallas_call(
        _in_proj_kernel,
        grid=(t // tm,),
        in_specs=[pl.BlockSpec((tm, D_MODEL), lambda i: (i, 0)), _full((1, D_MODEL)),
                  _full(wa.shape), _full(wb.shape)],
        out_specs=[pl.BlockSpec((tm, RWKV_COLS), lambda i: (i, 0)),
                   pl.BlockSpec((tm, SSD_COLS_PAD), lambda i: (i, 0))],
        out_shape=[jax.ShapeDtypeStruct((t, RWKV_COLS), F32),
                   jax.ShapeDtypeStruct((t, SSD_COLS_PAD), F32)],
        compiler_params=_params("parallel"),
        name="in_proj",
    )(x2, g, wa, wb)


def _rwkv_prep_kernel(pa_ref, mu_ref, w0_ref, wd_ref, a0_ref, wi_ref, wg_ref, kk_ref, ka_ref, rk_ref,
                      seg_ref, r_ref, k_ref, v_ref, lw_ref, kkn_ref, b_ref, g_ref, bonus_ref, carry_ref):
    @pl.when(pl.program_id(1) == 0)
    def _():
        carry_ref[...] = jnp.zeros_like(carry_ref)

    pa = pa_ref[...]
    tm = pa.shape[0]
    row = lax.broadcasted_iota(jnp.int32, pa.shape, 0)
    prev = jnp.where(row == 0, carry_ref[...], pltpu.roll(pa, 1, axis=0))
    carry_ref[...] = pa[tm - 1:tm, :]
    x = pa + (prev - pa) * mu_ref[...]

    r = x[:, 0:RWKV_DIM]
    k = x[:, RWKV_DIM:2 * RWKV_DIM]
    v = x[:, 2 * RWKV_DIM:3 * RWKV_DIM]
    lora = x[:, 3 * RWKV_DIM:3 * RWKV_DIM + LANES]
    gl = x[:, 3 * RWKV_DIM + LANES:]
    seg = seg_ref[...]

    w_raw = w0_ref[...] + _bdot(jnp.tanh(lora), wd_ref[...])
    lw_ref[...] = (-math.exp(-0.5)) * _sigmoid(w_raw)
    iclr = _sigmoid(a0_ref[...] + _bdot(lora, wi_ref[...]))
    g_ref[...] = _bdot(_sigmoid(gl), wg_ref[...])

    kk = k * kk_ref[...]
    kkn = kk * lax.rsqrt(_dot_exact_rhs(kk * kk, seg) + 1e-12)
    k2 = k * (1.0 + (iclr - 1.0) * ka_ref[...])
    r_ref[...] = r
    k_ref[...] = k2
    v_ref[...] = v
    kkn_ref[...] = kkn
    b_ref[...] = kkn * iclr
    bonus_ref[...] = _dot_exact_rhs(r * k2 * rk_ref[...], seg) * v


def _rwkv_prep(pa, bsz, seq, mu, w0, wd, a0, wi, wg, k_k, k_a, r_k, seg):
    t = pa.shape[0]
    tm = _tile(seq, 256)
    nt = seq // tm
    row = lambda b, i: (b * nt + i, 0)
    out = jax.ShapeDtypeStruct((t, RWKV_DIM), F32)
    return pl.pallas_call(
        _rwkv_prep_kernel,
        grid=(bsz, nt),
        in_specs=[pl.BlockSpec((tm, RWKV_COLS), row), _full(mu.shape), _full(w0.shape), _full(wd.shape),
                  _full(a0.shape), _full(wi.shape), _full(wg.shape), _full(k_k.shape), _full(k_a.shape),
                  _full(r_k.shape), _full(seg.shape)],
        out_specs=[pl.BlockSpec((tm, RWKV_DIM), row)] * 8,
        out_shape=[out] * 8,
        scratch_shapes=[pltpu.VMEM((1, RWKV_COLS), F32)],
        compiler_params=_params("arbitrary", "arbitrary"),
        name="rwkv_prep",
    )(pa, mu, w0, wd, a0, wi, wg, k_k, k_a, r_k, seg)


def _rwkv_chunk_kernel(r_ref, k_ref, v_ref, lw_ref, kk_ref, b_ref, y_ref, z_ref, *, lt):
    c_len = RWKV_CHUNK
    n = HEAD_DIM
    gl = RWKV_GROUP
    nchunk = gl // c_len
    nheads = LANES // n
    shift = c_len.bit_length() - 1

    @pl.when(pl.program_id(2) == 0)
    def _():
        z_ref[...] = jnp.zeros_like(z_ref)

    ri = lax.broadcasted_iota(jnp.int32, (gl, gl), 0)
    ci = lax.broadcasted_iota(jnp.int32, (gl, gl), 1)
    tri_bd = jnp.where((ci <= ri) & (ci >= ((ri >> shift) << shift)), 1.0, 0.0).astype(BF16)
    re_ = lax.broadcasted_iota(jnp.int32, (c_len, c_len), 0)
    ce_ = lax.broadcasted_iota(jnp.int32, (c_len, c_len), 1)
    eye = re_ == ce_
    ri2 = lax.broadcasted_iota(jnp.int32, (2 * gl, gl), 0)
    ci2 = lax.broadcasted_iota(jnp.int32, (2 * gl, gl), 1)
    t2 = jnp.where(ri2 < gl, ri2, ri2 - gl)
    mask2 = (ci2 <= jnp.where(ri2 < gl, t2 - 1, t2)) & (ci2 >= ((t2 >> shift) << shift))
    zeros = jnp.zeros((c_len, n), F32)

    nsub = lt // gl
    sysid = [(s, h) for s in range(nsub) for h in range(nheads)]
    pre = []
    for s in range(nsub):
        sl = pl.ds(s * gl, gl)
        lw = lw_ref[sl, :]
        g_in = _dot_exact_lhs(tri_bd, lw)
        ends = [g_in[(c + 1) * c_len - 1:(c + 1) * c_len, :] for c in range(nchunk)]
        g_end = jnp.concatenate([jnp.broadcast_to(e, (c_len, LANES)) for e in ends], axis=0)
        e_end = jnp.exp(g_end - g_in)
        en = jnp.exp(-g_in)
        k = k_ref[sl, :]
        b = b_ref[sl, :]
        pre.append(dict(
            ends=ends, v=v_ref[sl, :], rt=r_ref[sl, :] * jnp.exp(g_in),
            at=-kk_ref[sl, :] * jnp.exp(g_in - lw), kt=k * en, bt=b * en, bend=b * e_end, kend=k * e_end))

    def hs(name, s, h):
        return pre[s][name][:, h * n:(h + 1) * n]

    xb, xk, xkv, p, x = {}, {}, {}, {}, {}
    for q in sysid:
        la = jnp.concatenate([hs("at", *q), hs("rt", *q)], axis=0)
        xb[q] = jnp.where(mask2, _bdot_nt(la, hs("bt", *q)), 0.0)
        xk[q] = jnp.where(mask2, _bdot_nt(la, hs("kt", *q)), 0.0)
    for q in sysid:
        xkv[q] = _bdot(xk[q], hs("v", *q))
        p[q] = xb[q][0:gl]
        x[q] = jnp.concatenate([hs("at", *q), xkv[q][0:gl]], axis=1)
    for i in range(6):
        for q in sysid:
            x[q] = x[q] + _bdot(p[q], x[q])
        if i < 5:
            for q in sysid:
                p[q] = _bdot(p[q], p[q])
    rq, y0, mn = {}, {}, {}
    for q in sysid:
        yy = _bdot(xb[q][gl:], x[q])
        rq[q] = yy[:, 0:n] + hs("rt", *q)
        y0[q] = yy[:, n:] + xkv[q][gl:]
        bend_h, kend_h, v_h = hs("bend", *q), hs("kend", *q), hs("v", *q)
        for c in range(nchunk):
            rows = slice(c * c_len, (c + 1) * c_len)
            lhs = jnp.concatenate([bend_h[rows], kend_h[rows]], axis=0)
            rhs = jnp.concatenate([x[q][rows], jnp.concatenate([zeros, v_h[rows]], axis=1)], axis=0)
            mn[q + (c,)] = _bdot_tn(lhs, rhs)
    zs = [z_ref[h] for h in range(nheads)]
    for s in range(nsub):
        yh = [[] for _ in range(nheads)]
        for c in range(nchunk):
            rows = slice(c * c_len, (c + 1) * c_len)
            for h in range(nheads):
                q = (s, h)
                pend = jnp.exp(pre[s]["ends"][c][:, h * n:(h + 1) * n])
                m_mat = mn[q + (c,)][:, 0:n] + jnp.where(eye, pend, 0.0)
                yh[h].append(_bdot(rq[q][rows], zs[h]) + y0[q][rows])
                zs[h] = _bdot(m_mat, zs[h]) + mn[q + (c,)][:, n:]
        y_ref[pl.ds(s * gl, gl), :] = jnp.concatenate(
            [jnp.concatenate(yh[h], axis=0) for h in range(nheads)], axis=1)
    for h in range(nheads):
        z_ref[h] = zs[h]


def _rwkv_chunk(r, k, v, lw, kkn, b, bsz, seq):
    t = r.shape[0]
    lt = _tile(seq, 512)
    nt = seq // lt
    spec = pl.BlockSpec((lt, LANES), lambda bi, hp, i: (bi * nt + i, hp))
    return pl.pallas_call(
        functools.partial(_rwkv_chunk_kernel, lt=lt),
        grid=(bsz, RWKV_DIM // LANES, nt),
        in_specs=[spec] * 6,
        out_specs=spec,
        out_shape=jax.ShapeDtypeStruct((t, RWKV_DIM), F32),
        scratch_shapes=[pltpu.VMEM((2, HEAD_DIM, HEAD_DIM), F32)],
        compiler_params=_params("arbitrary", "arbitrary", "arbitrary"),
        name="rwkv_chunk",
    )(r, k, v, lw, kkn, b)


def _ssd_kernel(pb_ref, cw_ref, cb_ref, dtb_ref, alog_ref, dskip_ref, nrm_ref, ys_ref, ext_ref, st_ref):
    q = SSD_CHUNK
    p = HEAD_DIM
    hpg = SSD_HEADS // SSD_GROUPS

    @pl.when(pl.program_id(1) == 0)
    def _():
        ext_ref[0:8, :] = jnp.zeros((8, SSD_XBC), F32)
        st_ref[...] = jnp.zeros_like(st_ref)

    z = pb_ref[:, 0:SSD_DIM]
    u = pb_ref[:, SSD_DIM:SSD_DIM + SSD_XBC]
    dt_raw = pb_ref[:, SSD_DIM + SSD_XBC:]

    ext_ref[8:8 + q, :] = u
    conv = cb_ref[...] + cw_ref[SSD_CONV - 1:SSD_CONV, :] * u
    for j in range(SSD_CONV - 1):
        off = 8 - (SSD_CONV - 1) + j
        conv = conv + cw_ref[j:j + 1, :] * ext_ref[off:off + q, :]
    ext_ref[0:8, :] = u[q - 8:q, :]
    xbc = _silu(conv)
    xs = xbc[:, 0:SSD_DIM]
    bm = xbc[:, SSD_DIM:SSD_DIM + SSD_GROUPS * SSD_STATE]
    cm = xbc[:, SSD_DIM + SSD_GROUPS * SSD_STATE:]

    dt = _softplus(dt_raw + dtb_ref[...])
    a = -jnp.exp(alog_ref[...])
    ri = lax.broadcasted_iota(jnp.int32, (q, q), 0)
    ci = lax.broadcasted_iota(jnp.int32, (q, q), 1)
    causal = ri >= ci
    cum = _dot_exact_lhs(causal.astype(BF16), dt * a)
    cum_t = cum.T
    dt_t = dt.T
    cum_end = cum[q - 1:q, :]
    to_end = jnp.exp(cum_end - cum) * dt
    ecum = jnp.exp(cum)
    edec = jnp.exp(cum_end)

    ys = []
    for g in range(SSD_GROUPS):
        bm_g = bm[:, g * SSD_STATE:(g + 1) * SSD_STATE]
        cm_g = cm[:, g * SSD_STATE:(g + 1) * SSD_STATE]
        cb = _bdot_nt(cm_g, bm_g)
        bm_t = bm_g.T
        for hh in range(hpg):
            h = g * hpg + hh
            x_h = xs[:, h * p:(h + 1) * p]
            seg = cum[:, h:h + 1] - cum_t[h:h + 1, :]
            ldec = jnp.exp(jnp.where(causal, seg, -jnp.inf))
            wts = cb * ldec * dt_t[h:h + 1, :]
            y = _bdot(wts, x_h)
            h_prev = st_ref[h]
            y = y + _bdot(cm_g, h_prev) * ecum[:, h:h + 1]
            st_ref[h] = h_prev * edec[:, h:h + 1] + _bdot(bm_t, x_h * to_end[:, h:h + 1])
            ys.append(y + dskip_ref[:, h:h + 1] * x_h)
    yall = jnp.concatenate(ys, axis=1) * _silu(z)
    gw = SSD_DIM // SSD_GROUPS
    outs = []
    for g in range(SSD_GROUPS):
        yg = yall[:, g * gw:(g + 1) * gw]
        ms = jnp.mean(yg * yg, axis=-1, keepdims=True)
        outs.append(yg * lax.rsqrt(ms + NORM_EPS) * nrm_ref[:, g * gw:(g + 1) * gw])
    ys_ref[...] = jnp.concatenate(outs, axis=1)


def _ssd(pb, bsz, seq, conv_w, conv_b, dt_bias, a_log, d_skip, ssd_norm):
    t = pb.shape[0]
    nc = seq // SSD_CHUNK
    row = lambda b, i: (b * nc + i, 0)
    return pl.pallas_call(
        _ssd_kernel,
        grid=(bsz, nc),
        in_specs=[pl.BlockSpec((SSD_CHUNK, SSD_COLS_PAD), row), _full(conv_w.shape), _full(conv_b.shape),
                  _full(dt_bias.shape), _full(a_log.shape), _full(d_skip.shape), _full(ssd_norm.shape)],
        out_specs=pl.BlockSpec((SSD_CHUNK, SSD_DIM), row),
        out_shape=jax.ShapeDtypeStruct((t, SSD_DIM), F32),
        scratch_shapes=[pltpu.VMEM((8 + SSD_CHUNK, SSD_XBC), F32),
                        pltpu.VMEM((SSD_HEADS, SSD_STATE, HEAD_DIM), F32)],
        compiler_params=_params("arbitrary", "arbitrary"),
        name="ssd",
    )(pb, conv_w, conv_b, dt_bias, a_log, d_skip, ssd_norm)


def _mix_out_kernel(h_ref, y_ref, bonus_ref, g_ref, ys_ref, gnw_ref, gnb_ref, seg_ref, wa_ref, wb_ref, o_ref):
    y = y_ref[...]
    seg = seg_ref[...]
    mu = _dot_exact_rhs(y, seg) * (1.0 / HEAD_DIM)
    yc = y - mu
    var = _dot_exact_rhs(yc * yc, seg) * (1.0 / HEAD_DIM)
    yn = yc * lax.rsqrt(var + RWKV_GN_EPS) * gnw_ref[...] + gnb_ref[...]
    ya = (yn + bonus_ref[...]) * g_ref[...]
    o_ref[...] = h_ref[...] + _bdot(ya, wa_ref[...]) + _bdot(ys_ref[...], wb_ref[...])


def _mix_out(h2, y, bonus, g, ys, gn_w, gn_b, seg, wa, wb):
    t = h2.shape[0]
    tm = _tile(t, 512)
    row = lambda i: (i, 0)
    half = pl.BlockSpec((tm, RWKV_DIM), row)
    return pl.pallas_call(
        _mix_out_kernel,
        grid=(t // tm,),
        in_specs=[pl.BlockSpec((tm, D_MODEL), row), half, half, half, half, _full(gn_w.shape),
                  _full(gn_b.shape), _full(seg.shape), _full(wa.shape), _full(wb.shape)],
        out_specs=pl.BlockSpec((tm, D_MODEL), row),
        out_shape=jax.ShapeDtypeStruct((t, D_MODEL), F32),
        compiler_params=_params("parallel"),
        name="mix_out",
    )(h2, y, bonus, g, ys, gn_w, gn_b, seg, wa, wb)


def _ffn_kernel(h_ref, g_ref, wg_ref, wu_ref, wd_ref, o_ref, xn_ref):
    j = pl.program_id(1)

    @pl.when(j == 0)
    def _():
        h = h_ref[...]
        xn_ref[...] = _rms(h, g_ref[...]).astype(BF16)
        o_ref[...] = h

    xn = xn_ref[...]
    act = _silu(jnp.dot(xn, wg_ref[...], preferred_element_type=F32)) * jnp.dot(
        xn, wu_ref[...], preferred_element_type=F32)
    o_ref[...] += jnp.dot(act.astype(BF16), wd_ref[...], preferred_element_type=F32)


def _ffn(h2, g, wg, wu, wd):
    t = h2.shape[0]
    f = wg.shape[1]
    tm = _tile(t, 512)
    tf = 256
    return pl.pallas_call(
        _ffn_kernel,
        grid=(t // tm, f // tf),
        in_specs=[pl.BlockSpec((tm, D_MODEL), lambda i, j: (i, 0)), _full((1, D_MODEL)),
                  pl.BlockSpec((D_MODEL, tf), lambda i, j: (0, j)),
                  pl.BlockSpec((D_MODEL, tf), lambda i, j: (0, j)),
                  pl.BlockSpec((tf, D_MODEL), lambda i, j: (j, 0))],
        out_specs=pl.BlockSpec((tm, D_MODEL), lambda i, j: (i, 0)),
        out_shape=jax.ShapeDtypeStruct((t, D_MODEL), F32),
        scratch_shapes=[pltpu.VMEM((tm, D_MODEL), BF16)],
        compiler_params=_params("parallel", "arbitrary"),
        name="ffn",
    )(h2, g, wg, wu, wd)


def _rope_table_kernel(pos_ref, freq_ref, cos_ref, sin_ref):
    ang = pos_ref[...] * freq_ref[...]
    cos_ref[...] = jnp.cos(ang)
    sin_ref[...] = jnp.sin(ang)


def _rope_tables(positions):
    t = positions.size
    half = ROPE_DIM // 2
    rows = t * half // LANES
    inv_freq = ROPE_THETA ** (-jnp.arange(0, ROPE_DIM, 2, dtype=F32) / ROPE_DIM)
    pos_rep = jnp.repeat(positions.reshape(-1).astype(F32), half).reshape(rows, LANES)
    freq = jnp.tile(inv_freq, LANES // half).reshape(1, LANES)
    tr = _tile(rows, 256)
    cos, sin = pl.pallas_call(
        _rope_table_kernel,
        grid=(rows // tr,),
        in_specs=[pl.BlockSpec((tr, LANES), lambda i: (i, 0)), _full((1, LANES))],
        out_specs=[pl.BlockSpec((tr, LANES), lambda i: (i, 0))] * 2,
        out_shape=[jax.ShapeDtypeStruct((rows, LANES), F32)] * 2,
        compiler_params=_params("parallel"),
        name="rope_tables",
    )(pos_rep, freq)
    cos = cos.reshape(t, half)
    sin = sin.reshape(t, half)
    pad = HEAD_DIM - ROPE_DIM
    one = jnp.ones((t, pad), F32)
    zero = jnp.zeros((t, pad), F32)
    zh = jnp.zeros((t, half), F32)
    cos_f = jnp.concatenate([cos, cos, one], axis=1)
    sin_a = jnp.concatenate([-sin, zh, zero], axis=1)
    sin_b = jnp.concatenate([zh, sin, zero], axis=1)
    return tuple(jnp.tile(m, (1, LANES // HEAD_DIM)) for m in (cos_f, sin_a, sin_b))


def _qkv_kernel(h_ref, g_ref, w_ref, b_ref, qn_ref, kn_ref, cos_ref, sa_ref, sb_ref, seg_ref,
                q_ref, k_ref, v_ref):
    xn = _rms(h_ref[...], g_ref[...]).astype(BF16)
    qkv = jnp.dot(xn, w_ref[...], preferred_element_type=F32) + b_ref[...]
    cos = cos_ref[...]
    sa = sa_ref[...]
    sb = sb_ref[...]
    seg = seg_ref[...]

    def norm_rope(x, gain):
        ms = _dot_exact_rhs(x * x, seg) * (1.0 / HEAD_DIM)
        xn_ = x * lax.rsqrt(ms + NORM_EPS) * gain
        return (xn_ * cos + pltpu.roll(xn_, LANES - ROPE_DIM // 2, axis=1) * sa
                + pltpu.roll(xn_, ROPE_DIM // 2, axis=1) * sb)

    for c in range(Q_DIM // LANES):
        x = qkv[:, c * LANES:(c + 1) * LANES]
        q_ref[:, c * LANES:(c + 1) * LANES] = (norm_rope(x, qn_ref[...]) * (HEAD_DIM ** -0.5)).astype(BF16)
    for c in range(KV_DIM // LANES):
        x = qkv[:, Q_DIM + c * LANES:Q_DIM + (c + 1) * LANES]
        k_ref[:, c * LANES:(c + 1) * LANES] = norm_rope(x, kn_ref[...]).astype(BF16)
    v_ref[...] = qkv[:, Q_DIM + KV_DIM:].astype(BF16)


def _qkv(h2, g, w, b, qn, kn, cos_f, sin_a, sin_b, seg2):
    t = h2.shape[0]
    tm = _tile(t, 256)
    row = lambda i: (i, 0)
    tab = pl.BlockSpec((tm, LANES), row)
    return pl.pallas_call(
        _qkv_kernel,
        grid=(t // tm,),
        in_specs=[pl.BlockSpec((tm, D_MODEL), row), _full((1, D_MODEL)), _full(w.shape), _full(b.shape),
                  _full(qn.shape), _full(kn.shape), tab, tab, tab, _full(seg2.shape)],
        out_specs=[pl.BlockSpec((tm, Q_DIM), row), pl.BlockSpec((tm, KV_DIM), row),
                   pl.BlockSpec((tm, KV_DIM), row)],
        out_shape=[jax.ShapeDtypeStruct((t, Q_DIM), BF16), jax.ShapeDtypeStruct((t, KV_DIM), BF16),
                   jax.ShapeDtypeStruct((t, KV_DIM), BF16)],
        compiler_params=_params("parallel"),
        name="qkv",
    )(h2, g, w, b, qn, kn, cos_f, sin_a, sin_b, seg2)


def _attn_kernel(h_ref, q_ref, kc_ref, kp_ref, vc_ref, vp_ref, sink_ref, wo_ref, bo_ref, o_ref, *, nqb):
    w = WINDOW
    n = HEAD_DIM
    rows = GQA_GROUP * w
    qi = lax.broadcasted_iota(jnp.int32, (rows, 2 * w), 0) % w
    kj = lax.broadcasted_iota(jnp.int32, (rows, 2 * w), 1)
    diff = w + qi - kj
    band = (diff >= 0) & (diff < w)
    band0 = band & (kj >= jnp.where(pl.program_id(1) == 0, w, 0))
    kall = jnp.concatenate([kp_ref[...], kc_ref[...]], axis=0)
    vall = jnp.concatenate([vp_ref[...], vc_ref[...]], axis=0)
    units = [(qb, g) for qb in range(nqb) for g in range(KV_HEADS)]
    s, e, den = {}, {}, {}
    for u in units:
        qb, g = u
        qs = jnp.concatenate(
            [q_ref[qb * w:(qb + 1) * w, (g * GQA_GROUP + i) * n:(g * GQA_GROUP + i + 1) * n]
             for i in range(GQA_GROUP)], axis=0)
        kk = kall[qb * w:(qb + 2) * w, g * n:(g + 1) * n]
        sc = lax.dot_general(qs, kk, (((1,), (1,)), ((), ())), preferred_element_type=F32)
        s[u] = jnp.where(band0 if qb == 0 else band, sc, -jnp.inf)
    sinks = [jnp.concatenate(
        [jnp.broadcast_to(sink_ref[:, g * GQA_GROUP + i:g * GQA_GROUP + i + 1], (w, 1))
         for i in range(GQA_GROUP)], axis=0) for g in range(KV_HEADS)]
    for u in units:
        sink = sinks[u[1]]
        m = jnp.maximum(jnp.max(s[u], axis=-1, keepdims=True), sink)
        e[u] = jnp.exp(s[u] - m)
        den[u] = jnp.sum(e[u], axis=-1, keepdims=True) + jnp.exp(sink - m)
    outs = [[None] * Q_HEADS for _ in range(nqb)]
    for u in units:
        qb, g = u
        vv = vall[qb * w:(qb + 2) * w, g * n:(g + 1) * n]
        o = jnp.dot(e[u].astype(BF16), vv, preferred_element_type=F32) / den[u]
        for i in range(GQA_GROUP):
            outs[qb][g * GQA_GROUP + i] = o[i * w:(i + 1) * w]
    att = jnp.concatenate([jnp.concatenate(outs[qb], axis=1) for qb in range(nqb)], axis=0)
    o_ref[...] = h_ref[...] + _bdot(att, wo_ref[...]) + bo_ref[...]


def _attn(h2, q, k, v, sinks, wo, bo, bsz, seq):
    t = h2.shape[0]
    nqb = 2 if seq % (2 * WINDOW) == 0 else 1
    tq = nqb * WINDOW
    nb = seq // tq
    cur = lambda b, i: (b * nb + i, 0)
    prv = lambda b, i: (b * nb * nqb + jnp.maximum(i * nqb - 1, 0), 0)
    return pl.pallas_call(
        functools.partial(_attn_kernel, nqb=nqb),
        grid=(bsz, nb),
        in_specs=[pl.BlockSpec((tq, D_MODEL), cur), pl.BlockSpec((tq, Q_DIM), cur),
                  pl.BlockSpec((tq, KV_DIM), cur), pl.BlockSpec((WINDOW, KV_DIM), prv),
                  pl.BlockSpec((tq, KV_DIM), cur), pl.BlockSpec((WINDOW, KV_DIM), prv),
                  _full(sinks.shape), _full(wo.shape), _full(bo.shape)],
        out_specs=pl.BlockSpec((tq, D_MODEL), cur),
        out_shape=jax.ShapeDtypeStruct((t, D_MODEL), F32),
        compiler_params=_params("parallel", "parallel"),
        name="attn",
    )(h2, q, k, k, v, v, sinks, wo, bo)


def _router_kernel(h_ref, g_ref, wr_ref, xn_ref, meta_ref, cnt_ref, base_ref):
    @pl.when(pl.program_id(0) == 0)
    def _():
        base_ref[...] = jnp.zeros_like(base_ref)

    xn = _rms(h_ref[...], g_ref[...])
    xn_ref[...] = xn
    hi, lo = _split(xn)
    whi = wr_ref[0]
    wlo = wr_ref[1]
    logits = (jnp.dot(hi, whi, preferred_element_type=F32) + jnp.dot(hi, wlo, preferred_element_type=F32)
              + jnp.dot(lo, whi, preferred_element_type=F32))
    tm = logits.shape[0]
    lane = lax.broadcasted_iota(jnp.int32, logits.shape, 1)
    logits = jnp.where(lane < N_EXPERTS, logits, -jnp.inf)
    m1 = jnp.max(logits, axis=-1, keepdims=True)
    i1 = jnp.min(jnp.where(logits == m1, lane, LANES), axis=-1, keepdims=True)
    rest = jnp.where(lane == i1, -jnp.inf, logits)
    m2 = jnp.max(rest, axis=-1, keepdims=True)
    i2 = jnp.min(jnp.where(rest == m2, lane, LANES), axis=-1, keepdims=True)
    e2 = jnp.exp(m2 - m1)
    w1 = 1.0 / (1.0 + e2)
    w2 = e2 / (1.0 + e2)

    sel1 = lane == i1
    sel2 = lane == i2
    onehot = jnp.where(sel1, 1.0, 0.0) + jnp.where(sel2, 1.0, 0.0)
    ri = lax.broadcasted_iota(jnp.int32, (tm, tm), 0)
    ci = lax.broadcasted_iota(jnp.int32, (tm, tm), 1)
    before = jnp.dot(jnp.where(ci < ri, 1.0, 0.0).astype(BF16), onehot.astype(BF16),
                     preferred_element_type=F32)
    rank = base_ref[...] + before
    r1 = jnp.sum(jnp.where(sel1, rank, 0.0), axis=-1, keepdims=True)
    r2 = jnp.sum(jnp.where(sel2, rank, 0.0), axis=-1, keepdims=True)
    total = base_ref[...] + jnp.sum(onehot, axis=0, keepdims=True)
    base_ref[...] = total
    cnt_ref[...] = total
    cols = (i1.astype(F32), i2.astype(F32), w1, w2, r1, r2)
    meta = jnp.zeros(logits.shape, F32)
    for c, val in enumerate(cols):
        meta = jnp.where(lane == c, val, meta)
    meta_ref[...] = meta


def _router(h2, g, wr):
    t = h2.shape[0]
    tm = _tile(t, 512)
    row = lambda i: (i, 0)
    return pl.pallas_call(
        _router_kernel,
        grid=(t // tm,),
        in_specs=[pl.BlockSpec((tm, D_MODEL), row), _full((1, D_MODEL)), _full(wr.shape)],
        out_specs=[pl.BlockSpec((tm, D_MODEL), row), pl.BlockSpec((tm, LANES), row), _full((1, LANES))],
        out_shape=[jax.ShapeDtypeStruct((t, D_MODEL), F32), jax.ShapeDtypeStruct((t, LANES), F32),
                   jax.ShapeDtypeStruct((1, LANES), F32)],
        scratch_shapes=[pltpu.VMEM((1, LANES), F32)],
        compiler_params=_params("arbitrary"),
        name="router",
    )(h2, g, wr)


def _row_copy(src_ref, src_row, dst_ref, dst_row, sem):
    return pltpu.make_async_copy(src_ref.at[pl.ds(src_row, 1)], dst_ref.at[pl.ds(dst_row, 1)], sem)


def _dispatch_kernel(e1_ref, e2_ref, r1_ref, r2_ref, off_ref, pad_ref, x_ref, zero_ref, o_ref, sem, *, ntok):
    chunk = MOE_DMA_CHUNK

    def issue(t, c):
        _row_copy(x_ref, t, o_ref, off_ref[e1_ref[t]] + r1_ref[t], sem).start()
        _row_copy(x_ref, t, o_ref, off_ref[e2_ref[t]] + r2_ref[t], sem).start()
        return c

    def drain(n):
        def body(_, c):
            _row_copy(x_ref, 0, o_ref, 0, sem).wait()
            return c
        lax.fori_loop(0, n, body, 0)

    def step(c, carry):
        lax.fori_loop(0, chunk, lambda j, cc: issue(c * chunk + j, cc), 0, unroll=8)

        @pl.when(c > 0)
        def _():
            drain(2 * chunk)
        return carry

    lax.fori_loop(0, ntok // chunk, step, 0)
    drain(2 * chunk)

    for e in range(N_EXPERTS):
        lo = pad_ref[2 * e]
        hi = pad_ref[2 * e + 1]

        def fill(p, c):
            _row_copy(zero_ref, 0, o_ref, p, sem).start()
            return c

        lax.fori_loop(lo, hi, fill, 0)
        drain(hi - lo)


def _dispatch(e1, e2, r1, r2, off, pad, xn, rows):
    t = xn.shape[0]
    zero = jnp.zeros((8, D_MODEL), F32)
    any_spec = pl.BlockSpec(memory_space=pl.ANY)
    return pl.pallas_call(
        functools.partial(_dispatch_kernel, ntok=t),
        grid_spec=pltpu.PrefetchScalarGridSpec(
            num_scalar_prefetch=6, grid=(1,), in_specs=[any_spec, any_spec], out_specs=any_spec,
            scratch_shapes=[pltpu.SemaphoreType.DMA(())]),
        out_shape=jax.ShapeDtypeStruct((rows, D_MODEL), F32),
        compiler_params=_params("arbitrary"),
        name="moe_dispatch",
    )(e1, e2, r1, r2, off, pad, xn, zero)


def _experts_kernel(te_ref, nused_ref, x_ref, wg_ref, wu_ref, wd_ref, o_ref):
    i = pl.program_id(0)

    @pl.when(i < nused_ref[0])
    def _():
        xb = x_ref[...].astype(BF16)
        act = _silu(jnp.dot(xb, wg_ref[0], preferred_element_type=F32)) * jnp.dot(
            xb, wu_ref[0], preferred_element_type=F32)
        o_ref[...] = jnp.dot(act.astype(BF16), wd_ref[0], preferred_element_type=F32)

    @pl.when(i >= nused_ref[0])
    def _():
        o_ref[...] = jnp.zeros_like(o_ref)


def _experts(tile_expert, nused, xs, wg, wu, wd):
    rows = xs.shape[0]
    tm = MOE_TM
    xrow = lambda i, te, nu: (jnp.minimum(i, nu[0] - 1), 0)
    wsel = lambda i, te, nu: (te[i], 0, 0)
    return pl.pallas_call(
        _experts_kernel,
        grid_spec=pltpu.PrefetchScalarGridSpec(
            num_scalar_prefetch=2, grid=(rows // tm,),
            in_specs=[pl.BlockSpec((tm, D_MODEL), xrow),
                      pl.BlockSpec((1, D_MODEL, FFN_EXPERT), wsel),
                      pl.BlockSpec((1, D_MODEL, FFN_EXPERT), wsel),
                      pl.BlockSpec((1, FFN_EXPERT, D_MODEL), wsel)],
            out_specs=pl.BlockSpec((tm, D_MODEL), lambda i, te, nu: (i, 0))),
        out_shape=jax.ShapeDtypeStruct((rows, D_MODEL), F32),
        compiler_params=_params("arbitrary"),
        name="moe_experts",
    )(tile_expert, nused, xs, wg, wu, wd)


def _combine_kernel(e1_ref, e2_ref, r1_ref, r2_ref, off_ref, h_ref, meta_ref, y_ref, o_ref, buf, sem):
    i = pl.program_id(0)
    n = pl.num_programs(0)
    tm = o_ref.shape[0]

    def issue(tile, slot):
        def body(j, c):
            t = tile * tm + j
            _row_copy(y_ref, off_ref[e1_ref[t]] + r1_ref[t], buf.at[slot, 0], j, sem.at[slot]).start()
            _row_copy(y_ref, off_ref[e2_ref[t]] + r2_ref[t], buf.at[slot, 1], j, sem.at[slot]).start()
            return c
        lax.fori_loop(0, tm, body, 0, unroll=8)

    @pl.when(i == 0)
    def _():
        issue(0, 0)

    @pl.when(i + 1 < n)
    def _():
        issue(i + 1, (i + 1) % 2)

    slot = i % 2

    def drain(_, c):
        _row_copy(y_ref, 0, buf.at[slot, 0], 0, sem.at[slot]).wait()
        return c
    lax.fori_loop(0, 2 * tm, drain, 0)

    meta = meta_ref[...]
    o_ref[...] = h_ref[...] + meta[:, 2:3] * buf[slot, 0] + meta[:, 3:4] * buf[slot, 1]


def _combine(e1, e2, r1, r2, off, h2, meta, ys):
    t = h2.shape[0]
    tm = _tile(t, 256)
    row = lambda i, *_: (i, 0)
    return pl.pallas_call(
        _combine_kernel,
        grid_spec=pltpu.PrefetchScalarGridSpec(
            num_scalar_prefetch=5, grid=(t // tm,),
            in_specs=[pl.BlockSpec((tm, D_MODEL), row), pl.BlockSpec((tm, LANES), row),
                      pl.BlockSpec(memory_space=pl.ANY)],
            out_specs=pl.BlockSpec((tm, D_MODEL), row),
            scratch_shapes=[pltpu.VMEM((2, 2, tm, D_MODEL), F32), pltpu.SemaphoreType.DMA((2,))]),
        out_shape=jax.ShapeDtypeStruct((t, D_MODEL), F32),
        compiler_params=_params("arbitrary"),
        name="moe_combine",
    )(e1, e2, r1, r2, off, h2, meta, ys)


def _moe(h2, g, router, wg, wu, wd):
    t = h2.shape[0]
    tm = MOE_TM
    wr = jnp.pad(router, ((0, 0), (0, LANES - N_EXPERTS)))
    wr_hi = wr.astype(BF16)
    wr_lo = (wr - wr_hi.astype(F32)).astype(BF16)
    xn, meta, cnt = _router(h2, g, jnp.stack([wr_hi, wr_lo]))

    e1, e2, r1, r2 = (meta[:, c].astype(jnp.int32) for c in (0, 1, 4, 5))
    count = cnt[0, :N_EXPERTS].astype(jnp.int32)
    ntile = (count + tm - 1) // tm
    tile_end = jnp.cumsum(ntile)
    off = (tile_end - ntile) * tm
    pad = jnp.stack([off + count, off + ntile * tm], axis=1).reshape(-1)
    rows = 2 * t + N_EXPERTS * tm
    tile_expert = jnp.minimum(
        jnp.sum(jnp.arange(rows // tm)[:, None] >= tile_end[None, :], axis=1), N_EXPERTS - 1).astype(jnp.int32)
    nused = tile_end[-1:].astype(jnp.int32)

    xs = _dispatch(e1, e2, r1, r2, off, pad, xn, rows)
    ys = _experts(tile_expert, nused, xs, wg, wu, wd)
    return _combine(e1, e2, r1, r2, off, h2, meta, ys)


def _segment_matrix(width):
    i = jnp.arange(width) // HEAD_DIM
    return (i[:, None] == i[None, :]).astype(BF16)


def _layer0(h2, bsz, seq, norm_mix, w_in, mu_shift, w0, w_decay_up, a0, w_iclr_up, w_gate_up, k_k, k_a, r_k,
            gn_w, gn_b, conv_w, conv_b, dt_bias, a_log, d_skip, ssd_norm, w_out, norm_ffn, ffn_gate, ffn_up,
            ffn_down):
    row = lambda a: a.reshape(1, -1)
    seg = _segment_matrix(RWKV_DIM)
    wa = w_in[:, :RWKV_COLS].astype(BF16)
    wb = jnp.pad(w_in[:, RWKV_COLS:], ((0, 0), (0, SSD_COLS_PAD - SSD_COLS))).astype(BF16)
    pa, pb = _in_proj(h2, row(norm_mix), wa, wb)

    lora = w_decay_up.shape[0]
    wd = jnp.concatenate([w_decay_up, jnp.zeros((LANES - lora, RWKV_DIM), F32)], axis=0).astype(BF16)
    wi = jnp.concatenate([jnp.zeros((LANES - lora, RWKV_DIM), F32), w_iclr_up], axis=0).astype(BF16)
    r, k, v, lw, kkn, b, g, bonus = _rwkv_prep(
        pa, bsz, seq, row(mu_shift), row(w0), wd, row(a0), wi, w_gate_up.astype(BF16), row(k_k), row(k_a),
        row(r_k), seg)
    y = _rwkv_chunk(r, k, v, lw, kkn, b, bsz, seq)

    lane_pad = lambda a: jnp.pad(row(a), ((0, 0), (0, LANES - a.size)))
    ys = _ssd(pb, bsz, seq, conv_w, row(conv_b), lane_pad(dt_bias), lane_pad(a_log), lane_pad(d_skip),
              row(ssd_norm))

    w_out = w_out.astype(BF16)
    h2 = _mix_out(h2, y, bonus, g, ys, row(gn_w), row(gn_b), seg, w_out[:RWKV_DIM], w_out[RWKV_DIM:])
    return _ffn(h2, row(norm_ffn), ffn_gate.astype(BF16), ffn_up.astype(BF16), ffn_down.astype(BF16))


def _layer1(h2, bsz, seq, tables, norm_mix, w_qkv, b_qkv, q_norm, k_norm, sinks, w_o, b_o, norm_ffn, router,
            exp_gate, exp_up, exp_down):
    row = lambda a: a.reshape(1, -1)
    two = lambda a: jnp.tile(a, LANES // HEAD_DIM).reshape(1, LANES)
    cos_f, sin_a, sin_b = tables
    q, k, v = _qkv(h2, row(norm_mix), w_qkv.astype(BF16), row(b_qkv), two(q_norm), two(k_norm), cos_f, sin_a,
                   sin_b, _segment_matrix(LANES))
    h2 = _attn(h2, q, k, v, row(sinks), w_o.astype(BF16), row(b_o), bsz, seq)

    return _moe(h2, row(norm_ffn), router, exp_gate.astype(BF16), exp_up.astype(BF16), exp_down.astype(BF16))


def kernel(x, positions, ev_norm_mix, ev_w_in, ev_mu_shift, ev_w0, ev_w_decay_up, ev_a0, ev_w_iclr_up, ev_w_gate_up, ev_k_k, ev_k_a, ev_r_k, ev_gn_w, ev_gn_b, ev_conv_w, ev_conv_b, ev_dt_bias, ev_a_log, ev_d_skip, ev_ssd_norm, ev_w_out, ev_norm_ffn, ev_ffn_gate, ev_ffn_up, ev_ffn_down, od_norm_mix, od_w_qkv, od_b_qkv, od_q_norm, od_k_norm, od_sinks, od_w_o, od_b_o, od_norm_ffn, od_router, od_exp_gate, od_exp_up, od_exp_down):
    bsz, seq, d = x.shape
    depth = ev_norm_mix.shape[0] + od_norm_mix.shape[0]
    tables = _rope_tables(positions)
    h2 = x.reshape(bsz * seq, d)
    for layer in range(depth):
        i = layer // 2
        if layer % 2 == 0:
            h2 = _layer0(h2, bsz, seq, ev_norm_mix[i], ev_w_in[i], ev_mu_shift[i], ev_w0[i], ev_w_decay_up[i],
                         ev_a0[i], ev_w_iclr_up[i], ev_w_gate_up[i], ev_k_k[i], ev_k_a[i], ev_r_k[i],
                         ev_gn_w[i], ev_gn_b[i], ev_conv_w[i], ev_conv_b[i], ev_dt_bias[i], ev_a_log[i],
                         ev_d_skip[i], ev_ssd_norm[i], ev_w_out[i], ev_norm_ffn[i], ev_ffn_gate[i],
                         ev_ffn_up[i], ev_ffn_down[i])
        else:
            h2 = _layer1(h2, bsz, seq, tables, od_norm_mix[i], od_w_qkv[i], od_b_qkv[i], od_q_norm[i],
                         od_k_norm[i], od_sinks[i], od_w_o[i], od_b_o[i], od_norm_ffn[i], od_router[i],
                         od_exp_gate[i], od_exp_up[i], od_exp_down[i])
    return h2.reshape(bsz, seq, d)
```

```python
import functools
import math

import jax
import jax.numpy as jnp
from jax import lax
from jax.experimental import pallas as pl
from jax.experimental.pallas import tpu as pltpu

F32 = jnp.float32
BF16 = jnp.bfloat16

D_MODEL = 1024
HEAD_DIM = 64
NORM_EPS = 1e-6

RWKV_HEADS = 8
RWKV_DIM = 512
RWKV_COLS = 1792
RWKV_GN_EPS = 64e-5
RWKV_CHUNK = 64
RWKV_GROUP = 256

SSD_HEADS = 8
SSD_DIM = 512
SSD_GROUPS = 2
SSD_STATE = 128
SSD_CONV = 4
SSD_CHUNK = 128
SSD_XBC = 1024
SSD_COLS = 1544
SSD_COLS_PAD = 1664

Q_HEADS = 16
KV_HEADS = 4
GQA_GROUP = 4
Q_DIM = 1024
KV_DIM = 256
WINDOW = 128
ROPE_THETA = 500000.0
ROPE_DIM = 16

FFN_DENSE = 2816
N_EXPERTS = 8
FFN_EXPERT = 1408
MOE_TM = 256

LANES = 128
VMEM_LIMIT_BYTES = 56 * 1024 * 1024


def _params(*sem):
    return pltpu.CompilerParams(dimension_semantics=sem, vmem_limit_bytes=VMEM_LIMIT_BYTES)


def _bdot(a, b):
    return jnp.dot(a.astype(BF16), b.astype(BF16), preferred_element_type=F32)


def _bdot_nt(a, b):
    return lax.dot_general(a.astype(BF16), b.astype(BF16), (((1,), (1,)), ((), ())),
                           preferred_element_type=F32)


def _bdot_tn(a, b):
    return lax.dot_general(a.astype(BF16), b.astype(BF16), (((0,), (0,)), ((), ())),
                           preferred_element_type=F32)


def _split(x):
    hi = x.astype(BF16)
    lo = (x - hi.astype(F32)).astype(BF16)
    return hi, lo


def _dot_exact_rhs(x, m):
    hi, lo = _split(x)
    return (jnp.dot(hi, m, preferred_element_type=F32) + jnp.dot(lo, m, preferred_element_type=F32))


def _dot_exact_lhs(m, x):
    hi, lo = _split(x)
    return (jnp.dot(m, hi, preferred_element_type=F32) + jnp.dot(m, lo, preferred_element_type=F32))


def _sigmoid(x):
    return 1.0 / (1.0 + jnp.exp(-x))


def _silu(x):
    return x * _sigmoid(x)


def _softplus(x):
    return jnp.maximum(x, 0.0) + jnp.log(1.0 + jnp.exp(-jnp.abs(x)))


def _rms(x, g):
    ms = jnp.mean(x * x, axis=-1, keepdims=True)
    return x * lax.rsqrt(ms + NORM_EPS) * g


def _tile(n, pref):
    t = min(n, pref)
    while n % t:
        t //= 2
    return t


def _full(shape):
    nd = len(shape)
    return pl.BlockSpec(shape, lambda *_: (0,) * nd)


def _in_proj_kernel(x_ref, g_ref, wa_ref, wb_ref, pa_ref, pb_ref):
    xn = _rms(x_ref[...], g_ref[...]).astype(BF16)
    pa_ref[...] = jnp.dot(xn, wa_ref[...], preferred_element_type=F32)
    pb_ref[...] = jnp.dot(xn, wb_ref[...], preferred_element_type=F32)


def _in_proj(x2, g, wa, wb):
    t = x2.shape[0]
    tm = _tile(t, 256)
    return pl.pallas_call(
        _in_proj_kernel,
        grid=(t // tm,),
        in_specs=[pl.BlockSpec((tm, D_MODEL), lambda i: (i, 0)), _full((1, D_MODEL)),
                  _full(wa.shape), _full(wb.shape)],
        out_specs=[pl.BlockSpec((tm, RWKV_COLS), lambda i: (i, 0)),
                   pl.BlockSpec((tm, SSD_COLS_PAD), lambda i: (i, 0))],
        out_shape=[jax.ShapeDtypeStruct((t, RWKV_COLS), F32),
                   jax.ShapeDtypeStruct((t, SSD_COLS_PAD), F32)],
        compiler_params=_params("parallel"),
        name="in_proj",
    )(x2, g, wa, wb)


def _rwkv_prep_kernel(pa_ref, mu_ref, w0_ref, wd_ref, a0_ref, wi_ref, wg_ref, kk_ref, ka_ref, rk_ref,
                      seg_ref, r_ref, k_ref, v_ref, lw_ref, kkn_ref, b_ref, g_ref, bonus_ref, carry_ref):
    @pl.when(pl.program_id(1) == 0)
    def _():
        carry_ref[...] = jnp.zeros_like(carry_ref)

    pa = pa_ref[...]
    tm = pa.shape[0]
    row = lax.broadcasted_iota(jnp.int32, pa.shape, 0)
    prev = jnp.where(row == 0, carry_ref[...], pltpu.roll(pa, 1, axis=0))
    carry_ref[...] = pa[tm - 1:tm, :]
    x = pa + (prev - pa) * mu_ref[...]

    r = x[:, 0:RWKV_DIM]
    k = x[:, RWKV_DIM:2 * RWKV_DIM]
    v = x[:, 2 * RWKV_DIM:3 * RWKV_DIM]
    lora = x[:, 3 * RWKV_DIM:3 * RWKV_DIM + LANES]
    gl = x[:, 3 * RWKV_DIM + LANES:]
    seg = seg_ref[...]

    w_raw = w0_ref[...] + _bdot(jnp.tanh(lora), wd_ref[...])
    lw_ref[...] = (-math.exp(-0.5)) * _sigmoid(w_raw)
    iclr = _sigmoid(a0_ref[...] + _bdot(lora, wi_ref[...]))
    g_ref[...] = _bdot(_sigmoid(gl), wg_ref[...])

    kk = k * kk_ref[...]
    kkn = kk * lax.rsqrt(_dot_exact_rhs(kk * kk, seg) + 1e-12)
    k2 = k * (1.0 + (iclr - 1.0) * ka_ref[...])
    r_ref[...] = r
    k_ref[...] = k2
    v_ref[...] = v
    kkn_ref[...] = kkn
    b_ref[...] = kkn * iclr
    bonus_ref[...] = _dot_exact_rhs(r * k2 * rk_ref[...], seg) * v


def _rwkv_prep(pa, bsz, seq, mu, w0, wd, a0, wi, wg, k_k, k_a, r_k, seg):
    t = pa.shape[0]
    tm = _tile(seq, 256)
    nt = seq // tm
    row = lambda b, i: (b * nt + i, 0)
    out = jax.ShapeDtypeStruct((t, RWKV_DIM), F32)
    return pl.pallas_call(
        _rwkv_prep_kernel,
        grid=(bsz, nt),
        in_specs=[pl.BlockSpec((tm, RWKV_COLS), row), _full(mu.shape), _full(w0.shape), _full(wd.shape),
                  _full(a0.shape), _full(wi.shape), _full(wg.shape), _full(k_k.shape), _full(k_a.shape),
                  _full(r_k.shape), _full(seg.shape)],
        out_specs=[pl.BlockSpec((tm, RWKV_DIM), row)] * 8,
        out_shape=[out] * 8,
        scratch_shapes=[pltpu.VMEM((1, RWKV_COLS), F32)],
        compiler_params=_params("arbitrary", "arbitrary"),
        name="rwkv_prep",
    )(pa, mu, w0, wd, a0, wi, wg, k_k, k_a, r_k, seg)


def _rwkv_chunk_kernel(r_ref, k_ref, v_ref, lw_ref, kk_ref, b_ref, y_ref, z_ref, *, lt):
    c_len = RWKV_CHUNK
    n = HEAD_DIM
    gl = RWKV_GROUP
    nchunk = gl // c_len
    nheads = LANES // n
    shift = c_len.bit_length() - 1

    @pl.when(pl.program_id(2) == 0)
    def _():
        z_ref[...] = jnp.zeros_like(z_ref)

    ri = lax.broadcasted_iota(jnp.int32, (gl, gl), 0)
    ci = lax.broadcasted_iota(jnp.int32, (gl, gl), 1)
    tri_bd = jnp.where((ci <= ri) & (ci >= ((ri >> shift) << shift)), 1.0, 0.0).astype(BF16)
    re_ = lax.broadcasted_iota(jnp.int32, (c_len, c_len), 0)
    ce_ = lax.broadcasted_iota(jnp.int32, (c_len, c_len), 1)
    eye = re_ == ce_
    ri2 = lax.broadcasted_iota(jnp.int32, (2 * gl, gl), 0)
    ci2 = lax.broadcasted_iota(jnp.int32, (2 * gl, gl), 1)
    t2 = jnp.where(ri2 < gl, ri2, ri2 - gl)
    mask2 = (ci2 <= jnp.where(ri2 < gl, t2 - 1, t2)) & (ci2 >= ((t2 >> shift) << shift))
    zeros = jnp.zeros((c_len, n), F32)

    nsub = lt // gl
    sysid = [(s, h) for s in range(nsub) for h in range(nheads)]
    pre = []
    for s in range(nsub):
        sl = pl.ds(s * gl, gl)
        lw = lw_ref[sl, :]
        g_in = _dot_exact_lhs(tri_bd, lw)
        ends = [g_in[(c + 1) * c_len - 1:(c + 1) * c_len, :] for c in range(nchunk)]
        g_end = jnp.concatenate([jnp.broadcast_to(e, (c_len, LANES)) for e in ends], axis=0)
        e_end = jnp.exp(g_end - g_in)
        en = jnp.exp(-g_in)
        k = k_ref[sl, :]
        b = b_ref[sl, :]
        pre.append(dict(
            ends=ends, v=v_ref[sl, :], rt=r_ref[sl, :] * jnp.exp(g_in),
            at=-kk_ref[sl, :] * jnp.exp(g_in - lw), kt=k * en, bt=b * en, bend=b * e_end, kend=k * e_end))

    def hs(name, s, h):
        return pre[s][name][:, h * n:(h + 1) * n]

    xb, xk, xkv, p, x = {}, {}, {}, {}, {}
    for q in sysid:
        la = jnp.concatenate([hs("at", *q), hs("rt", *q)], axis=0)
        xb[q] = jnp.where(mask2, _bdot_nt(la, hs("bt", *q)), 0.0)
        xk[q] = jnp.where(mask2, _bdot_nt(la, hs("kt", *q)), 0.0)
    for q in sysid:
        xkv[q] = _bdot(xk[q], hs("v", *q))
        p[q] = xb[q][0:gl]
        x[q] = jnp.concatenate([hs("at", *q), xkv[q][0:gl]], axis=1)
    for i in range(6):
        for q in sysid:
            x[q] = x[q] + _bdot(p[q], x[q])
        if i < 5:
            for q in sysid:
                p[q] = _bdot(p[q], p[q])
    rq, y0, mn = {}, {}, {}
    for q in sysid:
        yy = _bdot(xb[q][gl:], x[q])
        rq[q] = yy[:, 0:n] + hs("rt", *q)
        y0[q] = yy[:, n:] + xkv[q][gl:]
        bend_h, kend_h, v_h = hs("bend", *q), hs("kend", *q), hs("v", *q)
        for c in range(nchunk):
            rows = slice(c * c_len, (c + 1) * c_len)
            lhs = jnp.concatenate([bend_h[rows], kend_h[rows]], axis=0)
            rhs = jnp.concatenate([x[q][rows], jnp.concatenate([zeros, v_h[rows]], axis=1)], axis=0)
            mn[q + (c,)] = _bdot_tn(lhs, rhs)
    zs = [z_ref[h] for h in range(nheads)]
    for s in range(nsub):
        yh = [[] for _ in range(nheads)]
        for c in range(nchunk):
            rows = slice(c * c_len, (c + 1) * c_len)
            for h in range(nheads):
                q = (s, h)
                pend = jnp.exp(pre[s]["ends"][c][:, h * n:(h + 1) * n])
                m_mat = mn[q + (c,)][:, 0:n] + jnp.where(eye, pend, 0.0)
                yh[h].append(_bdot(rq[q][rows], zs[h]) + y0[q][rows])
                zs[h] = _bdot(m_mat, zs[h]) + mn[q + (c,)][:, n:]
        y_ref[pl.ds(s * gl, gl), :] = jnp.concatenate(
            [jnp.concatenate(yh[h], axis=0) for h in range(nheads)], axis=1)
    for h in range(nheads):
        z_ref[h] = zs[h]


def _rwkv_chunk(r, k, v, lw, kkn, b, bsz, seq):
    t = r.shape[0]
    lt = _tile(seq, 512)
    nt = seq // lt
    spec = pl.BlockSpec((lt, LANES), lambda bi, hp, i: (bi * nt + i, hp))
    return pl.pallas_call(
        functools.partial(_rwkv_chunk_kernel, lt=lt),
        grid=(bsz, RWKV_DIM // LANES, nt),
        in_specs=[spec] * 6,
        out_specs=spec,
        out_shape=jax.ShapeDtypeStruct((t, RWKV_DIM), F32),
        scratch_shapes=[pltpu.VMEM((2, HEAD_DIM, HEAD_DIM), F32)],
        compiler_params=_params("arbitrary", "arbitrary", "arbitrary"),
        name="rwkv_chunk",
    )(r, k, v, lw, kkn, b)


def _ssd_kernel(pb_ref, cw_ref, cb_ref, dtb_ref, alog_ref, dskip_ref, nrm_ref, ys_ref, ext_ref, st_ref):
    q = SSD_CHUNK
    p = HEAD_DIM
    hpg = SSD_HEADS // SSD_GROUPS

    @pl.when(pl.program_id(1) == 0)
    def _():
        ext_ref[0:8, :] = jnp.zeros((8, SSD_XBC), F32)
        st_ref[...] = jnp.zeros_like(st_ref)

    z = pb_ref[:, 0:SSD_DIM]
    u = pb_ref[:, SSD_DIM:SSD_DIM + SSD_XBC]
    dt_raw = pb_ref[:, SSD_DIM + SSD_XBC:]

    ext_ref[8:8 + q, :] = u
    conv = cb_ref[...] + cw_ref[SSD_CONV - 1:SSD_CONV, :] * u
    for j in range(SSD_CONV - 1):
        off = 8 - (SSD_CONV - 1) + j
        conv = conv + cw_ref[j:j + 1, :] * ext_ref[off:off + q, :]
    ext_ref[0:8, :] = u[q - 8:q, :]
    xbc = _silu(conv)
    xs = xbc[:, 0:SSD_DIM]
    bm = xbc[:, SSD_DIM:SSD_DIM + SSD_GROUPS * SSD_STATE]
    cm = xbc[:, SSD_DIM + SSD_GROUPS * SSD_STATE:]

    dt = _softplus(dt_raw + dtb_ref[...])
    a = -jnp.exp(alog_ref[...])
    ri = lax.broadcasted_iota(jnp.int32, (q, q), 0)
    ci = lax.broadcasted_iota(jnp.int32, (q, q), 1)
    causal = ri >= ci
    cum = _dot_exact_lhs(causal.astype(BF16), dt * a)
    cum_t = cum.T
    dt_t = dt.T
    cum_end = cum[q - 1:q, :]
    to_end = jnp.exp(cum_end - cum) * dt
    ecum = jnp.exp(cum)
    edec = jnp.exp(cum_end)

    ys = []
    for g in range(SSD_GROUPS):
        bm_g = bm[:, g * SSD_STATE:(g + 1) * SSD_STATE]
        cm_g = cm[:, g * SSD_STATE:(g + 1) * SSD_STATE]
        cb = _bdot_nt(cm_g, bm_g)
        bm_t = bm_g.T
        for hh in range(hpg):
            h = g * hpg + hh
            x_h = xs[:, h * p:(h + 1) * p]
            seg = cum[:, h:h + 1] - cum_t[h:h + 1, :]
            ldec = jnp.exp(jnp.where(causal, seg, -jnp.inf))
            wts = cb * ldec * dt_t[h:h + 1, :]
            y = _bdot(wts, x_h)
            h_prev = st_ref[h]
            y = y + _bdot(cm_g, h_prev) * ecum[:, h:h + 1]
            st_ref[h] = h_prev * edec[:, h:h + 1] + _bdot(bm_t, x_h * to_end[:, h:h + 1])
            ys.append(y + dskip_ref[:, h:h + 1] * x_h)
    yall = jnp.concatenate(ys, axis=1) * _silu(z)
    gw = SSD_DIM // SSD_GROUPS
    outs = []
    for g in range(SSD_GROUPS):
        yg = yall[:, g * gw:(g + 1) * gw]
        ms = jnp.mean(yg * yg, axis=-1, keepdims=True)
        outs.append(yg * lax.rsqrt(ms + NORM_EPS) * nrm_ref[:, g * gw:(g + 1) * gw])
    ys_ref[...] = jnp.concatenate(outs, axis=1)


def _ssd(pb, bsz, seq, conv_w, conv_b, dt_bias, a_log, d_skip, ssd_norm):
    t = pb.shape[0]
    nc = seq // SSD_CHUNK
    row = lambda b, i: (b * nc + i, 0)
    return pl.pallas_call(
        _ssd_kernel,
        grid=(bsz, nc),
        in_specs=[pl.BlockSpec((SSD_CHUNK, SSD_COLS_PAD), row), _full(conv_w.shape), _full(conv_b.shape),
                  _full(dt_bias.shape), _full(a_log.shape), _full(d_skip.shape), _full(ssd_norm.shape)],
        out_specs=pl.BlockSpec((SSD_CHUNK, SSD_DIM), row),
        out_shape=jax.ShapeDtypeStruct((t, SSD_DIM), F32),
        scratch_shapes=[pltpu.VMEM((8 + SSD_CHUNK, SSD_XBC), F32),
                        pltpu.VMEM((SSD_HEADS, SSD_STATE, HEAD_DIM), F32)],
        compiler_params=_params("arbitrary", "arbitrary"),
        name="ssd",
    )(pb, conv_w, conv_b, dt_bias, a_log, d_skip, ssd_norm)


def _mix_out_kernel(h_ref, y_ref, bonus_ref, g_ref, ys_ref, gnw_ref, gnb_ref, seg_ref, wa_ref, wb_ref, o_ref):
    y = y_ref[...]
    seg = seg_ref[...]
    mu = _dot_exact_rhs(y, seg) * (1.0 / HEAD_DIM)
    yc = y - mu
    var = _dot_exact_rhs(yc * yc, seg) * (1.0 / HEAD_DIM)
    yn = yc * lax.rsqrt(var + RWKV_GN_EPS) * gnw_ref[...] + gnb_ref[...]
    ya = (yn + bonus_ref[...]) * g_ref[...]
    o_ref[...] = h_ref[...] + _bdot(ya, wa_ref[...]) + _bdot(ys_ref[...], wb_ref[...])


def _mix_out(h2, y, bonus, g, ys, gn_w, gn_b, seg, wa, wb):
    t = h2.shape[0]
    tm = _tile(t, 512)
    row = lambda i: (i, 0)
    half = pl.BlockSpec((tm, RWKV_DIM), row)
    return pl.pallas_call(
        _mix_out_kernel,
        grid=(t // tm,),
        in_specs=[pl.BlockSpec((tm, D_MODEL), row), half, half, half, half, _full(gn_w.shape),
                  _full(gn_b.shape), _full(seg.shape), _full(wa.shape), _full(wb.shape)],
        out_specs=pl.BlockSpec((tm, D_MODEL), row),
        out_shape=jax.ShapeDtypeStruct((t, D_MODEL), F32),
        compiler_params=_params("parallel"),
        name="mix_out",
    )(h2, y, bonus, g, ys, gn_w, gn_b, seg, wa, wb)


def _ffn_kernel(h_ref, g_ref, wg_ref, wu_ref, wd_ref, o_ref):
    h = h_ref[...]
    xn = _rms(h, g_ref[...]).astype(BF16)
    act = _silu(jnp.dot(xn, wg_ref[...], preferred_element_type=F32)) * jnp.dot(
        xn, wu_ref[...], preferred_element_type=F32)
    o_ref[...] = h + jnp.dot(act.astype(BF16), wd_ref[...], preferred_element_type=F32)


def _resident(shape):
    nd = len(shape)
    return pl.BlockSpec(shape, lambda *_: (0,) * nd, pipeline_mode=pl.Buffered(1))


def _ffn(h2, g, wg, wu, wd):
    t = h2.shape[0]
    tm = _tile(t, 512)
    row = lambda i: (i, 0)
    return pl.pallas_call(
        _ffn_kernel,
        grid=(t // tm,),
        in_specs=[pl.BlockSpec((tm, D_MODEL), row), _full((1, D_MODEL)), _resident(wg.shape),
                  _resident(wu.shape), _resident(wd.shape)],
        out_specs=pl.BlockSpec((tm, D_MODEL), row),
        out_shape=jax.ShapeDtypeStruct((t, D_MODEL), F32),
        compiler_params=_params("parallel"),
        name="ffn",
    )(h2, g, wg, wu, wd)


def _rope_table_kernel(pos_ref, freq_ref, cos_ref, sin_ref):
    ang = pos_ref[...] * freq_ref[...]
    cos_ref[...] = jnp.cos(ang)
    sin_ref[...] = jnp.sin(ang)


def _rope_tables(positions):
    t = positions.size
    half = ROPE_DIM // 2
    rows = t * half // LANES
    inv_freq = ROPE_THETA ** (-jnp.arange(0, ROPE_DIM, 2, dtype=F32) / ROPE_DIM)
    pos_rep = jnp.repeat(positions.reshape(-1).astype(F32), half).reshape(rows, LANES)
    freq = jnp.tile(inv_freq, LANES // half).reshape(1, LANES)
    tr = _tile(rows, 256)
    cos, sin = pl.pallas_call(
        _rope_table_kernel,
        grid=(rows // tr,),
        in_specs=[pl.BlockSpec((tr, LANES), lambda i: (i, 0)), _full((1, LANES))],
        out_specs=[pl.BlockSpec((tr, LANES), lambda i: (i, 0))] * 2,
        out_shape=[jax.ShapeDtypeStruct((rows, LANES), F32)] * 2,
        compiler_params=_params("parallel"),
        name="rope_tables",
    )(pos_rep, freq)
    cos = cos.reshape(t, half)
    sin = sin.reshape(t, half)
    pad = HEAD_DIM - ROPE_DIM
    one = jnp.ones((t, pad), F32)
    zero = jnp.zeros((t, pad), F32)
    zh = jnp.zeros((t, half), F32)
    cos_f = jnp.concatenate([cos, cos, one], axis=1)
    sin_a = jnp.concatenate([-sin, zh, zero], axis=1)
    sin_b = jnp.concatenate([zh, sin, zero], axis=1)
    return tuple(jnp.tile(m, (1, LANES // HEAD_DIM)) for m in (cos_f, sin_a, sin_b))


def _qkv_kernel(h_ref, g_ref, w_ref, b_ref, qn_ref, kn_ref, cos_ref, sa_ref, sb_ref, seg_ref,
                q_ref, k_ref, v_ref):
    xn = _rms(h_ref[...], g_ref[...]).astype(BF16)
    qkv = jnp.dot(xn, w_ref[...], preferred_element_type=F32) + b_ref[...]
    cos = cos_ref[...]
    sa = sa_ref[...]
    sb = sb_ref[...]
    seg = seg_ref[...]

    def norm_rope(x, gain):
        ms = _dot_exact_rhs(x * x, seg) * (1.0 / HEAD_DIM)
        xn_ = x * lax.rsqrt(ms + NORM_EPS) * gain
        return (xn_ * cos + pltpu.roll(xn_, LANES - ROPE_DIM // 2, axis=1) * sa
                + pltpu.roll(xn_, ROPE_DIM // 2, axis=1) * sb)

    for c in range(Q_DIM // LANES):
        x = qkv[:, c * LANES:(c + 1) * LANES]
        q_ref[:, c * LANES:(c + 1) * LANES] = (norm_rope(x, qn_ref[...]) * (HEAD_DIM ** -0.5)).astype(BF16)
    for c in range(KV_DIM // LANES):
        x = qkv[:, Q_DIM + c * LANES:Q_DIM + (c + 1) * LANES]
        k_ref[:, c * LANES:(c + 1) * LANES] = norm_rope(x, kn_ref[...]).astype(BF16)
    v_ref[...] = qkv[:, Q_DIM + KV_DIM:].astype(BF16)


def _qkv(h2, g, w, b, qn, kn, cos_f, sin_a, sin_b, seg2):
    t = h2.shape[0]
    tm = _tile(t, 256)
    row = lambda i: (i, 0)
    tab = pl.BlockSpec((tm, LANES), row)
    return pl.pallas_call(
        _qkv_kernel,
        grid=(t // tm,),
        in_specs=[pl.BlockSpec((tm, D_MODEL), row), _full((1, D_MODEL)), _full(w.shape), _full(b.shape),
                  _full(qn.shape), _full(kn.shape), tab, tab, tab, _full(seg2.shape)],
        out_specs=[pl.BlockSpec((tm, Q_DIM), row), pl.BlockSpec((tm, KV_DIM), row),
                   pl.BlockSpec((tm, KV_DIM), row)],
        out_shape=[jax.ShapeDtypeStruct((t, Q_DIM), BF16), jax.ShapeDtypeStruct((t, KV_DIM), BF16),
                   jax.ShapeDtypeStruct((t, KV_DIM), BF16)],
        compiler_params=_params("parallel"),
        name="qkv",
    )(h2, g, w, b, qn, kn, cos_f, sin_a, sin_b, seg2)


def _attn_kernel(h_ref, q_ref, kc_ref, kp_ref, vc_ref, vp_ref, sink_ref, wo_ref, bo_ref, o_ref, *, nqb):
    w = WINDOW
    n = HEAD_DIM
    rows = GQA_GROUP * w
    qi = lax.broadcasted_iota(jnp.int32, (rows, 2 * w), 0) % w
    kj = lax.broadcasted_iota(jnp.int32, (rows, 2 * w), 1)
    diff = w + qi - kj
    band = (diff >= 0) & (diff < w)
    band0 = band & (kj >= jnp.where(pl.program_id(1) == 0, w, 0))
    kall = jnp.concatenate([kp_ref[...], kc_ref[...]], axis=0)
    vall = jnp.concatenate([vp_ref[...], vc_ref[...]], axis=0)
    units = [(qb, g) for qb in range(nqb) for g in range(KV_HEADS)]
    s, e, den = {}, {}, {}
    for u in units:
        qb, g = u
        qs = jnp.concatenate(
            [q_ref[qb * w:(qb + 1) * w, (g * GQA_GROUP + i) * n:(g * GQA_GROUP + i + 1) * n]
             for i in range(GQA_GROUP)], axis=0)
        kk = kall[qb * w:(qb + 2) * w, g * n:(g + 1) * n]
        sc = lax.dot_general(qs, kk, (((1,), (1,)), ((), ())), preferred_element_type=F32)
        s[u] = jnp.where(band0 if qb == 0 else band, sc, -jnp.inf)
    sinks = [jnp.concatenate(
        [jnp.broadcast_to(sink_ref[:, g * GQA_GROUP + i:g * GQA_GROUP + i + 1], (w, 1))
         for i in range(GQA_GROUP)], axis=0) for g in range(KV_HEADS)]
    for u in units:
        sink = sinks[u[1]]
        m = jnp.maximum(jnp.max(s[u], axis=-1, keepdims=True), sink)
        e[u] = jnp.exp(s[u] - m)
        den[u] = jnp.sum(e[u], axis=-1, keepdims=True) + jnp.exp(sink - m)
    outs = [[None] * Q_HEADS for _ in range(nqb)]
    for u in units:
        qb, g = u
        vv = vall[qb * w:(qb + 2) * w, g * n:(g + 1) * n]
        o = jnp.dot(e[u].astype(BF16), vv, preferred_element_type=F32) / den[u]
        for i in range(GQA_GROUP):
            outs[qb][g * GQA_GROUP + i] = o[i * w:(i + 1) * w]
    att = jnp.concatenate([jnp.concatenate(outs[qb], axis=1) for qb in range(nqb)], axis=0)
    o_ref[...] = h_ref[...] + _bdot(att, wo_ref[...]) + bo_ref[...]


def _attn(h2, q, k, v, sinks, wo, bo, bsz, seq):
    t = h2.shape[0]
    nqb = 2 if seq % (2 * WINDOW) == 0 else 1
    tq = nqb * WINDOW
    nb = seq // tq
    cur = lambda b, i: (b * nb + i, 0)
    prv = lambda b, i: (b * nb * nqb + jnp.maximum(i * nqb - 1, 0), 0)
    return pl.pallas_call(
        functools.partial(_attn_kernel, nqb=nqb),
        grid=(bsz, nb),
        in_specs=[pl.BlockSpec((tq, D_MODEL), cur), pl.BlockSpec((tq, Q_DIM), cur),
                  pl.BlockSpec((tq, KV_DIM), cur), pl.BlockSpec((WINDOW, KV_DIM), prv),
                  pl.BlockSpec((tq, KV_DIM), cur), pl.BlockSpec((WINDOW, KV_DIM), prv),
                  _full(sinks.shape), _full(wo.shape), _full(bo.shape)],
        out_specs=pl.BlockSpec((tq, D_MODEL), cur),
        out_shape=jax.ShapeDtypeStruct((t, D_MODEL), F32),
        compiler_params=_params("parallel", "parallel"),
        name="attn",
    )(h2, q, k, k, v, v, sinks, wo, bo)


def _router_kernel(h_ref, g_ref, wr_ref, xn_ref, meta_ref, cnt_ref, base_ref):
    @pl.when(pl.program_id(0) == 0)
    def _():
        base_ref[...] = jnp.zeros_like(base_ref)

    xn = _rms(h_ref[...], g_ref[...])
    xn_ref[...] = xn
    hi, lo = _split(xn)
    whi = wr_ref[0]
    wlo = wr_ref[1]
    logits = (jnp.dot(hi, whi, preferred_element_type=F32) + jnp.dot(hi, wlo, preferred_element_type=F32)
              + jnp.dot(lo, whi, preferred_element_type=F32))
    tm = logits.shape[0]
    lane = lax.broadcasted_iota(jnp.int32, logits.shape, 1)
    logits = jnp.where(lane < N_EXPERTS, logits, -jnp.inf)
    m1 = jnp.max(logits, axis=-1, keepdims=True)
    i1 = jnp.min(jnp.where(logits == m1, lane, LANES), axis=-1, keepdims=True)
    rest = jnp.where(lane == i1, -jnp.inf, logits)
    m2 = jnp.max(rest, axis=-1, keepdims=True)
    i2 = jnp.min(jnp.where(rest == m2, lane, LANES), axis=-1, keepdims=True)
    e2 = jnp.exp(m2 - m1)
    w1 = 1.0 / (1.0 + e2)
    w2 = e2 / (1.0 + e2)

    sel1 = lane == i1
    sel2 = lane == i2
    onehot = jnp.where(sel1, 1.0, 0.0) + jnp.where(sel2, 1.0, 0.0)
    ri = lax.broadcasted_iota(jnp.int32, (tm, tm), 0)
    ci = lax.broadcasted_iota(jnp.int32, (tm, tm), 1)
    before = jnp.dot(jnp.where(ci < ri, 1.0, 0.0).astype(BF16), onehot.astype(BF16),
                     preferred_element_type=F32)
    rank = base_ref[...] + before
    r1 = jnp.sum(jnp.where(sel1, rank, 0.0), axis=-1, keepdims=True)
    r2 = jnp.sum(jnp.where(sel2, rank, 0.0), axis=-1, keepdims=True)
    total = base_ref[...] + jnp.sum(onehot, axis=0, keepdims=True)
    base_ref[...] = total
    cnt_ref[...] = total
    cols = (i1.astype(F32), i2.astype(F32), w1, w2, r1, r2)
    meta = jnp.zeros(logits.shape, F32)
    for c, val in enumerate(cols):
        meta = jnp.where(lane == c, val, meta)
    meta_ref[...] = meta


def _router(h2, g, wr):
    t = h2.shape[0]
    tm = _tile(t, 512)
    row = lambda i: (i, 0)
    return pl.pallas_call(
        _router_kernel,
        grid=(t // tm,),
        in_specs=[pl.BlockSpec((tm, D_MODEL), row), _full((1, D_MODEL)), _full(wr.shape)],
        out_specs=[pl.BlockSpec((tm, D_MODEL), row), pl.BlockSpec((tm, LANES), row), _full((1, LANES))],
        out_shape=[jax.ShapeDtypeStruct((t, D_MODEL), F32), jax.ShapeDtypeStruct((t, LANES), F32),
                   jax.ShapeDtypeStruct((1, LANES), F32)],
        scratch_shapes=[pltpu.VMEM((1, LANES), F32)],
        compiler_params=_params("arbitrary"),
        name="router",
    )(h2, g, wr)


def _moe_index_kernel(e1_ref, e2_ref, r1_ref, r2_ref, off_ref, dst0_ref, src_ref, dst_ref, *, ntok):
    def init(p, c):
        src_ref[p] = 0
        dst_ref[p] = dst0_ref[p]
        return c

    lax.fori_loop(0, src_ref.shape[0], init, 0, unroll=8)

    def body(t, c):
        p1 = off_ref[e1_ref[t]] + r1_ref[t]
        p2 = off_ref[e2_ref[t]] + r2_ref[t]
        src_ref[p1] = t
        dst_ref[p1] = t
        src_ref[p2] = t
        dst_ref[p2] = ntok + t
        return c

    lax.fori_loop(0, ntok, body, 0, unroll=8)


def _moe_index(e1, e2, r1, r2, off, dst0):
    smem = pl.BlockSpec(memory_space=pltpu.SMEM)
    return pl.pallas_call(
        functools.partial(_moe_index_kernel, ntok=e1.shape[0]),
        grid_spec=pltpu.PrefetchScalarGridSpec(
            num_scalar_prefetch=6, grid=(1,), in_specs=[], out_specs=[smem, smem]),
        out_shape=[jax.ShapeDtypeStruct(dst0.shape, jnp.int32)] * 2,
        compiler_params=_params("arbitrary"),
        name="moe_index",
    )(e1, e2, r1, r2, off, dst0)


def _row_copy(src_ref, src_row, dst_ref, dst_row, sem):
    return pltpu.make_async_copy(src_ref.at[pl.ds(src_row, 1)], dst_ref.at[pl.ds(dst_row, 1)], sem)


def _experts_kernel(te_ref, nused_ref, src_ref, dst_ref, x_ref, wg_ref, wu_ref, wd_ref, y_ref,
                    xbuf, ybuf, gsem, ssem):
    del te_ref
    i = pl.program_id(0)
    nused = nused_ref[0]
    tm = MOE_TM

    def gather(tile, slot):
        def body(j, c):
            _row_copy(x_ref, src_ref[tile * tm + j], xbuf.at[slot], j, gsem.at[slot]).start()
            return c
        lax.fori_loop(0, tm, body, 0, unroll=8)

    def scatter(tile, slot):
        def body(j, c):
            _row_copy(ybuf.at[slot], j, y_ref, dst_ref[tile * tm + j], ssem.at[slot]).start()
            return c
        lax.fori_loop(0, tm, body, 0, unroll=8)

    def wait_gather(slot):
        pltpu.make_async_copy(x_ref.at[pl.ds(0, tm)], xbuf.at[slot], gsem.at[slot]).wait()

    def wait_scatter(slot):
        pltpu.make_async_copy(ybuf.at[slot], y_ref.at[pl.ds(0, tm)], ssem.at[slot]).wait()

    @pl.when(i == 0)
    def _():
        gather(0, 0)

    @pl.when(i + 1 < nused)
    def _():
        gather(i + 1, (i + 1) % 2)

    @pl.when(i < nused)
    def _():
        slot = i % 2
        wait_gather(slot)

        @pl.when(i >= 2)
        def _():
            wait_scatter(slot)

        xb = xbuf[slot].astype(BF16)
        act = _silu(jnp.dot(xb, wg_ref[0], preferred_element_type=F32)) * jnp.dot(
            xb, wu_ref[0], preferred_element_type=F32)
        ybuf[slot] = jnp.dot(act.astype(BF16), wd_ref[0], preferred_element_type=F32)
        scatter(i, slot)

        @pl.when(i == nused - 1)
        def _():
            @pl.when(i >= 1)
            def _():
                wait_scatter(1 - slot)
            wait_scatter(slot)


def _experts(tile_expert, nused, src, dst, xn, wg, wu, wd, out_rows):
    ntile = tile_expert.shape[0]
    tm = MOE_TM
    wsel = lambda i, te, nu, s, d: (te[i], 0, 0)
    any_spec = pl.BlockSpec(memory_space=pl.ANY)
    return pl.pallas_call(
        _experts_kernel,
        grid_spec=pltpu.PrefetchScalarGridSpec(
            num_scalar_prefetch=4, grid=(ntile,),
            in_specs=[any_spec,
                      pl.BlockSpec((1, D_MODEL, FFN_EXPERT), wsel),
                      pl.BlockSpec((1, D_MODEL, FFN_EXPERT), wsel),
                      pl.BlockSpec((1, FFN_EXPERT, D_MODEL), wsel)],
            out_specs=any_spec,
            scratch_shapes=[pltpu.VMEM((2, tm, D_MODEL), F32), pltpu.VMEM((2, tm, D_MODEL), F32),
                            pltpu.SemaphoreType.DMA((2,)), pltpu.SemaphoreType.DMA((2,))]),
        out_shape=jax.ShapeDtypeStruct((out_rows, D_MODEL), F32),
        compiler_params=_params("arbitrary"),
        name="moe_experts",
    )(tile_expert, nused, src, dst, xn, wg, wu, wd)


def _combine_kernel(h_ref, meta_ref, y1_ref, y2_ref, o_ref):
    meta = meta_ref[...]
    o_ref[...] = h_ref[...] + meta[:, 2:3] * y1_ref[...] + meta[:, 3:4] * y2_ref[...]


def _combine(h2, meta, ys):
    t = h2.shape[0]
    tm = _tile(t, 512)
    nt = t // tm
    row = lambda i: (i, 0)
    return pl.pallas_call(
        _combine_kernel,
        grid=(nt,),
        in_specs=[pl.BlockSpec((tm, D_MODEL), row), pl.BlockSpec((tm, LANES), row),
                  pl.BlockSpec((tm, D_MODEL), row), pl.BlockSpec((tm, D_MODEL), lambda i: (nt + i, 0))],
        out_specs=pl.BlockSpec((tm, D_MODEL), row),
        out_shape=jax.ShapeDtypeStruct((t, D_MODEL), F32),
        compiler_params=_params("parallel"),
        name="moe_combine",
    )(h2, meta, ys, ys)


def _moe(h2, g, router, wg, wu, wd):
    t = h2.shape[0]
    tm = MOE_TM
    wr = jnp.pad(router, ((0, 0), (0, LANES - N_EXPERTS)))
    wr_hi = wr.astype(BF16)
    wr_lo = (wr - wr_hi.astype(F32)).astype(BF16)
    xn, meta, cnt = _router(h2, g, jnp.stack([wr_hi, wr_lo]))

    e1, e2, r1, r2 = (meta[:, c].astype(jnp.int32) for c in (0, 1, 4, 5))
    count = cnt[0, :N_EXPERTS].astype(jnp.int32)
    ntile = (count + tm - 1) // tm
    tile_end = jnp.cumsum(ntile)
    off = (tile_end - ntile) * tm
    rows = 2 * t + N_EXPERTS * tm
    tile_id = jnp.arange(rows // tm)
    tile_expert = jnp.minimum(
        jnp.sum(tile_id[:, None] >= tile_end[None, :], axis=1), N_EXPERTS - 1).astype(jnp.int32)
    nused = tile_end[-1:].astype(jnp.int32)
    row_id = jnp.arange(rows, dtype=jnp.int32)
    dst0 = 2 * t + jnp.repeat(tile_expert, tm) * tm + row_id % tm
    src, dst = _moe_index(e1, e2, r1, r2, off, dst0)

    ys = _experts(tile_expert, nused, src, dst, xn, wg, wu, wd, rows)
    return _combine(h2, meta, ys)


def _segment_matrix(width):
    i = jnp.arange(width) // HEAD_DIM
    return (i[:, None] == i[None, :]).astype(BF16)


def _layer0(h2, bsz, seq, norm_mix, w_in, mu_shift, w0, w_decay_up, a0, w_iclr_up, w_gate_up, k_k, k_a, r_k,
            gn_w, gn_b, conv_w, conv_b, dt_bias, a_log, d_skip, ssd_norm, w_out, norm_ffn, ffn_gate, ffn_up,
            ffn_down):
    row = lambda a: a.reshape(1, -1)
    seg = _segment_matrix(RWKV_DIM)
    wa = w_in[:, :RWKV_COLS].astype(BF16)
    wb = jnp.pad(w_in[:, RWKV_COLS:], ((0, 0), (0, SSD_COLS_PAD - SSD_COLS))).astype(BF16)
    pa, pb = _in_proj(h2, row(norm_mix), wa, wb)

    lora = w_decay_up.shape[0]
    wd = jnp.concatenate([w_decay_up, jnp.zeros((LANES - lora, RWKV_DIM), F32)], axis=0).astype(BF16)
    wi = jnp.concatenate([jnp.zeros((LANES - lora, RWKV_DIM), F32), w_iclr_up], axis=0).astype(BF16)
    r, k, v, lw, kkn, b, g, bonus = _rwkv_prep(
        pa, bsz, seq, row(mu_shift), row(w0), wd, row(a0), wi, w_gate_up.astype(BF16), row(k_k), row(k_a),
        row(r_k), seg)
    y = _rwkv_chunk(r, k, v, lw, kkn, b, bsz, seq)

    lane_pad = lambda a: jnp.pad(row(a), ((0, 0), (0, LANES - a.size)))
    ys = _ssd(pb, bsz, seq, conv_w, row(conv_b), lane_pad(dt_bias), lane_pad(a_log), lane_pad(d_skip),
              row(ssd_norm))

    w_out = w_out.astype(BF16)
    h2 = _mix_out(h2, y, bonus, g, ys, row(gn_w), row(gn_b), seg, w_out[:RWKV_DIM], w_out[RWKV_DIM:])
    return _ffn(h2, row(norm_ffn), ffn_gate.astype(BF16), ffn_up.astype(BF16), ffn_down.astype(BF16))


def _layer1(h2, bsz, seq, tables, norm_mix, w_qkv, b_qkv, q_norm, k_norm, sinks, w_o, b_o, norm_ffn, router,
            exp_gate, exp_up, exp_down):
    row = lambda a: a.reshape(1, -1)
    two = lambda a: jnp.tile(a, LANES // HEAD_DIM).reshape(1, LANES)
    cos_f, sin_a, sin_b = tables
    q, k, v = _qkv(h2, row(norm_mix), w_qkv.astype(BF16), row(b_qkv), two(q_norm), two(k_norm), cos_f, sin_a,
                   sin_b, _segment_matrix(LANES))
    h2 = _attn(h2, q, k, v, row(sinks), w_o.astype(BF16), row(b_o), bsz, seq)

    return _moe(h2, row(norm_ffn), router, exp_gate.astype(BF16), exp_up.astype(BF16), exp_down.astype(BF16))


def kernel(x, positions, ev_norm_mix, ev_w_in, ev_mu_shift, ev_w0, ev_w_decay_up, ev_a0, ev_w_iclr_up, ev_w_gate_up, ev_k_k, ev_k_a, ev_r_k, ev_gn_w, ev_gn_b, ev_conv_w, ev_conv_b, ev_dt_bias, ev_a_log, ev_d_skip, ev_ssd_norm, ev_w_out, ev_norm_ffn, ev_ffn_gate, ev_ffn_up, ev_ffn_down, od_norm_mix, od_w_qkv, od_b_qkv, od_q_norm, od_k_norm, od_sinks, od_w_o, od_b_o, od_norm_ffn, od_router, od_exp_gate, od_exp_up, od_exp_down):
    bsz, seq, d = x.shape
    depth = ev_norm_mix.shape[0] + od_norm_mix.shape[0]
    tables = _rope_tables(positions)
    h2 = x.reshape(bsz * seq, d)
    for layer in range(depth):
        i = layer // 2
        if layer % 2 == 0:
            h2 = _layer0(h2, bsz, seq, ev_norm_mix[i], ev_w_in[i], ev_mu_shift[i], ev_w0[i], ev_w_decay_up[i],
                         ev_a0[i], ev_w_iclr_up[i], ev_w_gate_up[i], ev_k_k[i], ev_k_a[i], ev_r_k[i],
                         ev_gn_w[i], ev_gn_b[i], ev_conv_w[i], ev_conv_b[i], ev_dt_bias[i], ev_a_log[i],
                         ev_d_skip[i], ev_ssd_norm[i], ev_w_out[i], ev_norm_ffn[i], ev_ffn_gate[i],
                         ev_ffn_up[i], ev_ffn_down[i])
        else:
            h2 = _layer1(h2, bsz, seq, tables, od_norm_mix[i], od_w_qkv[i], od_b_qkv[i], od_q_norm[i],
                         od_k_norm[i], od_sinks[i], od_w_o[i], od_b_o[i], od_norm_ffn[i], od_router[i],
                         od_exp_gate[i], od_exp_up[i], od_exp_down[i])
    return h2.reshape(bsz, seq, d)
```

```python
import functools
import math

import jax
import jax.numpy as jnp
from jax import lax
from jax.experimental import pallas as pl
from jax.experimental.pallas import tpu as pltpu

F32 = jnp.float32
BF16 = jnp.bfloat16

D_MODEL = 1024
HEAD_DIM = 64
NORM_EPS = 1e-6

RWKV_HEADS = 8
RWKV_DIM = 512
RWKV_COLS = 1792
RWKV_GN_EPS = 64e-5
RWKV_CHUNK = 64
RWKV_GROUP = 256

SSD_HEADS = 8
SSD_DIM = 512
SSD_GROUPS = 2
SSD_STATE = 128
SSD_CONV = 4
SSD_CHUNK = 128
SSD_XBC = 1024
SSD_COLS = 1544
SSD_COLS_PAD = 1664

Q_HEADS = 16
KV_HEADS = 4
GQA_GROUP = 4
Q_DIM = 1024
KV_DIM = 256
WINDOW = 128
ROPE_THETA = 500000.0
ROPE_DIM = 16

FFN_DENSE = 2816
N_EXPERTS = 8
FFN_EXPERT = 1408
MOE_TM = 256

LANES = 128
VMEM_LIMIT_BYTES = 56 * 1024 * 1024


def _params(*sem):
    return pltpu.CompilerParams(dimension_semantics=sem, vmem_limit_bytes=VMEM_LIMIT_BYTES)


def _bdot(a, b):
    return jnp.dot(a.astype(BF16), b.astype(BF16), preferred_element_type=F32)


def _bdot_nt(a, b):
    return lax.dot_general(a.astype(BF16), b.astype(BF16), (((1,), (1,)), ((), ())),
                           preferred_element_type=F32)


def _bdot_tn(a, b):
    return lax.dot_general(a.astype(BF16), b.astype(BF16), (((0,), (0,)), ((), ())),
                           preferred_element_type=F32)


def _split(x):
    hi = x.astype(BF16)
    lo = (x - hi.astype(F32)).astype(BF16)
    return hi, lo


def _dot_exact_rhs(x, m):
    hi, lo = _split(x)
    return (jnp.dot(hi, m, preferred_element_type=F32) + jnp.dot(lo, m, preferred_element_type=F32))


def _dot_exact_lhs(m, x):
    hi, lo = _split(x)
    return (jnp.dot(m, hi, preferred_element_type=F32) + jnp.dot(m, lo, preferred_element_type=F32))


def _sigmoid(x):
    return 1.0 / (1.0 + jnp.exp(-x))


def _silu(x):
    return x * _sigmoid(x)


def _softplus(x):
    return jnp.maximum(x, 0.0) + jnp.log(1.0 + jnp.exp(-jnp.abs(x)))


def _rms(x, g):
    ms = jnp.mean(x * x, axis=-1, keepdims=True)
    return x * lax.rsqrt(ms + NORM_EPS) * g


def _tile(n, pref):
    t = min(n, pref)
    while n % t:
        t //= 2
    return t


def _full(shape):
    nd = len(shape)
    return pl.BlockSpec(shape, lambda *_: (0,) * nd)


def _in_proj_kernel(x_ref, g_ref, wa_ref, wb_ref, pa_ref, pb_ref):
    xn = _rms(x_ref[...], g_ref[...]).astype(BF16)
    pa_ref[...] = jnp.dot(xn, wa_ref[...], preferred_element_type=F32)
    pb_ref[...] = jnp.dot(xn, wb_ref[...], preferred_element_type=F32)


def _in_proj(x2, g, wa, wb):
    t = x2.shape[0]
    tm = _tile(t, 256)
    return pl.pallas_call(
        _in_proj_kernel,
        grid=(t // tm,),
        in_specs=[pl.BlockSpec((tm, D_MODEL), lambda i: (i, 0)), _full((1, D_MODEL)),
                  _full(wa.shape), _full(wb.shape)],
        out_specs=[pl.BlockSpec((tm, RWKV_COLS), lambda i: (i, 0)),
                   pl.BlockSpec((tm, SSD_COLS_PAD), lambda i: (i, 0))],
        out_shape=[jax.ShapeDtypeStruct((t, RWKV_COLS), F32),
                   jax.ShapeDtypeStruct((t, SSD_COLS_PAD), F32)],
        compiler_params=_params("parallel"),
        name="in_proj",
    )(x2, g, wa, wb)


def _rwkv_prep_kernel(pa_ref, mu_ref, w0_ref, wd_ref, a0_ref, wi_ref, wg_ref, kk_ref, ka_ref, rk_ref,
                      seg_ref, r_ref, k_ref, v_ref, lw_ref, kkn_ref, b_ref, g_ref, bonus_ref, carry_ref):
    @pl.when(pl.program_id(1) == 0)
    def _():
        carry_ref[...] = jnp.zeros_like(carry_ref)

    pa = pa_ref[...]
    tm = pa.shape[0]
    row = lax.broadcasted_iota(jnp.int32, pa.shape, 0)
    prev = jnp.where(row == 0, carry_ref[...], pltpu.roll(pa, 1, axis=0))
    carry_ref[...] = pa[tm - 1:tm, :]
    x = pa + (prev - pa) * mu_ref[...]

    r = x[:, 0:RWKV_DIM]
    k = x[:, RWKV_DIM:2 * RWKV_DIM]
    v = x[:, 2 * RWKV_DIM:3 * RWKV_DIM]
    lora = x[:, 3 * RWKV_DIM:3 * RWKV_DIM + LANES]
    gl = x[:, 3 * RWKV_DIM + LANES:]
    seg = seg_ref[...]

    w_raw = w0_ref[...] + _bdot(jnp.tanh(lora), wd_ref[...])
    lw_ref[...] = (-math.exp(-0.5)) * _sigmoid(w_raw)
    iclr = _sigmoid(a0_ref[...] + _bdot(lora, wi_ref[...]))
    g_ref[...] = _bdot(_sigmoid(gl), wg_ref[...])

    kk = k * kk_ref[...]
    kkn = kk * lax.rsqrt(_dot_exact_rhs(kk * kk, seg) + 1e-12)
    k2 = k * (1.0 + (iclr - 1.0) * ka_ref[...])
    r_ref[...] = r
    k_ref[...] = k2
    v_ref[...] = v
    kkn_ref[...] = kkn
    b_ref[...] = kkn * iclr
    bonus_ref[...] = _dot_exact_rhs(r * k2 * rk_ref[...], seg) * v


def _rwkv_prep(pa, bsz, seq, mu, w0, wd, a0, wi, wg, k_k, k_a, r_k, seg):
    t = pa.shape[0]
    tm = _tile(seq, 256)
    nt = seq // tm
    row = lambda b, i: (b * nt + i, 0)
    out = jax.ShapeDtypeStruct((t, RWKV_DIM), F32)
    return pl.pallas_call(
        _rwkv_prep_kernel,
        grid=(bsz, nt),
        in_specs=[pl.BlockSpec((tm, RWKV_COLS), row), _full(mu.shape), _full(w0.shape), _full(wd.shape),
                  _full(a0.shape), _full(wi.shape), _full(wg.shape), _full(k_k.shape), _full(k_a.shape),
                  _full(r_k.shape), _full(seg.shape)],
        out_specs=[pl.BlockSpec((tm, RWKV_DIM), row)] * 8,
        out_shape=[out] * 8,
        scratch_shapes=[pltpu.VMEM((1, RWKV_COLS), F32)],
        compiler_params=_params("arbitrary", "arbitrary"),
        name="rwkv_prep",
    )(pa, mu, w0, wd, a0, wi, wg, k_k, k_a, r_k, seg)


def _rwkv_chunk_kernel(r_ref, k_ref, v_ref, lw_ref, kk_ref, b_ref, y_ref, z_ref, *, lt):
    c_len = RWKV_CHUNK
    n = HEAD_DIM
    gl = RWKV_GROUP
    nchunk = gl // c_len
    nheads = LANES // n
    shift = c_len.bit_length() - 1

    @pl.when(pl.program_id(2) == 0)
    def _():
        z_ref[...] = jnp.zeros_like(z_ref)

    ri = lax.broadcasted_iota(jnp.int32, (gl, gl), 0)
    ci = lax.broadcasted_iota(jnp.int32, (gl, gl), 1)
    tri_bd = jnp.where((ci <= ri) & (ci >= ((ri >> shift) << shift)), 1.0, 0.0).astype(BF16)
    re_ = lax.broadcasted_iota(jnp.int32, (c_len, c_len), 0)
    ce_ = lax.broadcasted_iota(jnp.int32, (c_len, c_len), 1)
    eye = re_ == ce_
    ri2 = lax.broadcasted_iota(jnp.int32, (2 * gl, gl), 0)
    ci2 = lax.broadcasted_iota(jnp.int32, (2 * gl, gl), 1)
    t2 = jnp.where(ri2 < gl, ri2, ri2 - gl)
    mask2 = (ci2 <= jnp.where(ri2 < gl, t2 - 1, t2)) & (ci2 >= ((t2 >> shift) << shift))
    zeros = jnp.zeros((c_len, n), F32)

    nsub = lt // gl
    sysid = [(s, h) for s in range(nsub) for h in range(nheads)]
    pre = []
    for s in range(nsub):
        sl = pl.ds(s * gl, gl)
        lw = lw_ref[sl, :]
        g_in = _dot_exact_lhs(tri_bd, lw)
        ends = [g_in[(c + 1) * c_len - 1:(c + 1) * c_len, :] for c in range(nchunk)]
        g_end = jnp.concatenate([jnp.broadcast_to(e, (c_len, LANES)) for e in ends], axis=0)
        e_end = jnp.exp(g_end - g_in)
        en = jnp.exp(-g_in)
        k = k_ref[sl, :]
        b = b_ref[sl, :]
        pre.append(dict(
            ends=ends, v=v_ref[sl, :], rt=r_ref[sl, :] * jnp.exp(g_in),
            at=-kk_ref[sl, :] * jnp.exp(g_in - lw), kt=k * en, bt=b * en, bend=b * e_end, kend=k * e_end))

    def hs(name, s, h):
        return pre[s][name][:, h * n:(h + 1) * n]

    xb, xk, xkv, p, x = {}, {}, {}, {}, {}
    for q in sysid:
        la = jnp.concatenate([hs("at", *q), hs("rt", *q)], axis=0)
        xb[q] = jnp.where(mask2, _bdot_nt(la, hs("bt", *q)), 0.0)
        xk[q] = jnp.where(mask2, _bdot_nt(la, hs("kt", *q)), 0.0)
    for q in sysid:
        xkv[q] = _bdot(xk[q], hs("v", *q))
        p[q] = xb[q][0:gl]
        x[q] = jnp.concatenate([hs("at", *q), xkv[q][0:gl]], axis=1)
    for i in range(6):
        for q in sysid:
            x[q] = x[q] + _bdot(p[q], x[q])
        if i < 5:
            for q in sysid:
                p[q] = _bdot(p[q], p[q])
    rq, y0, mn = {}, {}, {}
    for q in sysid:
        yy = _bdot(xb[q][gl:], x[q])
        rq[q] = yy[:, 0:n] + hs("rt", *q)
        y0[q] = yy[:, n:] + xkv[q][gl:]
        bend_h, kend_h, v_h = hs("bend", *q), hs("kend", *q), hs("v", *q)
        for c in range(nchunk):
            rows = slice(c * c_len, (c + 1) * c_len)
            lhs = jnp.concatenate([bend_h[rows], kend_h[rows]], axis=0)
            rhs = jnp.concatenate([x[q][rows], jnp.concatenate([zeros, v_h[rows]], axis=1)], axis=0)
            mn[q + (c,)] = _bdot_tn(lhs, rhs)
    zs = [z_ref[h] for h in range(nheads)]
    for s in range(nsub):
        yh = [[] for _ in range(nheads)]
        for c in range(nchunk):
            rows = slice(c * c_len, (c + 1) * c_len)
            for h in range(nheads):
                q = (s, h)
                pend = jnp.exp(pre[s]["ends"][c][:, h * n:(h + 1) * n])
                m_mat = mn[q + (c,)][:, 0:n] + jnp.where(eye, pend, 0.0)
                yh[h].append(_bdot(rq[q][rows], zs[h]) + y0[q][rows])
                zs[h] = _bdot(m_mat, zs[h]) + mn[q + (c,)][:, n:]
        y_ref[pl.ds(s * gl, gl), :] = jnp.concatenate(
            [jnp.concatenate(yh[h], axis=0) for h in range(nheads)], axis=1)
    for h in range(nheads):
        z_ref[h] = zs[h]


def _rwkv_chunk(r, k, v, lw, kkn, b, bsz, seq):
    t = r.shape[0]
    lt = _tile(seq, 512)
    nt = seq // lt
    spec = pl.BlockSpec((lt, LANES), lambda bi, hp, i: (bi * nt + i, hp))
    return pl.pallas_call(
        functools.partial(_rwkv_chunk_kernel, lt=lt),
        grid=(bsz, RWKV_DIM // LANES, nt),
        in_specs=[spec] * 6,
        out_specs=spec,
        out_shape=jax.ShapeDtypeStruct((t, RWKV_DIM), F32),
        scratch_shapes=[pltpu.VMEM((2, HEAD_DIM, HEAD_DIM), F32)],
        compiler_params=_params("arbitrary", "arbitrary", "arbitrary"),
        name="rwkv_chunk",
    )(r, k, v, lw, kkn, b)


def _ssd_kernel(pb_ref, cw_ref, cb_ref, dtb_ref, alog_ref, dskip_ref, nrm_ref, ys_ref, ext_ref, st_ref):
    q = SSD_CHUNK
    p = HEAD_DIM
    hpg = SSD_HEADS // SSD_GROUPS

    @pl.when(pl.program_id(1) == 0)
    def _():
        ext_ref[0:8, :] = jnp.zeros((8, SSD_XBC), F32)
        st_ref[...] = jnp.zeros_like(st_ref)

    z = pb_ref[:, 0:SSD_DIM]
    u = pb_ref[:, SSD_DIM:SSD_DIM + SSD_XBC]
    dt_raw = pb_ref[:, SSD_DIM + SSD_XBC:]

    ext_ref[8:8 + q, :] = u
    conv = cb_ref[...] + cw_ref[SSD_CONV - 1:SSD_CONV, :] * u
    for j in range(SSD_CONV - 1):
        off = 8 - (SSD_CONV - 1) + j
        conv = conv + cw_ref[j:j + 1, :] * ext_ref[off:off + q, :]
    ext_ref[0:8, :] = u[q - 8:q, :]
    xbc = _silu(conv)
    xs = xbc[:, 0:SSD_DIM]
    bm = xbc[:, SSD_DIM:SSD_DIM + SSD_GROUPS * SSD_STATE]
    cm = xbc[:, SSD_DIM + SSD_GROUPS * SSD_STATE:]

    dt = _softplus(dt_raw + dtb_ref[...])
    a = -jnp.exp(alog_ref[...])
    ri = lax.broadcasted_iota(jnp.int32, (q, q), 0)
    ci = lax.broadcasted_iota(jnp.int32, (q, q), 1)
    causal = ri >= ci
    cum = _dot_exact_lhs(causal.astype(BF16), dt * a)
    cum_t = cum.T
    dt_t = dt.T
    cum_end = cum[q - 1:q, :]
    to_end = jnp.exp(cum_end - cum) * dt
    ecum = jnp.exp(cum)
    edec = jnp.exp(cum_end)

    ys = []
    for g in range(SSD_GROUPS):
        bm_g = bm[:, g * SSD_STATE:(g + 1) * SSD_STATE]
        cm_g = cm[:, g * SSD_STATE:(g + 1) * SSD_STATE]
        cb = _bdot_nt(cm_g, bm_g)
        bm_t = bm_g.T
        for hh in range(hpg):
            h = g * hpg + hh
            x_h = xs[:, h * p:(h + 1) * p]
            seg = cum[:, h:h + 1] - cum_t[h:h + 1, :]
            ldec = jnp.exp(jnp.where(causal, seg, -jnp.inf))
            wts = cb * ldec * dt_t[h:h + 1, :]
            y = _bdot(wts, x_h)
            h_prev = st_ref[h]
            y = y + _bdot(cm_g, h_prev) * ecum[:, h:h + 1]
            st_ref[h] = h_prev * edec[:, h:h + 1] + _bdot(bm_t, x_h * to_end[:, h:h + 1])
            ys.append(y + dskip_ref[:, h:h + 1] * x_h)
    yall = jnp.concatenate(ys, axis=1) * _silu(z)
    gw = SSD_DIM // SSD_GROUPS
    outs = []
    for g in range(SSD_GROUPS):
        yg = yall[:, g * gw:(g + 1) * gw]
        ms = jnp.mean(yg * yg, axis=-1, keepdims=True)
        outs.append(yg * lax.rsqrt(ms + NORM_EPS) * nrm_ref[:, g * gw:(g + 1) * gw])
    ys_ref[...] = jnp.concatenate(outs, axis=1)


def _ssd(pb, bsz, seq, conv_w, conv_b, dt_bias, a_log, d_skip, ssd_norm):
    t = pb.shape[0]
    nc = seq // SSD_CHUNK
    row = lambda b, i: (b * nc + i, 0)
    return pl.pallas_call(
        _ssd_kernel,
        grid=(bsz, nc),
        in_specs=[pl.BlockSpec((SSD_CHUNK, SSD_COLS_PAD), row), _full(conv_w.shape), _full(conv_b.shape),
                  _full(dt_bias.shape), _full(a_log.shape), _full(d_skip.shape), _full(ssd_norm.shape)],
        out_specs=pl.BlockSpec((SSD_CHUNK, SSD_DIM), row),
        out_shape=jax.ShapeDtypeStruct((t, SSD_DIM), F32),
        scratch_shapes=[pltpu.VMEM((8 + SSD_CHUNK, SSD_XBC), F32),
                        pltpu.VMEM((SSD_HEADS, SSD_STATE, HEAD_DIM), F32)],
        compiler_params=_params("arbitrary", "arbitrary"),
        name="ssd",
    )(pb, conv_w, conv_b, dt_bias, a_log, d_skip, ssd_norm)


def _mix_out_kernel(h_ref, y_ref, bonus_ref, g_ref, ys_ref, gnw_ref, gnb_ref, seg_ref, wa_ref, wb_ref, o_ref):
    y = y_ref[...]
    seg = seg_ref[...]
    mu = _dot_exact_rhs(y, seg) * (1.0 / HEAD_DIM)
    yc = y - mu
    var = _dot_exact_rhs(yc * yc, seg) * (1.0 / HEAD_DIM)
    yn = yc * lax.rsqrt(var + RWKV_GN_EPS) * gnw_ref[...] + gnb_ref[...]
    ya = (yn + bonus_ref[...]) * g_ref[...]
    o_ref[...] = h_ref[...] + _bdot(ya, wa_ref[...]) + _bdot(ys_ref[...], wb_ref[...])


def _mix_out(h2, y, bonus, g, ys, gn_w, gn_b, seg, wa, wb):
    t = h2.shape[0]
    tm = _tile(t, 512)
    row = lambda i: (i, 0)
    half = pl.BlockSpec((tm, RWKV_DIM), row)
    return pl.pallas_call(
        _mix_out_kernel,
        grid=(t // tm,),
        in_specs=[pl.BlockSpec((tm, D_MODEL), row), half, half, half, half, _full(gn_w.shape),
                  _full(gn_b.shape), _full(seg.shape), _full(wa.shape), _full(wb.shape)],
        out_specs=pl.BlockSpec((tm, D_MODEL), row),
        out_shape=jax.ShapeDtypeStruct((t, D_MODEL), F32),
        compiler_params=_params("parallel"),
        name="mix_out",
    )(h2, y, bonus, g, ys, gn_w, gn_b, seg, wa, wb)


def _ffn_kernel(h_ref, g_ref, wg_ref, wu_ref, wd_ref, o_ref):
    h = h_ref[...]
    xn = _rms(h, g_ref[...]).astype(BF16)
    act = _silu(jnp.dot(xn, wg_ref[...], preferred_element_type=F32)) * jnp.dot(
        xn, wu_ref[...], preferred_element_type=F32)
    o_ref[...] = h + jnp.dot(act.astype(BF16), wd_ref[...], preferred_element_type=F32)


def _resident(shape):
    nd = len(shape)
    return pl.BlockSpec(shape, lambda *_: (0,) * nd, pipeline_mode=pl.Buffered(1))


def _ffn(h2, g, wg, wu, wd):
    t = h2.shape[0]
    tm = _tile(t, 512)
    row = lambda i: (i, 0)
    return pl.pallas_call(
        _ffn_kernel,
        grid=(t // tm,),
        in_specs=[pl.BlockSpec((tm, D_MODEL), row), _full((1, D_MODEL)), _resident(wg.shape),
                  _resident(wu.shape), _resident(wd.shape)],
        out_specs=pl.BlockSpec((tm, D_MODEL), row),
        out_shape=jax.ShapeDtypeStruct((t, D_MODEL), F32),
        compiler_params=_params("parallel"),
        name="ffn",
    )(h2, g, wg, wu, wd)


def _rope_table_kernel(pos_ref, freq_ref, cos_ref, sin_ref):
    ang = pos_ref[...] * freq_ref[...]
    cos_ref[...] = jnp.cos(ang)
    sin_ref[...] = jnp.sin(ang)


def _rope_tables(positions):
    t = positions.size
    half = ROPE_DIM // 2
    rows = t * half // LANES
    inv_freq = ROPE_THETA ** (-jnp.arange(0, ROPE_DIM, 2, dtype=F32) / ROPE_DIM)
    pos_rep = jnp.repeat(positions.reshape(-1).astype(F32), half).reshape(rows, LANES)
    freq = jnp.tile(inv_freq, LANES // half).reshape(1, LANES)
    tr = _tile(rows, 256)
    cos, sin = pl.pallas_call(
        _rope_table_kernel,
        grid=(rows // tr,),
        in_specs=[pl.BlockSpec((tr, LANES), lambda i: (i, 0)), _full((1, LANES))],
        out_specs=[pl.BlockSpec((tr, LANES), lambda i: (i, 0))] * 2,
        out_shape=[jax.ShapeDtypeStruct((rows, LANES), F32)] * 2,
        compiler_params=_params("parallel"),
        name="rope_tables",
    )(pos_rep, freq)
    cos = cos.reshape(t, half)
    sin = sin.reshape(t, half)
    pad = HEAD_DIM - ROPE_DIM
    one = jnp.ones((t, pad), F32)
    zero = jnp.zeros((t, pad), F32)
    zh = jnp.zeros((t, half), F32)
    cos_f = jnp.concatenate([cos, cos, one], axis=1)
    sin_a = jnp.concatenate([-sin, zh, zero], axis=1)
    sin_b = jnp.concatenate([zh, sin, zero], axis=1)
    return tuple(jnp.tile(m, (1, LANES // HEAD_DIM)) for m in (cos_f, sin_a, sin_b))


def _qkv_kernel(h_ref, g_ref, w_ref, b_ref, qn_ref, kn_ref, cos_ref, sa_ref, sb_ref, seg_ref,
                q_ref, k_ref, v_ref):
    xn = _rms(h_ref[...], g_ref[...]).astype(BF16)
    qkv = jnp.dot(xn, w_ref[...], preferred_element_type=F32) + b_ref[...]
    cos = cos_ref[...]
    sa = sa_ref[...]
    sb = sb_ref[...]
    seg = seg_ref[...]

    def norm_rope(x, gain):
        ms = _dot_exact_rhs(x * x, seg) * (1.0 / HEAD_DIM)
        xn_ = x * lax.rsqrt(ms + NORM_EPS) * gain
        return (xn_ * cos + pltpu.roll(xn_, LANES - ROPE_DIM // 2, axis=1) * sa
                + pltpu.roll(xn_, ROPE_DIM // 2, axis=1) * sb)

    for c in range(Q_DIM // LANES):
        x = qkv[:, c * LANES:(c + 1) * LANES]
        q_ref[:, c * LANES:(c + 1) * LANES] = (norm_rope(x, qn_ref[...]) * (HEAD_DIM ** -0.5)).astype(BF16)
    for c in range(KV_DIM // LANES):
        x = qkv[:, Q_DIM + c * LANES:Q_DIM + (c + 1) * LANES]
        k_ref[:, c * LANES:(c + 1) * LANES] = norm_rope(x, kn_ref[...]).astype(BF16)
    v_ref[...] = qkv[:, Q_DIM + KV_DIM:].astype(BF16)


def _qkv(h2, g, w, b, qn, kn, cos_f, sin_a, sin_b, seg2):
    t = h2.shape[0]
    tm = _tile(t, 256)
    row = lambda i: (i, 0)
    tab = pl.BlockSpec((tm, LANES), row)
    return pl.pallas_call(
        _qkv_kernel,
        grid=(t // tm,),
        in_specs=[pl.BlockSpec((tm, D_MODEL), row), _full((1, D_MODEL)), _full(w.shape), _full(b.shape),
                  _full(qn.shape), _full(kn.shape), tab, tab, tab, _full(seg2.shape)],
        out_specs=[pl.BlockSpec((tm, Q_DIM), row), pl.BlockSpec((tm, KV_DIM), row),
                   pl.BlockSpec((tm, KV_DIM), row)],
        out_shape=[jax.ShapeDtypeStruct((t, Q_DIM), BF16), jax.ShapeDtypeStruct((t, KV_DIM), BF16),
                   jax.ShapeDtypeStruct((t, KV_DIM), BF16)],
        compiler_params=_params("parallel"),
        name="qkv",
    )(h2, g, w, b, qn, kn, cos_f, sin_a, sin_b, seg2)


def _attn_kernel(h_ref, q_ref, kc_ref, kp_ref, vc_ref, vp_ref, sink_ref, wo_ref, bo_ref, o_ref, *, nqb):
    w = WINDOW
    n = HEAD_DIM
    rows = GQA_GROUP * w
    qi = lax.broadcasted_iota(jnp.int32, (rows, 2 * w), 0) % w
    kj = lax.broadcasted_iota(jnp.int32, (rows, 2 * w), 1)
    diff = w + qi - kj
    band = (diff >= 0) & (diff < w)
    band0 = band & (kj >= jnp.where(pl.program_id(1) == 0, w, 0))
    kall = jnp.concatenate([kp_ref[...], kc_ref[...]], axis=0)
    vall = jnp.concatenate([vp_ref[...], vc_ref[...]], axis=0)
    units = [(qb, g) for qb in range(nqb) for g in range(KV_HEADS)]
    s, e, den = {}, {}, {}
    for u in units:
        qb, g = u
        qs = jnp.concatenate(
            [q_ref[qb * w:(qb + 1) * w, (g * GQA_GROUP + i) * n:(g * GQA_GROUP + i + 1) * n]
             for i in range(GQA_GROUP)], axis=0)
        kk = kall[qb * w:(qb + 2) * w, g * n:(g + 1) * n]
        sc = lax.dot_general(qs, kk, (((1,), (1,)), ((), ())), preferred_element_type=F32)
        s[u] = jnp.where(band0 if qb == 0 else band, sc, -jnp.inf)
    sinks = [jnp.concatenate(
        [jnp.broadcast_to(sink_ref[:, g * GQA_GROUP + i:g * GQA_GROUP + i + 1], (w, 1))
         for i in range(GQA_GROUP)], axis=0) for g in range(KV_HEADS)]
    for u in units:
        sink = sinks[u[1]]
        m = jnp.maximum(jnp.max(s[u], axis=-1, keepdims=True), sink)
        e[u] = jnp.exp(s[u] - m)
        den[u] = jnp.sum(e[u], axis=-1, keepdims=True) + jnp.exp(sink - m)
    outs = [[None] * Q_HEADS for _ in range(nqb)]
    for u in units:
        qb, g = u
        vv = vall[qb * w:(qb + 2) * w, g * n:(g + 1) * n]
        o = jnp.dot(e[u].astype(BF16), vv, preferred_element_type=F32) / den[u]
        for i in range(GQA_GROUP):
            outs[qb][g * GQA_GROUP + i] = o[i * w:(i + 1) * w]
    att = jnp.concatenate([jnp.concatenate(outs[qb], axis=1) for qb in range(nqb)], axis=0)
    o_ref[...] = h_ref[...] + _bdot(att, wo_ref[...]) + bo_ref[...]


def _attn(h2, q, k, v, sinks, wo, bo, bsz, seq):
    t = h2.shape[0]
    nqb = 2 if seq % (2 * WINDOW) == 0 else 1
    tq = nqb * WINDOW
    nb = seq // tq
    cur = lambda b, i: (b * nb + i, 0)
    prv = lambda b, i: (b * nb * nqb + jnp.maximum(i * nqb - 1, 0), 0)
    return pl.pallas_call(
        functools.partial(_attn_kernel, nqb=nqb),
        grid=(bsz, nb),
        in_specs=[pl.BlockSpec((tq, D_MODEL), cur), pl.BlockSpec((tq, Q_DIM), cur),
                  pl.BlockSpec((tq, KV_DIM), cur), pl.BlockSpec((WINDOW, KV_DIM), prv),
                  pl.BlockSpec((tq, KV_DIM), cur), pl.BlockSpec((WINDOW, KV_DIM), prv),
                  _full(sinks.shape), _full(wo.shape), _full(bo.shape)],
        out_specs=pl.BlockSpec((tq, D_MODEL), cur),
        out_shape=jax.ShapeDtypeStruct((t, D_MODEL), F32),
        compiler_params=_params("parallel", "parallel"),
        name="attn",
    )(h2, q, k, k, v, v, sinks, wo, bo)


def _router_kernel(h_ref, g_ref, wr_ref, xn_ref, meta_ref, cnt_ref, base_ref):
    @pl.when(pl.program_id(0) == 0)
    def _():
        base_ref[...] = jnp.zeros_like(base_ref)

    xn = _rms(h_ref[...], g_ref[...])
    xn_ref[...] = xn
    hi, lo = _split(xn)
    whi = wr_ref[0]
    wlo = wr_ref[1]
    logits = (jnp.dot(hi, whi, preferred_element_type=F32) + jnp.dot(hi, wlo, preferred_element_type=F32)
              + jnp.dot(lo, whi, preferred_element_type=F32))
    tm = logits.shape[0]
    lane = lax.broadcasted_iota(jnp.int32, logits.shape, 1)
    logits = jnp.where(lane < N_EXPERTS, logits, -jnp.inf)
    m1 = jnp.max(logits, axis=-1, keepdims=True)
    i1 = jnp.min(jnp.where(logits == m1, lane, LANES), axis=-1, keepdims=True)
    rest = jnp.where(lane == i1, -jnp.inf, logits)
    m2 = jnp.max(rest, axis=-1, keepdims=True)
    i2 = jnp.min(jnp.where(rest == m2, lane, LANES), axis=-1, keepdims=True)
    e2 = jnp.exp(m2 - m1)
    w1 = 1.0 / (1.0 + e2)
    w2 = e2 / (1.0 + e2)

    sel1 = lane == i1
    sel2 = lane == i2
    onehot = jnp.where(sel1, 1.0, 0.0) + jnp.where(sel2, 1.0, 0.0)
    ri = lax.broadcasted_iota(jnp.int32, (tm, tm), 0)
    ci = lax.broadcasted_iota(jnp.int32, (tm, tm), 1)
    before = jnp.dot(jnp.where(ci < ri, 1.0, 0.0).astype(BF16), onehot.astype(BF16),
                     preferred_element_type=F32)
    rank = base_ref[...] + before
    r1 = jnp.sum(jnp.where(sel1, rank, 0.0), axis=-1, keepdims=True)
    r2 = jnp.sum(jnp.where(sel2, rank, 0.0), axis=-1, keepdims=True)
    total = base_ref[...] + jnp.sum(onehot, axis=0, keepdims=True)
    base_ref[...] = total
    cnt_ref[...] = total
    cols = (i1.astype(F32), i2.astype(F32), w1, w2, r1, r2)
    meta = jnp.zeros(logits.shape, F32)
    for c, val in enumerate(cols):
        meta = jnp.where(lane == c, val, meta)
    meta_ref[...] = meta


def _router(h2, g, wr):
    t = h2.shape[0]
    tm = _tile(t, 512)
    row = lambda i: (i, 0)
    return pl.pallas_call(
        _router_kernel,
        grid=(t // tm,),
        in_specs=[pl.BlockSpec((tm, D_MODEL), row), _full((1, D_MODEL)), _full(wr.shape)],
        out_specs=[pl.BlockSpec((tm, D_MODEL), row), pl.BlockSpec((tm, LANES), row), _full((1, LANES))],
        out_shape=[jax.ShapeDtypeStruct((t, D_MODEL), F32), jax.ShapeDtypeStruct((t, LANES), F32),
                   jax.ShapeDtypeStruct((1, LANES), F32)],
        scratch_shapes=[pltpu.VMEM((1, LANES), F32)],
        compiler_params=_params("arbitrary"),
        name="router",
    )(h2, g, wr)


def _moe_index_kernel(e1_ref, e2_ref, r1_ref, r2_ref, off_ref, dst0_ref, src_ref, dst_ref, *, ntok):
    def init(p, c):
        src_ref[p] = 0
        dst_ref[p] = dst0_ref[p]
        return c

    lax.fori_loop(0, src_ref.shape[0], init, 0, unroll=8)

    def body(t, c):
        p1 = off_ref[e1_ref[t]] + r1_ref[t]
        p2 = off_ref[e2_ref[t]] + r2_ref[t]
        src_ref[p1] = t
        dst_ref[p1] = t
        src_ref[p2] = t
        dst_ref[p2] = ntok + t
        return c

    lax.fori_loop(0, ntok, body, 0, unroll=2)


def _moe_index(e1, e2, r1, r2, off, dst0):
    smem = pl.BlockSpec(memory_space=pltpu.SMEM)
    return pl.pallas_call(
        functools.partial(_moe_index_kernel, ntok=e1.shape[0]),
        grid_spec=pltpu.PrefetchScalarGridSpec(
            num_scalar_prefetch=6, grid=(1,), in_specs=[], out_specs=[smem, smem]),
        out_shape=[jax.ShapeDtypeStruct(dst0.shape, jnp.int32)] * 2,
        compiler_params=_params("arbitrary"),
        name="moe_index",
    )(e1, e2, r1, r2, off, dst0)


def _row_copy(src_ref, src_row, dst_ref, dst_row, sem):
    return pltpu.make_async_copy(src_ref.at[pl.ds(src_row, 1)], dst_ref.at[pl.ds(dst_row, 1)], sem)


def _experts_kernel(te_ref, nused_ref, src_ref, dst_ref, x_ref, wg_ref, wu_ref, wd_ref, y_ref,
                    xbuf, ybuf, gsem, ssem):
    del te_ref
    i = pl.program_id(0)
    nused = nused_ref[0]
    tm = MOE_TM

    def gather(tile, slot):
        def body(j, c):
            _row_copy(x_ref, src_ref[tile * tm + j], xbuf.at[slot], j, gsem.at[slot]).start()
            return c
        lax.fori_loop(0, tm, body, 0, unroll=8)

    def scatter(tile, slot):
        def body(j, c):
            _row_copy(ybuf.at[slot], j, y_ref, dst_ref[tile * tm + j], ssem.at[slot]).start()
            return c
        lax.fori_loop(0, tm, body, 0, unroll=8)

    def wait_gather(slot):
        pltpu.make_async_copy(x_ref.at[pl.ds(0, tm)], xbuf.at[slot], gsem.at[slot]).wait()

    def wait_scatter(slot):
        pltpu.make_async_copy(ybuf.at[slot], y_ref.at[pl.ds(0, tm)], ssem.at[slot]).wait()

    @pl.when(i == 0)
    def _():
        gather(0, 0)

    @pl.when(i + 1 < nused)
    def _():
        gather(i + 1, (i + 1) % 2)

    @pl.when(i < nused)
    def _():
        slot = i % 2
        wait_gather(slot)

        @pl.when(i >= 2)
        def _():
            wait_scatter(slot)

        xb = xbuf[slot].astype(BF16)
        act = _silu(jnp.dot(xb, wg_ref[0], preferred_element_type=F32)) * jnp.dot(
            xb, wu_ref[0], preferred_element_type=F32)
        ybuf[slot] = jnp.dot(act.astype(BF16), wd_ref[0], preferred_element_type=F32)
        scatter(i, slot)

        @pl.when(i == nused - 1)
        def _():
            @pl.when(i >= 1)
            def _():
                wait_scatter(1 - slot)
            wait_scatter(slot)


def _experts(tile_expert, nused, src, dst, xn, wg, wu, wd, out_rows):
    ntile = tile_expert.shape[0]
    tm = MOE_TM
    wsel = lambda i, te, nu, s, d: (te[i], 0, 0)
    any_spec = pl.BlockSpec(memory_space=pl.ANY)
    return pl.pallas_call(
        _experts_kernel,
        grid_spec=pltpu.PrefetchScalarGridSpec(
            num_scalar_prefetch=4, grid=(ntile,),
            in_specs=[any_spec,
                      pl.BlockSpec((1, D_MODEL, FFN_EXPERT), wsel),
                      pl.BlockSpec((1, D_MODEL, FFN_EXPERT), wsel),
                      pl.BlockSpec((1, FFN_EXPERT, D_MODEL), wsel)],
            out_specs=any_spec,
            scratch_shapes=[pltpu.VMEM((2, tm, D_MODEL), F32), pltpu.VMEM((2, tm, D_MODEL), F32),
                            pltpu.SemaphoreType.DMA((2,)), pltpu.SemaphoreType.DMA((2,))]),
        out_shape=jax.ShapeDtypeStruct((out_rows, D_MODEL), F32),
        compiler_params=_params("arbitrary"),
        name="moe_experts",
    )(tile_expert, nused, src, dst, xn, wg, wu, wd)


def _combine_kernel(h_ref, meta_ref, y1_ref, y2_ref, o_ref):
    meta = meta_ref[...]
    o_ref[...] = h_ref[...] + meta[:, 2:3] * y1_ref[...] + meta[:, 3:4] * y2_ref[...]


def _combine(h2, meta, ys):
    t = h2.shape[0]
    tm = _tile(t, 512)
    nt = t // tm
    row = lambda i: (i, 0)
    return pl.pallas_call(
        _combine_kernel,
        grid=(nt,),
        in_specs=[pl.BlockSpec((tm, D_MODEL), row), pl.BlockSpec((tm, LANES), row),
                  pl.BlockSpec((tm, D_MODEL), row), pl.BlockSpec((tm, D_MODEL), lambda i: (nt + i, 0))],
        out_specs=pl.BlockSpec((tm, D_MODEL), row),
        out_shape=jax.ShapeDtypeStruct((t, D_MODEL), F32),
        compiler_params=_params("parallel"),
        name="moe_combine",
    )(h2, meta, ys, ys)


def _moe(h2, g, router, wg, wu, wd):
    t = h2.shape[0]
    tm = MOE_TM
    wr = jnp.pad(router, ((0, 0), (0, LANES - N_EXPERTS)))
    wr_hi = wr.astype(BF16)
    wr_lo = (wr - wr_hi.astype(F32)).astype(BF16)
    xn, meta, cnt = _router(h2, g, jnp.stack([wr_hi, wr_lo]))

    e1, e2, r1, r2 = (meta[:, c].astype(jnp.int32) for c in (0, 1, 4, 5))
    count = cnt[0, :N_EXPERTS].astype(jnp.int32)
    ntile = (count + tm - 1) // tm
    tile_end = jnp.cumsum(ntile)
    off = (tile_end - ntile) * tm
    rows = 2 * t + N_EXPERTS * tm
    tile_id = jnp.arange(rows // tm)
    tile_expert = jnp.minimum(
        jnp.sum(tile_id[:, None] >= tile_end[None, :], axis=1), N_EXPERTS - 1).astype(jnp.int32)
    nused = tile_end[-1:].astype(jnp.int32)
    row_id = jnp.arange(rows, dtype=jnp.int32)
    dst0 = 2 * t + jnp.repeat(tile_expert, tm) * tm + row_id % tm
    src, dst = _moe_index(e1, e2, r1, r2, off, dst0)

    ys = _experts(tile_expert, nused, src, dst, xn, wg, wu, wd, rows)
    return _combine(h2, meta, ys)


def _segment_matrix(width):
    i = jnp.arange(width) // HEAD_DIM
    return (i[:, None] == i[None, :]).astype(BF16)


def _layer0(h2, bsz, seq, norm_mix, w_in, mu_shift, w0, w_decay_up, a0, w_iclr_up, w_gate_up, k_k, k_a, r_k,
            gn_w, gn_b, conv_w, conv_b, dt_bias, a_log, d_skip, ssd_norm, w_out, norm_ffn, ffn_gate, ffn_up,
            ffn_down):
    row = lambda a: a.reshape(1, -1)
    seg = _segment_matrix(RWKV_DIM)
    wa = w_in[:, :RWKV_COLS].astype(BF16)
    wb = jnp.pad(w_in[:, RWKV_COLS:], ((0, 0), (0, SSD_COLS_PAD - SSD_COLS))).astype(BF16)
    pa, pb = _in_proj(h2, row(norm_mix), wa, wb)

    lora = w_decay_up.shape[0]
    wd = jnp.concatenate([w_decay_up, jnp.zeros((LANES - lora, RWKV_DIM), F32)], axis=0).astype(BF16)
    wi = jnp.concatenate([jnp.zeros((LANES - lora, RWKV_DIM), F32), w_iclr_up], axis=0).astype(BF16)
    r, k, v, lw, kkn, b, g, bonus = _rwkv_prep(
        pa, bsz, seq, row(mu_shift), row(w0), wd, row(a0), wi, w_gate_up.astype(BF16), row(k_k), row(k_a),
        row(r_k), seg)
    y = _rwkv_chunk(r, k, v, lw, kkn, b, bsz, seq)

    lane_pad = lambda a: jnp.pad(row(a), ((0, 0), (0, LANES - a.size)))
    ys = _ssd(pb, bsz, seq, conv_w, row(conv_b), lane_pad(dt_bias), lane_pad(a_log), lane_pad(d_skip),
              row(ssd_norm))

    w_out = w_out.astype(BF16)
    h2 = _mix_out(h2, y, bonus, g, ys, row(gn_w), row(gn_b), seg, w_out[:RWKV_DIM], w_out[RWKV_DIM:])
    return _ffn(h2, row(norm_ffn), ffn_gate.astype(BF16), ffn_up.astype(BF16), ffn_down.astype(BF16))


def _layer1(h2, bsz, seq, tables, norm_mix, w_qkv, b_qkv, q_norm, k_norm, sinks, w_o, b_o, norm_ffn, router,
            exp_gate, exp_up, exp_down):
    row = lambda a: a.reshape(1, -1)
    two = lambda a: jnp.tile(a, LANES // HEAD_DIM).reshape(1, LANES)
    cos_f, sin_a, sin_b = tables
    q, k, v = _qkv(h2, row(norm_mix), w_qkv.astype(BF16), row(b_qkv), two(q_norm), two(k_norm), cos_f, sin_a,
                   sin_b, _segment_matrix(LANES))
    h2 = _attn(h2, q, k, v, row(sinks), w_o.astype(BF16), row(b_o), bsz, seq)

    return _moe(h2, row(norm_ffn), router, exp_gate.astype(BF16), exp_up.astype(BF16), exp_down.astype(BF16))


def kernel(x, positions, ev_norm_mix, ev_w_in, ev_mu_shift, ev_w0, ev_w_decay_up, ev_a0, ev_w_iclr_up, ev_w_gate_up, ev_k_k, ev_k_a, ev_r_k, ev_gn_w, ev_gn_b, ev_conv_w, ev_conv_b, ev_dt_bias, ev_a_log, ev_d_skip, ev_ssd_norm, ev_w_out, ev_norm_ffn, ev_ffn_gate, ev_ffn_up, ev_ffn_down, od_norm_mix, od_w_qkv, od_b_qkv, od_q_norm, od_k_norm, od_sinks, od_w_o, od_b_o, od_norm_ffn, od_router, od_exp_gate, od_exp_up, od_exp_down):
    bsz, seq, d = x.shape
    depth = ev_norm_mix.shape[0] + od_norm_mix.shape[0]
    tables = _rope_tables(positions)
    h2 = x.reshape(bsz * seq, d)
    for layer in range(depth):
        i = layer // 2
        if layer % 2 == 0:
            h2 = _layer0(h2, bsz, seq, ev_norm_mix[i], ev_w_in[i], ev_mu_shift[i], ev_w0[i], ev_w_decay_up[i],
                         ev_a0[i], ev_w_iclr_up[i], ev_w_gate_up[i], ev_k_k[i], ev_k_a[i], ev_r_k[i],
                         ev_gn_w[i], ev_gn_b[i], ev_conv_w[i], ev_conv_b[i], ev_dt_bias[i], ev_a_log[i],
                         ev_d_skip[i], ev_ssd_norm[i], ev_w_out[i], ev_norm_ffn[i], ev_ffn_gate[i],
                         ev_ffn_up[i], ev_ffn_down[i])
        else:
            h2 = _layer1(h2, bsz, seq, tables, od_norm_mix[i], od_w_qkv[i], od_b_qkv[i], od_q_norm[i],
                         od_k_norm[i], od_sinks[i], od_w_o[i], od_b_o[i], od_norm_ffn[i], od_router[i],
                         od_exp_gate[i], od_exp_up[i], od_exp_down[i])
    return h2.reshape(bsz, seq, d)
```

```python
import functools
import math

import jax
import jax.numpy as jnp
from jax import lax
from jax.experimental import pallas as pl
from jax.experimental.pallas import tpu as pltpu

F32 = jnp.float32
BF16 = jnp.bfloat16

D_MODEL = 1024
HEAD_DIM = 64
NORM_EPS = 1e-6

RWKV_HEADS = 8
RWKV_DIM = 512
RWKV_COLS = 1792
RWKV_GN_EPS = 64e-5
RWKV_CHUNK = 64
RWKV_GROUP = 256

SSD_HEADS = 8
SSD_DIM = 512
SSD_GROUPS = 2
SSD_STATE = 128
SSD_CONV = 4
SSD_CHUNK = 128
SSD_XBC = 1024
SSD_COLS = 1544
SSD_COLS_PAD = 1664

Q_HEADS = 16
KV_HEADS = 4
GQA_GROUP = 4
Q_DIM = 1024
KV_DIM = 256
WINDOW = 128
ROPE_THETA = 500000.0
ROPE_DIM = 16

FFN_DENSE = 2816
N_EXPERTS = 8
FFN_EXPERT = 1408
MOE_TM = 256
MOE_SUB = 8

LANES = 128
VMEM_LIMIT_BYTES = 56 * 1024 * 1024


def _params(*sem):
    return pltpu.CompilerParams(dimension_semantics=sem, vmem_limit_bytes=VMEM_LIMIT_BYTES)


def _bdot(a, b):
    return jnp.dot(a.astype(BF16), b.astype(BF16), preferred_element_type=F32)


def _bdot_nt(a, b):
    return lax.dot_general(a.astype(BF16), b.astype(BF16), (((1,), (1,)), ((), ())),
                           preferred_element_type=F32)


def _bdot_tn(a, b):
    return lax.dot_general(a.astype(BF16), b.astype(BF16), (((0,), (0,)), ((), ())),
                           preferred_element_type=F32)


def _split(x):
    hi = x.astype(BF16)
    lo = (x - hi.astype(F32)).astype(BF16)
    return hi, lo


def _dot_exact_rhs(x, m):
    hi, lo = _split(x)
    return (jnp.dot(hi, m, preferred_element_type=F32) + jnp.dot(lo, m, preferred_element_type=F32))


def _dot_exact_lhs(m, x):
    hi, lo = _split(x)
    return (jnp.dot(m, hi, preferred_element_type=F32) + jnp.dot(m, lo, preferred_element_type=F32))


def _sigmoid(x):
    return 1.0 / (1.0 + jnp.exp(-x))


def _silu(x):
    return x * _sigmoid(x)


def _softplus(x):
    return jnp.maximum(x, 0.0) + jnp.log(1.0 + jnp.exp(-jnp.abs(x)))


def _rms(x, g):
    ms = jnp.mean(x * x, axis=-1, keepdims=True)
    return x * lax.rsqrt(ms + NORM_EPS) * g


def _tile(n, pref):
    t = min(n, pref)
    while n % t:
        t //= 2
    return t


def _full(shape):
    nd = len(shape)
    return pl.BlockSpec(shape, lambda *_: (0,) * nd)


def _in_proj_kernel(x_ref, g_ref, wa_ref, wb_ref, pa_ref, pb_ref):
    xn = _rms(x_ref[...], g_ref[...]).astype(BF16)
    pa_ref[...] = jnp.dot(xn, wa_ref[...], preferred_element_type=F32)
    pb_ref[...] = jnp.dot(xn, wb_ref[...], preferred_element_type=F32)


def _in_proj(x2, g, wa, wb):
    t = x2.shape[0]
    tm = _tile(t, 256)
    return pl.pallas_call(
        _in_proj_kernel,
        grid=(t // tm,),
        in_specs=[pl.BlockSpec((tm, D_MODEL), lambda i: (i, 0)), _full((1, D_MODEL)),
                  _full(wa.shape), _full(wb.shape)],
        out_specs=[pl.BlockSpec((tm, RWKV_COLS), lambda i: (i, 0)),
                   pl.BlockSpec((tm, SSD_COLS_PAD), lambda i: (i, 0))],
        out_shape=[jax.ShapeDtypeStruct((t, RWKV_COLS), F32),
                   jax.ShapeDtypeStruct((t, SSD_COLS_PAD), F32)],
        compiler_params=_params("parallel"),
        name="in_proj",
    )(x2, g, wa, wb)


def _rwkv_prep_kernel(pa_ref, mu_ref, w0_ref, wd_ref, a0_ref, wi_ref, wg_ref, kk_ref, ka_ref, rk_ref,
                      seg_ref, r_ref, k_ref, v_ref, lw_ref, kkn_ref, b_ref, g_ref, bonus_ref, carry_ref):
    @pl.when(pl.program_id(1) == 0)
    def _():
        carry_ref[...] = jnp.zeros_like(carry_ref)

    pa = pa_ref[...]
    tm = pa.shape[0]
    row = lax.broadcasted_iota(jnp.int32, pa.shape, 0)
    prev = jnp.where(row == 0, carry_ref[...], pltpu.roll(pa, 1, axis=0))
    carry_ref[...] = pa[tm - 1:tm, :]
    x = pa + (prev - pa) * mu_ref[...]

    r = x[:, 0:RWKV_DIM]
    k = x[:, RWKV_DIM:2 * RWKV_DIM]
    v = x[:, 2 * RWKV_DIM:3 * RWKV_DIM]
    lora = x[:, 3 * RWKV_DIM:3 * RWKV_DIM + LANES]
    gl = x[:, 3 * RWKV_DIM + LANES:]
    seg = seg_ref[...]

    w_raw = w0_ref[...] + _bdot(jnp.tanh(lora), wd_ref[...])
    lw_ref[...] = (-math.exp(-0.5)) * _sigmoid(w_raw)
    iclr = _sigmoid(a0_ref[...] + _bdot(lora, wi_ref[...]))
    g_ref[...] = _bdot(_sigmoid(gl), wg_ref[...])

    kk = k * kk_ref[...]
    kkn = kk * lax.rsqrt(_dot_exact_rhs(kk * kk, seg) + 1e-12)
    k2 = k * (1.0 + (iclr - 1.0) * ka_ref[...])
    r_ref[...] = r
    k_ref[...] = k2
    v_ref[...] = v
    kkn_ref[...] = kkn
    b_ref[...] = kkn * iclr
    bonus_ref[...] = _dot_exact_rhs(r * k2 * rk_ref[...], seg) * v


def _rwkv_prep(pa, bsz, seq, mu, w0, wd, a0, wi, wg, k_k, k_a, r_k, seg):
    t = pa.shape[0]
    tm = _tile(seq, 256)
    nt = seq // tm
    row = lambda b, i: (b * nt + i, 0)
    out = jax.ShapeDtypeStruct((t, RWKV_DIM), F32)
    return pl.pallas_call(
        _rwkv_prep_kernel,
        grid=(bsz, nt),
        in_specs=[pl.BlockSpec((tm, RWKV_COLS), row), _full(mu.shape), _full(w0.shape), _full(wd.shape),
                  _full(a0.shape), _full(wi.shape), _full(wg.shape), _full(k_k.shape), _full(k_a.shape),
                  _full(r_k.shape), _full(seg.shape)],
        out_specs=[pl.BlockSpec((tm, RWKV_DIM), row)] * 8,
        out_shape=[out] * 8,
        scratch_shapes=[pltpu.VMEM((1, RWKV_COLS), F32)],
        compiler_params=_params("arbitrary", "arbitrary"),
        name="rwkv_prep",
    )(pa, mu, w0, wd, a0, wi, wg, k_k, k_a, r_k, seg)


def _rwkv_chunk_kernel(r_ref, k_ref, v_ref, lw_ref, kk_ref, b_ref, y_ref, z_ref, *, lt):
    c_len = RWKV_CHUNK
    n = HEAD_DIM
    gl = RWKV_GROUP
    nchunk = gl // c_len
    nheads = LANES // n
    shift = c_len.bit_length() - 1

    @pl.when(pl.program_id(2) == 0)
    def _():
        z_ref[...] = jnp.zeros_like(z_ref)

    ri = lax.broadcasted_iota(jnp.int32, (gl, gl), 0)
    ci = lax.broadcasted_iota(jnp.int32, (gl, gl), 1)
    tri_bd = jnp.where((ci <= ri) & (ci >= ((ri >> shift) << shift)), 1.0, 0.0).astype(BF16)
    re_ = lax.broadcasted_iota(jnp.int32, (c_len, c_len), 0)
    ce_ = lax.broadcasted_iota(jnp.int32, (c_len, c_len), 1)
    eye = re_ == ce_
    ri2 = lax.broadcasted_iota(jnp.int32, (2 * gl, gl), 0)
    ci2 = lax.broadcasted_iota(jnp.int32, (2 * gl, gl), 1)
    t2 = jnp.where(ri2 < gl, ri2, ri2 - gl)
    mask2 = (ci2 <= jnp.where(ri2 < gl, t2 - 1, t2)) & (ci2 >= ((t2 >> shift) << shift))
    zeros = jnp.zeros((c_len, n), F32)

    nsub = lt // gl
    sysid = [(s, h) for s in range(nsub) for h in range(nheads)]
    pre = []
    for s in range(nsub):
        sl = pl.ds(s * gl, gl)
        lw = lw_ref[sl, :]
        g_in = _dot_exact_lhs(tri_bd, lw)
        ends = [g_in[(c + 1) * c_len - 1:(c + 1) * c_len, :] for c in range(nchunk)]
        g_end = jnp.concatenate([jnp.broadcast_to(e, (c_len, LANES)) for e in ends], axis=0)
        e_end = jnp.exp(g_end - g_in)
        en = jnp.exp(-g_in)
        k = k_ref[sl, :]
        b = b_ref[sl, :]
        pre.append(dict(
            ends=ends, v=v_ref[sl, :], rt=r_ref[sl, :] * jnp.exp(g_in),
            at=-kk_ref[sl, :] * jnp.exp(g_in - lw), kt=k * en, bt=b * en, bend=b * e_end, kend=k * e_end))

    def hs(name, s, h):
        return pre[s][name][:, h * n:(h + 1) * n]

    xb, xk, xkv, p, x = {}, {}, {}, {}, {}
    for q in sysid:
        la = jnp.concatenate([hs("at", *q), hs("rt", *q)], axis=0)
        xb[q] = jnp.where(mask2, _bdot_nt(la, hs("bt", *q)), 0.0)
        xk[q] = jnp.where(mask2, _bdot_nt(la, hs("kt", *q)), 0.0)
    for q in sysid:
        xkv[q] = _bdot(xk[q], hs("v", *q))
        p[q] = xb[q][0:gl]
        x[q] = jnp.concatenate([hs("at", *q), xkv[q][0:gl]], axis=1)
    for i in range(6):
        for q in sysid:
            x[q] = x[q] + _bdot(p[q], x[q])
        if i < 5:
            for q in sysid:
                p[q] = _bdot(p[q], p[q])
    rq, y0, mn = {}, {}, {}
    for q in sysid:
        yy = _bdot(xb[q][gl:], x[q])
        rq[q] = yy[:, 0:n] + hs("rt", *q)
        y0[q] = yy[:, n:] + xkv[q][gl:]
        bend_h, kend_h, v_h = hs("bend", *q), hs("kend", *q), hs("v", *q)
        for c in range(nchunk):
            rows = slice(c * c_len, (c + 1) * c_len)
            lhs = jnp.concatenate([bend_h[rows], kend_h[rows]], axis=0)
            rhs = jnp.concatenate([x[q][rows], jnp.concatenate([zeros, v_h[rows]], axis=1)], axis=0)
            mn[q + (c,)] = _bdot_tn(lhs, rhs)
    zs = [z_ref[h] for h in range(nheads)]
    for s in range(nsub):
        yh = [[] for _ in range(nheads)]
        for c in range(nchunk):
            rows = slice(c * c_len, (c + 1) * c_len)
            for h in range(nheads):
                q = (s, h)
                pend = jnp.exp(pre[s]["ends"][c][:, h * n:(h + 1) * n])
                m_mat = mn[q + (c,)][:, 0:n] + jnp.where(eye, pend, 0.0)
                yh[h].append(_bdot(rq[q][rows], zs[h]) + y0[q][rows])
                zs[h] = _bdot(m_mat, zs[h]) + mn[q + (c,)][:, n:]
        y_ref[pl.ds(s * gl, gl), :] = jnp.concatenate(
            [jnp.concatenate(yh[h], axis=0) for h in range(nheads)], axis=1)
    for h in range(nheads):
        z_ref[h] = zs[h]


def _rwkv_chunk(r, k, v, lw, kkn, b, bsz, seq):
    t = r.shape[0]
    lt = _tile(seq, 512)
    nt = seq // lt
    spec = pl.BlockSpec((lt, LANES), lambda bi, hp, i: (bi * nt + i, hp))
    return pl.pallas_call(
        functools.partial(_rwkv_chunk_kernel, lt=lt),
        grid=(bsz, RWKV_DIM // LANES, nt),
        in_specs=[spec] * 6,
        out_specs=spec,
        out_shape=jax.ShapeDtypeStruct((t, RWKV_DIM), F32),
        scratch_shapes=[pltpu.VMEM((2, HEAD_DIM, HEAD_DIM), F32)],
        compiler_params=_params("arbitrary", "arbitrary", "arbitrary"),
        name="rwkv_chunk",
    )(r, k, v, lw, kkn, b)


def _ssd_kernel(pb_ref, cw_ref, cb_ref, dtb_ref, alog_ref, dskip_ref, nrm_ref, ys_ref, ext_ref, st_ref):
    q = SSD_CHUNK
    p = HEAD_DIM
    hpg = SSD_HEADS // SSD_GROUPS

    @pl.when(pl.program_id(1) == 0)
    def _():
        ext_ref[0:8, :] = jnp.zeros((8, SSD_XBC), F32)
        st_ref[...] = jnp.zeros_like(st_ref)

    z = pb_ref[:, 0:SSD_DIM]
    u = pb_ref[:, SSD_DIM:SSD_DIM + SSD_XBC]
    dt_raw = pb_ref[:, SSD_DIM + SSD_XBC:]

    ext_ref[8:8 + q, :] = u
    conv = cb_ref[...] + cw_ref[SSD_CONV - 1:SSD_CONV, :] * u
    for j in range(SSD_CONV - 1):
        off = 8 - (SSD_CONV - 1) + j
        conv = conv + cw_ref[j:j + 1, :] * ext_ref[off:off + q, :]
    ext_ref[0:8, :] = u[q - 8:q, :]
    xbc = _silu(conv)
    xs = xbc[:, 0:SSD_DIM]
    bm = xbc[:, SSD_DIM:SSD_DIM + SSD_GROUPS * SSD_STATE]
    cm = xbc[:, SSD_DIM + SSD_GROUPS * SSD_STATE:]

    dt = _softplus(dt_raw + dtb_ref[...])
    a = -jnp.exp(alog_ref[...])
    ri = lax.broadcasted_iota(jnp.int32, (q, q), 0)
    ci = lax.broadcasted_iota(jnp.int32, (q, q), 1)
    causal = ri >= ci
    cum = _dot_exact_lhs(causal.astype(BF16), dt * a)
    cum_t = cum.T
    dt_t = dt.T
    cum_end = cum[q - 1:q, :]
    to_end = jnp.exp(cum_end - cum) * dt
    ecum = jnp.exp(cum)
    edec = jnp.exp(cum_end)

    ys = []
    for g in range(SSD_GROUPS):
        bm_g = bm[:, g * SSD_STATE:(g + 1) * SSD_STATE]
        cm_g = cm[:, g * SSD_STATE:(g + 1) * SSD_STATE]
        cb = _bdot_nt(cm_g, bm_g)
        bm_t = bm_g.T
        for hh in range(hpg):
            h = g * hpg + hh
            x_h = xs[:, h * p:(h + 1) * p]
            seg = cum[:, h:h + 1] - cum_t[h:h + 1, :]
            ldec = jnp.exp(jnp.where(causal, seg, -jnp.inf))
            wts = cb * ldec * dt_t[h:h + 1, :]
            y = _bdot(wts, x_h)
            h_prev = st_ref[h]
            y = y + _bdot(cm_g, h_prev) * ecum[:, h:h + 1]
            st_ref[h] = h_prev * edec[:, h:h + 1] + _bdot(bm_t, x_h * to_end[:, h:h + 1])
            ys.append(y + dskip_ref[:, h:h + 1] * x_h)
    yall = jnp.concatenate(ys, axis=1) * _silu(z)
    gw = SSD_DIM // SSD_GROUPS
    outs = []
    for g in range(SSD_GROUPS):
        yg = yall[:, g * gw:(g + 1) * gw]
        ms = jnp.mean(yg * yg, axis=-1, keepdims=True)
        outs.append(yg * lax.rsqrt(ms + NORM_EPS) * nrm_ref[:, g * gw:(g + 1) * gw])
    ys_ref[...] = jnp.concatenate(outs, axis=1)


def _ssd(pb, bsz, seq, conv_w, conv_b, dt_bias, a_log, d_skip, ssd_norm):
    t = pb.shape[0]
    nc = seq // SSD_CHUNK
    row = lambda b, i: (b * nc + i, 0)
    return pl.pallas_call(
        _ssd_kernel,
        grid=(bsz, nc),
        in_specs=[pl.BlockSpec((SSD_CHUNK, SSD_COLS_PAD), row), _full(conv_w.shape), _full(conv_b.shape),
                  _full(dt_bias.shape), _full(a_log.shape), _full(d_skip.shape), _full(ssd_norm.shape)],
        out_specs=pl.BlockSpec((SSD_CHUNK, SSD_DIM), row),
        out_shape=jax.ShapeDtypeStruct((t, SSD_DIM), F32),
        scratch_shapes=[pltpu.VMEM((8 + SSD_CHUNK, SSD_XBC), F32),
                        pltpu.VMEM((SSD_HEADS, SSD_STATE, HEAD_DIM), F32)],
        compiler_params=_params("arbitrary", "arbitrary"),
        name="ssd",
    )(pb, conv_w, conv_b, dt_bias, a_log, d_skip, ssd_norm)


def _mix_out_kernel(h_ref, y_ref, bonus_ref, g_ref, ys_ref, gnw_ref, gnb_ref, seg_ref, wa_ref, wb_ref, o_ref):
    y = y_ref[...]
    seg = seg_ref[...]
    mu = _dot_exact_rhs(y, seg) * (1.0 / HEAD_DIM)
    yc = y - mu
    var = _dot_exact_rhs(yc * yc, seg) * (1.0 / HEAD_DIM)
    yn = yc * lax.rsqrt(var + RWKV_GN_EPS) * gnw_ref[...] + gnb_ref[...]
    ya = (yn + bonus_ref[...]) * g_ref[...]
    o_ref[...] = h_ref[...] + _bdot(ya, wa_ref[...]) + _bdot(ys_ref[...], wb_ref[...])


def _mix_out(h2, y, bonus, g, ys, gn_w, gn_b, seg, wa, wb):
    t = h2.shape[0]
    tm = _tile(t, 512)
    row = lambda i: (i, 0)
    half = pl.BlockSpec((tm, RWKV_DIM), row)
    return pl.pallas_call(
        _mix_out_kernel,
        grid=(t // tm,),
        in_specs=[pl.BlockSpec((tm, D_MODEL), row), half, half, half, half, _full(gn_w.shape),
                  _full(gn_b.shape), _full(seg.shape), _full(wa.shape), _full(wb.shape)],
        out_specs=pl.BlockSpec((tm, D_MODEL), row),
        out_shape=jax.ShapeDtypeStruct((t, D_MODEL), F32),
        compiler_params=_params("parallel"),
        name="mix_out",
    )(h2, y, bonus, g, ys, gn_w, gn_b, seg, wa, wb)


def _ffn_kernel(h_ref, g_ref, wg_ref, wu_ref, wd_ref, o_ref):
    h = h_ref[...]
    xn = _rms(h, g_ref[...]).astype(BF16)
    act = _silu(jnp.dot(xn, wg_ref[...], preferred_element_type=F32)) * jnp.dot(
        xn, wu_ref[...], preferred_element_type=F32)
    o_ref[...] = h + jnp.dot(act.astype(BF16), wd_ref[...], preferred_element_type=F32)


def _resident(shape):
    nd = len(shape)
    return pl.BlockSpec(shape, lambda *_: (0,) * nd, pipeline_mode=pl.Buffered(1))


def _ffn(h2, g, wg, wu, wd):
    t = h2.shape[0]
    tm = _tile(t, 512)
    row = lambda i: (i, 0)
    return pl.pallas_call(
        _ffn_kernel,
        grid=(t // tm,),
        in_specs=[pl.BlockSpec((tm, D_MODEL), row), _full((1, D_MODEL)), _resident(wg.shape),
                  _resident(wu.shape), _resident(wd.shape)],
        out_specs=pl.BlockSpec((tm, D_MODEL), row),
        out_shape=jax.ShapeDtypeStruct((t, D_MODEL), F32),
        compiler_params=_params("parallel"),
        name="ffn",
    )(h2, g, wg, wu, wd)


def _rope_table_kernel(pos_ref, freq_ref, cos_ref, sin_ref):
    ang = pos_ref[...] * freq_ref[...]
    cos_ref[...] = jnp.cos(ang)
    sin_ref[...] = jnp.sin(ang)


def _rope_tables(positions):
    t = positions.size
    half = ROPE_DIM // 2
    rows = t * half // LANES
    inv_freq = ROPE_THETA ** (-jnp.arange(0, ROPE_DIM, 2, dtype=F32) / ROPE_DIM)
    pos_rep = jnp.repeat(positions.reshape(-1).astype(F32), half).reshape(rows, LANES)
    freq = jnp.tile(inv_freq, LANES // half).reshape(1, LANES)
    tr = _tile(rows, 256)
    cos, sin = pl.pallas_call(
        _rope_table_kernel,
        grid=(rows // tr,),
        in_specs=[pl.BlockSpec((tr, LANES), lambda i: (i, 0)), _full((1, LANES))],
        out_specs=[pl.BlockSpec((tr, LANES), lambda i: (i, 0))] * 2,
        out_shape=[jax.ShapeDtypeStruct((rows, LANES), F32)] * 2,
        compiler_params=_params("parallel"),
        name="rope_tables",
    )(pos_rep, freq)
    cos = cos.reshape(t, half)
    sin = sin.reshape(t, half)
    pad = HEAD_DIM - ROPE_DIM
    one = jnp.ones((t, pad), F32)
    zero = jnp.zeros((t, pad), F32)
    zh = jnp.zeros((t, half), F32)
    cos_f = jnp.concatenate([cos, cos, one], axis=1)
    sin_a = jnp.concatenate([-sin, zh, zero], axis=1)
    sin_b = jnp.concatenate([zh, sin, zero], axis=1)
    return tuple(jnp.tile(m, (1, LANES // HEAD_DIM)) for m in (cos_f, sin_a, sin_b))


def _qkv_kernel(h_ref, g_ref, w_ref, b_ref, qn_ref, kn_ref, cos_ref, sa_ref, sb_ref, seg_ref,
                q_ref, k_ref, v_ref):
    xn = _rms(h_ref[...], g_ref[...]).astype(BF16)
    qkv = jnp.dot(xn, w_ref[...], preferred_element_type=F32) + b_ref[...]
    cos = cos_ref[...]
    sa = sa_ref[...]
    sb = sb_ref[...]
    seg = seg_ref[...]

    def norm_rope(x, gain):
        ms = _dot_exact_rhs(x * x, seg) * (1.0 / HEAD_DIM)
        xn_ = x * lax.rsqrt(ms + NORM_EPS) * gain
        return (xn_ * cos + pltpu.roll(xn_, LANES - ROPE_DIM // 2, axis=1) * sa
                + pltpu.roll(xn_, ROPE_DIM // 2, axis=1) * sb)

    for c in range(Q_DIM // LANES):
        x = qkv[:, c * LANES:(c + 1) * LANES]
        q_ref[:, c * LANES:(c + 1) * LANES] = (norm_rope(x, qn_ref[...]) * (HEAD_DIM ** -0.5)).astype(BF16)
    for c in range(KV_DIM // LANES):
        x = qkv[:, Q_DIM + c * LANES:Q_DIM + (c + 1) * LANES]
        k_ref[:, c * LANES:(c + 1) * LANES] = norm_rope(x, kn_ref[...]).astype(BF16)
    v_ref[...] = qkv[:, Q_DIM + KV_DIM:].astype(BF16)


def _qkv(h2, g, w, b, qn, kn, cos_f, sin_a, sin_b, seg2):
    t = h2.shape[0]
    tm = _tile(t, 256)
    row = lambda i: (i, 0)
    tab = pl.BlockSpec((tm, LANES), row)
    return pl.pallas_call(
        _qkv_kernel,
        grid=(t // tm,),
        in_specs=[pl.BlockSpec((tm, D_MODEL), row), _full((1, D_MODEL)), _full(w.shape), _full(b.shape),
                  _full(qn.shape), _full(kn.shape), tab, tab, tab, _full(seg2.shape)],
        out_specs=[pl.BlockSpec((tm, Q_DIM), row), pl.BlockSpec((tm, KV_DIM), row),
                   pl.BlockSpec((tm, KV_DIM), row)],
        out_shape=[jax.ShapeDtypeStruct((t, Q_DIM), BF16), jax.ShapeDtypeStruct((t, KV_DIM), BF16),
                   jax.ShapeDtypeStruct((t, KV_DIM), BF16)],
        compiler_params=_params("parallel"),
        name="qkv",
    )(h2, g, w, b, qn, kn, cos_f, sin_a, sin_b, seg2)


def _attn_kernel(h_ref, q_ref, kc_ref, kp_ref, vc_ref, vp_ref, sink_ref, wo_ref, bo_ref, o_ref, *, nqb):
    w = WINDOW
    n = HEAD_DIM
    rows = GQA_GROUP * w
    qi = lax.broadcasted_iota(jnp.int32, (rows, 2 * w), 0) % w
    kj = lax.broadcasted_iota(jnp.int32, (rows, 2 * w), 1)
    diff = w + qi - kj
    band = (diff >= 0) & (diff < w)
    band0 = band & (kj >= jnp.where(pl.program_id(1) == 0, w, 0))
    kall = jnp.concatenate([kp_ref[...], kc_ref[...]], axis=0)
    vall = jnp.concatenate([vp_ref[...], vc_ref[...]], axis=0)
    units = [(qb, g) for qb in range(nqb) for g in range(KV_HEADS)]
    s, e, den = {}, {}, {}
    for u in units:
        qb, g = u
        qs = jnp.concatenate(
            [q_ref[qb * w:(qb + 1) * w, (g * GQA_GROUP + i) * n:(g * GQA_GROUP + i + 1) * n]
             for i in range(GQA_GROUP)], axis=0)
        kk = kall[qb * w:(qb + 2) * w, g * n:(g + 1) * n]
        sc = lax.dot_general(qs, kk, (((1,), (1,)), ((), ())), preferred_element_type=F32)
        s[u] = jnp.where(band0 if qb == 0 else band, sc, -jnp.inf)
    sinks = [jnp.concatenate(
        [jnp.broadcast_to(sink_ref[:, g * GQA_GROUP + i:g * GQA_GROUP + i + 1], (w, LANES))
         for i in range(GQA_GROUP)], axis=0) for g in range(KV_HEADS)]
    ones = jnp.ones((2 * w, LANES), BF16)
    for u in units:
        sink = sinks[u[1]]
        m = jnp.maximum(jnp.broadcast_to(jnp.max(s[u], axis=-1, keepdims=True), (rows, LANES)), sink)
        e[u] = jnp.exp(s[u] - jnp.concatenate([m, m], axis=1)).astype(BF16)
        den[u] = jnp.exp(sink - m)
    outs = [[None] * Q_HEADS for _ in range(nqb)]
    for u in units:
        qb, g = u
        vv = vall[qb * w:(qb + 2) * w, g * n:(g + 1) * n]
        total = jnp.dot(e[u], ones, preferred_element_type=F32) + den[u]
        o = jnp.dot(e[u], vv, preferred_element_type=F32) / total[:, 0:n]
        for i in range(GQA_GROUP):
            outs[qb][g * GQA_GROUP + i] = o[i * w:(i + 1) * w]
    att = jnp.concatenate([jnp.concatenate(outs[qb], axis=1) for qb in range(nqb)], axis=0)
    o_ref[...] = h_ref[...] + _bdot(att, wo_ref[...]) + bo_ref[...]


def _attn(h2, q, k, v, sinks, wo, bo, bsz, seq):
    t = h2.shape[0]
    nqb = 2 if seq % (2 * WINDOW) == 0 else 1
    tq = nqb * WINDOW
    nb = seq // tq
    cur = lambda b, i: (b * nb + i, 0)
    prv = lambda b, i: (b * nb * nqb + jnp.maximum(i * nqb - 1, 0), 0)
    return pl.pallas_call(
        functools.partial(_attn_kernel, nqb=nqb),
        grid=(bsz, nb),
        in_specs=[pl.BlockSpec((tq, D_MODEL), cur), pl.BlockSpec((tq, Q_DIM), cur),
                  pl.BlockSpec((tq, KV_DIM), cur), pl.BlockSpec((WINDOW, KV_DIM), prv),
                  pl.BlockSpec((tq, KV_DIM), cur), pl.BlockSpec((WINDOW, KV_DIM), prv),
                  _full(sinks.shape), _full(wo.shape), _full(bo.shape)],
        out_specs=pl.BlockSpec((tq, D_MODEL), cur),
        out_shape=jax.ShapeDtypeStruct((t, D_MODEL), F32),
        compiler_params=_params("parallel", "parallel"),
        name="attn",
    )(h2, q, k, k, v, v, sinks, wo, bo)


def _router_kernel(h_ref, g_ref, wr_ref, xn_ref, meta_ref, cnt_ref, base_ref):
    @pl.when(pl.program_id(0) == 0)
    def _():
        base_ref[...] = jnp.zeros_like(base_ref)

    xn = _rms(h_ref[...], g_ref[...])
    xn_ref[...] = xn
    hi, lo = _split(xn)
    whi = wr_ref[0]
    wlo = wr_ref[1]
    logits = (jnp.dot(hi, whi, preferred_element_type=F32) + jnp.dot(hi, wlo, preferred_element_type=F32)
              + jnp.dot(lo, whi, preferred_element_type=F32))
    tm = logits.shape[0]
    lane = lax.broadcasted_iota(jnp.int32, logits.shape, 1)
    logits = jnp.where(lane < N_EXPERTS, logits, -jnp.inf)
    m1 = jnp.max(logits, axis=-1, keepdims=True)
    i1 = jnp.min(jnp.where(logits == m1, lane, LANES), axis=-1, keepdims=True)
    rest = jnp.where(lane == i1, -jnp.inf, logits)
    m2 = jnp.max(rest, axis=-1, keepdims=True)
    i2 = jnp.min(jnp.where(rest == m2, lane, LANES), axis=-1, keepdims=True)
    e2 = jnp.exp(m2 - m1)
    w1 = 1.0 / (1.0 + e2)
    w2 = e2 / (1.0 + e2)

    sel1 = lane == i1
    sel2 = lane == i2
    onehot = jnp.where(sel1, 1.0, 0.0) + jnp.where(sel2, 1.0, 0.0)
    ri = lax.broadcasted_iota(jnp.int32, (tm, tm), 0)
    ci = lax.broadcasted_iota(jnp.int32, (tm, tm), 1)
    before = jnp.dot(jnp.where(ci < ri, 1.0, 0.0).astype(BF16), onehot.astype(BF16),
                     preferred_element_type=F32)
    rank = base_ref[...] + before
    r1 = jnp.sum(jnp.where(sel1, rank, 0.0), axis=-1, keepdims=True)
    r2 = jnp.sum(jnp.where(sel2, rank, 0.0), axis=-1, keepdims=True)
    total = base_ref[...] + jnp.sum(onehot, axis=0, keepdims=True)
    base_ref[...] = total
    cnt_ref[...] = total
    cols = (i1.astype(F32), i2.astype(F32), w1, w2, r1, r2)
    meta = jnp.zeros(logits.shape, F32)
    for c, val in enumerate(cols):
        meta = jnp.where(lane == c, val, meta)
    meta_ref[...] = meta


def _router(h2, g, wr):
    t = h2.shape[0]
    tm = _tile(t, 512)
    row = lambda i: (i, 0)
    return pl.pallas_call(
        _router_kernel,
        grid=(t // tm,),
        in_specs=[pl.BlockSpec((tm, D_MODEL), row), _full((1, D_MODEL)), _full(wr.shape)],
        out_specs=[pl.BlockSpec((tm, D_MODEL), row), pl.BlockSpec((tm, LANES), row), _full((1, LANES))],
        out_shape=[jax.ShapeDtypeStruct((t, D_MODEL), F32), jax.ShapeDtypeStruct((t, LANES), F32),
                   jax.ShapeDtypeStruct((1, LANES), F32)],
        scratch_shapes=[pltpu.VMEM((1, LANES), F32)],
        compiler_params=_params("arbitrary"),
        name="router",
    )(h2, g, wr)


def _moe_index_kernel(p1_ref, p2_ref, pad_ref, src_ref, dst_ref, *, ntok):
    for e in range(N_EXPERTS + 1):
        def fill(p, c, e=e):
            src_ref[p] = 0
            dst_ref[p] = 2 * ntok + (e % N_EXPERTS) * MOE_TM + ((p - pad_ref[2 * e]) & (MOE_TM - 1))
            return c

        lax.fori_loop(pad_ref[2 * e], pad_ref[2 * e + 1], fill, 0)

    def body(t, c):
        p1 = p1_ref[t]
        p2 = p2_ref[t]
        src_ref[p1] = t
        dst_ref[p1] = t
        src_ref[p2] = t
        dst_ref[p2] = ntok + t
        return c

    lax.fori_loop(0, ntok, body, 0, unroll=4)


def _moe_index(pos1, pos2, pad, rows):
    smem = pl.BlockSpec(memory_space=pltpu.SMEM)
    return pl.pallas_call(
        functools.partial(_moe_index_kernel, ntok=pos1.shape[0]),
        grid_spec=pltpu.PrefetchScalarGridSpec(
            num_scalar_prefetch=3, grid=(1,), in_specs=[], out_specs=[smem, smem]),
        out_shape=[jax.ShapeDtypeStruct((rows,), jnp.int32)] * 2,
        compiler_params=_params("arbitrary"),
        name="moe_index",
    )(pos1, pos2, pad)


def _row_copy(src_ref, src_row, dst_ref, dst_row, sem):
    return pltpu.make_async_copy(src_ref.at[pl.ds(src_row, 1)], dst_ref.at[pl.ds(dst_row, 1)], sem)


def _experts_kernel(te_ref, nused_ref, src_ref, dst_ref, x_ref, wg_ref, wu_ref, wd_ref, y_ref,
                    xbuf, ybuf, gsem, ssem):
    del te_ref
    i = pl.program_id(0)
    nused = nused_ref[0]
    tm = MOE_TM

    sub = MOE_SUB

    def gather(tile, slot):
        def body(k, c):
            for u in range(sub):
                _row_copy(x_ref, src_ref[tile * tm + k * sub + u], xbuf.at[slot, k], u, gsem.at[slot]).start()
            return c
        lax.fori_loop(0, tm // sub, body, 0)

    def scatter(tile, slot):
        def body(k, c):
            for u in range(sub):
                _row_copy(ybuf.at[slot, k], u, y_ref, dst_ref[tile * tm + k * sub + u], ssem.at[slot]).start()
            return c
        lax.fori_loop(0, tm // sub, body, 0)

    def wait_gather(slot):
        def body(k, c):
            pltpu.make_async_copy(x_ref.at[pl.ds(0, sub)], xbuf.at[slot, k], gsem.at[slot]).wait()
            return c
        lax.fori_loop(0, tm // sub, body, 0)

    def wait_scatter(slot):
        def body(k, c):
            pltpu.make_async_copy(ybuf.at[slot, k], y_ref.at[pl.ds(0, sub)], ssem.at[slot]).wait()
            return c
        lax.fori_loop(0, tm // sub, body, 0)

    @pl.when(i == 0)
    def _():
        gather(0, 0)

    @pl.when(i + 1 < nused)
    def _():
        gather(i + 1, (i + 1) % 2)

    @pl.when(i < nused)
    def _():
        slot = i % 2
        wait_gather(slot)

        @pl.when(i >= 2)
        def _():
            wait_scatter(slot)

        xb = xbuf[slot].reshape(tm, D_MODEL).astype(BF16)
        act = _silu(jnp.dot(xb, wg_ref[0], preferred_element_type=F32)) * jnp.dot(
            xb, wu_ref[0], preferred_element_type=F32)
        y = jnp.dot(act.astype(BF16), wd_ref[0], preferred_element_type=F32)
        ybuf[slot] = y.reshape(tm // sub, sub, D_MODEL)
        scatter(i, slot)

        @pl.when(i == nused - 1)
        def _():
            @pl.when(i >= 1)
            def _():
                wait_scatter(1 - slot)
            wait_scatter(slot)


def _experts(tile_expert, nused, src, dst, xn, wg, wu, wd, out_rows):
    ntile = tile_expert.shape[0]
    tm = MOE_TM
    wsel = lambda i, te, nu, s, d: (te[i], 0, 0)
    any_spec = pl.BlockSpec(memory_space=pl.ANY)
    return pl.pallas_call(
        _experts_kernel,
        grid_spec=pltpu.PrefetchScalarGridSpec(
            num_scalar_prefetch=4, grid=(ntile,),
            in_specs=[any_spec,
                      pl.BlockSpec((1, D_MODEL, FFN_EXPERT), wsel),
                      pl.BlockSpec((1, D_MODEL, FFN_EXPERT), wsel),
                      pl.BlockSpec((1, FFN_EXPERT, D_MODEL), wsel)],
            out_specs=any_spec,
            scratch_shapes=[pltpu.VMEM((2, tm // MOE_SUB, MOE_SUB, D_MODEL), F32),
                            pltpu.VMEM((2, tm // MOE_SUB, MOE_SUB, D_MODEL), F32),
                            pltpu.SemaphoreType.DMA((2,)), pltpu.SemaphoreType.DMA((2,))]),
        out_shape=jax.ShapeDtypeStruct((out_rows, D_MODEL), F32),
        compiler_params=_params("arbitrary"),
        name="moe_experts",
    )(tile_expert, nused, src, dst, xn, wg, wu, wd)


def _combine_kernel(h_ref, meta_ref, y1_ref, y2_ref, o_ref):
    meta = meta_ref[...]
    o_ref[...] = h_ref[...] + meta[:, 2:3] * y1_ref[...] + meta[:, 3:4] * y2_ref[...]


def _combine(h2, meta, ys):
    t = h2.shape[0]
    tm = _tile(t, 512)
    nt = t // tm
    row = lambda i: (i, 0)
    return pl.pallas_call(
        _combine_kernel,
        grid=(nt,),
        in_specs=[pl.BlockSpec((tm, D_MODEL), row), pl.BlockSpec((tm, LANES), row),
                  pl.BlockSpec((tm, D_MODEL), row), pl.BlockSpec((tm, D_MODEL), lambda i: (nt + i, 0))],
        out_specs=pl.BlockSpec((tm, D_MODEL), row),
        out_shape=jax.ShapeDtypeStruct((t, D_MODEL), F32),
        compiler_params=_params("parallel"),
        name="moe_combine",
    )(h2, meta, ys, ys)


def _moe(h2, g, router, wg, wu, wd):
    t = h2.shape[0]
    tm = MOE_TM
    wr = jnp.pad(router, ((0, 0), (0, LANES - N_EXPERTS)))
    wr_hi = wr.astype(BF16)
    wr_lo = (wr - wr_hi.astype(F32)).astype(BF16)
    xn, meta, cnt = _router(h2, g, jnp.stack([wr_hi, wr_lo]))

    e1, e2, r1, r2 = (meta[:, c].astype(jnp.int32) for c in (0, 1, 4, 5))
    count = cnt[0, :N_EXPERTS].astype(jnp.int32)
    ntile = (count + tm - 1) // tm
    tile_end = jnp.cumsum(ntile)
    off = (tile_end - ntile) * tm
    rows = 2 * t + N_EXPERTS * tm
    tile_id = jnp.arange(rows // tm)
    tile_expert = jnp.minimum(
        jnp.sum(tile_id[:, None] >= tile_end[None, :], axis=1), N_EXPERTS - 1).astype(jnp.int32)
    nused = tile_end[-1:].astype(jnp.int32)
    pos1 = off[e1] + r1
    pos2 = off[e2] + r2
    pad = jnp.stack([jnp.append(off + count, tile_end[-1] * tm),
                     jnp.append(off + ntile * tm, rows)], axis=1).reshape(-1).astype(jnp.int32)
    src, dst = _moe_index(pos1, pos2, pad, rows)

    ys = _experts(tile_expert, nused, src, dst, xn, wg, wu, wd, rows)
    return _combine(h2, meta, ys)


def _segment_matrix(width):
    i = jnp.arange(width) // HEAD_DIM
    return (i[:, None] == i[None, :]).astype(BF16)


def _layer0(h2, bsz, seq, norm_mix, w_in, mu_shift, w0, w_decay_up, a0, w_iclr_up, w_gate_up, k_k, k_a, r_k,
            gn_w, gn_b, conv_w, conv_b, dt_bias, a_log, d_skip, ssd_norm, w_out, norm_ffn, ffn_gate, ffn_up,
            ffn_down):
    row = lambda a: a.reshape(1, -1)
    seg = _segment_matrix(RWKV_DIM)
    wa = w_in[:, :RWKV_COLS].astype(BF16)
    wb = jnp.pad(w_in[:, RWKV_COLS:], ((0, 0), (0, SSD_COLS_PAD - SSD_COLS))).astype(BF16)
    pa, pb = _in_proj(h2, row(norm_mix), wa, wb)

    lora = w_decay_up.shape[0]
    wd = jnp.concatenate([w_decay_up, jnp.zeros((LANES - lora, RWKV_DIM), F32)], axis=0).astype(BF16)
    wi = jnp.concatenate([jnp.zeros((LANES - lora, RWKV_DIM), F32), w_iclr_up], axis=0).astype(BF16)
    r, k, v, lw, kkn, b, g, bonus = _rwkv_prep(
        pa, bsz, seq, row(mu_shift), row(w0), wd, row(a0), wi, w_gate_up.astype(BF16), row(k_k), row(k_a),
        row(r_k), seg)
    y = _rwkv_chunk(r, k, v, lw, kkn, b, bsz, seq)

    lane_pad = lambda a: jnp.pad(row(a), ((0, 0), (0, LANES - a.size)))
    ys = _ssd(pb, bsz, seq, conv_w, row(conv_b), lane_pad(dt_bias), lane_pad(a_log), lane_pad(d_skip),
              row(ssd_norm))

    w_out = w_out.astype(BF16)
    h2 = _mix_out(h2, y, bonus, g, ys, row(gn_w), row(gn_b), seg, w_out[:RWKV_DIM], w_out[RWKV_DIM:])
    return _ffn(h2, row(norm_ffn), ffn_gate.astype(BF16), ffn_up.astype(BF16), ffn_down.astype(BF16))


def _layer1(h2, bsz, seq, tables, norm_mix, w_qkv, b_qkv, q_norm, k_norm, sinks, w_o, b_o, norm_ffn, router,
            exp_gate, exp_up, exp_down):
    row = lambda a: a.reshape(1, -1)
    two = lambda a: jnp.tile(a, LANES // HEAD_DIM).reshape(1, LANES)
    cos_f, sin_a, sin_b = tables
    q, k, v = _qkv(h2, row(norm_mix), w_qkv.astype(BF16), row(b_qkv), two(q_norm), two(k_norm), cos_f, sin_a,
                   sin_b, _segment_matrix(LANES))
    h2 = _attn(h2, q, k, v, row(sinks), w_o.astype(BF16), row(b_o), bsz, seq)

    return _moe(h2, row(norm_ffn), router, exp_gate.astype(BF16), exp_up.astype(BF16), exp_down.astype(BF16))


def kernel(x, positions, ev_norm_mix, ev_w_in, ev_mu_shift, ev_w0, ev_w_decay_up, ev_a0, ev_w_iclr_up, ev_w_gate_up, ev_k_k, ev_k_a, ev_r_k, ev_gn_w, ev_gn_b, ev_conv_w, ev_conv_b, ev_dt_bias, ev_a_log, ev_d_skip, ev_ssd_norm, ev_w_out, ev_norm_ffn, ev_ffn_gate, ev_ffn_up, ev_ffn_down, od_norm_mix, od_w_qkv, od_b_qkv, od_q_norm, od_k_norm, od_sinks, od_w_o, od_b_o, od_norm_ffn, od_router, od_exp_gate, od_exp_up, od_exp_down):
    bsz, seq, d = x.shape
    depth = ev_norm_mix.shape[0] + od_norm_mix.shape[0]
    tables = _rope_tables(positions)
    h2 = x.reshape(bsz * seq, d)
    for layer in range(depth):
        i = layer // 2
        if layer % 2 == 0:
            h2 = _layer0(h2, bsz, seq, ev_norm_mix[i], ev_w_in[i], ev_mu_shift[i], ev_w0[i], ev_w_decay_up[i],
                         ev_a0[i], ev_w_iclr_up[i], ev_w_gate_up[i], ev_k_k[i], ev_k_a[i], ev_r_k[i],
                         ev_gn_w[i], ev_gn_b[i], ev_conv_w[i], ev_conv_b[i], ev_dt_bias[i], ev_a_log[i],
                         ev_d_skip[i], ev_ssd_norm[i], ev_w_out[i], ev_norm_ffn[i], ev_ffn_gate[i],
                         ev_ffn_up[i], ev_ffn_down[i])
        else:
            h2 = _layer1(h2, bsz, seq, tables, od_norm_mix[i], od_w_qkv[i], od_b_qkv[i], od_q_norm[i],
                         od_k_norm[i], od_sinks[i], od_w_o[i], od_b_o[i], od_norm_ffn[i], od_router[i],
                         od_exp_gate[i], od_exp_up[i], od_exp_down[i])
    return h2.reshape(bsz, seq, d)
```

```python
import functools
import math

import jax
import jax.numpy as jnp
from jax import lax
from jax.experimental import pallas as pl
from jax.experimental.pallas import tpu as pltpu

F32 = jnp.float32
BF16 = jnp.bfloat16

D_MODEL = 1024
HEAD_DIM = 64
NORM_EPS = 1e-6

RWKV_HEADS = 8
RWKV_DIM = 512
RWKV_COLS = 1792
RWKV_GN_EPS = 64e-5
RWKV_CHUNK = 64
RWKV_GROUP = 256

SSD_HEADS = 8
SSD_DIM = 512
SSD_GROUPS = 2
SSD_STATE = 128
SSD_CONV = 4
SSD_CHUNK = 128
SSD_XBC = 1024
SSD_COLS = 1544
SSD_COLS_PAD = 1664

Q_HEADS = 16
KV_HEADS = 4
GQA_GROUP = 4
Q_DIM = 1024
KV_DIM = 256
WINDOW = 128
ROPE_THETA = 500000.0
ROPE_DIM = 16

FFN_DENSE = 2816
N_EXPERTS = 8
FFN_EXPERT = 1408
MOE_TM = 256
MOE_SUB = 8

LANES = 128
VMEM_LIMIT_BYTES = 56 * 1024 * 1024


def _params(*sem):
    return pltpu.CompilerParams(dimension_semantics=sem, vmem_limit_bytes=VMEM_LIMIT_BYTES)


def _bdot(a, b):
    return jnp.dot(a.astype(BF16), b.astype(BF16), preferred_element_type=F32)


def _bdot_nt(a, b):
    return lax.dot_general(a.astype(BF16), b.astype(BF16), (((1,), (1,)), ((), ())),
                           preferred_element_type=F32)


def _bdot_tn(a, b):
    return lax.dot_general(a.astype(BF16), b.astype(BF16), (((0,), (0,)), ((), ())),
                           preferred_element_type=F32)


def _split(x):
    hi = x.astype(BF16)
    lo = (x - hi.astype(F32)).astype(BF16)
    return hi, lo


def _dot_exact_rhs(x, m):
    hi, lo = _split(x)
    return (jnp.dot(hi, m, preferred_element_type=F32) + jnp.dot(lo, m, preferred_element_type=F32))


def _dot_exact_lhs(m, x):
    hi, lo = _split(x)
    return (jnp.dot(m, hi, preferred_element_type=F32) + jnp.dot(m, lo, preferred_element_type=F32))


def _sigmoid(x):
    return 1.0 / (1.0 + jnp.exp(-x))


def _silu(x):
    return x * _sigmoid(x)


def _softplus(x):
    return jnp.maximum(x, 0.0) + jnp.log(1.0 + jnp.exp(-jnp.abs(x)))


def _rms(x, g):
    ms = jnp.mean(x * x, axis=-1, keepdims=True)
    return x * lax.rsqrt(ms + NORM_EPS) * g


def _tile(n, pref):
    t = min(n, pref)
    while n % t:
        t //= 2
    return t


def _full(shape):
    nd = len(shape)
    return pl.BlockSpec(shape, lambda *_: (0,) * nd)


def _resident(shape):
    nd = len(shape)
    return pl.BlockSpec(shape, lambda *_: (0,) * nd, pipeline_mode=pl.Buffered(1))


def _in_proj_kernel(x_ref, g_ref, wa_ref, wb_ref, mu_ref, w0_ref, wd_ref, a0_ref, wi_ref, wg_ref, kk_ref, ka_ref,
                    rk_ref, seg_ref, pb_ref, r_ref, k_ref, v_ref, lw_ref, kkn_ref, b_ref, gate_ref, bonus_ref,
                    carry_ref):
    @pl.when(pl.program_id(1) == 0)
    def _():
        carry_ref[...] = jnp.zeros_like(carry_ref)

    xn = _rms(x_ref[...], g_ref[...]).astype(BF16)
    pb_ref[...] = jnp.dot(xn, wb_ref[...], preferred_element_type=F32)
    pa = jnp.dot(xn, wa_ref[...], preferred_element_type=F32)
    tm = pa.shape[0]
    row = lax.broadcasted_iota(jnp.int32, pa.shape, 0)
    prev = jnp.where(row == 0, carry_ref[...], pltpu.roll(pa, 1, axis=0))
    carry_ref[...] = pa[tm - 1:tm, :]
    x = pa + (prev - pa) * mu_ref[...]

    r = x[:, 0:RWKV_DIM]
    k = x[:, RWKV_DIM:2 * RWKV_DIM]
    v = x[:, 2 * RWKV_DIM:3 * RWKV_DIM]
    lora = x[:, 3 * RWKV_DIM:3 * RWKV_DIM + LANES]
    gl = x[:, 3 * RWKV_DIM + LANES:]
    seg = seg_ref[...]

    w_raw = w0_ref[...] + _bdot(jnp.tanh(lora), wd_ref[...])
    lw_ref[...] = (-math.exp(-0.5)) * _sigmoid(w_raw)
    iclr = _sigmoid(a0_ref[...] + _bdot(lora, wi_ref[...]))
    gate_ref[...] = _bdot(_sigmoid(gl), wg_ref[...])

    kk = k * kk_ref[...]
    kkn = kk * lax.rsqrt(_dot_exact_rhs(kk * kk, seg) + 1e-12)
    k2 = k * (1.0 + (iclr - 1.0) * ka_ref[...])
    r_ref[...] = r
    k_ref[...] = k2
    v_ref[...] = v
    kkn_ref[...] = kkn
    b_ref[...] = kkn * iclr
    bonus_ref[...] = _dot_exact_rhs(r * k2 * rk_ref[...], seg) * v


def _in_proj(x2, bsz, seq, g, wa, wb, mu, w0, wd, a0, wi, wg, k_k, k_a, r_k, seg):
    t = x2.shape[0]
    tm = _tile(seq, 256)
    nt = seq // tm
    row = lambda b, i: (b * nt + i, 0)
    out = jax.ShapeDtypeStruct((t, RWKV_DIM), F32)
    small = [mu, w0, wd, a0, wi, wg, k_k, k_a, r_k, seg]
    return pl.pallas_call(
        _in_proj_kernel,
        grid=(bsz, nt),
        in_specs=[pl.BlockSpec((tm, D_MODEL), row), _full((1, D_MODEL)), _resident(wa.shape),
                  _resident(wb.shape)] + [_full(a.shape) for a in small],
        out_specs=[pl.BlockSpec((tm, SSD_COLS_PAD), row)] + [pl.BlockSpec((tm, RWKV_DIM), row)] * 8,
        out_shape=[jax.ShapeDtypeStruct((t, SSD_COLS_PAD), F32)] + [out] * 8,
        scratch_shapes=[pltpu.VMEM((1, RWKV_COLS), F32)],
        compiler_params=_params("arbitrary", "arbitrary"),
        name="in_proj",
    )(x2, g, wa, wb, *small)


def _rwkv_chunk_kernel(r_ref, k_ref, v_ref, lw_ref, kk_ref, b_ref, y_ref, z_ref, *, lt):
    c_len = RWKV_CHUNK
    n = HEAD_DIM
    gl = RWKV_GROUP
    nchunk = gl // c_len
    nheads = LANES // n
    shift = c_len.bit_length() - 1

    @pl.when(pl.program_id(2) == 0)
    def _():
        z_ref[...] = jnp.zeros_like(z_ref)

    ri = lax.broadcasted_iota(jnp.int32, (gl, gl), 0)
    ci = lax.broadcasted_iota(jnp.int32, (gl, gl), 1)
    tri_bd = jnp.where((ci <= ri) & (ci >= ((ri >> shift) << shift)), 1.0, 0.0).astype(BF16)
    re_ = lax.broadcasted_iota(jnp.int32, (c_len, c_len), 0)
    ce_ = lax.broadcasted_iota(jnp.int32, (c_len, c_len), 1)
    eye = re_ == ce_
    ri2 = lax.broadcasted_iota(jnp.int32, (2 * gl, gl), 0)
    ci2 = lax.broadcasted_iota(jnp.int32, (2 * gl, gl), 1)
    t2 = jnp.where(ri2 < gl, ri2, ri2 - gl)
    mask2 = (ci2 <= jnp.where(ri2 < gl, t2 - 1, t2)) & (ci2 >= ((t2 >> shift) << shift))
    zeros = jnp.zeros((c_len, n), F32)

    nsub = lt // gl
    sysid = [(s, h) for s in range(nsub) for h in range(nheads)]
    pre = []
    for s in range(nsub):
        sl = pl.ds(s * gl, gl)
        lw = lw_ref[sl, :]
        g_in = _dot_exact_lhs(tri_bd, lw)
        ends = [g_in[(c + 1) * c_len - 1:(c + 1) * c_len, :] for c in range(nchunk)]
        g_end = jnp.concatenate([jnp.broadcast_to(e, (c_len, LANES)) for e in ends], axis=0)
        e_end = jnp.exp(g_end - g_in)
        en = jnp.exp(-g_in)
        k = k_ref[sl, :]
        b = b_ref[sl, :]
        pre.append(dict(
            ends=ends, v=v_ref[sl, :], rt=r_ref[sl, :] * jnp.exp(g_in),
            at=-kk_ref[sl, :] * jnp.exp(g_in - lw), kt=k * en, bt=b * en, bend=b * e_end, kend=k * e_end))

    def hs(name, s, h):
        return pre[s][name][:, h * n:(h + 1) * n]

    xb, xk, xkv, p, x = {}, {}, {}, {}, {}
    for q in sysid:
        la = jnp.concatenate([hs("at", *q), hs("rt", *q)], axis=0)
        xb[q] = jnp.where(mask2, _bdot_nt(la, hs("bt", *q)), 0.0)
        xk[q] = jnp.where(mask2, _bdot_nt(la, hs("kt", *q)), 0.0)
    for q in sysid:
        xkv[q] = _bdot(xk[q], hs("v", *q))
        p[q] = xb[q][0:gl]
        x[q] = jnp.concatenate([hs("at", *q), xkv[q][0:gl]], axis=1)
    for i in range(6):
        for q in sysid:
            x[q] = x[q] + _bdot(p[q], x[q])
        if i < 5:
            for q in sysid:
                p[q] = _bdot(p[q], p[q])
    rq, y0, mn = {}, {}, {}
    for q in sysid:
        yy = _bdot(xb[q][gl:], x[q])
        rq[q] = yy[:, 0:n] + hs("rt", *q)
        y0[q] = yy[:, n:] + xkv[q][gl:]
        bend_h, kend_h, v_h = hs("bend", *q), hs("kend", *q), hs("v", *q)
        for c in range(nchunk):
            rows = slice(c * c_len, (c + 1) * c_len)
            lhs = jnp.concatenate([bend_h[rows], kend_h[rows]], axis=0)
            rhs = jnp.concatenate([x[q][rows], jnp.concatenate([zeros, v_h[rows]], axis=1)], axis=0)
            mn[q + (c,)] = _bdot_tn(lhs, rhs)
    zs = [z_ref[h] for h in range(nheads)]
    for s in range(nsub):
        yh = [[] for _ in range(nheads)]
        for c in range(nchunk):
            rows = slice(c * c_len, (c + 1) * c_len)
            for h in range(nheads):
                q = (s, h)
                pend = jnp.exp(pre[s]["ends"][c][:, h * n:(h + 1) * n])
                m_mat = mn[q + (c,)][:, 0:n] + jnp.where(eye, pend, 0.0)
                yh[h].append(_bdot(rq[q][rows], zs[h]) + y0[q][rows])
                zs[h] = _bdot(m_mat, zs[h]) + mn[q + (c,)][:, n:]
        y_ref[pl.ds(s * gl, gl), :] = jnp.concatenate(
            [jnp.concatenate(yh[h], axis=0) for h in range(nheads)], axis=1)
    for h in range(nheads):
        z_ref[h] = zs[h]


def _rwkv_chunk(r, k, v, lw, kkn, b, bsz, seq):
    t = r.shape[0]
    lt = _tile(seq, 512)
    nt = seq // lt
    spec = pl.BlockSpec((lt, LANES), lambda bi, hp, i: (bi * nt + i, hp))
    return pl.pallas_call(
        functools.partial(_rwkv_chunk_kernel, lt=lt),
        grid=(bsz, RWKV_DIM // LANES, nt),
        in_specs=[spec] * 6,
        out_specs=spec,
        out_shape=jax.ShapeDtypeStruct((t, RWKV_DIM), F32),
        scratch_shapes=[pltpu.VMEM((2, HEAD_DIM, HEAD_DIM), F32)],
        compiler_params=_params("arbitrary", "arbitrary", "arbitrary"),
        name="rwkv_chunk",
    )(r, k, v, lw, kkn, b)


def _ssd_kernel(pb_ref, cw_ref, cb_ref, dtb_ref, alog_ref, dskip_ref, nrm_ref, ys_ref, ext_ref, st_ref):
    q = SSD_CHUNK
    p = HEAD_DIM
    hpg = SSD_HEADS // SSD_GROUPS

    @pl.when(pl.program_id(1) == 0)
    def _():
        ext_ref[0:8, :] = jnp.zeros((8, SSD_XBC), F32)
        st_ref[...] = jnp.zeros_like(st_ref)

    z = pb_ref[:, 0:SSD_DIM]
    u = pb_ref[:, SSD_DIM:SSD_DIM + SSD_XBC]
    dt_raw = pb_ref[:, SSD_DIM + SSD_XBC:]

    ext_ref[8:8 + q, :] = u
    conv = cb_ref[...] + cw_ref[SSD_CONV - 1:SSD_CONV, :] * u
    for j in range(SSD_CONV - 1):
        off = 8 - (SSD_CONV - 1) + j
        conv = conv + cw_ref[j:j + 1, :] * ext_ref[off:off + q, :]
    ext_ref[0:8, :] = u[q - 8:q, :]
    xbc = _silu(conv)
    xs = xbc[:, 0:SSD_DIM]
    bm = xbc[:, SSD_DIM:SSD_DIM + SSD_GROUPS * SSD_STATE]
    cm = xbc[:, SSD_DIM + SSD_GROUPS * SSD_STATE:]

    dt = _softplus(dt_raw + dtb_ref[...])
    a = -jnp.exp(alog_ref[...])
    ri = lax.broadcasted_iota(jnp.int32, (q, q), 0)
    ci = lax.broadcasted_iota(jnp.int32, (q, q), 1)
    causal = ri >= ci
    cum = _dot_exact_lhs(causal.astype(BF16), dt * a)
    cum_t = cum.T
    dt_t = dt.T
    cum_end = cum[q - 1:q, :]
    to_end = jnp.exp(cum_end - cum) * dt
    ecum = jnp.exp(cum)
    edec = jnp.exp(cum_end)

    ys = []
    for g in range(SSD_GROUPS):
        bm_g = bm[:, g * SSD_STATE:(g + 1) * SSD_STATE]
        cm_g = cm[:, g * SSD_STATE:(g + 1) * SSD_STATE]
        cb = _bdot_nt(cm_g, bm_g)
        bm_t = bm_g.T
        for hh in range(hpg):
            h = g * hpg + hh
            x_h = xs[:, h * p:(h + 1) * p]
            seg = cum[:, h:h + 1] - cum_t[h:h + 1, :]
            ldec = jnp.exp(jnp.where(causal, seg, -jnp.inf))
            wts = cb * ldec * dt_t[h:h + 1, :]
            y = _bdot(wts, x_h)
            h_prev = st_ref[h]
            y = y + _bdot(cm_g, h_prev) * ecum[:, h:h + 1]
            st_ref[h] = h_prev * edec[:, h:h + 1] + _bdot(bm_t, x_h * to_end[:, h:h + 1])
            ys.append(y + dskip_ref[:, h:h + 1] * x_h)
    yall = jnp.concatenate(ys, axis=1) * _silu(z)
    gw = SSD_DIM // SSD_GROUPS
    outs = []
    for g in range(SSD_GROUPS):
        yg = yall[:, g * gw:(g + 1) * gw]
        ms = jnp.mean(yg * yg, axis=-1, keepdims=True)
        outs.append(yg * lax.rsqrt(ms + NORM_EPS) * nrm_ref[:, g * gw:(g + 1) * gw])
    ys_ref[...] = jnp.concatenate(outs, axis=1)


def _ssd(pb, bsz, seq, conv_w, conv_b, dt_bias, a_log, d_skip, ssd_norm):
    t = pb.shape[0]
    nc = seq // SSD_CHUNK
    row = lambda b, i: (b * nc + i, 0)
    return pl.pallas_call(
        _ssd_kernel,
        grid=(bsz, nc),
        in_specs=[pl.BlockSpec((SSD_CHUNK, SSD_COLS_PAD), row), _full(conv_w.shape), _full(conv_b.shape),
                  _full(dt_bias.shape), _full(a_log.shape), _full(d_skip.shape), _full(ssd_norm.shape)],
        out_specs=pl.BlockSpec((SSD_CHUNK, SSD_DIM), row),
        out_shape=jax.ShapeDtypeStruct((t, SSD_DIM), F32),
        scratch_shapes=[pltpu.VMEM((8 + SSD_CHUNK, SSD_XBC), F32),
                        pltpu.VMEM((SSD_HEADS, SSD_STATE, HEAD_DIM), F32)],
        compiler_params=_params("arbitrary", "arbitrary"),
        name="ssd",
    )(pb, conv_w, conv_b, dt_bias, a_log, d_skip, ssd_norm)


def _mix_ffn_kernel(h_ref, y_ref, bonus_ref, gate_ref, ys_ref, gnw_ref, gnb_ref, seg_ref, wa_ref, wb_ref,
                    g_ref, wg_ref, wu_ref, wd_ref, o_ref):
    y = y_ref[...]
    seg = seg_ref[...]
    mu = _dot_exact_rhs(y, seg) * (1.0 / HEAD_DIM)
    yc = y - mu
    var = _dot_exact_rhs(yc * yc, seg) * (1.0 / HEAD_DIM)
    yn = yc * lax.rsqrt(var + RWKV_GN_EPS) * gnw_ref[...] + gnb_ref[...]
    ya = (yn + bonus_ref[...]) * gate_ref[...]
    h = h_ref[...] + _bdot(ya, wa_ref[...]) + _bdot(ys_ref[...], wb_ref[...])

    xn = _rms(h, g_ref[...]).astype(BF16)
    act = _silu(jnp.dot(xn, wg_ref[...], preferred_element_type=F32)) * jnp.dot(
        xn, wu_ref[...], preferred_element_type=F32)
    o_ref[...] = h + jnp.dot(act.astype(BF16), wd_ref[...], preferred_element_type=F32)


def _mix_ffn(h2, y, bonus, gate, ys, gn_w, gn_b, seg, wa, wb, g, wg, wu, wd):
    t = h2.shape[0]
    tm = _tile(t, 512)
    row = lambda i: (i, 0)
    half = pl.BlockSpec((tm, RWKV_DIM), row)
    return pl.pallas_call(
        _mix_ffn_kernel,
        grid=(t // tm,),
        in_specs=[pl.BlockSpec((tm, D_MODEL), row), half, half, half, half, _full(gn_w.shape),
                  _full(gn_b.shape), _resident(seg.shape), _resident(wa.shape), _resident(wb.shape),
                  _full((1, D_MODEL)), _resident(wg.shape), _resident(wu.shape), _resident(wd.shape)],
        out_specs=pl.BlockSpec((tm, D_MODEL), row),
        out_shape=jax.ShapeDtypeStruct((t, D_MODEL), F32),
        compiler_params=_params("parallel"),
        name="mix_ffn",
    )(h2, y, bonus, gate, ys, gn_w, gn_b, seg, wa, wb, g, wg, wu, wd)


def _rope_table_kernel(pos_ref, freq_ref, cos_ref, sin_ref):
    ang = pos_ref[...] * freq_ref[...]
    cos_ref[...] = jnp.cos(ang)
    sin_ref[...] = jnp.sin(ang)


def _rope_tables(positions):
    t = positions.size
    half = ROPE_DIM // 2
    rows = t * half // LANES
    inv_freq = ROPE_THETA ** (-jnp.arange(0, ROPE_DIM, 2, dtype=F32) / ROPE_DIM)
    pos_rep = jnp.repeat(positions.reshape(-1).astype(F32), half).reshape(rows, LANES)
    freq = jnp.tile(inv_freq, LANES // half).reshape(1, LANES)
    tr = _tile(rows, 256)
    cos, sin = pl.pallas_call(
        _rope_table_kernel,
        grid=(rows // tr,),
        in_specs=[pl.BlockSpec((tr, LANES), lambda i: (i, 0)), _full((1, LANES))],
        out_specs=[pl.BlockSpec((tr, LANES), lambda i: (i, 0))] * 2,
        out_shape=[jax.ShapeDtypeStruct((rows, LANES), F32)] * 2,
        compiler_params=_params("parallel"),
        name="rope_tables",
    )(pos_rep, freq)
    cos = cos.reshape(t, half)
    sin = sin.reshape(t, half)
    pad = HEAD_DIM - ROPE_DIM
    one = jnp.ones((t, pad), F32)
    zero = jnp.zeros((t, pad), F32)
    zh = jnp.zeros((t, half), F32)
    cos_f = jnp.concatenate([cos, cos, one], axis=1)
    sin_a = jnp.concatenate([-sin, zh, zero], axis=1)
    sin_b = jnp.concatenate([zh, sin, zero], axis=1)
    return tuple(jnp.tile(m, (1, LANES // HEAD_DIM)) for m in (cos_f, sin_a, sin_b))


def _qkv_kernel(h_ref, g_ref, w_ref, b_ref, qn_ref, kn_ref, cos_ref, sa_ref, sb_ref, seg_ref,
                q_ref, k_ref, v_ref):
    xn = _rms(h_ref[...], g_ref[...]).astype(BF16)
    qkv = jnp.dot(xn, w_ref[...], preferred_element_type=F32) + b_ref[...]
    cos = cos_ref[...]
    sa = sa_ref[...]
    sb = sb_ref[...]
    seg = seg_ref[...]

    def norm_rope(x, gain):
        ms = _dot_exact_rhs(x * x, seg) * (1.0 / HEAD_DIM)
        xn_ = x * lax.rsqrt(ms + NORM_EPS) * gain
        return (xn_ * cos + pltpu.roll(xn_, LANES - ROPE_DIM // 2, axis=1) * sa
                + pltpu.roll(xn_, ROPE_DIM // 2, axis=1) * sb)

    for c in range(Q_DIM // LANES):
        x = qkv[:, c * LANES:(c + 1) * LANES]
        q_ref[:, c * LANES:(c + 1) * LANES] = (norm_rope(x, qn_ref[...]) * (HEAD_DIM ** -0.5)).astype(BF16)
    for c in range(KV_DIM // LANES):
        x = qkv[:, Q_DIM + c * LANES:Q_DIM + (c + 1) * LANES]
        k_ref[:, c * LANES:(c + 1) * LANES] = norm_rope(x, kn_ref[...]).astype(BF16)
    v_ref[...] = qkv[:, Q_DIM + KV_DIM:].astype(BF16)


def _qkv(h2, g, w, b, qn, kn, cos_f, sin_a, sin_b, seg2):
    t = h2.shape[0]
    tm = _tile(t, 256)
    row = lambda i: (i, 0)
    tab = pl.BlockSpec((tm, LANES), row)
    return pl.pallas_call(
        _qkv_kernel,
        grid=(t // tm,),
        in_specs=[pl.BlockSpec((tm, D_MODEL), row), _full((1, D_MODEL)), _full(w.shape), _full(b.shape),
                  _full(qn.shape), _full(kn.shape), tab, tab, tab, _full(seg2.shape)],
        out_specs=[pl.BlockSpec((tm, Q_DIM), row), pl.BlockSpec((tm, KV_DIM), row),
                   pl.BlockSpec((tm, KV_DIM), row)],
        out_shape=[jax.ShapeDtypeStruct((t, Q_DIM), BF16), jax.ShapeDtypeStruct((t, KV_DIM), BF16),
                   jax.ShapeDtypeStruct((t, KV_DIM), BF16)],
        compiler_params=_params("parallel"),
        name="qkv",
    )(h2, g, w, b, qn, kn, cos_f, sin_a, sin_b, seg2)


def _attn_kernel(h_ref, q_ref, kc_ref, kp_ref, vc_ref, vp_ref, sink_ref, wo_ref, bo_ref, o_ref, *, nqb):
    w = WINDOW
    n = HEAD_DIM
    rows = GQA_GROUP * w
    qi = lax.broadcasted_iota(jnp.int32, (rows, 2 * w), 0) % w
    kj = lax.broadcasted_iota(jnp.int32, (rows, 2 * w), 1)
    diff = w + qi - kj
    band = (diff >= 0) & (diff < w)
    band0 = band & (kj >= jnp.where(pl.program_id(1) == 0, w, 0))
    kall = jnp.concatenate([kp_ref[...], kc_ref[...]], axis=0)
    vall = jnp.concatenate([vp_ref[...], vc_ref[...]], axis=0)
    units = [(qb, g) for qb in range(nqb) for g in range(KV_HEADS)]
    s, e, den = {}, {}, {}
    for u in units:
        qb, g = u
        qs = jnp.concatenate(
            [q_ref[qb * w:(qb + 1) * w, (g * GQA_GROUP + i) * n:(g * GQA_GROUP + i + 1) * n]
             for i in range(GQA_GROUP)], axis=0)
        kk = kall[qb * w:(qb + 2) * w, g * n:(g + 1) * n]
        sc = lax.dot_general(qs, kk, (((1,), (1,)), ((), ())), preferred_element_type=F32)
        s[u] = jnp.where(band0 if qb == 0 else band, sc, -jnp.inf)
    sinks = [jnp.concatenate(
        [jnp.broadcast_to(sink_ref[:, g * GQA_GROUP + i:g * GQA_GROUP + i + 1], (w, LANES))
         for i in range(GQA_GROUP)], axis=0) for g in range(KV_HEADS)]
    ones = jnp.ones((2 * w, LANES), BF16)
    for u in units:
        sink = sinks[u[1]]
        m = jnp.maximum(jnp.broadcast_to(jnp.max(s[u], axis=-1, keepdims=True), (rows, LANES)), sink)
        e[u] = jnp.exp(s[u] - jnp.concatenate([m, m], axis=1)).astype(BF16)
        den[u] = jnp.exp(sink - m)
    outs = [[None] * Q_HEADS for _ in range(nqb)]
    for u in units:
        qb, g = u
        vv = vall[qb * w:(qb + 2) * w, g * n:(g + 1) * n]
        total = jnp.dot(e[u], ones, preferred_element_type=F32) + den[u]
        o = jnp.dot(e[u], vv, preferred_element_type=F32) / total[:, 0:n]
        for i in range(GQA_GROUP):
            outs[qb][g * GQA_GROUP + i] = o[i * w:(i + 1) * w]
    att = jnp.concatenate([jnp.concatenate(outs[qb], axis=1) for qb in range(nqb)], axis=0)
    o_ref[...] = h_ref[...] + _bdot(att, wo_ref[...]) + bo_ref[...]


def _attn(h2, q, k, v, sinks, wo, bo, bsz, seq):
    t = h2.shape[0]
    nqb = 2 if seq % (2 * WINDOW) == 0 else 1
    tq = nqb * WINDOW
    nb = seq // tq
    cur = lambda b, i: (b * nb + i, 0)
    prv = lambda b, i: (b * nb * nqb + jnp.maximum(i * nqb - 1, 0), 0)
    return pl.pallas_call(
        functools.partial(_attn_kernel, nqb=nqb),
        grid=(bsz, nb),
        in_specs=[pl.BlockSpec((tq, D_MODEL), cur), pl.BlockSpec((tq, Q_DIM), cur),
                  pl.BlockSpec((tq, KV_DIM), cur), pl.BlockSpec((WINDOW, KV_DIM), prv),
                  pl.BlockSpec((tq, KV_DIM), cur), pl.BlockSpec((WINDOW, KV_DIM), prv),
                  _full(sinks.shape), _full(wo.shape), _full(bo.shape)],
        out_specs=pl.BlockSpec((tq, D_MODEL), cur),
        out_shape=jax.ShapeDtypeStruct((t, D_MODEL), F32),
        compiler_params=_params("parallel", "parallel"),
        name="attn",
    )(h2, q, k, k, v, v, sinks, wo, bo)


def _router_kernel(h_ref, g_ref, wr_ref, xn_ref, meta_ref, cnt_ref, base_ref):
    @pl.when(pl.program_id(0) == 0)
    def _():
        base_ref[...] = jnp.zeros_like(base_ref)

    xn = _rms(h_ref[...], g_ref[...])
    xn_ref[...] = xn
    hi, lo = _split(xn)
    whi = wr_ref[0]
    wlo = wr_ref[1]
    logits = (jnp.dot(hi, whi, preferred_element_type=F32) + jnp.dot(hi, wlo, preferred_element_type=F32)
              + jnp.dot(lo, whi, preferred_element_type=F32))
    tm = logits.shape[0]
    lane = lax.broadcasted_iota(jnp.int32, logits.shape, 1)
    logits = jnp.where(lane < N_EXPERTS, logits, -jnp.inf)
    m1 = jnp.max(logits, axis=-1, keepdims=True)
    i1 = jnp.min(jnp.where(logits == m1, lane, LANES), axis=-1, keepdims=True)
    rest = jnp.where(lane == i1, -jnp.inf, logits)
    m2 = jnp.max(rest, axis=-1, keepdims=True)
    i2 = jnp.min(jnp.where(rest == m2, lane, LANES), axis=-1, keepdims=True)
    e2 = jnp.exp(m2 - m1)
    w1 = 1.0 / (1.0 + e2)
    w2 = e2 / (1.0 + e2)

    sel1 = lane == i1
    sel2 = lane == i2
    onehot = jnp.where(sel1, 1.0, 0.0) + jnp.where(sel2, 1.0, 0.0)
    ri = lax.broadcasted_iota(jnp.int32, (tm, tm), 0)
    ci = lax.broadcasted_iota(jnp.int32, (tm, tm), 1)
    before = jnp.dot(jnp.where(ci < ri, 1.0, 0.0).astype(BF16), onehot.astype(BF16),
                     preferred_element_type=F32)
    rank = base_ref[...] + before
    r1 = jnp.sum(jnp.where(sel1, rank, 0.0), axis=-1, keepdims=True)
    r2 = jnp.sum(jnp.where(sel2, rank, 0.0), axis=-1, keepdims=True)
    total = base_ref[...] + jnp.sum(onehot, axis=0, keepdims=True)
    base_ref[...] = total
    cnt_ref[...] = total
    cols = (i1.astype(F32), i2.astype(F32), w1, w2, r1, r2)
    meta = jnp.zeros(logits.shape, F32)
    for c, val in enumerate(cols):
        meta = jnp.where(lane == c, val, meta)
    meta_ref[...] = meta


def _router(h2, g, wr):
    t = h2.shape[0]
    tm = _tile(t, 512)
    row = lambda i: (i, 0)
    return pl.pallas_call(
        _router_kernel,
        grid=(t // tm,),
        in_specs=[pl.BlockSpec((tm, D_MODEL), row), _full((1, D_MODEL)), _full(wr.shape)],
        out_specs=[pl.BlockSpec((tm, D_MODEL), row), pl.BlockSpec((tm, LANES), row), _full((1, LANES))],
        out_shape=[jax.ShapeDtypeStruct((t, D_MODEL), F32), jax.ShapeDtypeStruct((t, LANES), F32),
                   jax.ShapeDtypeStruct((1, LANES), F32)],
        scratch_shapes=[pltpu.VMEM((1, LANES), F32)],
        compiler_params=_params("arbitrary"),
        name="router",
    )(h2, g, wr)


def _moe_index_kernel(p1_ref, p2_ref, pad_ref, src_ref, dst_ref, *, ntok):
    for e in range(N_EXPERTS + 1):
        def fill(p, c, e=e):
            src_ref[p] = 0
            dst_ref[p] = 2 * ntok + (e % N_EXPERTS) * MOE_TM + ((p - pad_ref[2 * e]) & (MOE_TM - 1))
            return c

        lax.fori_loop(pad_ref[2 * e], pad_ref[2 * e + 1], fill, 0)

    def body(t, c):
        p1 = p1_ref[t]
        p2 = p2_ref[t]
        src_ref[p1] = t
        dst_ref[p1] = t
        src_ref[p2] = t
        dst_ref[p2] = ntok + t
        return c

    lax.fori_loop(0, ntok, body, 0, unroll=4)


def _moe_index(pos1, pos2, pad, rows):
    smem = pl.BlockSpec(memory_space=pltpu.SMEM)
    return pl.pallas_call(
        functools.partial(_moe_index_kernel, ntok=pos1.shape[0]),
        grid_spec=pltpu.PrefetchScalarGridSpec(
            num_scalar_prefetch=3, grid=(1,), in_specs=[], out_specs=[smem, smem]),
        out_shape=[jax.ShapeDtypeStruct((rows,), jnp.int32)] * 2,
        compiler_params=_params("arbitrary"),
        name="moe_index",
    )(pos1, pos2, pad)


def _row_copy(src_ref, src_row, dst_ref, dst_row, sem):
    return pltpu.make_async_copy(src_ref.at[pl.ds(src_row, 1)], dst_ref.at[pl.ds(dst_row, 1)], sem)


def _experts_kernel(te_ref, nused_ref, src_ref, dst_ref, x_ref, wg_ref, wu_ref, wd_ref, y_ref,
                    xbuf, ybuf, gsem, ssem):
    del te_ref
    i = pl.program_id(0)
    nused = nused_ref[0]
    tm = MOE_TM

    sub = MOE_SUB

    def gather(tile, slot):
        def body(k, c):
            for u in range(sub):
                _row_copy(x_ref, src_ref[tile * tm + k * sub + u], xbuf.at[slot, k], u,
                          gsem.at[slot]).start(priority=u % 2)
            return c
        lax.fori_loop(0, tm // sub, body, 0)

    def scatter(tile, slot):
        def body(k, c):
            for u in range(sub):
                _row_copy(ybuf.at[slot, k], u, y_ref, dst_ref[tile * tm + k * sub + u],
                          ssem.at[slot]).start(priority=u % 2)
            return c
        lax.fori_loop(0, tm // sub, body, 0)

    def wait_gather(slot):
        def body(k, c):
            pltpu.make_async_copy(x_ref.at[pl.ds(0, sub)], xbuf.at[slot, k], gsem.at[slot]).wait()
            return c
        lax.fori_loop(0, tm // sub, body, 0)

    def wait_scatter(slot):
        def body(k, c):
            pltpu.make_async_copy(ybuf.at[slot, k], y_ref.at[pl.ds(0, sub)], ssem.at[slot]).wait()
            return c
        lax.fori_loop(0, tm // sub, body, 0)

    @pl.when(i == 0)
    def _():
        gather(0, 0)

    @pl.when(i + 1 < nused)
    def _():
        gather(i + 1, (i + 1) % 2)

    @pl.when(i < nused)
    def _():
        slot = i % 2
        wait_gather(slot)

        @pl.when(i >= 2)
        def _():
            wait_scatter(slot)

        xb = xbuf[slot].reshape(tm, D_MODEL).astype(BF16)
        act = _silu(jnp.dot(xb, wg_ref[0], preferred_element_type=F32)) * jnp.dot(
            xb, wu_ref[0], preferred_element_type=F32)
        y = jnp.dot(act.astype(BF16), wd_ref[0], preferred_element_type=F32)
        ybuf[slot] = y.reshape(tm // sub, sub, D_MODEL)
        scatter(i, slot)

        @pl.when(i == nused - 1)
        def _():
            @pl.when(i >= 1)
            def _():
                wait_scatter(1 - slot)
            wait_scatter(slot)


def _experts(tile_expert, nused, src, dst, xn, wg, wu, wd, out_rows):
    ntile = tile_expert.shape[0]
    tm = MOE_TM
    wsel = lambda i, te, nu, s, d: (te[i], 0, 0)
    any_spec = pl.BlockSpec(memory_space=pl.ANY)
    return pl.pallas_call(
        _experts_kernel,
        grid_spec=pltpu.PrefetchScalarGridSpec(
            num_scalar_prefetch=4, grid=(ntile,),
            in_specs=[any_spec,
                      pl.BlockSpec((1, D_MODEL, FFN_EXPERT), wsel),
                      pl.BlockSpec((1, D_MODEL, FFN_EXPERT), wsel),
                      pl.BlockSpec((1, FFN_EXPERT, D_MODEL), wsel)],
            out_specs=any_spec,
            scratch_shapes=[pltpu.VMEM((2, tm // MOE_SUB, MOE_SUB, D_MODEL), F32),
                            pltpu.VMEM((2, tm // MOE_SUB, MOE_SUB, D_MODEL), F32),
                            pltpu.SemaphoreType.DMA((2,)), pltpu.SemaphoreType.DMA((2,))]),
        out_shape=jax.ShapeDtypeStruct((out_rows, D_MODEL), F32),
        compiler_params=_params("arbitrary"),
        name="moe_experts",
    )(tile_expert, nused, src, dst, xn, wg, wu, wd)


def _combine_kernel(h_ref, meta_ref, y1_ref, y2_ref, o_ref):
    meta = meta_ref[...]
    o_ref[...] = h_ref[...] + meta[:, 2:3] * y1_ref[...] + meta[:, 3:4] * y2_ref[...]


def _combine(h2, meta, ys):
    t = h2.shape[0]
    tm = _tile(t, 512)
    nt = t // tm
    row = lambda i: (i, 0)
    return pl.pallas_call(
        _combine_kernel,
        grid=(nt,),
        in_specs=[pl.BlockSpec((tm, D_MODEL), row), pl.BlockSpec((tm, LANES), row),
                  pl.BlockSpec((tm, D_MODEL), row), pl.BlockSpec((tm, D_MODEL), lambda i: (nt + i, 0))],
        out_specs=pl.BlockSpec((tm, D_MODEL), row),
        out_shape=jax.ShapeDtypeStruct((t, D_MODEL), F32),
        compiler_params=_params("parallel"),
        name="moe_combine",
    )(h2, meta, ys, ys)


def _moe(h2, g, router, wg, wu, wd):
    t = h2.shape[0]
    tm = MOE_TM
    wr = jnp.pad(router, ((0, 0), (0, LANES - N_EXPERTS)))
    wr_hi = wr.astype(BF16)
    wr_lo = (wr - wr_hi.astype(F32)).astype(BF16)
    xn, meta, cnt = _router(h2, g, jnp.stack([wr_hi, wr_lo]))

    e1, e2, r1, r2 = (meta[:, c].astype(jnp.int32) for c in (0, 1, 4, 5))
    count = cnt[0, :N_EXPERTS].astype(jnp.int32)
    ntile = (count + tm - 1) // tm
    tile_end = jnp.cumsum(ntile)
    off = (tile_end - ntile) * tm
    rows = 2 * t + N_EXPERTS * tm
    tile_id = jnp.arange(rows // tm)
    tile_expert = jnp.minimum(
        jnp.sum(tile_id[:, None] >= tile_end[None, :], axis=1), N_EXPERTS - 1).astype(jnp.int32)
    nused = tile_end[-1:].astype(jnp.int32)
    pos1 = off[e1] + r1
    pos2 = off[e2] + r2
    pad = jnp.stack([jnp.append(off + count, tile_end[-1] * tm),
                     jnp.append(off + ntile * tm, rows)], axis=1).reshape(-1).astype(jnp.int32)
    src, dst = _moe_index(pos1, pos2, pad, rows)

    ys = _experts(tile_expert, nused, src, dst, xn, wg, wu, wd, rows)
    return _combine(h2, meta, ys)


def _segment_matrix(width):
    i = jnp.arange(width) // HEAD_DIM
    return (i[:, None] == i[None, :]).astype(BF16)


def _layer0(h2, bsz, seq, norm_mix, w_in, mu_shift, w0, w_decay_up, a0, w_iclr_up, w_gate_up, k_k, k_a, r_k,
            gn_w, gn_b, conv_w, conv_b, dt_bias, a_log, d_skip, ssd_norm, w_out, norm_ffn, ffn_gate, ffn_up,
            ffn_down):
    row = lambda a: a.reshape(1, -1)
    seg = _segment_matrix(RWKV_DIM)
    wa = w_in[:, :RWKV_COLS].astype(BF16)
    wb = jnp.pad(w_in[:, RWKV_COLS:], ((0, 0), (0, SSD_COLS_PAD - SSD_COLS))).astype(BF16)
    lora = w_decay_up.shape[0]
    wd = jnp.concatenate([w_decay_up, jnp.zeros((LANES - lora, RWKV_DIM), F32)], axis=0).astype(BF16)
    wi = jnp.concatenate([jnp.zeros((LANES - lora, RWKV_DIM), F32), w_iclr_up], axis=0).astype(BF16)
    pb, r, k, v, lw, kkn, b, g, bonus = _in_proj(
        h2, bsz, seq, row(norm_mix), wa, wb, row(mu_shift), row(w0), wd, row(a0), wi, w_gate_up.astype(BF16),
        row(k_k), row(k_a), row(r_k), seg)
    y = _rwkv_chunk(r, k, v, lw, kkn, b, bsz, seq)

    lane_pad = lambda a: jnp.pad(row(a), ((0, 0), (0, LANES - a.size)))
    ys = _ssd(pb, bsz, seq, conv_w, row(conv_b), lane_pad(dt_bias), lane_pad(a_log), lane_pad(d_skip),
              row(ssd_norm))

    w_out = w_out.astype(BF16)
    return _mix_ffn(h2, y, bonus, g, ys, row(gn_w), row(gn_b), seg, w_out[:RWKV_DIM], w_out[RWKV_DIM:],
                    row(norm_ffn), ffn_gate.astype(BF16), ffn_up.astype(BF16), ffn_down.astype(BF16))


def _layer1(h2, bsz, seq, tables, norm_mix, w_qkv, b_qkv, q_norm, k_norm, sinks, w_o, b_o, norm_ffn, router,
            exp_gate, exp_up, exp_down):
    row = lambda a: a.reshape(1, -1)
    two = lambda a: jnp.tile(a, LANES // HEAD_DIM).reshape(1, LANES)
    cos_f, sin_a, sin_b = tables
    q, k, v = _qkv(h2, row(norm_mix), w_qkv.astype(BF16), row(b_qkv), two(q_norm), two(k_norm), cos_f, sin_a,
                   sin_b, _segment_matrix(LANES))
    h2 = _attn(h2, q, k, v, row(sinks), w_o.astype(BF16), row(b_o), bsz, seq)

    return _moe(h2, row(norm_ffn), router, exp_gate.astype(BF16), exp_up.astype(BF16), exp_down.astype(BF16))


def kernel(x, positions, ev_norm_mix, ev_w_in, ev_mu_shift, ev_w0, ev_w_decay_up, ev_a0, ev_w_iclr_up, ev_w_gate_up, ev_k_k, ev_k_a, ev_r_k, ev_gn_w, ev_gn_b, ev_conv_w, ev_conv_b, ev_dt_bias, ev_a_log, ev_d_skip, ev_ssd_norm, ev_w_out, ev_norm_ffn, ev_ffn_gate, ev_ffn_up, ev_ffn_down, od_norm_mix, od_w_qkv, od_b_qkv, od_q_norm, od_k_norm, od_sinks, od_w_o, od_b_o, od_norm_ffn, od_router, od_exp_gate, od_exp_up, od_exp_down):
    bsz, seq, d = x.shape
    depth = ev_norm_mix.shape[0] + od_norm_mix.shape[0]
    tables = _rope_tables(positions)
    h2 = x.reshape(bsz * seq, d)
    for layer in range(depth):
        i = layer // 2
        if layer % 2 == 0:
            h2 = _layer0(h2, bsz, seq, ev_norm_mix[i], ev_w_in[i], ev_mu_shift[i], ev_w0[i], ev_w_decay_up[i],
                         ev_a0[i], ev_w_iclr_up[i], ev_w_gate_up[i], ev_k_k[i], ev_k_a[i], ev_r_k[i],
                         ev_gn_w[i], ev_gn_b[i], ev_conv_w[i], ev_conv_b[i], ev_dt_bias[i], ev_a_log[i],
                         ev_d_skip[i], ev_ssd_norm[i], ev_w_out[i], ev_norm_ffn[i], ev_ffn_gate[i],
                         ev_ffn_up[i], ev_ffn_down[i])
        else:
            h2 = _layer1(h2, bsz, seq, tables, od_norm_mix[i], od_w_qkv[i], od_b_qkv[i], od_q_norm[i],
                         od_k_norm[i], od_sinks[i], od_w_o[i], od_b_o[i], od_norm_ffn[i], od_router[i],
                         od_exp_gate[i], od_exp_up[i], od_exp_down[i])
    return h2.reshape(bsz, seq, d)
```

```python
import functools
import math

import jax
import jax.numpy as jnp
from jax import lax
from jax.experimental import pallas as pl
from jax.experimental.pallas import tpu as pltpu

F32 = jnp.float32
BF16 = jnp.bfloat16

D_MODEL = 1024
HEAD_DIM = 64
NORM_EPS = 1e-6

RWKV_HEADS = 8
RWKV_DIM = 512
RWKV_COLS = 1792
RWKV_GN_EPS = 64e-5
RWKV_CHUNK = 64
RWKV_GROUP = 256

SSD_HEADS = 8
SSD_DIM = 512
SSD_GROUPS = 2
SSD_STATE = 128
SSD_CONV = 4
SSD_CHUNK = 128
SSD_XBC = 1024
SSD_COLS = 1544
SSD_COLS_PAD = 1664

Q_HEADS = 16
KV_HEADS = 4
GQA_GROUP = 4
Q_DIM = 1024
KV_DIM = 256
WINDOW = 128
ROPE_THETA = 500000.0
ROPE_DIM = 16

FFN_DENSE = 2816
N_EXPERTS = 8
FFN_EXPERT = 1408
MOE_TM = 256
MOE_SUB = 8
MOE_FCHUNK = 256

LANES = 128
VMEM_LIMIT_BYTES = 56 * 1024 * 1024


def _params(*sem):
    return pltpu.CompilerParams(dimension_semantics=sem, vmem_limit_bytes=VMEM_LIMIT_BYTES)


def _bdot(a, b):
    return jnp.dot(a.astype(BF16), b.astype(BF16), preferred_element_type=F32)


def _bdot_nt(a, b):
    return lax.dot_general(a.astype(BF16), b.astype(BF16), (((1,), (1,)), ((), ())),
                           preferred_element_type=F32)


def _bdot_tn(a, b):
    return lax.dot_general(a.astype(BF16), b.astype(BF16), (((0,), (0,)), ((), ())),
                           preferred_element_type=F32)


def _split(x):
    hi = x.astype(BF16)
    lo = (x - hi.astype(F32)).astype(BF16)
    return hi, lo


def _dot_exact_rhs(x, m):
    hi, lo = _split(x)
    return (jnp.dot(hi, m, preferred_element_type=F32) + jnp.dot(lo, m, preferred_element_type=F32))


def _dot_exact_lhs(m, x):
    hi, lo = _split(x)
    return (jnp.dot(m, hi, preferred_element_type=F32) + jnp.dot(m, lo, preferred_element_type=F32))


def _sigmoid(x):
    return 1.0 / (1.0 + jnp.exp(-x))


def _silu(x):
    return x * _sigmoid(x)


def _softplus(x):
    return jnp.maximum(x, 0.0) + jnp.log(1.0 + jnp.exp(-jnp.abs(x)))


def _rms(x, g):
    ms = jnp.mean(x * x, axis=-1, keepdims=True)
    return x * lax.rsqrt(ms + NORM_EPS) * g


def _tile(n, pref):
    t = min(n, pref)
    while n % t:
        t //= 2
    return t


def _full(shape):
    nd = len(shape)
    return pl.BlockSpec(shape, lambda *_: (0,) * nd)


def _resident(shape):
    nd = len(shape)
    return pl.BlockSpec(shape, lambda *_: (0,) * nd, pipeline_mode=pl.Buffered(1))


def _in_proj_kernel(x_ref, g_ref, wa_ref, wb_ref, mu_ref, w0_ref, wd_ref, a0_ref, wi_ref, wg_ref, kk_ref, ka_ref,
                    rk_ref, seg_ref, pb_ref, r_ref, k_ref, v_ref, lw_ref, kkn_ref, b_ref, gate_ref, bonus_ref,
                    carry_ref):
    @pl.when(pl.program_id(1) == 0)
    def _():
        carry_ref[...] = jnp.zeros_like(carry_ref)

    xn = _rms(x_ref[...], g_ref[...]).astype(BF16)
    pb_ref[...] = jnp.dot(xn, wb_ref[...], preferred_element_type=F32)
    pa = jnp.dot(xn, wa_ref[...], preferred_element_type=F32)
    tm = pa.shape[0]
    row = lax.broadcasted_iota(jnp.int32, pa.shape, 0)
    prev = jnp.where(row == 0, carry_ref[...], pltpu.roll(pa, 1, axis=0))
    carry_ref[...] = pa[tm - 1:tm, :]
    x = pa + (prev - pa) * mu_ref[...]

    r = x[:, 0:RWKV_DIM]
    k = x[:, RWKV_DIM:2 * RWKV_DIM]
    v = x[:, 2 * RWKV_DIM:3 * RWKV_DIM]
    lora = x[:, 3 * RWKV_DIM:3 * RWKV_DIM + LANES]
    gl = x[:, 3 * RWKV_DIM + LANES:]
    seg = seg_ref[...]

    w_raw = w0_ref[...] + _bdot(jnp.tanh(lora), wd_ref[...])
    lw_ref[...] = (-math.exp(-0.5)) * _sigmoid(w_raw)
    iclr = _sigmoid(a0_ref[...] + _bdot(lora, wi_ref[...]))
    gate_ref[...] = _bdot(_sigmoid(gl), wg_ref[...])

    kk = k * kk_ref[...]
    kkn = kk * lax.rsqrt(_dot_exact_rhs(kk * kk, seg) + 1e-12)
    k2 = k * (1.0 + (iclr - 1.0) * ka_ref[...])
    r_ref[...] = r
    k_ref[...] = k2
    v_ref[...] = v
    kkn_ref[...] = kkn
    b_ref[...] = kkn * iclr
    bonus_ref[...] = _dot_exact_rhs(r * k2 * rk_ref[...], seg) * v


def _in_proj(x2, bsz, seq, g, wa, wb, mu, w0, wd, a0, wi, wg, k_k, k_a, r_k, seg):
    t = x2.shape[0]
    tm = _tile(seq, 256)
    nt = seq // tm
    row = lambda b, i: (b * nt + i, 0)
    out = jax.ShapeDtypeStruct((t, RWKV_DIM), F32)
    small = [mu, w0, wd, a0, wi, wg, k_k, k_a, r_k, seg]
    return pl.pallas_call(
        _in_proj_kernel,
        grid=(bsz, nt),
        in_specs=[pl.BlockSpec((tm, D_MODEL), row), _full((1, D_MODEL)), _resident(wa.shape),
                  _resident(wb.shape)] + [_full(a.shape) for a in small],
        out_specs=[pl.BlockSpec((tm, SSD_COLS_PAD), row)] + [pl.BlockSpec((tm, RWKV_DIM), row)] * 8,
        out_shape=[jax.ShapeDtypeStruct((t, SSD_COLS_PAD), F32)] + [out] * 8,
        scratch_shapes=[pltpu.VMEM((1, RWKV_COLS), F32)],
        compiler_params=_params("arbitrary", "arbitrary"),
        name="in_proj",
    )(x2, g, wa, wb, *small)


def _rwkv_chunk_kernel(r_ref, k_ref, v_ref, lw_ref, kk_ref, b_ref, y_ref, z_ref, *, lt):
    c_len = RWKV_CHUNK
    n = HEAD_DIM
    gl = RWKV_GROUP
    nchunk = gl // c_len
    nheads = LANES // n
    shift = c_len.bit_length() - 1

    @pl.when(pl.program_id(2) == 0)
    def _():
        z_ref[...] = jnp.zeros_like(z_ref)

    ri = lax.broadcasted_iota(jnp.int32, (gl, gl), 0)
    ci = lax.broadcasted_iota(jnp.int32, (gl, gl), 1)
    tri_bd = jnp.where((ci <= ri) & (ci >= ((ri >> shift) << shift)), 1.0, 0.0).astype(BF16)
    re_ = lax.broadcasted_iota(jnp.int32, (c_len, c_len), 0)
    ce_ = lax.broadcasted_iota(jnp.int32, (c_len, c_len), 1)
    eye = re_ == ce_
    ri2 = lax.broadcasted_iota(jnp.int32, (2 * gl, gl), 0)
    ci2 = lax.broadcasted_iota(jnp.int32, (2 * gl, gl), 1)
    t2 = jnp.where(ri2 < gl, ri2, ri2 - gl)
    mask2 = (ci2 <= jnp.where(ri2 < gl, t2 - 1, t2)) & (ci2 >= ((t2 >> shift) << shift))
    zeros = jnp.zeros((c_len, n), F32)

    nsub = lt // gl
    sysid = [(s, h) for s in range(nsub) for h in range(nheads)]
    pre = []
    for s in range(nsub):
        sl = pl.ds(s * gl, gl)
        lw = lw_ref[sl, :]
        g_in = _dot_exact_lhs(tri_bd, lw)
        ends = [g_in[(c + 1) * c_len - 1:(c + 1) * c_len, :] for c in range(nchunk)]
        g_end = jnp.concatenate([jnp.broadcast_to(e, (c_len, LANES)) for e in ends], axis=0)
        e_end = jnp.exp(g_end - g_in)
        en = jnp.exp(-g_in)
        k = k_ref[sl, :]
        b = b_ref[sl, :]
        pre.append(dict(
            ends=ends, v=v_ref[sl, :], rt=r_ref[sl, :] * jnp.exp(g_in),
            at=-kk_ref[sl, :] * jnp.exp(g_in - lw), kt=k * en, bt=b * en, bend=b * e_end, kend=k * e_end))

    def hs(name, s, h):
        return pre[s][name][:, h * n:(h + 1) * n]

    xb, xk, xkv, p, x = {}, {}, {}, {}, {}
    for q in sysid:
        la = jnp.concatenate([hs("at", *q), hs("rt", *q)], axis=0)
        xb[q] = jnp.where(mask2, _bdot_nt(la, hs("bt", *q)), 0.0)
        xk[q] = jnp.where(mask2, _bdot_nt(la, hs("kt", *q)), 0.0)
    for q in sysid:
        xkv[q] = _bdot(xk[q], hs("v", *q))
        p[q] = xb[q][0:gl]
        x[q] = jnp.concatenate([hs("at", *q), xkv[q][0:gl]], axis=1)
    for i in range(6):
        for q in sysid:
            x[q] = x[q] + _bdot(p[q], x[q])
        if i < 5:
            for q in sysid:
                p[q] = _bdot(p[q], p[q])
    rq, y0, mn = {}, {}, {}
    for q in sysid:
        yy = _bdot(xb[q][gl:], x[q])
        rq[q] = yy[:, 0:n] + hs("rt", *q)
        y0[q] = yy[:, n:] + xkv[q][gl:]
        bend_h, kend_h, v_h = hs("bend", *q), hs("kend", *q), hs("v", *q)
        for c in range(nchunk):
            rows = slice(c * c_len, (c + 1) * c_len)
            lhs = jnp.concatenate([bend_h[rows], kend_h[rows]], axis=0)
            rhs = jnp.concatenate([x[q][rows], jnp.concatenate([zeros, v_h[rows]], axis=1)], axis=0)
            mn[q + (c,)] = _bdot_tn(lhs, rhs)
    zs = [z_ref[h] for h in range(nheads)]
    for s in range(nsub):
        yh = [[] for _ in range(nheads)]
        for c in range(nchunk):
            rows = slice(c * c_len, (c + 1) * c_len)
            for h in range(nheads):
                q = (s, h)
                pend = jnp.exp(pre[s]["ends"][c][:, h * n:(h + 1) * n])
                m_mat = mn[q + (c,)][:, 0:n] + jnp.where(eye, pend, 0.0)
                yh[h].append(_bdot(rq[q][rows], zs[h]) + y0[q][rows])
                zs[h] = _bdot(m_mat, zs[h]) + mn[q + (c,)][:, n:]
        y_ref[pl.ds(s * gl, gl), :] = jnp.concatenate(
            [jnp.concatenate(yh[h], axis=0) for h in range(nheads)], axis=1)
    for h in range(nheads):
        z_ref[h] = zs[h]


def _rwkv_chunk(r, k, v, lw, kkn, b, bsz, seq):
    t = r.shape[0]
    lt = _tile(seq, 512)
    nt = seq // lt
    spec = pl.BlockSpec((lt, LANES), lambda bi, hp, i: (bi * nt + i, hp))
    return pl.pallas_call(
        functools.partial(_rwkv_chunk_kernel, lt=lt),
        grid=(bsz, RWKV_DIM // LANES, nt),
        in_specs=[spec] * 6,
        out_specs=spec,
        out_shape=jax.ShapeDtypeStruct((t, RWKV_DIM), F32),
        scratch_shapes=[pltpu.VMEM((2, HEAD_DIM, HEAD_DIM), F32)],
        compiler_params=_params("arbitrary", "arbitrary", "arbitrary"),
        name="rwkv_chunk",
    )(r, k, v, lw, kkn, b)


def _ssd_kernel(pb_ref, cw_ref, cb_ref, dtb_ref, alog_ref, dskip_ref, nrm_ref, ys_ref, ext_ref, st_ref):
    q = SSD_CHUNK
    p = HEAD_DIM
    hpg = SSD_HEADS // SSD_GROUPS

    @pl.when(pl.program_id(1) == 0)
    def _():
        ext_ref[0:8, :] = jnp.zeros((8, SSD_XBC), F32)
        st_ref[...] = jnp.zeros_like(st_ref)

    z = pb_ref[:, 0:SSD_DIM]
    u = pb_ref[:, SSD_DIM:SSD_DIM + SSD_XBC]
    dt_raw = pb_ref[:, SSD_DIM + SSD_XBC:]

    ext_ref[8:8 + q, :] = u
    conv = cb_ref[...] + cw_ref[SSD_CONV - 1:SSD_CONV, :] * u
    for j in range(SSD_CONV - 1):
        off = 8 - (SSD_CONV - 1) + j
        conv = conv + cw_ref[j:j + 1, :] * ext_ref[off:off + q, :]
    ext_ref[0:8, :] = u[q - 8:q, :]
    xbc = _silu(conv)
    xs = xbc[:, 0:SSD_DIM]
    bm = xbc[:, SSD_DIM:SSD_DIM + SSD_GROUPS * SSD_STATE]
    cm = xbc[:, SSD_DIM + SSD_GROUPS * SSD_STATE:]

    dt = _softplus(dt_raw + dtb_ref[...])
    a = -jnp.exp(alog_ref[...])
    ri = lax.broadcasted_iota(jnp.int32, (q, q), 0)
    ci = lax.broadcasted_iota(jnp.int32, (q, q), 1)
    causal = ri >= ci
    cum = _dot_exact_lhs(causal.astype(BF16), dt * a)
    cum_t = cum.T
    dt_t = dt.T
    cum_end = cum[q - 1:q, :]
    to_end = jnp.exp(cum_end - cum) * dt
    ecum = jnp.exp(cum)
    edec = jnp.exp(cum_end)

    ys = []
    for g in range(SSD_GROUPS):
        bm_g = bm[:, g * SSD_STATE:(g + 1) * SSD_STATE]
        cm_g = cm[:, g * SSD_STATE:(g + 1) * SSD_STATE]
        cb = _bdot_nt(cm_g, bm_g)
        bm_t = bm_g.T
        for hh in range(hpg):
            h = g * hpg + hh
            x_h = xs[:, h * p:(h + 1) * p]
            seg = cum[:, h:h + 1] - cum_t[h:h + 1, :]
            ldec = jnp.exp(jnp.where(causal, seg, -jnp.inf))
            wts = cb * ldec * dt_t[h:h + 1, :]
            y = _bdot(wts, x_h)
            h_prev = st_ref[h]
            y = y + _bdot(cm_g, h_prev) * ecum[:, h:h + 1]
            st_ref[h] = h_prev * edec[:, h:h + 1] + _bdot(bm_t, x_h * to_end[:, h:h + 1])
            ys.append(y + dskip_ref[:, h:h + 1] * x_h)
    yall = jnp.concatenate(ys, axis=1) * _silu(z)
    gw = SSD_DIM // SSD_GROUPS
    outs = []
    for g in range(SSD_GROUPS):
        yg = yall[:, g * gw:(g + 1) * gw]
        ms = jnp.mean(yg * yg, axis=-1, keepdims=True)
        outs.append(yg * lax.rsqrt(ms + NORM_EPS) * nrm_ref[:, g * gw:(g + 1) * gw])
    ys_ref[...] = jnp.concatenate(outs, axis=1)


def _ssd(pb, bsz, seq, conv_w, conv_b, dt_bias, a_log, d_skip, ssd_norm):
    t = pb.shape[0]
    nc = seq // SSD_CHUNK
    row = lambda b, i: (b * nc + i, 0)
    return pl.pallas_call(
        _ssd_kernel,
        grid=(bsz, nc),
        in_specs=[pl.BlockSpec((SSD_CHUNK, SSD_COLS_PAD), row), _full(conv_w.shape), _full(conv_b.shape),
                  _full(dt_bias.shape), _full(a_log.shape), _full(d_skip.shape), _full(ssd_norm.shape)],
        out_specs=pl.BlockSpec((SSD_CHUNK, SSD_DIM), row),
        out_shape=jax.ShapeDtypeStruct((t, SSD_DIM), F32),
        scratch_shapes=[pltpu.VMEM((8 + SSD_CHUNK, SSD_XBC), F32),
                        pltpu.VMEM((SSD_HEADS, SSD_STATE, HEAD_DIM), F32)],
        compiler_params=_params("arbitrary", "arbitrary"),
        name="ssd",
    )(pb, conv_w, conv_b, dt_bias, a_log, d_skip, ssd_norm)


def _mix_ffn_kernel(h_ref, y_ref, bonus_ref, gate_ref, ys_ref, gnw_ref, gnb_ref, seg_ref, wa_ref, wb_ref,
                    g_ref, wg_ref, wu_ref, wd_ref, o_ref):
    y = y_ref[...]
    seg = seg_ref[...]
    mu = _dot_exact_rhs(y, seg) * (1.0 / HEAD_DIM)
    yc = y - mu
    var = _dot_exact_rhs(yc * yc, seg) * (1.0 / HEAD_DIM)
    yn = yc * lax.rsqrt(var + RWKV_GN_EPS) * gnw_ref[...] + gnb_ref[...]
    ya = (yn + bonus_ref[...]) * gate_ref[...]
    h = h_ref[...] + _bdot(ya, wa_ref[...]) + _bdot(ys_ref[...], wb_ref[...])

    xn = _rms(h, g_ref[...]).astype(BF16)
    act = _silu(jnp.dot(xn, wg_ref[...], preferred_element_type=F32)) * jnp.dot(
        xn, wu_ref[...], preferred_element_type=F32)
    o_ref[...] = h + jnp.dot(act.astype(BF16), wd_ref[...], preferred_element_type=F32)


def _mix_ffn(h2, y, bonus, gate, ys, gn_w, gn_b, seg, wa, wb, g, wg, wu, wd):
    t = h2.shape[0]
    tm = _tile(t, 512)
    row = lambda i: (i, 0)
    half = pl.BlockSpec((tm, RWKV_DIM), row)
    return pl.pallas_call(
        _mix_ffn_kernel,
        grid=(t // tm,),
        in_specs=[pl.BlockSpec((tm, D_MODEL), row), half, half, half, half, _full(gn_w.shape),
                  _full(gn_b.shape), _resident(seg.shape), _resident(wa.shape), _resident(wb.shape),
                  _full((1, D_MODEL)), _resident(wg.shape), _resident(wu.shape), _resident(wd.shape)],
        out_specs=pl.BlockSpec((tm, D_MODEL), row),
        out_shape=jax.ShapeDtypeStruct((t, D_MODEL), F32),
        compiler_params=_params("parallel"),
        name="mix_ffn",
    )(h2, y, bonus, gate, ys, gn_w, gn_b, seg, wa, wb, g, wg, wu, wd)


def _rope_table_kernel(pos_ref, freq_ref, cos_ref, sin_ref):
    ang = pos_ref[...] * freq_ref[...]
    cos_ref[...] = jnp.cos(ang)
    sin_ref[...] = jnp.sin(ang)


def _rope_tables(positions):
    t = positions.size
    half = ROPE_DIM // 2
    rows = t * half // LANES
    inv_freq = ROPE_THETA ** (-jnp.arange(0, ROPE_DIM, 2, dtype=F32) / ROPE_DIM)
    pos_rep = jnp.repeat(positions.reshape(-1).astype(F32), half).reshape(rows, LANES)
    freq = jnp.tile(inv_freq, LANES // half).reshape(1, LANES)
    tr = _tile(rows, 256)
    cos, sin = pl.pallas_call(
        _rope_table_kernel,
        grid=(rows // tr,),
        in_specs=[pl.BlockSpec((tr, LANES), lambda i: (i, 0)), _full((1, LANES))],
        out_specs=[pl.BlockSpec((tr, LANES), lambda i: (i, 0))] * 2,
        out_shape=[jax.ShapeDtypeStruct((rows, LANES), F32)] * 2,
        compiler_params=_params("parallel"),
        name="rope_tables",
    )(pos_rep, freq)
    cos = cos.reshape(t, half)
    sin = sin.reshape(t, half)
    pad = HEAD_DIM - ROPE_DIM
    one = jnp.ones((t, pad), F32)
    zero = jnp.zeros((t, pad), F32)
    zh = jnp.zeros((t, half), F32)
    cos_f = jnp.concatenate([cos, cos, one], axis=1)
    sin_a = jnp.concatenate([-sin, zh, zero], axis=1)
    sin_b = jnp.concatenate([zh, sin, zero], axis=1)
    return tuple(jnp.tile(m, (1, LANES // HEAD_DIM)) for m in (cos_f, sin_a, sin_b))


def _qkv_kernel(h_ref, g_ref, w_ref, b_ref, qn_ref, kn_ref, cos_ref, sa_ref, sb_ref, seg_ref,
                q_ref, k_ref, v_ref):
    xn = _rms(h_ref[...], g_ref[...]).astype(BF16)
    qkv = jnp.dot(xn, w_ref[...], preferred_element_type=F32) + b_ref[...]
    cos = cos_ref[...]
    sa = sa_ref[...]
    sb = sb_ref[...]
    seg = seg_ref[...]

    def norm_rope(x, gain):
        ms = _dot_exact_rhs(x * x, seg) * (1.0 / HEAD_DIM)
        xn_ = x * lax.rsqrt(ms + NORM_EPS) * gain
        return (xn_ * cos + pltpu.roll(xn_, LANES - ROPE_DIM // 2, axis=1) * sa
                + pltpu.roll(xn_, ROPE_DIM // 2, axis=1) * sb)

    for c in range(Q_DIM // LANES):
        x = qkv[:, c * LANES:(c + 1) * LANES]
        q_ref[:, c * LANES:(c + 1) * LANES] = (norm_rope(x, qn_ref[...]) * (HEAD_DIM ** -0.5)).astype(BF16)
    for c in range(KV_DIM // LANES):
        x = qkv[:, Q_DIM + c * LANES:Q_DIM + (c + 1) * LANES]
        k_ref[:, c * LANES:(c + 1) * LANES] = norm_rope(x, kn_ref[...]).astype(BF16)
    v_ref[...] = qkv[:, Q_DIM + KV_DIM:].astype(BF16)


def _qkv(h2, g, w, b, qn, kn, cos_f, sin_a, sin_b, seg2):
    t = h2.shape[0]
    tm = _tile(t, 256)
    row = lambda i: (i, 0)
    tab = pl.BlockSpec((tm, LANES), row)
    return pl.pallas_call(
        _qkv_kernel,
        grid=(t // tm,),
        in_specs=[pl.BlockSpec((tm, D_MODEL), row), _full((1, D_MODEL)), _full(w.shape), _full(b.shape),
                  _full(qn.shape), _full(kn.shape), tab, tab, tab, _full(seg2.shape)],
        out_specs=[pl.BlockSpec((tm, Q_DIM), row), pl.BlockSpec((tm, KV_DIM), row),
                   pl.BlockSpec((tm, KV_DIM), row)],
        out_shape=[jax.ShapeDtypeStruct((t, Q_DIM), BF16), jax.ShapeDtypeStruct((t, KV_DIM), BF16),
                   jax.ShapeDtypeStruct((t, KV_DIM), BF16)],
        compiler_params=_params("parallel"),
        name="qkv",
    )(h2, g, w, b, qn, kn, cos_f, sin_a, sin_b, seg2)


def _attn_kernel(h_ref, q_ref, kc_ref, kp_ref, vc_ref, vp_ref, sink_ref, wo_ref, bo_ref, o_ref, *, nqb):
    w = WINDOW
    n = HEAD_DIM
    rows = GQA_GROUP * w
    qi = lax.broadcasted_iota(jnp.int32, (rows, 2 * w), 0) % w
    kj = lax.broadcasted_iota(jnp.int32, (rows, 2 * w), 1)
    diff = w + qi - kj
    band = (diff >= 0) & (diff < w)
    band0 = band & (kj >= jnp.where(pl.program_id(1) == 0, w, 0))
    kall = jnp.concatenate([kp_ref[...], kc_ref[...]], axis=0)
    vall = jnp.concatenate([vp_ref[...], vc_ref[...]], axis=0)
    units = [(qb, g) for qb in range(nqb) for g in range(KV_HEADS)]
    s, e, den = {}, {}, {}
    for u in units:
        qb, g = u
        qs = jnp.concatenate(
            [q_ref[qb * w:(qb + 1) * w, (g * GQA_GROUP + i) * n:(g * GQA_GROUP + i + 1) * n]
             for i in range(GQA_GROUP)], axis=0)
        kk = kall[qb * w:(qb + 2) * w, g * n:(g + 1) * n]
        sc = lax.dot_general(qs, kk, (((1,), (1,)), ((), ())), preferred_element_type=F32)
        s[u] = jnp.where(band0 if qb == 0 else band, sc, -jnp.inf)
    sinks = [jnp.concatenate(
        [jnp.broadcast_to(sink_ref[:, g * GQA_GROUP + i:g * GQA_GROUP + i + 1], (w, LANES))
         for i in range(GQA_GROUP)], axis=0) for g in range(KV_HEADS)]
    ones = jnp.ones((2 * w, LANES), BF16)
    for u in units:
        sink = sinks[u[1]]
        m = jnp.maximum(jnp.broadcast_to(jnp.max(s[u], axis=-1, keepdims=True), (rows, LANES)), sink)
        e[u] = jnp.exp(s[u] - jnp.concatenate([m, m], axis=1)).astype(BF16)
        den[u] = jnp.exp(sink - m)
    outs = [[None] * Q_HEADS for _ in range(nqb)]
    for u in units:
        qb, g = u
        vv = vall[qb * w:(qb + 2) * w, g * n:(g + 1) * n]
        total = jnp.dot(e[u], ones, preferred_element_type=F32) + den[u]
        o = jnp.dot(e[u], vv, preferred_element_type=F32) / total[:, 0:n]
        for i in range(GQA_GROUP):
            outs[qb][g * GQA_GROUP + i] = o[i * w:(i + 1) * w]
    att = jnp.concatenate([jnp.concatenate(outs[qb], axis=1) for qb in range(nqb)], axis=0)
    o_ref[...] = h_ref[...] + _bdot(att, wo_ref[...]) + bo_ref[...]


def _attn(h2, q, k, v, sinks, wo, bo, bsz, seq):
    t = h2.shape[0]
    nqb = 2 if seq % (2 * WINDOW) == 0 else 1
    tq = nqb * WINDOW
    nb = seq // tq
    cur = lambda b, i: (b * nb + i, 0)
    prv = lambda b, i: (b * nb * nqb + jnp.maximum(i * nqb - 1, 0), 0)
    return pl.pallas_call(
        functools.partial(_attn_kernel, nqb=nqb),
        grid=(bsz, nb),
        in_specs=[pl.BlockSpec((tq, D_MODEL), cur), pl.BlockSpec((tq, Q_DIM), cur),
                  pl.BlockSpec((tq, KV_DIM), cur), pl.BlockSpec((WINDOW, KV_DIM), prv),
                  pl.BlockSpec((tq, KV_DIM), cur), pl.BlockSpec((WINDOW, KV_DIM), prv),
                  _full(sinks.shape), _full(wo.shape), _full(bo.shape)],
        out_specs=pl.BlockSpec((tq, D_MODEL), cur),
        out_shape=jax.ShapeDtypeStruct((t, D_MODEL), F32),
        compiler_params=_params("parallel", "parallel"),
        name="attn",
    )(h2, q, k, k, v, v, sinks, wo, bo)


def _router_kernel(h_ref, g_ref, wr_ref, xn_ref, meta_ref, cnt_ref, base_ref):
    @pl.when(pl.program_id(0) == 0)
    def _():
        base_ref[...] = jnp.zeros_like(base_ref)

    xn = _rms(h_ref[...], g_ref[...])
    xn_ref[...] = xn
    hi, lo = _split(xn)
    whi = wr_ref[0]
    wlo = wr_ref[1]
    logits = (jnp.dot(hi, whi, preferred_element_type=F32) + jnp.dot(hi, wlo, preferred_element_type=F32)
              + jnp.dot(lo, whi, preferred_element_type=F32))
    tm = logits.shape[0]
    lane = lax.broadcasted_iota(jnp.int32, logits.shape, 1)
    logits = jnp.where(lane < N_EXPERTS, logits, -jnp.inf)
    m1 = jnp.max(logits, axis=-1, keepdims=True)
    i1 = jnp.min(jnp.where(logits == m1, lane, LANES), axis=-1, keepdims=True)
    rest = jnp.where(lane == i1, -jnp.inf, logits)
    m2 = jnp.max(rest, axis=-1, keepdims=True)
    i2 = jnp.min(jnp.where(rest == m2, lane, LANES), axis=-1, keepdims=True)
    e2 = jnp.exp(m2 - m1)
    w1 = 1.0 / (1.0 + e2)
    w2 = e2 / (1.0 + e2)

    sel1 = lane == i1
    sel2 = lane == i2
    onehot = jnp.where(sel1, 1.0, 0.0) + jnp.where(sel2, 1.0, 0.0)
    ri = lax.broadcasted_iota(jnp.int32, (tm, tm), 0)
    ci = lax.broadcasted_iota(jnp.int32, (tm, tm), 1)
    before = jnp.dot(jnp.where(ci < ri, 1.0, 0.0).astype(BF16), onehot.astype(BF16),
                     preferred_element_type=F32)
    rank = base_ref[...] + before
    r1 = jnp.sum(jnp.where(sel1, rank, 0.0), axis=-1, keepdims=True)
    r2 = jnp.sum(jnp.where(sel2, rank, 0.0), axis=-1, keepdims=True)
    total = base_ref[...] + jnp.sum(onehot, axis=0, keepdims=True)
    base_ref[...] = total
    cnt_ref[...] = total
    cols = (i1.astype(F32), i2.astype(F32), w1, w2, r1, r2)
    meta = jnp.zeros(logits.shape, F32)
    for c, val in enumerate(cols):
        meta = jnp.where(lane == c, val, meta)
    meta_ref[...] = meta


def _router(h2, g, wr):
    t = h2.shape[0]
    tm = _tile(t, 512)
    row = lambda i: (i, 0)
    return pl.pallas_call(
        _router_kernel,
        grid=(t // tm,),
        in_specs=[pl.BlockSpec((tm, D_MODEL), row), _full((1, D_MODEL)), _full(wr.shape)],
        out_specs=[pl.BlockSpec((tm, D_MODEL), row), pl.BlockSpec((tm, LANES), row), _full((1, LANES))],
        out_shape=[jax.ShapeDtypeStruct((t, D_MODEL), F32), jax.ShapeDtypeStruct((t, LANES), F32),
                   jax.ShapeDtypeStruct((1, LANES), F32)],
        scratch_shapes=[pltpu.VMEM((1, LANES), F32)],
        compiler_params=_params("arbitrary"),
        name="router",
    )(h2, g, wr)


def _moe_index_kernel(p1_ref, p2_ref, pad_ref, src_ref, dst_ref, *, ntok):
    for e in range(N_EXPERTS + 1):
        def fill(p, c, e=e):
            src_ref[p] = 0
            dst_ref[p] = 2 * ntok + (e % N_EXPERTS) * MOE_TM + ((p - pad_ref[2 * e]) & (MOE_TM - 1))
            return c

        lax.fori_loop(pad_ref[2 * e], pad_ref[2 * e + 1], fill, 0)

    def body(t, c):
        p1 = p1_ref[t]
        p2 = p2_ref[t]
        src_ref[p1] = t
        dst_ref[p1] = t
        src_ref[p2] = t
        dst_ref[p2] = ntok + t
        return c

    lax.fori_loop(0, ntok, body, 0, unroll=4)


def _moe_index(pos1, pos2, pad, rows):
    smem = pl.BlockSpec(memory_space=pltpu.SMEM)
    return pl.pallas_call(
        functools.partial(_moe_index_kernel, ntok=pos1.shape[0]),
        grid_spec=pltpu.PrefetchScalarGridSpec(
            num_scalar_prefetch=3, grid=(1,), in_specs=[], out_specs=[smem, smem]),
        out_shape=[jax.ShapeDtypeStruct((rows,), jnp.int32)] * 2,
        compiler_params=_params("arbitrary"),
        name="moe_index",
    )(pos1, pos2, pad)


def _row_copy(src_ref, src_row, dst_ref, dst_row, sem):
    return pltpu.make_async_copy(src_ref.at[pl.ds(src_row, 1)], dst_ref.at[pl.ds(dst_row, 1)], sem)


def _experts_kernel(te_ref, nused_ref, src_ref, dst_ref, x_ref, wg_ref, wu_ref, wd_ref, y_ref,
                    xbuf, ybuf, gsem, ssem):
    del te_ref
    i = pl.program_id(0)
    nused = nused_ref[0]
    tm = MOE_TM

    sub = MOE_SUB
    ngroup = tm // sub
    dump0 = y_ref.shape[0] - tm

    def gather_group(tile, slot, k):
        for u in range(sub):
            _row_copy(x_ref, src_ref[tile * tm + k * sub + u], xbuf.at[slot, k], u, gsem.at[slot]).start()

    def scatter_group(tile, slot, k, placeholder):
        for u in range(sub):
            j = k * sub + u
            row = dst_ref[tile * tm + j]
            if placeholder is not None:
                row = jnp.where(placeholder, dump0 + j, row)
            _row_copy(ybuf.at[slot, k], u, y_ref, row, ssem.at[slot]).start()

    def wait_gather(slot):
        def body(k, c):
            pltpu.make_async_copy(x_ref.at[pl.ds(0, sub)], xbuf.at[slot, k], gsem.at[slot]).wait()
            return c
        lax.fori_loop(0, ngroup, body, 0)

    def wait_scatter(slot):
        def body(k, c):
            pltpu.make_async_copy(ybuf.at[slot, k], y_ref.at[pl.ds(0, sub)], ssem.at[slot]).wait()
            return c
        lax.fori_loop(0, ngroup, body, 0)

    @pl.when(i == 0)
    def _():
        ybuf[1] = jnp.zeros(ybuf.shape[1:], F32)

        def body(k, c):
            gather_group(0, 0, k)
            return c
        lax.fori_loop(0, ngroup, body, 0)

    @pl.when(i < nused)
    def _():
        slot = i % 2
        other = 1 - slot
        nxt = jnp.minimum(i + 1, nused - 1)
        prev = jnp.maximum(i - 1, 0)
        first = i == 0
        wait_gather(slot)

        @pl.when(i >= 1)
        def _():
            wait_scatter(slot)

        xb = xbuf[slot].reshape(tm, D_MODEL).astype(BF16)
        bounds = list(range(0, FFN_EXPERT, MOE_FCHUNK)) + [FFN_EXPERT]
        nchunk = len(bounds) - 1
        acts = []
        for c in range(nchunk):
            c0, c1 = bounds[c], bounds[c + 1]
            gate = jnp.dot(xb, wg_ref[0, :, c0:c1], preferred_element_type=F32)
            up = jnp.dot(xb, wu_ref[0, :, c0:c1], preferred_element_type=F32)
            acts.append((_silu(gate) * up).astype(BF16))
            for k in range(c * ngroup // nchunk, (c + 1) * ngroup // nchunk):
                gather_group(nxt, other, k)
                scatter_group(prev, other, k, first)
        y = jnp.dot(jnp.concatenate(acts, axis=1), wd_ref[0], preferred_element_type=F32)
        ybuf[slot] = y.reshape(ngroup, sub, D_MODEL)

        @pl.when(i == nused - 1)
        def _():
            def body(k, c):
                scatter_group(i, slot, k, None)
                return c
            lax.fori_loop(0, ngroup, body, 0)
            wait_scatter(other)
            wait_scatter(slot)
            wait_gather(other)


def _experts(tile_expert, nused, src, dst, xn, wg, wu, wd, out_rows):
    ntile = tile_expert.shape[0]
    tm = MOE_TM
    wsel = lambda i, te, nu, s, d: (te[i], 0, 0)
    any_spec = pl.BlockSpec(memory_space=pl.ANY)
    return pl.pallas_call(
        _experts_kernel,
        grid_spec=pltpu.PrefetchScalarGridSpec(
            num_scalar_prefetch=4, grid=(ntile,),
            in_specs=[any_spec,
                      pl.BlockSpec((1, D_MODEL, FFN_EXPERT), wsel),
                      pl.BlockSpec((1, D_MODEL, FFN_EXPERT), wsel),
                      pl.BlockSpec((1, FFN_EXPERT, D_MODEL), wsel)],
            out_specs=any_spec,
            scratch_shapes=[pltpu.VMEM((2, tm // MOE_SUB, MOE_SUB, D_MODEL), F32),
                            pltpu.VMEM((2, tm // MOE_SUB, MOE_SUB, D_MODEL), F32),
                            pltpu.SemaphoreType.DMA((2,)), pltpu.SemaphoreType.DMA((2,))]),
        out_shape=jax.ShapeDtypeStruct((out_rows, D_MODEL), F32),
        compiler_params=_params("arbitrary"),
        name="moe_experts",
    )(tile_expert, nused, src, dst, xn, wg, wu, wd)


def _combine_kernel(h_ref, meta_ref, y1_ref, y2_ref, o_ref):
    meta = meta_ref[...]
    o_ref[...] = h_ref[...] + meta[:, 2:3] * y1_ref[...] + meta[:, 3:4] * y2_ref[...]


def _combine(h2, meta, ys):
    t = h2.shape[0]
    tm = _tile(t, 512)
    nt = t // tm
    row = lambda i: (i, 0)
    return pl.pallas_call(
        _combine_kernel,
        grid=(nt,),
        in_specs=[pl.BlockSpec((tm, D_MODEL), row), pl.BlockSpec((tm, LANES), row),
                  pl.BlockSpec((tm, D_MODEL), row), pl.BlockSpec((tm, D_MODEL), lambda i: (nt + i, 0))],
        out_specs=pl.BlockSpec((tm, D_MODEL), row),
        out_shape=jax.ShapeDtypeStruct((t, D_MODEL), F32),
        compiler_params=_params("parallel"),
        name="moe_combine",
    )(h2, meta, ys, ys)


def _moe(h2, g, router, wg, wu, wd):
    t = h2.shape[0]
    tm = MOE_TM
    wr = jnp.pad(router, ((0, 0), (0, LANES - N_EXPERTS)))
    wr_hi = wr.astype(BF16)
    wr_lo = (wr - wr_hi.astype(F32)).astype(BF16)
    xn, meta, cnt = _router(h2, g, jnp.stack([wr_hi, wr_lo]))

    e1, e2, r1, r2 = (meta[:, c].astype(jnp.int32) for c in (0, 1, 4, 5))
    count = cnt[0, :N_EXPERTS].astype(jnp.int32)
    ntile = (count + tm - 1) // tm
    tile_end = jnp.cumsum(ntile)
    off = (tile_end - ntile) * tm
    rows = 2 * t + N_EXPERTS * tm
    tile_id = jnp.arange(rows // tm)
    tile_expert = jnp.minimum(
        jnp.sum(tile_id[:, None] >= tile_end[None, :], axis=1), N_EXPERTS - 1).astype(jnp.int32)
    nused = tile_end[-1:].astype(jnp.int32)
    pos1 = off[e1] + r1
    pos2 = off[e2] + r2
    pad = jnp.stack([jnp.append(off + count, tile_end[-1] * tm),
                     jnp.append(off + ntile * tm, rows)], axis=1).reshape(-1).astype(jnp.int32)
    src, dst = _moe_index(pos1, pos2, pad, rows)

    ys = _experts(tile_expert, nused, src, dst, xn, wg, wu, wd, rows + tm)
    return _combine(h2, meta, ys)


def _segment_matrix(width):
    i = jnp.arange(width) // HEAD_DIM
    return (i[:, None] == i[None, :]).astype(BF16)


def _layer0(h2, bsz, seq, norm_mix, w_in, mu_shift, w0, w_decay_up, a0, w_iclr_up, w_gate_up, k_k, k_a, r_k,
            gn_w, gn_b, conv_w, conv_b, dt_bias, a_log, d_skip, ssd_norm, w_out, norm_ffn, ffn_gate, ffn_up,
            ffn_down):
    row = lambda a: a.reshape(1, -1)
    seg = _segment_matrix(RWKV_DIM)
    wa = w_in[:, :RWKV_COLS].astype(BF16)
    wb = jnp.pad(w_in[:, RWKV_COLS:], ((0, 0), (0, SSD_COLS_PAD - SSD_COLS))).astype(BF16)
    lora = w_decay_up.shape[0]
    wd = jnp.concatenate([w_decay_up, jnp.zeros((LANES - lora, RWKV_DIM), F32)], axis=0).astype(BF16)
    wi = jnp.concatenate([jnp.zeros((LANES - lora, RWKV_DIM), F32), w_iclr_up], axis=0).astype(BF16)
    pb, r, k, v, lw, kkn, b, g, bonus = _in_proj(
        h2, bsz, seq, row(norm_mix), wa, wb, row(mu_shift), row(w0), wd, row(a0), wi, w_gate_up.astype(BF16),
        row(k_k), row(k_a), row(r_k), seg)
    y = _rwkv_chunk(r, k, v, lw, kkn, b, bsz, seq)

    lane_pad = lambda a: jnp.pad(row(a), ((0, 0), (0, LANES - a.size)))
    ys = _ssd(pb, bsz, seq, conv_w, row(conv_b), lane_pad(dt_bias), lane_pad(a_log), lane_pad(d_skip),
              row(ssd_norm))

    w_out = w_out.astype(BF16)
    return _mix_ffn(h2, y, bonus, g, ys, row(gn_w), row(gn_b), seg, w_out[:RWKV_DIM], w_out[RWKV_DIM:],
                    row(norm_ffn), ffn_gate.astype(BF16), ffn_up.astype(BF16), ffn_down.astype(BF16))


def _layer1(h2, bsz, seq, tables, norm_mix, w_qkv, b_qkv, q_norm, k_norm, sinks, w_o, b_o, norm_ffn, router,
            exp_gate, exp_up, exp_down):
    row = lambda a: a.reshape(1, -1)
    two = lambda a: jnp.tile(a, LANES // HEAD_DIM).reshape(1, LANES)
    cos_f, sin_a, sin_b = tables
    q, k, v = _qkv(h2, row(norm_mix), w_qkv.astype(BF16), row(b_qkv), two(q_norm), two(k_norm), cos_f, sin_a,
                   sin_b, _segment_matrix(LANES))
    h2 = _attn(h2, q, k, v, row(sinks), w_o.astype(BF16), row(b_o), bsz, seq)

    return _moe(h2, row(norm_ffn), router, exp_gate.astype(BF16), exp_up.astype(BF16), exp_down.astype(BF16))


def kernel(x, positions, ev_norm_mix, ev_w_in, ev_mu_shift, ev_w0, ev_w_decay_up, ev_a0, ev_w_iclr_up, ev_w_gate_up, ev_k_k, ev_k_a, ev_r_k, ev_gn_w, ev_gn_b, ev_conv_w, ev_conv_b, ev_dt_bias, ev_a_log, ev_d_skip, ev_ssd_norm, ev_w_out, ev_norm_ffn, ev_ffn_gate, ev_ffn_up, ev_ffn_down, od_norm_mix, od_w_qkv, od_b_qkv, od_q_norm, od_k_norm, od_sinks, od_w_o, od_b_o, od_norm_ffn, od_router, od_exp_gate, od_exp_up, od_exp_down):
    bsz, seq, d = x.shape
    depth = ev_norm_mix.shape[0] + od_norm_mix.shape[0]
    tables = _rope_tables(positions)
    h2 = x.reshape(bsz * seq, d)
    for layer in range(depth):
        i = layer // 2
        if layer % 2 == 0:
            h2 = _layer0(h2, bsz, seq, ev_norm_mix[i], ev_w_in[i], ev_mu_shift[i], ev_w0[i], ev_w_decay_up[i],
                         ev_a0[i], ev_w_iclr_up[i], ev_w_gate_up[i], ev_k_k[i], ev_k_a[i], ev_r_k[i],
                         ev_gn_w[i], ev_gn_b[i], ev_conv_w[i], ev_conv_b[i], ev_dt_bias[i], ev_a_log[i],
                         ev_d_skip[i], ev_ssd_norm[i], ev_w_out[i], ev_norm_ffn[i], ev_ffn_gate[i],
                         ev_ffn_up[i], ev_ffn_down[i])
        else:
            h2 = _layer1(h2, bsz, seq, tables, od_norm_mix[i], od_w_qkv[i], od_b_qkv[i], od_q_norm[i],
                         od_k_norm[i], od_sinks[i], od_w_o[i], od_b_o[i], od_norm_ffn[i], od_router[i],
                         od_exp_gate[i], od_exp_up[i], od_exp_down[i])
    return h2.reshape(bsz, seq, d)
```

```python
import functools
import math

import jax
import jax.numpy as jnp
from jax import lax
from jax.experimental import pallas as pl
from jax.experimental.pallas import tpu as pltpu

F32 = jnp.float32
BF16 = jnp.bfloat16

D_MODEL = 1024
HEAD_DIM = 64
NORM_EPS = 1e-6

RWKV_HEADS = 8
RWKV_DIM = 512
RWKV_COLS = 1792
RWKV_GN_EPS = 64e-5
RWKV_CHUNK = 64
RWKV_GROUP = 256

SSD_HEADS = 8
SSD_DIM = 512
SSD_GROUPS = 2
SSD_STATE = 128
SSD_CONV = 4
SSD_CHUNK = 128
SSD_XBC = 1024
SSD_COLS = 1544
SSD_COLS_PAD = 1664

Q_HEADS = 16
KV_HEADS = 4
GQA_GROUP = 4
Q_DIM = 1024
KV_DIM = 256
WINDOW = 128
ROPE_THETA = 500000.0
ROPE_DIM = 16

FFN_DENSE = 2816
N_EXPERTS = 8
FFN_EXPERT = 1408
MOE_TM = 256
MOE_SUB = 8
MOE_FCHUNK = 256

LANES = 128
SEG_WIDTH = 256
VMEM_LIMIT_BYTES = 56 * 1024 * 1024


def _params(*sem):
    return pltpu.CompilerParams(dimension_semantics=sem, vmem_limit_bytes=VMEM_LIMIT_BYTES)


def _bdot(a, b):
    return jnp.dot(a.astype(BF16), b.astype(BF16), preferred_element_type=F32)


def _bdot_nt(a, b):
    return lax.dot_general(a.astype(BF16), b.astype(BF16), (((1,), (1,)), ((), ())),
                           preferred_element_type=F32)


def _bdot_tn(a, b):
    return lax.dot_general(a.astype(BF16), b.astype(BF16), (((0,), (0,)), ((), ())),
                           preferred_element_type=F32)


def _split(x):
    hi = x.astype(BF16)
    lo = (x - hi.astype(F32)).astype(BF16)
    return hi, lo


def _dot_exact_lhs(m, x):
    hi, lo = _split(x)
    return (jnp.dot(m, hi, preferred_element_type=F32) + jnp.dot(m, lo, preferred_element_type=F32))


def _head_sums(x, seg):
    w = seg.shape[0]
    xb = x.astype(BF16)
    return jnp.concatenate(
        [jnp.dot(xb[:, c:c + w], seg, preferred_element_type=F32) for c in range(0, x.shape[1], w)], axis=1)


def _sigmoid(x):
    return 1.0 / (1.0 + jnp.exp(-x))


def _silu(x):
    return x * _sigmoid(x)


def _softplus(x):
    return jnp.maximum(x, 0.0) + jnp.log(1.0 + jnp.exp(-jnp.abs(x)))


def _rms(x, g):
    ms = jnp.mean(x * x, axis=-1, keepdims=True)
    return x * lax.rsqrt(ms + NORM_EPS) * g


def _tile(n, pref):
    t = min(n, pref)
    while n % t:
        t //= 2
    return t


def _full(shape):
    nd = len(shape)
    return pl.BlockSpec(shape, lambda *_: (0,) * nd)


def _resident(shape):
    nd = len(shape)
    return pl.BlockSpec(shape, lambda *_: (0,) * nd, pipeline_mode=pl.Buffered(1))


def _in_proj_kernel(x_ref, g_ref, wa_ref, wb_ref, mu_ref, w0_ref, wd_ref, a0_ref, wi_ref, wg_ref, kk_ref, ka_ref,
                    rk_ref, seg_ref, pb_ref, r_ref, k_ref, v_ref, lw_ref, kkn_ref, b_ref, gate_ref, bonus_ref,
                    carry_ref):
    @pl.when(pl.program_id(1) == 0)
    def _():
        carry_ref[...] = jnp.zeros_like(carry_ref)

    xn = _rms(x_ref[...], g_ref[...]).astype(BF16)
    pb_ref[...] = jnp.dot(xn, wb_ref[...], preferred_element_type=F32)
    pa = jnp.dot(xn, wa_ref[...], preferred_element_type=F32)
    tm = pa.shape[0]
    row = lax.broadcasted_iota(jnp.int32, pa.shape, 0)
    prev = jnp.where(row == 0, carry_ref[...], pltpu.roll(pa, 1, axis=0))
    carry_ref[...] = pa[tm - 1:tm, :]
    x = pa + (prev - pa) * mu_ref[...]

    r = x[:, 0:RWKV_DIM]
    k = x[:, RWKV_DIM:2 * RWKV_DIM]
    v = x[:, 2 * RWKV_DIM:3 * RWKV_DIM]
    lora = x[:, 3 * RWKV_DIM:3 * RWKV_DIM + LANES]
    gl = x[:, 3 * RWKV_DIM + LANES:]
    seg = seg_ref[...]

    w_raw = w0_ref[...] + _bdot(jnp.tanh(lora), wd_ref[...])
    lw_ref[...] = (-math.exp(-0.5)) * _sigmoid(w_raw)
    iclr = _sigmoid(a0_ref[...] + _bdot(lora, wi_ref[...]))
    gate_ref[...] = _bdot(_sigmoid(gl), wg_ref[...])

    kk = k * kk_ref[...]
    kkn = kk * lax.rsqrt(_head_sums(kk * kk, seg) + 1e-12)
    k2 = k * (1.0 + (iclr - 1.0) * ka_ref[...])
    r_ref[...] = r
    k_ref[...] = k2
    v_ref[...] = v
    kkn_ref[...] = kkn
    b_ref[...] = kkn * iclr
    bonus_ref[...] = _head_sums(r * k2 * rk_ref[...], seg) * v


def _in_proj(x2, bsz, seq, g, wa, wb, mu, w0, wd, a0, wi, wg, k_k, k_a, r_k, seg):
    t = x2.shape[0]
    tm = _tile(seq, 256)
    nt = seq // tm
    row = lambda b, i: (b * nt + i, 0)
    out = jax.ShapeDtypeStruct((t, RWKV_DIM), F32)
    small = [mu, w0, wd, a0, wi, wg, k_k, k_a, r_k, seg]
    return pl.pallas_call(
        _in_proj_kernel,
        grid=(bsz, nt),
        in_specs=[pl.BlockSpec((tm, D_MODEL), row), _full((1, D_MODEL)), _resident(wa.shape),
                  _resident(wb.shape)] + [_full(a.shape) for a in small],
        out_specs=[pl.BlockSpec((tm, SSD_COLS_PAD), row)] + [pl.BlockSpec((tm, RWKV_DIM), row)] * 8,
        out_shape=[jax.ShapeDtypeStruct((t, SSD_COLS_PAD), F32)] + [out] * 8,
        scratch_shapes=[pltpu.VMEM((1, RWKV_COLS), F32)],
        compiler_params=_params("arbitrary", "arbitrary"),
        name="in_proj",
    )(x2, g, wa, wb, *small)


def _rwkv_chunk_kernel(r_ref, k_ref, v_ref, lw_ref, kk_ref, b_ref, y_ref, z_ref, *, lt):
    c_len = RWKV_CHUNK
    n = HEAD_DIM
    gl = RWKV_GROUP
    nchunk = gl // c_len
    nheads = LANES // n
    shift = c_len.bit_length() - 1

    @pl.when(pl.program_id(2) == 0)
    def _():
        z_ref[...] = jnp.zeros_like(z_ref)

    ri = lax.broadcasted_iota(jnp.int32, (gl, gl), 0)
    ci = lax.broadcasted_iota(jnp.int32, (gl, gl), 1)
    tri_bd = jnp.where((ci <= ri) & (ci >= ((ri >> shift) << shift)), 1.0, 0.0).astype(BF16)
    re_ = lax.broadcasted_iota(jnp.int32, (c_len, c_len), 0)
    ce_ = lax.broadcasted_iota(jnp.int32, (c_len, c_len), 1)
    eye = re_ == ce_
    ri2 = lax.broadcasted_iota(jnp.int32, (2 * gl, gl), 0)
    ci2 = lax.broadcasted_iota(jnp.int32, (2 * gl, gl), 1)
    t2 = jnp.where(ri2 < gl, ri2, ri2 - gl)
    mask2 = (ci2 <= jnp.where(ri2 < gl, t2 - 1, t2)) & (ci2 >= ((t2 >> shift) << shift))
    zeros = jnp.zeros((c_len, n), F32)

    nsub = lt // gl
    sysid = [(s, h) for s in range(nsub) for h in range(nheads)]
    pre = []
    for s in range(nsub):
        sl = pl.ds(s * gl, gl)
        lw = lw_ref[sl, :]
        g_in = _dot_exact_lhs(tri_bd, lw)
        ends = [g_in[(c + 1) * c_len - 1:(c + 1) * c_len, :] for c in range(nchunk)]
        g_end = jnp.concatenate([jnp.broadcast_to(e, (c_len, LANES)) for e in ends], axis=0)
        e_end = jnp.exp(g_end - g_in)
        en = jnp.exp(-g_in)
        k = k_ref[sl, :]
        b = b_ref[sl, :]
        pre.append(dict(
            ends=ends, v=v_ref[sl, :], rt=r_ref[sl, :] * jnp.exp(g_in),
            at=-kk_ref[sl, :] * jnp.exp(g_in - lw), kt=k * en, bt=b * en, bend=b * e_end, kend=k * e_end))

    def hs(name, s, h):
        return pre[s][name][:, h * n:(h + 1) * n]

    xb, xk, xkv, p, x = {}, {}, {}, {}, {}
    for q in sysid:
        la = jnp.concatenate([hs("at", *q), hs("rt", *q)], axis=0)
        xb[q] = jnp.where(mask2, _bdot_nt(la, hs("bt", *q)), 0.0)
        xk[q] = jnp.where(mask2, _bdot_nt(la, hs("kt", *q)), 0.0)
    for q in sysid:
        xkv[q] = _bdot(xk[q], hs("v", *q))
        p[q] = xb[q][0:gl]
        x[q] = jnp.concatenate([hs("at", *q), xkv[q][0:gl]], axis=1)
    for i in range(6):
        for q in sysid:
            x[q] = x[q] + _bdot(p[q], x[q])
        if i < 5:
            for q in sysid:
                p[q] = _bdot(p[q], p[q])
    rq, y0, mn = {}, {}, {}
    for q in sysid:
        yy = _bdot(xb[q][gl:], x[q])
        rq[q] = yy[:, 0:n] + hs("rt", *q)
        y0[q] = yy[:, n:] + xkv[q][gl:]
        bend_h, kend_h, v_h = hs("bend", *q), hs("kend", *q), hs("v", *q)
        for c in range(nchunk):
            rows = slice(c * c_len, (c + 1) * c_len)
            lhs = jnp.concatenate([bend_h[rows], kend_h[rows]], axis=0)
            rhs = jnp.concatenate([x[q][rows], jnp.concatenate([zeros, v_h[rows]], axis=1)], axis=0)
            mn[q + (c,)] = _bdot_tn(lhs, rhs)
    zs = [z_ref[h] for h in range(nheads)]
    for s in range(nsub):
        yh = [[] for _ in range(nheads)]
        for c in range(nchunk):
            rows = slice(c * c_len, (c + 1) * c_len)
            for h in range(nheads):
                q = (s, h)
                pend = jnp.exp(pre[s]["ends"][c][:, h * n:(h + 1) * n])
                m_mat = mn[q + (c,)][:, 0:n] + jnp.where(eye, pend, 0.0)
                yh[h].append(_bdot(rq[q][rows], zs[h]) + y0[q][rows])
                zs[h] = _bdot(m_mat, zs[h]) + mn[q + (c,)][:, n:]
        y_ref[pl.ds(s * gl, gl), :] = jnp.concatenate(
            [jnp.concatenate(yh[h], axis=0) for h in range(nheads)], axis=1)
    for h in range(nheads):
        z_ref[h] = zs[h]


def _rwkv_chunk(r, k, v, lw, kkn, b, bsz, seq):
    t = r.shape[0]
    lt = _tile(seq, 512)
    nt = seq // lt
    spec = pl.BlockSpec((lt, LANES), lambda bi, hp, i: (bi * nt + i, hp))
    return pl.pallas_call(
        functools.partial(_rwkv_chunk_kernel, lt=lt),
        grid=(bsz, RWKV_DIM // LANES, nt),
        in_specs=[spec] * 6,
        out_specs=spec,
        out_shape=jax.ShapeDtypeStruct((t, RWKV_DIM), F32),
        scratch_shapes=[pltpu.VMEM((2, HEAD_DIM, HEAD_DIM), F32)],
        compiler_params=_params("arbitrary", "arbitrary", "arbitrary"),
        name="rwkv_chunk",
    )(r, k, v, lw, kkn, b)


def _ssd_kernel(pb_ref, cw_ref, cb_ref, dtb_ref, alog_ref, dskip_ref, nrm_ref, ys_ref, ext_ref, st_ref):
    q = SSD_CHUNK
    p = HEAD_DIM
    hpg = SSD_HEADS // SSD_GROUPS

    @pl.when(pl.program_id(1) == 0)
    def _():
        ext_ref[0:8, :] = jnp.zeros((8, SSD_XBC), F32)
        st_ref[...] = jnp.zeros_like(st_ref)

    z = pb_ref[:, 0:SSD_DIM]
    u = pb_ref[:, SSD_DIM:SSD_DIM + SSD_XBC]
    dt_raw = pb_ref[:, SSD_DIM + SSD_XBC:]

    ext_ref[8:8 + q, :] = u
    conv = cb_ref[...] + cw_ref[SSD_CONV - 1:SSD_CONV, :] * u
    for j in range(SSD_CONV - 1):
        off = 8 - (SSD_CONV - 1) + j
        conv = conv + cw_ref[j:j + 1, :] * ext_ref[off:off + q, :]
    ext_ref[0:8, :] = u[q - 8:q, :]
    xbc = _silu(conv)
    xs = xbc[:, 0:SSD_DIM]
    bm = xbc[:, SSD_DIM:SSD_DIM + SSD_GROUPS * SSD_STATE]
    cm = xbc[:, SSD_DIM + SSD_GROUPS * SSD_STATE:]

    dt = _softplus(dt_raw + dtb_ref[...])
    a = -jnp.exp(alog_ref[...])
    ri = lax.broadcasted_iota(jnp.int32, (q, q), 0)
    ci = lax.broadcasted_iota(jnp.int32, (q, q), 1)
    causal = ri >= ci
    cum = _dot_exact_lhs(causal.astype(BF16), dt * a)
    cum_t = cum.T
    dt_t = dt.T
    cum_end = cum[q - 1:q, :]
    to_end = jnp.exp(cum_end - cum) * dt
    ecum = jnp.exp(cum)
    edec = jnp.exp(cum_end)

    ys = []
    for g in range(SSD_GROUPS):
        bm_g = bm[:, g * SSD_STATE:(g + 1) * SSD_STATE]
        cm_g = cm[:, g * SSD_STATE:(g + 1) * SSD_STATE]
        cb = _bdot_nt(cm_g, bm_g)
        bm_t = bm_g.T
        for hh in range(hpg):
            h = g * hpg + hh
            x_h = xs[:, h * p:(h + 1) * p]
            seg = cum[:, h:h + 1] - cum_t[h:h + 1, :]
            ldec = jnp.exp(jnp.where(causal, seg, -jnp.inf))
            wts = cb * ldec * dt_t[h:h + 1, :]
            y = _bdot(wts, x_h)
            h_prev = st_ref[h]
            y = y + _bdot(cm_g, h_prev) * ecum[:, h:h + 1]
            st_ref[h] = h_prev * edec[:, h:h + 1] + _bdot(bm_t, x_h * to_end[:, h:h + 1])
            ys.append(y + dskip_ref[:, h:h + 1] * x_h)
    yall = jnp.concatenate(ys, axis=1) * _silu(z)
    gw = SSD_DIM // SSD_GROUPS
    outs = []
    for g in range(SSD_GROUPS):
        yg = yall[:, g * gw:(g + 1) * gw]
        ms = jnp.mean(yg * yg, axis=-1, keepdims=True)
        outs.append(yg * lax.rsqrt(ms + NORM_EPS) * nrm_ref[:, g * gw:(g + 1) * gw])
    ys_ref[...] = jnp.concatenate(outs, axis=1)


def _ssd(pb, bsz, seq, conv_w, conv_b, dt_bias, a_log, d_skip, ssd_norm):
    t = pb.shape[0]
    nc = seq // SSD_CHUNK
    row = lambda b, i: (b * nc + i, 0)
    return pl.pallas_call(
        _ssd_kernel,
        grid=(bsz, nc),
        in_specs=[pl.BlockSpec((SSD_CHUNK, SSD_COLS_PAD), row), _full(conv_w.shape), _full(conv_b.shape),
                  _full(dt_bias.shape), _full(a_log.shape), _full(d_skip.shape), _full(ssd_norm.shape)],
        out_specs=pl.BlockSpec((SSD_CHUNK, SSD_DIM), row),
        out_shape=jax.ShapeDtypeStruct((t, SSD_DIM), F32),
        scratch_shapes=[pltpu.VMEM((8 + SSD_CHUNK, SSD_XBC), F32),
                        pltpu.VMEM((SSD_HEADS, SSD_STATE, HEAD_DIM), F32)],
        compiler_params=_params("arbitrary", "arbitrary"),
        name="ssd",
    )(pb, conv_w, conv_b, dt_bias, a_log, d_skip, ssd_norm)


def _mix_ffn_kernel(h_ref, y_ref, bonus_ref, gate_ref, ys_ref, gnw_ref, gnb_ref, seg_ref, wa_ref, wb_ref,
                    g_ref, wg_ref, wu_ref, wd_ref, o_ref):
    y = y_ref[...]
    seg = seg_ref[...]
    mu = _head_sums(y, seg) * (1.0 / HEAD_DIM)
    yc = y - mu
    var = _head_sums(yc * yc, seg) * (1.0 / HEAD_DIM)
    yn = yc * lax.rsqrt(var + RWKV_GN_EPS) * gnw_ref[...] + gnb_ref[...]
    ya = (yn + bonus_ref[...]) * gate_ref[...]
    h = h_ref[...] + _bdot(ya, wa_ref[...]) + _bdot(ys_ref[...], wb_ref[...])

    xn = _rms(h, g_ref[...]).astype(BF16)
    act = _silu(jnp.dot(xn, wg_ref[...], preferred_element_type=F32)) * jnp.dot(
        xn, wu_ref[...], preferred_element_type=F32)
    o_ref[...] = h + jnp.dot(act.astype(BF16), wd_ref[...], preferred_element_type=F32)


def _mix_ffn(h2, y, bonus, gate, ys, gn_w, gn_b, seg, wa, wb, g, wg, wu, wd):
    t = h2.shape[0]
    tm = _tile(t, 512)
    row = lambda i: (i, 0)
    half = pl.BlockSpec((tm, RWKV_DIM), row)
    return pl.pallas_call(
        _mix_ffn_kernel,
        grid=(t // tm,),
        in_specs=[pl.BlockSpec((tm, D_MODEL), row), half, half, half, half, _full(gn_w.shape),
                  _full(gn_b.shape), _resident(seg.shape), _resident(wa.shape), _resident(wb.shape),
                  _full((1, D_MODEL)), _resident(wg.shape), _resident(wu.shape), _resident(wd.shape)],
        out_specs=pl.BlockSpec((tm, D_MODEL), row),
        out_shape=jax.ShapeDtypeStruct((t, D_MODEL), F32),
        compiler_params=_params("parallel"),
        name="mix_ffn",
    )(h2, y, bonus, gate, ys, gn_w, gn_b, seg, wa, wb, g, wg, wu, wd)


def _rope_table_kernel(pos_ref, freq_ref, cos_ref, sin_ref):
    ang = pos_ref[...] * freq_ref[...]
    cos_ref[...] = jnp.cos(ang)
    sin_ref[...] = jnp.sin(ang)


def _rope_tables(positions):
    t = positions.size
    half = ROPE_DIM // 2
    rows = t * half // LANES
    inv_freq = ROPE_THETA ** (-jnp.arange(0, ROPE_DIM, 2, dtype=F32) / ROPE_DIM)
    pos_rep = jnp.repeat(positions.reshape(-1).astype(F32), half).reshape(rows, LANES)
    freq = jnp.tile(inv_freq, LANES // half).reshape(1, LANES)
    tr = _tile(rows, 256)
    cos, sin = pl.pallas_call(
        _rope_table_kernel,
        grid=(rows // tr,),
        in_specs=[pl.BlockSpec((tr, LANES), lambda i: (i, 0)), _full((1, LANES))],
        out_specs=[pl.BlockSpec((tr, LANES), lambda i: (i, 0))] * 2,
        out_shape=[jax.ShapeDtypeStruct((rows, LANES), F32)] * 2,
        compiler_params=_params("parallel"),
        name="rope_tables",
    )(pos_rep, freq)
    return jnp.concatenate([cos.reshape(t, half) - 1.0, sin.reshape(t, half)], axis=1)


def _rope_expansion(width):
    half = ROPE_DIM // 2
    d = jnp.arange(width) % HEAD_DIM
    j = jnp.arange(half)[:, None]
    zero = jnp.zeros((half, width), F32)
    cos_rows = jnp.where((d[None, :] < ROPE_DIM) & (d[None, :] % half == j), 1.0, 0.0)
    sin_a = jnp.where((d[None, :] < half) & (d[None, :] == j), -1.0, 0.0)
    sin_b = jnp.where((d[None, :] >= half) & (d[None, :] < ROPE_DIM) & (d[None, :] - half == j), 1.0, 0.0)
    one = jnp.concatenate([jnp.concatenate([cos_rows, zero, zero], axis=1),
                           jnp.concatenate([zero, sin_a, sin_b], axis=1)], axis=0)
    return jnp.concatenate([one, one], axis=0).astype(BF16)


def _qkv_kernel(h_ref, g_ref, w_ref, b_ref, qn_ref, kn_ref, cs_ref, exp_ref, seg_ref, q_ref, k_ref, v_ref):
    xn = _rms(h_ref[...], g_ref[...]).astype(BF16)
    qkv = jnp.dot(xn, w_ref[...], preferred_element_type=F32) + b_ref[...]
    seg = seg_ref[...]
    wide = seg.shape[0]
    hi, lo = _split(cs_ref[...])
    tab = jnp.dot(jnp.concatenate([hi, lo], axis=1), exp_ref[...], preferred_element_type=F32)
    cos = 1.0 + tab[:, 0:wide]
    sa = tab[:, wide:2 * wide]
    sb = tab[:, 2 * wide:]

    def norm_rope(x, gain):
        ms = _head_sums(x * x, seg) * (1.0 / HEAD_DIM)
        xn_ = x * lax.rsqrt(ms + NORM_EPS) * gain
        return (xn_ * cos + pltpu.roll(xn_, wide - ROPE_DIM // 2, axis=1) * sa
                + pltpu.roll(xn_, ROPE_DIM // 2, axis=1) * sb)

    for c in range(Q_DIM // wide):
        x = qkv[:, c * wide:(c + 1) * wide]
        q_ref[:, c * wide:(c + 1) * wide] = (norm_rope(x, qn_ref[...]) * (HEAD_DIM ** -0.5)).astype(BF16)
    for c in range(KV_DIM // wide):
        x = qkv[:, Q_DIM + c * wide:Q_DIM + (c + 1) * wide]
        k_ref[:, c * wide:(c + 1) * wide] = norm_rope(x, kn_ref[...]).astype(BF16)
    v_ref[...] = qkv[:, Q_DIM + KV_DIM:].astype(BF16)


def _qkv(h2, g, w, b, qn, kn, cs, expand, seg):
    t = h2.shape[0]
    tm = _tile(t, 256)
    row = lambda i: (i, 0)
    return pl.pallas_call(
        _qkv_kernel,
        grid=(t // tm,),
        in_specs=[pl.BlockSpec((tm, D_MODEL), row), _full((1, D_MODEL)), _resident(w.shape), _full(b.shape),
                  _full(qn.shape), _full(kn.shape), pl.BlockSpec((tm, cs.shape[1]), row), _full(expand.shape),
                  _full(seg.shape)],
        out_specs=[pl.BlockSpec((tm, Q_DIM), row), pl.BlockSpec((tm, KV_DIM), row),
                   pl.BlockSpec((tm, KV_DIM), row)],
        out_shape=[jax.ShapeDtypeStruct((t, Q_DIM), BF16), jax.ShapeDtypeStruct((t, KV_DIM), BF16),
                   jax.ShapeDtypeStruct((t, KV_DIM), BF16)],
        compiler_params=_params("parallel"),
        name="qkv",
    )(h2, g, w, b, qn, kn, cs, expand, seg)


def _attn_kernel(h_ref, q_ref, kc_ref, kp_ref, vc_ref, vp_ref, sink_ref, wo_ref, bo_ref, o_ref, *, nqb):
    w = WINDOW
    n = HEAD_DIM
    rows = GQA_GROUP * w
    qi = lax.broadcasted_iota(jnp.int32, (rows, 2 * w), 0) % w
    kj = lax.broadcasted_iota(jnp.int32, (rows, 2 * w), 1)
    diff = w + qi - kj
    band = (diff >= 0) & (diff < w)
    band0 = band & (kj >= jnp.where(pl.program_id(1) == 0, w, 0))
    kall = jnp.concatenate([kp_ref[...], kc_ref[...]], axis=0)
    vall = jnp.concatenate([vp_ref[...], vc_ref[...]], axis=0)
    units = [(qb, g) for qb in range(nqb) for g in range(KV_HEADS)]
    s, e, den = {}, {}, {}
    for u in units:
        qb, g = u
        qs = jnp.concatenate(
            [q_ref[qb * w:(qb + 1) * w, (g * GQA_GROUP + i) * n:(g * GQA_GROUP + i + 1) * n]
             for i in range(GQA_GROUP)], axis=0)
        kk = kall[qb * w:(qb + 2) * w, g * n:(g + 1) * n]
        sc = lax.dot_general(qs, kk, (((1,), (1,)), ((), ())), preferred_element_type=F32)
        s[u] = jnp.where(band0 if qb == 0 else band, sc, -jnp.inf)
    sinks = [jnp.concatenate(
        [jnp.broadcast_to(sink_ref[:, g * GQA_GROUP + i:g * GQA_GROUP + i + 1], (w, LANES))
         for i in range(GQA_GROUP)], axis=0) for g in range(KV_HEADS)]
    ones = jnp.ones((2 * w, LANES), BF16)
    for u in units:
        sink = sinks[u[1]]
        m = jnp.maximum(jnp.broadcast_to(jnp.max(s[u], axis=-1, keepdims=True), (rows, LANES)), sink)
        e[u] = jnp.exp(s[u] - jnp.concatenate([m, m], axis=1)).astype(BF16)
        den[u] = jnp.exp(sink - m)
    outs = [[None] * Q_HEADS for _ in range(nqb)]
    for u in units:
        qb, g = u
        vv = vall[qb * w:(qb + 2) * w, g * n:(g + 1) * n]
        total = jnp.dot(e[u], ones, preferred_element_type=F32) + den[u]
        o = jnp.dot(e[u], vv, preferred_element_type=F32) / total[:, 0:n]
        for i in range(GQA_GROUP):
            outs[qb][g * GQA_GROUP + i] = o[i * w:(i + 1) * w]
    att = jnp.concatenate([jnp.concatenate(outs[qb], axis=1) for qb in range(nqb)], axis=0)
    o_ref[...] = h_ref[...] + _bdot(att, wo_ref[...]) + bo_ref[...]


def _attn(h2, q, k, v, sinks, wo, bo, bsz, seq):
    t = h2.shape[0]
    nqb = 2 if seq % (2 * WINDOW) == 0 else 1
    tq = nqb * WINDOW
    nb = seq // tq
    cur = lambda b, i: (b * nb + i, 0)
    prv = lambda b, i: (b * nb * nqb + jnp.maximum(i * nqb - 1, 0), 0)
    return pl.pallas_call(
        functools.partial(_attn_kernel, nqb=nqb),
        grid=(bsz, nb),
        in_specs=[pl.BlockSpec((tq, D_MODEL), cur), pl.BlockSpec((tq, Q_DIM), cur),
                  pl.BlockSpec((tq, KV_DIM), cur), pl.BlockSpec((WINDOW, KV_DIM), prv),
                  pl.BlockSpec((tq, KV_DIM), cur), pl.BlockSpec((WINDOW, KV_DIM), prv),
                  _full(sinks.shape), _full(wo.shape), _full(bo.shape)],
        out_specs=pl.BlockSpec((tq, D_MODEL), cur),
        out_shape=jax.ShapeDtypeStruct((t, D_MODEL), F32),
        compiler_params=_params("parallel", "parallel"),
        name="attn",
    )(h2, q, k, k, v, v, sinks, wo, bo)


def _router_kernel(h_ref, g_ref, wr_ref, xn_ref, meta_ref, cnt_ref, base_ref):
    @pl.when(pl.program_id(0) == 0)
    def _():
        base_ref[...] = jnp.zeros_like(base_ref)

    xn = _rms(h_ref[...], g_ref[...])
    xn_ref[...] = xn
    hi, lo = _split(xn)
    whi = wr_ref[0]
    wlo = wr_ref[1]
    logits = (jnp.dot(hi, whi, preferred_element_type=F32) + jnp.dot(hi, wlo, preferred_element_type=F32)
              + jnp.dot(lo, whi, preferred_element_type=F32))
    tm = logits.shape[0]
    lane = lax.broadcasted_iota(jnp.int32, logits.shape, 1)
    logits = jnp.where(lane < N_EXPERTS, logits, -jnp.inf)
    m1 = jnp.max(logits, axis=-1, keepdims=True)
    i1 = jnp.min(jnp.where(logits == m1, lane, LANES), axis=-1, keepdims=True)
    rest = jnp.where(lane == i1, -jnp.inf, logits)
    m2 = jnp.max(rest, axis=-1, keepdims=True)
    i2 = jnp.min(jnp.where(rest == m2, lane, LANES), axis=-1, keepdims=True)
    e2 = jnp.exp(m2 - m1)
    w1 = 1.0 / (1.0 + e2)
    w2 = e2 / (1.0 + e2)

    sel1 = lane == i1
    sel2 = lane == i2
    onehot = jnp.where(sel1, 1.0, 0.0) + jnp.where(sel2, 1.0, 0.0)
    ri = lax.broadcasted_iota(jnp.int32, (tm, tm), 0)
    ci = lax.broadcasted_iota(jnp.int32, (tm, tm), 1)
    before = jnp.dot(jnp.where(ci < ri, 1.0, 0.0).astype(BF16), onehot.astype(BF16),
                     preferred_element_type=F32)
    rank = base_ref[...] + before
    r1 = jnp.sum(jnp.where(sel1, rank, 0.0), axis=-1, keepdims=True)
    r2 = jnp.sum(jnp.where(sel2, rank, 0.0), axis=-1, keepdims=True)
    total = base_ref[...] + jnp.sum(onehot, axis=0, keepdims=True)
    base_ref[...] = total
    cnt_ref[...] = total
    cols = (i1.astype(F32), i2.astype(F32), w1, w2, r1, r2)
    meta = jnp.zeros(logits.shape, F32)
    for c, val in enumerate(cols):
        meta = jnp.where(lane == c, val, meta)
    meta_ref[...] = meta


def _router(h2, g, wr):
    t = h2.shape[0]
    tm = _tile(t, 512)
    row = lambda i: (i, 0)
    return pl.pallas_call(
        _router_kernel,
        grid=(t // tm,),
        in_specs=[pl.BlockSpec((tm, D_MODEL), row), _full((1, D_MODEL)), _full(wr.shape)],
        out_specs=[pl.BlockSpec((tm, D_MODEL), row), pl.BlockSpec((tm, LANES), row), _full((1, LANES))],
        out_shape=[jax.ShapeDtypeStruct((t, D_MODEL), F32), jax.ShapeDtypeStruct((t, LANES), F32),
                   jax.ShapeDtypeStruct((1, LANES), F32)],
        scratch_shapes=[pltpu.VMEM((1, LANES), F32)],
        compiler_params=_params("arbitrary"),
        name="router",
    )(h2, g, wr)


def _moe_index_kernel(p1_ref, p2_ref, pad_ref, src_ref, dst_ref, *, ntok):
    for e in range(N_EXPERTS + 1):
        def fill(p, c, e=e):
            src_ref[p] = 0
            dst_ref[p] = 2 * ntok + (e % N_EXPERTS) * MOE_TM + ((p - pad_ref[2 * e]) & (MOE_TM - 1))
            return c

        lax.fori_loop(pad_ref[2 * e], pad_ref[2 * e + 1], fill, 0)

    def body(t, c):
        p1 = p1_ref[t]
        p2 = p2_ref[t]
        src_ref[p1] = t
        dst_ref[p1] = t
        src_ref[p2] = t
        dst_ref[p2] = ntok + t
        return c

    lax.fori_loop(0, ntok, body, 0, unroll=4)


def _moe_index(pos1, pos2, pad, rows):
    smem = pl.BlockSpec(memory_space=pltpu.SMEM)
    return pl.pallas_call(
        functools.partial(_moe_index_kernel, ntok=pos1.shape[0]),
        grid_spec=pltpu.PrefetchScalarGridSpec(
            num_scalar_prefetch=3, grid=(1,), in_specs=[], out_specs=[smem, smem]),
        out_shape=[jax.ShapeDtypeStruct((rows,), jnp.int32)] * 2,
        compiler_params=_params("arbitrary"),
        name="moe_index",
    )(pos1, pos2, pad)


def _row_copy(src_ref, src_row, dst_ref, dst_row, sem):
    return pltpu.make_async_copy(src_ref.at[pl.ds(src_row, 1)], dst_ref.at[pl.ds(dst_row, 1)], sem)


def _experts_kernel(te_ref, nused_ref, src_ref, dst_ref, x_ref, wg_ref, wu_ref, wd_ref, y_ref,
                    xbuf, ybuf, gsem, ssem):
    del te_ref
    i = pl.program_id(0)
    nused = nused_ref[0]
    tm = MOE_TM

    sub = MOE_SUB
    ngroup = tm // sub
    dump0 = y_ref.shape[0] - tm

    def gather_group(tile, slot, k):
        for u in range(sub):
            _row_copy(x_ref, src_ref[tile * tm + k * sub + u], xbuf.at[slot, k], u, gsem.at[slot]).start()

    def scatter_group(tile, slot, k, placeholder):
        for u in range(sub):
            j = k * sub + u
            row = dst_ref[tile * tm + j]
            if placeholder is not None:
                row = jnp.where(placeholder, dump0 + j, row)
            _row_copy(ybuf.at[slot, k], u, y_ref, row, ssem.at[slot]).start()

    def wait_gather(slot):
        def body(k, c):
            pltpu.make_async_copy(x_ref.at[pl.ds(0, sub)], xbuf.at[slot, k], gsem.at[slot]).wait()
            return c
        lax.fori_loop(0, ngroup, body, 0)

    def wait_scatter(slot):
        def body(k, c):
            pltpu.make_async_copy(ybuf.at[slot, k], y_ref.at[pl.ds(0, sub)], ssem.at[slot]).wait()
            return c
        lax.fori_loop(0, ngroup, body, 0)

    @pl.when(i == 0)
    def _():
        ybuf[1] = jnp.zeros(ybuf.shape[1:], F32)

        def body(k, c):
            gather_group(0, 0, k)
            return c
        lax.fori_loop(0, ngroup, body, 0)

    @pl.when(i < nused)
    def _():
        slot = i % 2
        other = 1 - slot
        nxt = jnp.minimum(i + 1, nused - 1)
        prev = jnp.maximum(i - 1, 0)
        first = i == 0
        wait_gather(slot)

        @pl.when(i >= 1)
        def _():
            wait_scatter(slot)

        xb = xbuf[slot].reshape(tm, D_MODEL).astype(BF16)
        bounds = list(range(0, FFN_EXPERT, MOE_FCHUNK)) + [FFN_EXPERT]
        nchunk = len(bounds) - 1
        acts = []
        for c in range(nchunk):
            c0, c1 = bounds[c], bounds[c + 1]
            gate = jnp.dot(xb, wg_ref[0, :, c0:c1], preferred_element_type=F32)
            up = jnp.dot(xb, wu_ref[0, :, c0:c1], preferred_element_type=F32)
            acts.append((_silu(gate) * up).astype(BF16))
            for k in range(c * ngroup // nchunk, (c + 1) * ngroup // nchunk):
                gather_group(nxt, other, k)
                scatter_group(prev, other, k, first)
        y = jnp.dot(jnp.concatenate(acts, axis=1), wd_ref[0], preferred_element_type=F32)
        ybuf[slot] = y.reshape(ngroup, sub, D_MODEL)

        @pl.when(i == nused - 1)
        def _():
            def body(k, c):
                scatter_group(i, slot, k, None)
                return c
            lax.fori_loop(0, ngroup, body, 0)
            wait_scatter(other)
            wait_scatter(slot)
            wait_gather(other)


def _experts(tile_expert, nused, src, dst, xn, wg, wu, wd, out_rows):
    ntile = tile_expert.shape[0]
    tm = MOE_TM
    wsel = lambda i, te, nu, s, d: (te[i], 0, 0)
    any_spec = pl.BlockSpec(memory_space=pl.ANY)
    return pl.pallas_call(
        _experts_kernel,
        grid_spec=pltpu.PrefetchScalarGridSpec(
            num_scalar_prefetch=4, grid=(ntile,),
            in_specs=[any_spec,
                      pl.BlockSpec((1, D_MODEL, FFN_EXPERT), wsel),
                      pl.BlockSpec((1, D_MODEL, FFN_EXPERT), wsel),
                      pl.BlockSpec((1, FFN_EXPERT, D_MODEL), wsel)],
            out_specs=any_spec,
            scratch_shapes=[pltpu.VMEM((2, tm // MOE_SUB, MOE_SUB, D_MODEL), F32),
                            pltpu.VMEM((2, tm // MOE_SUB, MOE_SUB, D_MODEL), F32),
                            pltpu.SemaphoreType.DMA((2,)), pltpu.SemaphoreType.DMA((2,))]),
        out_shape=jax.ShapeDtypeStruct((out_rows, D_MODEL), F32),
        compiler_params=_params("arbitrary"),
        name="moe_experts",
    )(tile_expert, nused, src, dst, xn, wg, wu, wd)


def _combine_kernel(h_ref, meta_ref, y1_ref, y2_ref, o_ref):
    meta = meta_ref[...]
    o_ref[...] = h_ref[...] + meta[:, 2:3] * y1_ref[...] + meta[:, 3:4] * y2_ref[...]


def _combine(h2, meta, ys):
    t = h2.shape[0]
    tm = _tile(t, 512)
    nt = t // tm
    row = lambda i: (i, 0)
    return pl.pallas_call(
        _combine_kernel,
        grid=(nt,),
        in_specs=[pl.BlockSpec((tm, D_MODEL), row), pl.BlockSpec((tm, LANES), row),
                  pl.BlockSpec((tm, D_MODEL), row), pl.BlockSpec((tm, D_MODEL), lambda i: (nt + i, 0))],
        out_specs=pl.BlockSpec((tm, D_MODEL), row),
        out_shape=jax.ShapeDtypeStruct((t, D_MODEL), F32),
        compiler_params=_params("parallel"),
        name="moe_combine",
    )(h2, meta, ys, ys)


def _moe(h2, g, router, wg, wu, wd):
    t = h2.shape[0]
    tm = MOE_TM
    wr = jnp.pad(router, ((0, 0), (0, LANES - N_EXPERTS)))
    wr_hi = wr.astype(BF16)
    wr_lo = (wr - wr_hi.astype(F32)).astype(BF16)
    xn, meta, cnt = _router(h2, g, jnp.stack([wr_hi, wr_lo]))

    cols = meta[:, :8].T.astype(jnp.int32)
    e1, e2, r1, r2 = cols[0], cols[1], cols[4], cols[5]
    count = cnt[0, :N_EXPERTS].astype(jnp.int32)
    ntile = (count + tm - 1) // tm
    tile_end = jnp.cumsum(ntile)
    off = (tile_end - ntile) * tm
    rows = 2 * t + N_EXPERTS * tm
    tile_id = jnp.arange(rows // tm)
    tile_expert = jnp.minimum(
        jnp.sum(tile_id[:, None] >= tile_end[None, :], axis=1), N_EXPERTS - 1).astype(jnp.int32)
    nused = tile_end[-1:].astype(jnp.int32)
    expert = jnp.arange(N_EXPERTS, dtype=jnp.int32)[:, None]
    pos1 = r1 + jnp.sum(jnp.where(e1[None, :] == expert, off[:, None], 0), axis=0)
    pos2 = r2 + jnp.sum(jnp.where(e2[None, :] == expert, off[:, None], 0), axis=0)
    pad = jnp.stack([jnp.append(off + count, tile_end[-1] * tm),
                     jnp.append(off + ntile * tm, rows)], axis=1).reshape(-1).astype(jnp.int32)
    src, dst = _moe_index(pos1, pos2, pad, rows)

    ys = _experts(tile_expert, nused, src, dst, xn, wg, wu, wd, rows + tm)
    return _combine(h2, meta, ys)


def _segment_matrix(width):
    i = jnp.arange(width) // HEAD_DIM
    return (i[:, None] == i[None, :]).astype(BF16)


def _layer0(h2, bsz, seq, norm_mix, w_in, mu_shift, w0, w_decay_up, a0, w_iclr_up, w_gate_up, k_k, k_a, r_k,
            gn_w, gn_b, conv_w, conv_b, dt_bias, a_log, d_skip, ssd_norm, w_out, norm_ffn, ffn_gate, ffn_up,
            ffn_down):
    row = lambda a: a.reshape(1, -1)
    seg = _segment_matrix(SEG_WIDTH)
    wa = w_in[:, :RWKV_COLS].astype(BF16)
    wb = jnp.pad(w_in[:, RWKV_COLS:], ((0, 0), (0, SSD_COLS_PAD - SSD_COLS))).astype(BF16)
    lora = w_decay_up.shape[0]
    wd = jnp.concatenate([w_decay_up, jnp.zeros((LANES - lora, RWKV_DIM), F32)], axis=0).astype(BF16)
    wi = jnp.concatenate([jnp.zeros((LANES - lora, RWKV_DIM), F32), w_iclr_up], axis=0).astype(BF16)
    pb, r, k, v, lw, kkn, b, g, bonus = _in_proj(
        h2, bsz, seq, row(norm_mix), wa, wb, row(mu_shift), row(w0), wd, row(a0), wi, w_gate_up.astype(BF16),
        row(k_k), row(k_a), row(r_k), seg)
    y = _rwkv_chunk(r, k, v, lw, kkn, b, bsz, seq)

    lane_pad = lambda a: jnp.pad(row(a), ((0, 0), (0, LANES - a.size)))
    ys = _ssd(pb, bsz, seq, conv_w, row(conv_b), lane_pad(dt_bias), lane_pad(a_log), lane_pad(d_skip),
              row(ssd_norm))

    w_out = w_out.astype(BF16)
    return _mix_ffn(h2, y, bonus, g, ys, row(gn_w), row(gn_b), seg, w_out[:RWKV_DIM], w_out[RWKV_DIM:],
                    row(norm_ffn), ffn_gate.astype(BF16), ffn_up.astype(BF16), ffn_down.astype(BF16))


def _layer1(h2, bsz, seq, tables, norm_mix, w_qkv, b_qkv, q_norm, k_norm, sinks, w_o, b_o, norm_ffn, router,
            exp_gate, exp_up, exp_down):
    row = lambda a: a.reshape(1, -1)
    wide = lambda a: jnp.tile(a, SEG_WIDTH // HEAD_DIM).reshape(1, SEG_WIDTH)
    q, k, v = _qkv(h2, row(norm_mix), w_qkv.astype(BF16), row(b_qkv), wide(q_norm), wide(k_norm), tables,
                   _rope_expansion(SEG_WIDTH), _segment_matrix(SEG_WIDTH))
    h2 = _attn(h2, q, k, v, row(sinks), w_o.astype(BF16), row(b_o), bsz, seq)

    return _moe(h2, row(norm_ffn), router, exp_gate.astype(BF16), exp_up.astype(BF16), exp_down.astype(BF16))


def kernel(x, positions, ev_norm_mix, ev_w_in, ev_mu_shift, ev_w0, ev_w_decay_up, ev_a0, ev_w_iclr_up, ev_w_gate_up, ev_k_k, ev_k_a, ev_r_k, ev_gn_w, ev_gn_b, ev_conv_w, ev_conv_b, ev_dt_bias, ev_a_log, ev_d_skip, ev_ssd_norm, ev_w_out, ev_norm_ffn, ev_ffn_gate, ev_ffn_up, ev_ffn_down, od_norm_mix, od_w_qkv, od_b_qkv, od_q_norm, od_k_norm, od_sinks, od_w_o, od_b_o, od_norm_ffn, od_router, od_exp_gate, od_exp_up, od_exp_down):
    bsz, seq, d = x.shape
    depth = ev_norm_mix.shape[0] + od_norm_mix.shape[0]
    tables = _rope_tables(positions)
    h2 = x.reshape(bsz * seq, d)
    for layer in range(depth):
        i = layer // 2
        if layer % 2 == 0:
            h2 = _layer0(h2, bsz, seq, ev_norm_mix[i], ev_w_in[i], ev_mu_shift[i], ev_w0[i], ev_w_decay_up[i],
                         ev_a0[i], ev_w_iclr_up[i], ev_w_gate_up[i], ev_k_k[i], ev_k_a[i], ev_r_k[i],
                         ev_gn_w[i], ev_gn_b[i], ev_conv_w[i], ev_conv_b[i], ev_dt_bias[i], ev_a_log[i],
                         ev_d_skip[i], ev_ssd_norm[i], ev_w_out[i], ev_norm_ffn[i], ev_ffn_gate[i],
                         ev_ffn_up[i], ev_ffn_down[i])
        else:
            h2 = _layer1(h2, bsz, seq, tables, od_norm_mix[i], od_w_qkv[i], od_b_qkv[i], od_q_norm[i],
                         od_k_norm[i], od_sinks[i], od_w_o[i], od_b_o[i], od_norm_ffn[i], od_router[i],
                         od_exp_gate[i], od_exp_up[i], od_exp_down[i])
    return h2.reshape(bsz, seq, d)
```

```python
import functools
import math

import jax
import jax.numpy as jnp
from jax import lax
from jax.experimental import pallas as pl
from jax.experimental.pallas import tpu as pltpu

F32 = jnp.float32
BF16 = jnp.bfloat16

D_MODEL = 1024
HEAD_DIM = 64
NORM_EPS = 1e-6

RWKV_HEADS = 8
RWKV_DIM = 512
RWKV_COLS = 1792
RWKV_GN_EPS = 64e-5
RWKV_CHUNK = 64
RWKV_GROUP = 256

SSD_HEADS = 8
SSD_DIM = 512
SSD_GROUPS = 2
SSD_STATE = 128
SSD_CONV = 4
SSD_CHUNK = 128
SSD_XBC = 1024
SSD_COLS = 1544
SSD_COLS_PAD = 1664

Q_HEADS = 16
KV_HEADS = 4
GQA_GROUP = 4
Q_DIM = 1024
KV_DIM = 256
WINDOW = 128
ROPE_THETA = 500000.0
ROPE_DIM = 16

FFN_DENSE = 2816
N_EXPERTS = 8
FFN_EXPERT = 1408
MOE_TM = 256
MOE_SUB = 8
MOE_FCHUNK = 256

LANES = 128
SEG_WIDTH = 256
VMEM_LIMIT_BYTES = 56 * 1024 * 1024


def _params(*sem):
    return pltpu.CompilerParams(dimension_semantics=sem, vmem_limit_bytes=VMEM_LIMIT_BYTES)


def _bdot(a, b):
    return jnp.dot(a.astype(BF16), b.astype(BF16), preferred_element_type=F32)


def _bdot_nt(a, b):
    return lax.dot_general(a.astype(BF16), b.astype(BF16), (((1,), (1,)), ((), ())),
                           preferred_element_type=F32)


def _bdot_tn(a, b):
    return lax.dot_general(a.astype(BF16), b.astype(BF16), (((0,), (0,)), ((), ())),
                           preferred_element_type=F32)


def _split(x):
    hi = x.astype(BF16)
    lo = (x - hi.astype(F32)).astype(BF16)
    return hi, lo


def _dot_exact_lhs(m, x):
    hi, lo = _split(x)
    return (jnp.dot(m, hi, preferred_element_type=F32) + jnp.dot(m, lo, preferred_element_type=F32))


def _head_sums(x, seg):
    w = seg.shape[0]
    xb = x.astype(BF16)
    return jnp.concatenate(
        [jnp.dot(xb[:, c:c + w], seg, preferred_element_type=F32) for c in range(0, x.shape[1], w)], axis=1)


def _sigmoid(x):
    return 1.0 / (1.0 + jnp.exp(-x))


def _silu(x):
    return x * _sigmoid(x)


def _softplus(x):
    return jnp.maximum(x, 0.0) + jnp.log(1.0 + jnp.exp(-jnp.abs(x)))


def _rms(x, g):
    ms = jnp.mean(x * x, axis=-1, keepdims=True)
    return x * lax.rsqrt(ms + NORM_EPS) * g


def _tile(n, pref):
    t = min(n, pref)
    while n % t:
        t //= 2
    return t


def _full(shape):
    nd = len(shape)
    return pl.BlockSpec(shape, lambda *_: (0,) * nd)


def _resident(shape):
    nd = len(shape)
    return pl.BlockSpec(shape, lambda *_: (0,) * nd, pipeline_mode=pl.Buffered(1))


def _in_proj_kernel(x_ref, g_ref, wa_ref, wb_ref, mu_ref, w0_ref, wd_ref, a0_ref, wi_ref, wg_ref, kk_ref, ka_ref,
                    rk_ref, seg_ref, pb_ref, r_ref, k_ref, v_ref, lw_ref, kkn_ref, b_ref, gate_ref, bonus_ref,
                    carry_ref):
    @pl.when(pl.program_id(1) == 0)
    def _():
        carry_ref[...] = jnp.zeros_like(carry_ref)

    xn = _rms(x_ref[...], g_ref[...]).astype(BF16)
    pb_ref[...] = jnp.dot(xn, wb_ref[...], preferred_element_type=F32)
    pa = jnp.dot(xn, wa_ref[...], preferred_element_type=F32)
    tm = pa.shape[0]
    row = lax.broadcasted_iota(jnp.int32, pa.shape, 0)
    prev = jnp.where(row == 0, carry_ref[...], pltpu.roll(pa, 1, axis=0))
    carry_ref[...] = pa[tm - 1:tm, :]
    x = pa + (prev - pa) * mu_ref[...]

    r = x[:, 0:RWKV_DIM]
    k = x[:, RWKV_DIM:2 * RWKV_DIM]
    v = x[:, 2 * RWKV_DIM:3 * RWKV_DIM]
    lora = x[:, 3 * RWKV_DIM:3 * RWKV_DIM + LANES]
    gl = x[:, 3 * RWKV_DIM + LANES:]
    seg = seg_ref[...]

    w_raw = w0_ref[...] + _bdot(jnp.tanh(lora), wd_ref[...])
    lw_ref[...] = (-math.exp(-0.5)) * _sigmoid(w_raw)
    iclr = _sigmoid(a0_ref[...] + _bdot(lora, wi_ref[...]))
    gate_ref[...] = _bdot(_sigmoid(gl), wg_ref[...])

    kk = k * kk_ref[...]
    kkn = kk * lax.rsqrt(_head_sums(kk * kk, seg) + 1e-12)
    k2 = k * (1.0 + (iclr - 1.0) * ka_ref[...])
    r_ref[...] = r
    k_ref[...] = k2
    v_ref[...] = v
    kkn_ref[...] = kkn
    b_ref[...] = kkn * iclr
    bonus_ref[...] = _head_sums(r * k2 * rk_ref[...], seg) * v


def _in_proj(x2, bsz, seq, g, wa, wb, mu, w0, wd, a0, wi, wg, k_k, k_a, r_k, seg):
    t = x2.shape[0]
    tm = _tile(seq, 512)
    nt = seq // tm
    row = lambda b, i: (b * nt + i, 0)
    out = jax.ShapeDtypeStruct((t, RWKV_DIM), F32)
    small = [mu, w0, wd, a0, wi, wg, k_k, k_a, r_k, seg]
    return pl.pallas_call(
        _in_proj_kernel,
        grid=(bsz, nt),
        in_specs=[pl.BlockSpec((tm, D_MODEL), row), _full((1, D_MODEL)), _resident(wa.shape),
                  _resident(wb.shape)] + [_full(a.shape) for a in small],
        out_specs=[pl.BlockSpec((tm, SSD_COLS_PAD), row)] + [pl.BlockSpec((tm, RWKV_DIM), row)] * 8,
        out_shape=[jax.ShapeDtypeStruct((t, SSD_COLS_PAD), F32)] + [out] * 8,
        scratch_shapes=[pltpu.VMEM((1, RWKV_COLS), F32)],
        compiler_params=_params("arbitrary", "arbitrary"),
        name="in_proj",
    )(x2, g, wa, wb, *small)


def _rwkv_chunk_kernel(r_ref, k_ref, v_ref, lw_ref, kk_ref, b_ref, y_ref, z_ref, *, lt):
    c_len = RWKV_CHUNK
    n = HEAD_DIM
    gl = RWKV_GROUP
    nchunk = gl // c_len
    nheads = LANES // n
    shift = c_len.bit_length() - 1

    @pl.when(pl.program_id(2) == 0)
    def _():
        z_ref[...] = jnp.zeros_like(z_ref)

    ri = lax.broadcasted_iota(jnp.int32, (gl, gl), 0)
    ci = lax.broadcasted_iota(jnp.int32, (gl, gl), 1)
    tri_bd = jnp.where((ci <= ri) & (ci >= ((ri >> shift) << shift)), 1.0, 0.0).astype(BF16)
    re_ = lax.broadcasted_iota(jnp.int32, (c_len, c_len), 0)
    ce_ = lax.broadcasted_iota(jnp.int32, (c_len, c_len), 1)
    eye = re_ == ce_
    ri2 = lax.broadcasted_iota(jnp.int32, (2 * gl, gl), 0)
    ci2 = lax.broadcasted_iota(jnp.int32, (2 * gl, gl), 1)
    t2 = jnp.where(ri2 < gl, ri2, ri2 - gl)
    mask2 = (ci2 <= jnp.where(ri2 < gl, t2 - 1, t2)) & (ci2 >= ((t2 >> shift) << shift))
    zeros = jnp.zeros((c_len, n), F32)

    nsub = lt // gl
    sysid = [(s, h) for s in range(nsub) for h in range(nheads)]
    pre = []
    for s in range(nsub):
        sl = pl.ds(s * gl, gl)
        lw = lw_ref[sl, :]
        g_in = _dot_exact_lhs(tri_bd, lw)
        ends = [g_in[(c + 1) * c_len - 1:(c + 1) * c_len, :] for c in range(nchunk)]
        g_end = jnp.concatenate([jnp.broadcast_to(e, (c_len, LANES)) for e in ends], axis=0)
        e_end = jnp.exp(g_end - g_in)
        en = jnp.exp(-g_in)
        k = k_ref[sl, :]
        b = b_ref[sl, :]
        pre.append(dict(
            ends=ends, v=v_ref[sl, :], rt=r_ref[sl, :] * jnp.exp(g_in),
            at=-kk_ref[sl, :] * jnp.exp(g_in - lw), kt=k * en, bt=b * en, bend=b * e_end, kend=k * e_end))

    def hs(name, s, h):
        return pre[s][name][:, h * n:(h + 1) * n]

    xb, xk, xkv, p, x = {}, {}, {}, {}, {}
    for q in sysid:
        la = jnp.concatenate([hs("at", *q), hs("rt", *q)], axis=0)
        xb[q] = jnp.where(mask2, _bdot_nt(la, hs("bt", *q)), 0.0)
        xk[q] = jnp.where(mask2, _bdot_nt(la, hs("kt", *q)), 0.0)
    for q in sysid:
        xkv[q] = _bdot(xk[q], hs("v", *q))
        p[q] = xb[q][0:gl]
        x[q] = jnp.concatenate([hs("at", *q), xkv[q][0:gl]], axis=1)
    for i in range(6):
        for q in sysid:
            x[q] = x[q] + _bdot(p[q], x[q])
        if i < 5:
            for q in sysid:
                p[q] = _bdot(p[q], p[q])
    rq, y0, mn = {}, {}, {}
    for q in sysid:
        yy = _bdot(xb[q][gl:], x[q])
        rq[q] = yy[:, 0:n] + hs("rt", *q)
        y0[q] = yy[:, n:] + xkv[q][gl:]
        bend_h, kend_h, v_h = hs("bend", *q), hs("kend", *q), hs("v", *q)
        for c in range(nchunk):
            rows = slice(c * c_len, (c + 1) * c_len)
            lhs = jnp.concatenate([bend_h[rows], kend_h[rows]], axis=0)
            rhs = jnp.concatenate([x[q][rows], jnp.concatenate([zeros, v_h[rows]], axis=1)], axis=0)
            mn[q + (c,)] = _bdot_tn(lhs, rhs)
    zero_n = jnp.zeros((n, n), F32)
    comp = {q: [] for q in sysid}
    for c in range(nchunk):
        for q in sysid:
            s, h = q
            pend = jnp.exp(pre[s]["ends"][c][:, h * n:(h + 1) * n])
            m_mat = mn[q + (c,)][:, 0:n] + jnp.where(eye, pend, 0.0)
            n_mat = mn[q + (c,)][:, n:]
            if c == 0:
                comp[q].append(jnp.concatenate([m_mat, n_mat], axis=1))
            else:
                comp[q].append(_bdot(m_mat, comp[q][c - 1]) + jnp.concatenate([zero_n, n_mat], axis=1))
    zs = [z_ref[h] for h in range(nheads)]
    for s in range(nsub):
        yh = []
        for h in range(nheads):
            q = (s, h)
            pm = jnp.concatenate([comp[q][c][:, 0:n] for c in range(nchunk)], axis=0)
            pn = jnp.concatenate([comp[q][c][:, n:] for c in range(nchunk)], axis=0)
            after = _bdot(pm, zs[h]) + pn
            z_in = [zs[h]] + [after[c * c_len:(c + 1) * c_len] for c in range(nchunk - 1)]
            yh.append(jnp.concatenate(
                [_bdot(rq[q][c * c_len:(c + 1) * c_len], z_in[c]) + y0[q][c * c_len:(c + 1) * c_len]
                 for c in range(nchunk)], axis=0))
            zs[h] = after[(nchunk - 1) * c_len:]
        y_ref[pl.ds(s * gl, gl), :] = jnp.concatenate(yh, axis=1)
    for h in range(nheads):
        z_ref[h] = zs[h]


def _rwkv_chunk(r, k, v, lw, kkn, b, bsz, seq):
    t = r.shape[0]
    lt = _tile(seq, 1024)
    nt = seq // lt
    spec = pl.BlockSpec((lt, LANES), lambda bi, hp, i: (bi * nt + i, hp))
    return pl.pallas_call(
        functools.partial(_rwkv_chunk_kernel, lt=lt),
        grid=(bsz, RWKV_DIM // LANES, nt),
        in_specs=[spec] * 6,
        out_specs=spec,
        out_shape=jax.ShapeDtypeStruct((t, RWKV_DIM), F32),
        scratch_shapes=[pltpu.VMEM((2, HEAD_DIM, HEAD_DIM), F32)],
        compiler_params=_params("arbitrary", "arbitrary", "arbitrary"),
        name="rwkv_chunk",
    )(r, k, v, lw, kkn, b)


def _ssd_kernel(pb_ref, cw_ref, cb_ref, dtb_ref, alog_ref, dskip_ref, nrm_ref, ys_ref, ext_ref, st_ref, *, nck):
    q = SSD_CHUNK

    @pl.when(pl.program_id(1) == 0)
    def _():
        ext_ref[0:8, :] = jnp.zeros((8, SSD_XBC), F32)
        st_ref[...] = jnp.zeros_like(st_ref)

    ext_ref[8:8 + nck * q, :] = pb_ref[:, SSD_DIM:SSD_DIM + SSD_XBC]
    for c in range(nck):
        _ssd_chunk(pb_ref, cw_ref, cb_ref, dtb_ref, alog_ref, dskip_ref, nrm_ref, ys_ref, ext_ref, st_ref, c * q)
    ext_ref[0:8, :] = ext_ref[nck * q:nck * q + 8, :]


def _ssd_chunk(pb_ref, cw_ref, cb_ref, dtb_ref, alog_ref, dskip_ref, nrm_ref, ys_ref, ext_ref, st_ref, r0):
    q = SSD_CHUNK
    p = HEAD_DIM
    hpg = SSD_HEADS // SSD_GROUPS
    z = pb_ref[r0:r0 + q, 0:SSD_DIM]
    u = ext_ref[8 + r0:8 + r0 + q, :]
    dt_raw = pb_ref[r0:r0 + q, SSD_DIM + SSD_XBC:]

    conv = cb_ref[...] + cw_ref[SSD_CONV - 1:SSD_CONV, :] * u
    for j in range(SSD_CONV - 1):
        off = r0 + 8 - (SSD_CONV - 1) + j
        conv = conv + cw_ref[j:j + 1, :] * ext_ref[off:off + q, :]
    xbc = _silu(conv)
    xs = xbc[:, 0:SSD_DIM]
    bm = xbc[:, SSD_DIM:SSD_DIM + SSD_GROUPS * SSD_STATE]
    cm = xbc[:, SSD_DIM + SSD_GROUPS * SSD_STATE:]

    dt = _softplus(dt_raw + dtb_ref[...])
    a = -jnp.exp(alog_ref[...])
    ri = lax.broadcasted_iota(jnp.int32, (q, q), 0)
    ci = lax.broadcasted_iota(jnp.int32, (q, q), 1)
    causal = ri >= ci
    cum = _dot_exact_lhs(causal.astype(BF16), dt * a)
    cum_t = cum.T
    dt_t = dt.T
    cum_end = cum[q - 1:q, :]
    to_end = jnp.exp(cum_end - cum) * dt
    ecum = jnp.exp(cum)
    edec = jnp.exp(cum_end)

    ys = []
    for g in range(SSD_GROUPS):
        bm_g = bm[:, g * SSD_STATE:(g + 1) * SSD_STATE]
        cm_g = cm[:, g * SSD_STATE:(g + 1) * SSD_STATE]
        cb = _bdot_nt(cm_g, bm_g)
        bm_t = bm_g.T
        for hh in range(hpg):
            h = g * hpg + hh
            x_h = xs[:, h * p:(h + 1) * p]
            seg = cum[:, h:h + 1] - cum_t[h:h + 1, :]
            ldec = jnp.exp(jnp.where(causal, seg, -jnp.inf))
            wts = cb * ldec * dt_t[h:h + 1, :]
            y = _bdot(wts, x_h)
            h_prev = st_ref[h]
            y = y + _bdot(cm_g, h_prev) * ecum[:, h:h + 1]
            st_ref[h] = h_prev * edec[:, h:h + 1] + _bdot(bm_t, x_h * to_end[:, h:h + 1])
            ys.append(y + dskip_ref[:, h:h + 1] * x_h)
    yall = jnp.concatenate(ys, axis=1) * _silu(z)
    gw = SSD_DIM // SSD_GROUPS
    outs = []
    for g in range(SSD_GROUPS):
        yg = yall[:, g * gw:(g + 1) * gw]
        ms = jnp.mean(yg * yg, axis=-1, keepdims=True)
        outs.append(yg * lax.rsqrt(ms + NORM_EPS) * nrm_ref[:, g * gw:(g + 1) * gw])
    ys_ref[r0:r0 + q, :] = jnp.concatenate(outs, axis=1)


def _ssd(pb, bsz, seq, conv_w, conv_b, dt_bias, a_log, d_skip, ssd_norm):
    t = pb.shape[0]
    nck = 2 if seq % (2 * SSD_CHUNK) == 0 else 1
    rows = nck * SSD_CHUNK
    nc = seq // rows
    row = lambda b, i: (b * nc + i, 0)
    return pl.pallas_call(
        functools.partial(_ssd_kernel, nck=nck),
        grid=(bsz, nc),
        in_specs=[pl.BlockSpec((rows, SSD_COLS_PAD), row), _full(conv_w.shape), _full(conv_b.shape),
                  _full(dt_bias.shape), _full(a_log.shape), _full(d_skip.shape), _full(ssd_norm.shape)],
        out_specs=pl.BlockSpec((rows, SSD_DIM), row),
        out_shape=jax.ShapeDtypeStruct((t, SSD_DIM), F32),
        scratch_shapes=[pltpu.VMEM((8 + rows, SSD_XBC), F32),
                        pltpu.VMEM((SSD_HEADS, SSD_STATE, HEAD_DIM), F32)],
        compiler_params=_params("arbitrary", "arbitrary"),
        name="ssd",
    )(pb, conv_w, conv_b, dt_bias, a_log, d_skip, ssd_norm)


def _mix_ffn_kernel(h_ref, y_ref, bonus_ref, gate_ref, ys_ref, gnw_ref, gnb_ref, seg_ref, wa_ref, wb_ref,
                    g_ref, wg_ref, wu_ref, wd_ref, o_ref):
    y = y_ref[...]
    seg = seg_ref[...]
    mu = _head_sums(y, seg) * (1.0 / HEAD_DIM)
    yc = y - mu
    var = _head_sums(yc * yc, seg) * (1.0 / HEAD_DIM)
    yn = yc * lax.rsqrt(var + RWKV_GN_EPS) * gnw_ref[...] + gnb_ref[...]
    ya = (yn + bonus_ref[...]) * gate_ref[...]
    h = h_ref[...] + _bdot(ya, wa_ref[...]) + _bdot(ys_ref[...], wb_ref[...])

    xn = _rms(h, g_ref[...]).astype(BF16)
    act = _silu(jnp.dot(xn, wg_ref[...], preferred_element_type=F32)) * jnp.dot(
        xn, wu_ref[...], preferred_element_type=F32)
    o_ref[...] = h + jnp.dot(act.astype(BF16), wd_ref[...], preferred_element_type=F32)


def _mix_ffn(h2, y, bonus, gate, ys, gn_w, gn_b, seg, wa, wb, g, wg, wu, wd):
    t = h2.shape[0]
    tm = _tile(t, 512)
    row = lambda i: (i, 0)
    half = pl.BlockSpec((tm, RWKV_DIM), row)
    return pl.pallas_call(
        _mix_ffn_kernel,
        grid=(t // tm,),
        in_specs=[pl.BlockSpec((tm, D_MODEL), row), half, half, half, half, _full(gn_w.shape),
                  _full(gn_b.shape), _resident(seg.shape), _resident(wa.shape), _resident(wb.shape),
                  _full((1, D_MODEL)), _resident(wg.shape), _resident(wu.shape), _resident(wd.shape)],
        out_specs=pl.BlockSpec((tm, D_MODEL), row),
        out_shape=jax.ShapeDtypeStruct((t, D_MODEL), F32),
        compiler_params=_params("parallel"),
        name="mix_ffn",
    )(h2, y, bonus, gate, ys, gn_w, gn_b, seg, wa, wb, g, wg, wu, wd)


def _rope_table_kernel(pos_ref, freq_ref, cos_ref, sin_ref):
    ang = pos_ref[...] * freq_ref[...]
    cos_ref[...] = jnp.cos(ang)
    sin_ref[...] = jnp.sin(ang)


def _rope_tables(positions):
    t = positions.size
    half = ROPE_DIM // 2
    rows = t * half // LANES
    inv_freq = ROPE_THETA ** (-jnp.arange(0, ROPE_DIM, 2, dtype=F32) / ROPE_DIM)
    pos_rep = jnp.repeat(positions.reshape(-1).astype(F32), half).reshape(rows, LANES)
    freq = jnp.tile(inv_freq, LANES // half).reshape(1, LANES)
    tr = _tile(rows, 256)
    cos, sin = pl.pallas_call(
        _rope_table_kernel,
        grid=(rows // tr,),
        in_specs=[pl.BlockSpec((tr, LANES), lambda i: (i, 0)), _full((1, LANES))],
        out_specs=[pl.BlockSpec((tr, LANES), lambda i: (i, 0))] * 2,
        out_shape=[jax.ShapeDtypeStruct((rows, LANES), F32)] * 2,
        compiler_params=_params("parallel"),
        name="rope_tables",
    )(pos_rep, freq)
    return jnp.concatenate([cos.reshape(t, half) - 1.0, sin.reshape(t, half)], axis=1)


def _rope_expansion(width):
    half = ROPE_DIM // 2
    d = jnp.arange(width) % HEAD_DIM
    j = jnp.arange(half)[:, None]
    zero = jnp.zeros((half, width), F32)
    cos_rows = jnp.where((d[None, :] < ROPE_DIM) & (d[None, :] % half == j), 1.0, 0.0)
    sin_a = jnp.where((d[None, :] < half) & (d[None, :] == j), -1.0, 0.0)
    sin_b = jnp.where((d[None, :] >= half) & (d[None, :] < ROPE_DIM) & (d[None, :] - half == j), 1.0, 0.0)
    one = jnp.concatenate([jnp.concatenate([cos_rows, zero, zero], axis=1),
                           jnp.concatenate([zero, sin_a, sin_b], axis=1)], axis=0)
    return jnp.concatenate([one, one], axis=0).astype(BF16)


def _qkv_kernel(h_ref, g_ref, w_ref, b_ref, qn_ref, kn_ref, cs_ref, exp_ref, seg_ref, q_ref, k_ref, v_ref):
    xn = _rms(h_ref[...], g_ref[...]).astype(BF16)
    qkv = jnp.dot(xn, w_ref[...], preferred_element_type=F32) + b_ref[...]
    seg = seg_ref[...]
    wide = seg.shape[0]
    hi, lo = _split(cs_ref[...])
    tab = jnp.dot(jnp.concatenate([hi, lo], axis=1), exp_ref[...], preferred_element_type=F32)
    cos = 1.0 + tab[:, 0:wide]
    sa = tab[:, wide:2 * wide]
    sb = tab[:, 2 * wide:]

    def norm_rope(x, gain):
        ms = _head_sums(x * x, seg) * (1.0 / HEAD_DIM)
        xn_ = x * lax.rsqrt(ms + NORM_EPS) * gain
        return (xn_ * cos + pltpu.roll(xn_, wide - ROPE_DIM // 2, axis=1) * sa
                + pltpu.roll(xn_, ROPE_DIM // 2, axis=1) * sb)

    for c in range(Q_DIM // wide):
        x = qkv[:, c * wide:(c + 1) * wide]
        q_ref[:, c * wide:(c + 1) * wide] = (norm_rope(x, qn_ref[...]) * (HEAD_DIM ** -0.5)).astype(BF16)
    for c in range(KV_DIM // wide):
        x = qkv[:, Q_DIM + c * wide:Q_DIM + (c + 1) * wide]
        k_ref[:, c * wide:(c + 1) * wide] = norm_rope(x, kn_ref[...]).astype(BF16)
    v_ref[...] = qkv[:, Q_DIM + KV_DIM:].astype(BF16)


def _qkv(h2, g, w, b, qn, kn, cs, expand, seg):
    t = h2.shape[0]
    tm = _tile(t, 512)
    row = lambda i: (i, 0)
    return pl.pallas_call(
        _qkv_kernel,
        grid=(t // tm,),
        in_specs=[pl.BlockSpec((tm, D_MODEL), row), _full((1, D_MODEL)), _resident(w.shape), _full(b.shape),
                  _full(qn.shape), _full(kn.shape), pl.BlockSpec((tm, cs.shape[1]), row), _full(expand.shape),
                  _full(seg.shape)],
        out_specs=[pl.BlockSpec((tm, Q_DIM), row), pl.BlockSpec((tm, KV_DIM), row),
                   pl.BlockSpec((tm, KV_DIM), row)],
        out_shape=[jax.ShapeDtypeStruct((t, Q_DIM), BF16), jax.ShapeDtypeStruct((t, KV_DIM), BF16),
                   jax.ShapeDtypeStruct((t, KV_DIM), BF16)],
        compiler_params=_params("parallel"),
        name="qkv",
    )(h2, g, w, b, qn, kn, cs, expand, seg)


def _attn_kernel(h_ref, q_ref, kc_ref, kp_ref, vc_ref, vp_ref, sink_ref, wo_ref, bo_ref, o_ref, *, nqb):
    w = WINDOW
    n = HEAD_DIM
    rows = GQA_GROUP * w
    qi = lax.broadcasted_iota(jnp.int32, (rows, 2 * w), 0) % w
    kj = lax.broadcasted_iota(jnp.int32, (rows, 2 * w), 1)
    diff = w + qi - kj
    band = (diff >= 0) & (diff < w)
    band0 = band & (kj >= jnp.where(pl.program_id(1) == 0, w, 0))
    kall = jnp.concatenate([kp_ref[...], kc_ref[...]], axis=0)
    vall = jnp.concatenate([vp_ref[...], vc_ref[...]], axis=0)
    units = [(qb, g) for qb in range(nqb) for g in range(KV_HEADS)]
    s, e, den = {}, {}, {}
    for u in units:
        qb, g = u
        qs = jnp.concatenate(
            [q_ref[qb * w:(qb + 1) * w, (g * GQA_GROUP + i) * n:(g * GQA_GROUP + i + 1) * n]
             for i in range(GQA_GROUP)], axis=0)
        kk = kall[qb * w:(qb + 2) * w, g * n:(g + 1) * n]
        sc = lax.dot_general(qs, kk, (((1,), (1,)), ((), ())), preferred_element_type=F32)
        s[u] = jnp.where(band0 if qb == 0 else band, sc, -jnp.inf)
    sinks = [jnp.concatenate(
        [jnp.broadcast_to(sink_ref[:, g * GQA_GROUP + i:g * GQA_GROUP + i + 1], (w, LANES))
         for i in range(GQA_GROUP)], axis=0) for g in range(KV_HEADS)]
    ones = jnp.ones((2 * w, LANES), BF16)
    for u in units:
        sink = sinks[u[1]]
        m = jnp.maximum(jnp.broadcast_to(jnp.max(s[u], axis=-1, keepdims=True), (rows, LANES)), sink)
        e[u] = jnp.exp(s[u] - jnp.concatenate([m, m], axis=1)).astype(BF16)
        den[u] = jnp.exp(sink - m)
    outs = [[None] * Q_HEADS for _ in range(nqb)]
    for u in units:
        qb, g = u
        vv = vall[qb * w:(qb + 2) * w, g * n:(g + 1) * n]
        total = jnp.dot(e[u], ones, preferred_element_type=F32) + den[u]
        o = jnp.dot(e[u], vv, preferred_element_type=F32) / total[:, 0:n]
        for i in range(GQA_GROUP):
            outs[qb][g * GQA_GROUP + i] = o[i * w:(i + 1) * w]
    att = jnp.concatenate([jnp.concatenate(outs[qb], axis=1) for qb in range(nqb)], axis=0)
    o_ref[...] = h_ref[...] + _bdot(att, wo_ref[...]) + bo_ref[...]


def _attn(h2, q, k, v, sinks, wo, bo, bsz, seq):
    t = h2.shape[0]
    nqb = 4 if seq % (4 * WINDOW) == 0 else 1
    tq = nqb * WINDOW
    nb = seq // tq
    cur = lambda b, i: (b * nb + i, 0)
    prv = lambda b, i: (b * nb * nqb + jnp.maximum(i * nqb - 1, 0), 0)
    return pl.pallas_call(
        functools.partial(_attn_kernel, nqb=nqb),
        grid=(bsz, nb),
        in_specs=[pl.BlockSpec((tq, D_MODEL), cur), pl.BlockSpec((tq, Q_DIM), cur),
                  pl.BlockSpec((tq, KV_DIM), cur), pl.BlockSpec((WINDOW, KV_DIM), prv),
                  pl.BlockSpec((tq, KV_DIM), cur), pl.BlockSpec((WINDOW, KV_DIM), prv),
                  _full(sinks.shape), _full(wo.shape), _full(bo.shape)],
        out_specs=pl.BlockSpec((tq, D_MODEL), cur),
        out_shape=jax.ShapeDtypeStruct((t, D_MODEL), F32),
        compiler_params=_params("parallel", "parallel"),
        name="attn",
    )(h2, q, k, k, v, v, sinks, wo, bo)


def _router_kernel(h_ref, g_ref, wr_ref, xn_ref, meta_ref, cnt_ref, base_ref):
    @pl.when(pl.program_id(0) == 0)
    def _():
        base_ref[...] = jnp.zeros_like(base_ref)

    xn = _rms(h_ref[...], g_ref[...])
    xn_ref[...] = xn
    hi, lo = _split(xn)
    whi = wr_ref[0]
    wlo = wr_ref[1]
    logits = (jnp.dot(hi, whi, preferred_element_type=F32) + jnp.dot(hi, wlo, preferred_element_type=F32)
              + jnp.dot(lo, whi, preferred_element_type=F32))
    tm = logits.shape[0]
    lane = lax.broadcasted_iota(jnp.int32, logits.shape, 1)
    logits = jnp.where(lane < N_EXPERTS, logits, -jnp.inf)
    m1 = jnp.max(logits, axis=-1, keepdims=True)
    i1 = jnp.min(jnp.where(logits == m1, lane, LANES), axis=-1, keepdims=True)
    rest = jnp.where(lane == i1, -jnp.inf, logits)
    m2 = jnp.max(rest, axis=-1, keepdims=True)
    i2 = jnp.min(jnp.where(rest == m2, lane, LANES), axis=-1, keepdims=True)
    e2 = jnp.exp(m2 - m1)
    w1 = 1.0 / (1.0 + e2)
    w2 = e2 / (1.0 + e2)

    sel1 = lane == i1
    sel2 = lane == i2
    onehot = jnp.where(sel1, 1.0, 0.0) + jnp.where(sel2, 1.0, 0.0)
    ri = lax.broadcasted_iota(jnp.int32, (tm, tm), 0)
    ci = lax.broadcasted_iota(jnp.int32, (tm, tm), 1)
    before = jnp.dot(jnp.where(ci < ri, 1.0, 0.0).astype(BF16), onehot.astype(BF16),
                     preferred_element_type=F32)
    rank = base_ref[...] + before
    r1 = jnp.sum(jnp.where(sel1, rank, 0.0), axis=-1, keepdims=True)
    r2 = jnp.sum(jnp.where(sel2, rank, 0.0), axis=-1, keepdims=True)
    total = base_ref[...] + jnp.sum(onehot, axis=0, keepdims=True)
    base_ref[...] = total
    cnt_ref[...] = total
    cols = (i1.astype(F32), i2.astype(F32), w1, w2, r1, r2)
    meta = jnp.zeros(logits.shape, F32)
    for c, val in enumerate(cols):
        meta = jnp.where(lane == c, val, meta)
    meta_ref[...] = meta


def _router(h2, g, wr):
    t = h2.shape[0]
    tm = _tile(t, 512)
    row = lambda i: (i, 0)
    return pl.pallas_call(
        _router_kernel,
        grid=(t // tm,),
        in_specs=[pl.BlockSpec((tm, D_MODEL), row), _full((1, D_MODEL)), _full(wr.shape)],
        out_specs=[pl.BlockSpec((tm, D_MODEL), row), pl.BlockSpec((tm, LANES), row), _full((1, LANES))],
        out_shape=[jax.ShapeDtypeStruct((t, D_MODEL), F32), jax.ShapeDtypeStruct((t, LANES), F32),
                   jax.ShapeDtypeStruct((1, LANES), F32)],
        scratch_shapes=[pltpu.VMEM((1, LANES), F32)],
        compiler_params=_params("arbitrary"),
        name="router",
    )(h2, g, wr)


def _moe_index_kernel(p1_ref, p2_ref, pad_ref, src_ref, dst_ref, *, ntok):
    tm = MOE_TM
    nrow = src_ref.shape[0]

    def virtual(j, c):
        dst_ref[j] = nrow + j
        return c

    lax.fori_loop(0, tm, virtual, 0)

    for e in range(N_EXPERTS + 1):
        def fill(p, c, e=e):
            src_ref[p] = 0
            dst_ref[tm + p] = 2 * ntok + (e % N_EXPERTS) * tm + ((p - pad_ref[2 * e]) & (tm - 1))
            return c

        lax.fori_loop(pad_ref[2 * e], pad_ref[2 * e + 1], fill, 0)

    def body(t, c):
        p1 = p1_ref[t]
        p2 = p2_ref[t]
        src_ref[p1] = t
        dst_ref[tm + p1] = t
        src_ref[p2] = t
        dst_ref[tm + p2] = ntok + t
        return c

    lax.fori_loop(0, ntok, body, 0, unroll=4)


def _moe_index(pos1, pos2, pad, rows):
    smem = pl.BlockSpec(memory_space=pltpu.SMEM)
    return pl.pallas_call(
        functools.partial(_moe_index_kernel, ntok=pos1.shape[0]),
        grid_spec=pltpu.PrefetchScalarGridSpec(
            num_scalar_prefetch=3, grid=(1,), in_specs=[], out_specs=[smem, smem]),
        out_shape=[jax.ShapeDtypeStruct((rows,), jnp.int32), jax.ShapeDtypeStruct((rows + MOE_TM,), jnp.int32)],
        compiler_params=_params("arbitrary"),
        name="moe_index",
    )(pos1, pos2, pad)


def _row_copy(src_ref, src_row, dst_ref, dst_row, sem):
    return pltpu.make_async_copy(src_ref.at[pl.ds(src_row, 1)], dst_ref.at[pl.ds(dst_row, 1)], sem)


def _experts_kernel(te_ref, nused_ref, src_ref, dst_ref, x_ref, wg_ref, wu_ref, wd_ref, y_ref,
                    xbuf, ybuf, gsem, ssem):
    del te_ref
    i = pl.program_id(0)
    nused = nused_ref[0]
    tm = MOE_TM

    sub = MOE_SUB
    ngroup = tm // sub

    def gather_group(tile, slot, k):
        for u in range(sub):
            _row_copy(x_ref, src_ref[tile * tm + k * sub + u], xbuf.at[slot, k], u, gsem.at[slot]).start()

    def scatter_group(tile, slot, k):
        for u in range(sub):
            _row_copy(ybuf.at[slot, k], u, y_ref, dst_ref[(tile + 1) * tm + k * sub + u], ssem.at[slot]).start()

    def wait_gather(slot):
        def body(k, c):
            pltpu.make_async_copy(x_ref.at[pl.ds(0, sub)], xbuf.at[slot, k], gsem.at[slot]).wait()
            return c
        lax.fori_loop(0, ngroup, body, 0)

    def wait_scatter(slot):
        def body(k, c):
            pltpu.make_async_copy(ybuf.at[slot, k], y_ref.at[pl.ds(0, sub)], ssem.at[slot]).wait()
            return c
        lax.fori_loop(0, ngroup, body, 0)

    @pl.when(i == 0)
    def _():
        ybuf[1] = jnp.zeros(ybuf.shape[1:], F32)

        def body(k, c):
            gather_group(0, 0, k)
            return c
        lax.fori_loop(0, ngroup, body, 0)

    @pl.when(i < nused)
    def _():
        slot = i % 2
        other = 1 - slot
        nxt = jnp.minimum(i + 1, nused - 1)
        wait_gather(slot)

        @pl.when(i >= 1)
        def _():
            wait_scatter(slot)

        xb = xbuf[slot].reshape(tm, D_MODEL).astype(BF16)
        bounds = list(range(0, FFN_EXPERT, MOE_FCHUNK)) + [FFN_EXPERT]
        nchunk = len(bounds) - 1
        acts = []
        for c in range(nchunk):
            c0, c1 = bounds[c], bounds[c + 1]
            gate = jnp.dot(xb, wg_ref[0, :, c0:c1], preferred_element_type=F32)
            up = jnp.dot(xb, wu_ref[0, :, c0:c1], preferred_element_type=F32)
            acts.append((_silu(gate) * up).astype(BF16))
            for k in range(c * ngroup // nchunk, (c + 1) * ngroup // nchunk):
                gather_group(nxt, other, k)
                scatter_group(i - 1, other, k)
        y = jnp.dot(jnp.concatenate(acts, axis=1), wd_ref[0], preferred_element_type=F32)
        ybuf[slot] = y.reshape(ngroup, sub, D_MODEL)

        @pl.when(i == nused - 1)
        def _():
            def body(k, c):
                scatter_group(i, slot, k)
                return c
            lax.fori_loop(0, ngroup, body, 0)
            wait_scatter(other)
            wait_scatter(slot)
            wait_gather(other)


def _experts(tile_expert, nused, src, dst, xn, wg, wu, wd, out_rows):
    ntile = tile_expert.shape[0]
    tm = MOE_TM
    wsel = lambda i, te, nu, s, d: (te[i], 0, 0)
    any_spec = pl.BlockSpec(memory_space=pl.ANY)
    return pl.pallas_call(
        _experts_kernel,
        grid_spec=pltpu.PrefetchScalarGridSpec(
            num_scalar_prefetch=4, grid=(ntile,),
            in_specs=[any_spec,
                      pl.BlockSpec((1, D_MODEL, FFN_EXPERT), wsel),
                      pl.BlockSpec((1, D_MODEL, FFN_EXPERT), wsel),
                      pl.BlockSpec((1, FFN_EXPERT, D_MODEL), wsel)],
            out_specs=any_spec,
            scratch_shapes=[pltpu.VMEM((2, tm // MOE_SUB, MOE_SUB, D_MODEL), F32),
                            pltpu.VMEM((2, tm // MOE_SUB, MOE_SUB, D_MODEL), F32),
                            pltpu.SemaphoreType.DMA((2,)), pltpu.SemaphoreType.DMA((2,))]),
        out_shape=jax.ShapeDtypeStruct((out_rows, D_MODEL), F32),
        compiler_params=_params("arbitrary"),
        name="moe_experts",
    )(tile_expert, nused, src, dst, xn, wg, wu, wd)


def _combine_kernel(h_ref, meta_ref, y1_ref, y2_ref, o_ref):
    meta = meta_ref[...]
    o_ref[...] = h_ref[...] + meta[:, 2:3] * y1_ref[...] + meta[:, 3:4] * y2_ref[...]


def _combine(h2, meta, ys):
    t = h2.shape[0]
    tm = _tile(t, 512)
    nt = t // tm
    row = lambda i: (i, 0)
    return pl.pallas_call(
        _combine_kernel,
        grid=(nt,),
        in_specs=[pl.BlockSpec((tm, D_MODEL), row), pl.BlockSpec((tm, LANES), row),
                  pl.BlockSpec((tm, D_MODEL), row), pl.BlockSpec((tm, D_MODEL), lambda i: (nt + i, 0))],
        out_specs=pl.BlockSpec((tm, D_MODEL), row),
        out_shape=jax.ShapeDtypeStruct((t, D_MODEL), F32),
        compiler_params=_params("parallel"),
        name="moe_combine",
    )(h2, meta, ys, ys)


def _moe(h2, g, router, wg, wu, wd):
    t = h2.shape[0]
    tm = MOE_TM
    wr = jnp.pad(router, ((0, 0), (0, LANES - N_EXPERTS)))
    wr_hi = wr.astype(BF16)
    wr_lo = (wr - wr_hi.astype(F32)).astype(BF16)
    xn, meta, cnt = _router(h2, g, jnp.stack([wr_hi, wr_lo]))

    cols = meta[:, :8].T.astype(jnp.int32)
    e1, e2, r1, r2 = cols[0], cols[1], cols[4], cols[5]
    count = cnt[0, :N_EXPERTS].astype(jnp.int32)
    ntile = (count + tm - 1) // tm
    tile_end = jnp.cumsum(ntile)
    off = (tile_end - ntile) * tm
    rows = 2 * t + N_EXPERTS * tm
    tile_id = jnp.arange(rows // tm)
    tile_expert = jnp.minimum(
        jnp.sum(tile_id[:, None] >= tile_end[None, :], axis=1), N_EXPERTS - 1).astype(jnp.int32)
    nused = tile_end[-1:].astype(jnp.int32)
    expert = jnp.arange(N_EXPERTS, dtype=jnp.int32)[:, None]
    pos1 = r1 + jnp.sum(jnp.where(e1[None, :] == expert, off[:, None], 0), axis=0)
    pos2 = r2 + jnp.sum(jnp.where(e2[None, :] == expert, off[:, None], 0), axis=0)
    pad = jnp.stack([jnp.append(off + count, tile_end[-1] * tm),
                     jnp.append(off + ntile * tm, rows)], axis=1).reshape(-1).astype(jnp.int32)
    src, dst = _moe_index(pos1, pos2, pad, rows)

    ys = _experts(tile_expert, nused, src, dst, xn, wg, wu, wd, rows + tm)
    return _combine(h2, meta, ys)


def _segment_matrix(width):
    i = jnp.arange(width) // HEAD_DIM
    return (i[:, None] == i[None, :]).astype(BF16)


def _layer0(h2, bsz, seq, norm_mix, w_in, mu_shift, w0, w_decay_up, a0, w_iclr_up, w_gate_up, k_k, k_a, r_k,
            gn_w, gn_b, conv_w, conv_b, dt_bias, a_log, d_skip, ssd_norm, w_out, norm_ffn, ffn_gate, ffn_up,
            ffn_down):
    row = lambda a: a.reshape(1, -1)
    seg = _segment_matrix(SEG_WIDTH)
    wa = w_in[:, :RWKV_COLS].astype(BF16)
    wb = jnp.pad(w_in[:, RWKV_COLS:], ((0, 0), (0, SSD_COLS_PAD - SSD_COLS))).astype(BF16)
    lora = w_decay_up.shape[0]
    wd = jnp.concatenate([w_decay_up, jnp.zeros((LANES - lora, RWKV_DIM), F32)], axis=0).astype(BF16)
    wi = jnp.concatenate([jnp.zeros((LANES - lora, RWKV_DIM), F32), w_iclr_up], axis=0).astype(BF16)
    pb, r, k, v, lw, kkn, b, g, bonus = _in_proj(
        h2, bsz, seq, row(norm_mix), wa, wb, row(mu_shift), row(w0), wd, row(a0), wi, w_gate_up.astype(BF16),
        row(k_k), row(k_a), row(r_k), seg)
    y = _rwkv_chunk(r, k, v, lw, kkn, b, bsz, seq)

    lane_pad = lambda a: jnp.pad(row(a), ((0, 0), (0, LANES - a.size)))
    ys = _ssd(pb, bsz, seq, conv_w, row(conv_b), lane_pad(dt_bias), lane_pad(a_log), lane_pad(d_skip),
              row(ssd_norm))

    w_out = w_out.astype(BF16)
    return _mix_ffn(h2, y, bonus, g, ys, row(gn_w), row(gn_b), seg, w_out[:RWKV_DIM], w_out[RWKV_DIM:],
                    row(norm_ffn), ffn_gate.astype(BF16), ffn_up.astype(BF16), ffn_down.astype(BF16))


def _layer1(h2, bsz, seq, tables, norm_mix, w_qkv, b_qkv, q_norm, k_norm, sinks, w_o, b_o, norm_ffn, router,
            exp_gate, exp_up, exp_down):
    row = lambda a: a.reshape(1, -1)
    wide = lambda a: jnp.tile(a, SEG_WIDTH // HEAD_DIM).reshape(1, SEG_WIDTH)
    q, k, v = _qkv(h2, row(norm_mix), w_qkv.astype(BF16), row(b_qkv), wide(q_norm), wide(k_norm), tables,
                   _rope_expansion(SEG_WIDTH), _segment_matrix(SEG_WIDTH))
    h2 = _attn(h2, q, k, v, row(sinks), w_o.astype(BF16), row(b_o), bsz, seq)

    return _moe(h2, row(norm_ffn), router, exp_gate.astype(BF16), exp_up.astype(BF16), exp_down.astype(BF16))


def kernel(x, positions, ev_norm_mix, ev_w_in, ev_mu_shift, ev_w0, ev_w_decay_up, ev_a0, ev_w_iclr_up, ev_w_gate_up, ev_k_k, ev_k_a, ev_r_k, ev_gn_w, ev_gn_b, ev_conv_w, ev_conv_b, ev_dt_bias, ev_a_log, ev_d_skip, ev_ssd_norm, ev_w_out, ev_norm_ffn, ev_ffn_gate, ev_ffn_up, ev_ffn_down, od_norm_mix, od_w_qkv, od_b_qkv, od_q_norm, od_k_norm, od_sinks, od_w_o, od_b_o, od_norm_ffn, od_router, od_exp_gate, od_exp_up, od_exp_down):
    bsz, seq, d = x.shape
    depth = ev_norm_mix.shape[0] + od_norm_mix.shape[0]
    tables = _rope_tables(positions)
    h2 = x.reshape(bsz * seq, d)
    for layer in range(depth):
        i = layer // 2
        if layer % 2 == 0:
            h2 = _layer0(h2, bsz, seq, ev_norm_mix[i], ev_w_in[i], ev_mu_shift[i], ev_w0[i], ev_w_decay_up[i],
                         ev_a0[i], ev_w_iclr_up[i], ev_w_gate_up[i], ev_k_k[i], ev_k_a[i], ev_r_k[i],
                         ev_gn_w[i], ev_gn_b[i], ev_conv_w[i], ev_conv_b[i], ev_dt_bias[i], ev_a_log[i],
                         ev_d_skip[i], ev_ssd_norm[i], ev_w_out[i], ev_norm_ffn[i], ev_ffn_gate[i],
                         ev_ffn_up[i], ev_ffn_down[i])
        else:
            h2 = _layer1(h2, bsz, seq, tables, od_norm_mix[i], od_w_qkv[i], od_b_qkv[i], od_q_norm[i],
                         od_k_norm[i], od_sinks[i], od_w_o[i], od_b_o[i], od_norm_ffn[i], od_router[i],
                         od_exp_gate[i], od_exp_up[i], od_exp_down[i])
    return h2.reshape(bsz, seq, d)
```

```python
import functools
import math

import jax
import jax.numpy as jnp
from jax import lax
from jax.experimental import pallas as pl
from jax.experimental.pallas import tpu as pltpu

F32 = jnp.float32
BF16 = jnp.bfloat16

D_MODEL = 1024
HEAD_DIM = 64
NORM_EPS = 1e-6

RWKV_HEADS = 8
RWKV_DIM = 512
RWKV_COLS = 1792
RWKV_GN_EPS = 64e-5
RWKV_CHUNK = 64
RWKV_GROUP = 256

SSD_HEADS = 8
SSD_DIM = 512
SSD_GROUPS = 2
SSD_STATE = 128
SSD_CONV = 4
SSD_CHUNK = 128
SSD_XBC = 1024
SSD_COLS = 1544
SSD_COLS_PAD = 1664

Q_HEADS = 16
KV_HEADS = 4
GQA_GROUP = 4
Q_DIM = 1024
KV_DIM = 256
WINDOW = 128
ROPE_THETA = 500000.0
ROPE_DIM = 16

FFN_DENSE = 2816
N_EXPERTS = 8
FFN_EXPERT = 1408
MOE_TM = 256
MOE_SUB = 8
MOE_FCHUNK = 256

LANES = 128
MOE_ROW = D_MODEL + LANES
SEG_WIDTH = 256
VMEM_LIMIT_BYTES = 56 * 1024 * 1024


def _params(*sem):
    return pltpu.CompilerParams(dimension_semantics=sem, vmem_limit_bytes=VMEM_LIMIT_BYTES)


def _bdot(a, b):
    return jnp.dot(a.astype(BF16), b.astype(BF16), preferred_element_type=F32)


def _bdot_nt(a, b):
    return lax.dot_general(a.astype(BF16), b.astype(BF16), (((1,), (1,)), ((), ())),
                           preferred_element_type=F32)


def _bdot_tn(a, b):
    return lax.dot_general(a.astype(BF16), b.astype(BF16), (((0,), (0,)), ((), ())),
                           preferred_element_type=F32)


def _split(x):
    hi = x.astype(BF16)
    lo = (x - hi.astype(F32)).astype(BF16)
    return hi, lo


def _dot_exact_lhs(m, x):
    hi, lo = _split(x)
    return (jnp.dot(m, hi, preferred_element_type=F32) + jnp.dot(m, lo, preferred_element_type=F32))


def _head_sums(x, seg):
    w = seg.shape[0]
    xb = x.astype(BF16)
    return jnp.concatenate(
        [jnp.dot(xb[:, c:c + w], seg, preferred_element_type=F32) for c in range(0, x.shape[1], w)], axis=1)


def _sigmoid(x):
    return 1.0 / (1.0 + jnp.exp(-x))


def _silu(x):
    return x * _sigmoid(x)


def _softplus(x):
    return jnp.maximum(x, 0.0) + jnp.log(1.0 + jnp.exp(-jnp.abs(x)))


def _rms(x, g):
    ms = jnp.mean(x * x, axis=-1, keepdims=True)
    return x * lax.rsqrt(ms + NORM_EPS) * g


def _tile(n, pref):
    t = min(n, pref)
    while n % t:
        t //= 2
    return t


def _full(shape):
    nd = len(shape)
    return pl.BlockSpec(shape, lambda *_: (0,) * nd)


def _resident(shape):
    nd = len(shape)
    return pl.BlockSpec(shape, lambda *_: (0,) * nd, pipeline_mode=pl.Buffered(1))


def _in_proj_kernel(x_ref, g_ref, wa_ref, wb_ref, mu_ref, w0_ref, wd_ref, a0_ref, wi_ref, wg_ref, kk_ref, ka_ref,
                    rk_ref, seg_ref, pb_ref, r_ref, k_ref, v_ref, lw_ref, kkn_ref, b_ref, gate_ref, bonus_ref,
                    carry_ref):
    @pl.when(pl.program_id(1) == 0)
    def _():
        carry_ref[...] = jnp.zeros_like(carry_ref)

    xn = _rms(x_ref[...], g_ref[...]).astype(BF16)
    pb_ref[...] = jnp.dot(xn, wb_ref[...], preferred_element_type=F32)
    pa = jnp.dot(xn, wa_ref[...], preferred_element_type=F32)
    tm = pa.shape[0]
    row = lax.broadcasted_iota(jnp.int32, pa.shape, 0)
    prev = jnp.where(row == 0, carry_ref[...], pltpu.roll(pa, 1, axis=0))
    carry_ref[...] = pa[tm - 1:tm, :]
    x = pa + (prev - pa) * mu_ref[...]

    r = x[:, 0:RWKV_DIM]
    k = x[:, RWKV_DIM:2 * RWKV_DIM]
    v = x[:, 2 * RWKV_DIM:3 * RWKV_DIM]
    lora = x[:, 3 * RWKV_DIM:3 * RWKV_DIM + LANES]
    gl = x[:, 3 * RWKV_DIM + LANES:]
    seg = seg_ref[...]

    w_raw = w0_ref[...] + _bdot(jnp.tanh(lora), wd_ref[...])
    lw_ref[...] = (-math.exp(-0.5)) * _sigmoid(w_raw)
    iclr = _sigmoid(a0_ref[...] + _bdot(lora, wi_ref[...]))
    gate_ref[...] = _bdot(_sigmoid(gl), wg_ref[...])

    kk = k * kk_ref[...]
    kkn = kk * lax.rsqrt(_head_sums(kk * kk, seg) + 1e-12)
    k2 = k * (1.0 + (iclr - 1.0) * ka_ref[...])
    r_ref[...] = r
    k_ref[...] = k2
    v_ref[...] = v
    kkn_ref[...] = kkn
    b_ref[...] = kkn * iclr
    bonus_ref[...] = _head_sums(r * k2 * rk_ref[...], seg) * v


def _in_proj(x2, bsz, seq, g, wa, wb, mu, w0, wd, a0, wi, wg, k_k, k_a, r_k, seg):
    t = x2.shape[0]
    tm = _tile(seq, 512)
    nt = seq // tm
    row = lambda b, i: (b * nt + i, 0)
    out = jax.ShapeDtypeStruct((t, RWKV_DIM), F32)
    small = [mu, w0, wd, a0, wi, wg, k_k, k_a, r_k, seg]
    return pl.pallas_call(
        _in_proj_kernel,
        grid=(bsz, nt),
        in_specs=[pl.BlockSpec((tm, D_MODEL), row), _full((1, D_MODEL)), _resident(wa.shape),
                  _resident(wb.shape)] + [_full(a.shape) for a in small],
        out_specs=[pl.BlockSpec((tm, SSD_COLS_PAD), row)] + [pl.BlockSpec((tm, RWKV_DIM), row)] * 8,
        out_shape=[jax.ShapeDtypeStruct((t, SSD_COLS_PAD), F32)] + [out] * 8,
        scratch_shapes=[pltpu.VMEM((1, RWKV_COLS), F32)],
        compiler_params=_params("arbitrary", "arbitrary"),
        name="in_proj",
    )(x2, g, wa, wb, *small)


def _rwkv_chunk_kernel(r_ref, k_ref, v_ref, lw_ref, kk_ref, b_ref, y_ref, z_ref, *, lt):
    c_len = RWKV_CHUNK
    n = HEAD_DIM
    gl = RWKV_GROUP
    nchunk = gl // c_len
    nheads = LANES // n
    shift = c_len.bit_length() - 1

    @pl.when(pl.program_id(2) == 0)
    def _():
        z_ref[...] = jnp.zeros_like(z_ref)

    ri = lax.broadcasted_iota(jnp.int32, (gl, gl), 0)
    ci = lax.broadcasted_iota(jnp.int32, (gl, gl), 1)
    tri_bd = jnp.where((ci <= ri) & (ci >= ((ri >> shift) << shift)), 1.0, 0.0).astype(BF16)
    re_ = lax.broadcasted_iota(jnp.int32, (c_len, c_len), 0)
    ce_ = lax.broadcasted_iota(jnp.int32, (c_len, c_len), 1)
    eye = re_ == ce_
    ri2 = lax.broadcasted_iota(jnp.int32, (2 * gl, gl), 0)
    ci2 = lax.broadcasted_iota(jnp.int32, (2 * gl, gl), 1)
    t2 = jnp.where(ri2 < gl, ri2, ri2 - gl)
    mask2 = (ci2 <= jnp.where(ri2 < gl, t2 - 1, t2)) & (ci2 >= ((t2 >> shift) << shift))
    zeros = jnp.zeros((c_len, n), F32)

    nsub = lt // gl
    sysid = [(s, h) for s in range(nsub) for h in range(nheads)]
    pre = []
    for s in range(nsub):
        sl = pl.ds(s * gl, gl)
        lw = lw_ref[sl, :]
        g_in = _dot_exact_lhs(tri_bd, lw)
        ends = [g_in[(c + 1) * c_len - 1:(c + 1) * c_len, :] for c in range(nchunk)]
        g_end = jnp.concatenate([jnp.broadcast_to(e, (c_len, LANES)) for e in ends], axis=0)
        e_end = jnp.exp(g_end - g_in)
        en = jnp.exp(-g_in)
        k = k_ref[sl, :]
        b = b_ref[sl, :]
        pre.append(dict(
            ends=ends, v=v_ref[sl, :], rt=r_ref[sl, :] * jnp.exp(g_in),
            at=-kk_ref[sl, :] * jnp.exp(g_in - lw), kt=k * en, bt=b * en, bend=b * e_end, kend=k * e_end))

    def hs(name, s, h):
        return pre[s][name][:, h * n:(h + 1) * n]

    xb, xk, xkv, p, x = {}, {}, {}, {}, {}
    for q in sysid:
        la = jnp.concatenate([hs("at", *q), hs("rt", *q)], axis=0)
        xb[q] = jnp.where(mask2, _bdot_nt(la, hs("bt", *q)), 0.0)
        xk[q] = jnp.where(mask2, _bdot_nt(la, hs("kt", *q)), 0.0)
    for q in sysid:
        xkv[q] = _bdot(xk[q], hs("v", *q))
        p[q] = xb[q][0:gl]
        x[q] = jnp.concatenate([hs("at", *q), xkv[q][0:gl]], axis=1)
    for i in range(6):
        for q in sysid:
            x[q] = x[q] + _bdot(p[q], x[q])
        if i < 5:
            for q in sysid:
                p[q] = _bdot(p[q], p[q])
    rq, y0, mn = {}, {}, {}
    for q in sysid:
        yy = _bdot(xb[q][gl:], x[q])
        rq[q] = yy[:, 0:n] + hs("rt", *q)
        y0[q] = yy[:, n:] + xkv[q][gl:]
        bend_h, kend_h, v_h = hs("bend", *q), hs("kend", *q), hs("v", *q)
        for c in range(nchunk):
            rows = slice(c * c_len, (c + 1) * c_len)
            lhs = jnp.concatenate([bend_h[rows], kend_h[rows]], axis=0)
            rhs = jnp.concatenate([x[q][rows], jnp.concatenate([zeros, v_h[rows]], axis=1)], axis=0)
            mn[q + (c,)] = _bdot_tn(lhs, rhs)
    zero_n = jnp.zeros((n, n), F32)
    comp = {q: [] for q in sysid}
    for c in range(nchunk):
        for q in sysid:
            s, h = q
            pend = jnp.exp(pre[s]["ends"][c][:, h * n:(h + 1) * n])
            m_mat = mn[q + (c,)][:, 0:n] + jnp.where(eye, pend, 0.0)
            n_mat = mn[q + (c,)][:, n:]
            if c == 0:
                comp[q].append(jnp.concatenate([m_mat, n_mat], axis=1))
            else:
                comp[q].append(_bdot(m_mat, comp[q][c - 1]) + jnp.concatenate([zero_n, n_mat], axis=1))
    zs = [z_ref[h] for h in range(nheads)]
    for s in range(nsub):
        yh = []
        for h in range(nheads):
            q = (s, h)
            pm = jnp.concatenate([comp[q][c][:, 0:n] for c in range(nchunk)], axis=0)
            pn = jnp.concatenate([comp[q][c][:, n:] for c in range(nchunk)], axis=0)
            after = _bdot(pm, zs[h]) + pn
            z_in = [zs[h]] + [after[c * c_len:(c + 1) * c_len] for c in range(nchunk - 1)]
            yh.append(jnp.concatenate(
                [_bdot(rq[q][c * c_len:(c + 1) * c_len], z_in[c]) + y0[q][c * c_len:(c + 1) * c_len]
                 for c in range(nchunk)], axis=0))
            zs[h] = after[(nchunk - 1) * c_len:]
        y_ref[pl.ds(s * gl, gl), :] = jnp.concatenate(yh, axis=1)
    for h in range(nheads):
        z_ref[h] = zs[h]


def _rwkv_chunk(r, k, v, lw, kkn, b, bsz, seq):
    t = r.shape[0]
    lt = _tile(seq, 1024)
    nt = seq // lt
    spec = pl.BlockSpec((lt, LANES), lambda bi, hp, i: (bi * nt + i, hp))
    return pl.pallas_call(
        functools.partial(_rwkv_chunk_kernel, lt=lt),
        grid=(bsz, RWKV_DIM // LANES, nt),
        in_specs=[spec] * 6,
        out_specs=spec,
        out_shape=jax.ShapeDtypeStruct((t, RWKV_DIM), F32),
        scratch_shapes=[pltpu.VMEM((2, HEAD_DIM, HEAD_DIM), F32)],
        compiler_params=_params("arbitrary", "arbitrary", "arbitrary"),
        name="rwkv_chunk",
    )(r, k, v, lw, kkn, b)


def _ssd_kernel(pb_ref, cw_ref, cb_ref, dtb_ref, alog_ref, dskip_ref, nrm_ref, ys_ref, ext_ref, st_ref, *, nck):
    q = SSD_CHUNK

    @pl.when(pl.program_id(1) == 0)
    def _():
        ext_ref[0:8, :] = jnp.zeros((8, SSD_XBC), F32)
        st_ref[...] = jnp.zeros_like(st_ref)

    ext_ref[8:8 + nck * q, :] = pb_ref[:, SSD_DIM:SSD_DIM + SSD_XBC]
    for c in range(nck):
        _ssd_chunk(pb_ref, cw_ref, cb_ref, dtb_ref, alog_ref, dskip_ref, nrm_ref, ys_ref, ext_ref, st_ref, c * q)
    ext_ref[0:8, :] = ext_ref[nck * q:nck * q + 8, :]


def _ssd_chunk(pb_ref, cw_ref, cb_ref, dtb_ref, alog_ref, dskip_ref, nrm_ref, ys_ref, ext_ref, st_ref, r0):
    q = SSD_CHUNK
    p = HEAD_DIM
    hpg = SSD_HEADS // SSD_GROUPS
    z = pb_ref[r0:r0 + q, 0:SSD_DIM]
    u = ext_ref[8 + r0:8 + r0 + q, :]
    dt_raw = pb_ref[r0:r0 + q, SSD_DIM + SSD_XBC:]

    conv = cb_ref[...] + cw_ref[SSD_CONV - 1:SSD_CONV, :] * u
    for j in range(SSD_CONV - 1):
        off = r0 + 8 - (SSD_CONV - 1) + j
        conv = conv + cw_ref[j:j + 1, :] * ext_ref[off:off + q, :]
    xbc = _silu(conv)
    xs = xbc[:, 0:SSD_DIM]
    bm = xbc[:, SSD_DIM:SSD_DIM + SSD_GROUPS * SSD_STATE]
    cm = xbc[:, SSD_DIM + SSD_GROUPS * SSD_STATE:]

    dt = _softplus(dt_raw + dtb_ref[...])
    a = -jnp.exp(alog_ref[...])
    ri = lax.broadcasted_iota(jnp.int32, (q, q), 0)
    ci = lax.broadcasted_iota(jnp.int32, (q, q), 1)
    causal = ri >= ci
    cum = _dot_exact_lhs(causal.astype(BF16), dt * a)
    cum_t = cum.T
    dt_t = dt.T
    cum_end = cum[q - 1:q, :]
    to_end = jnp.exp(cum_end - cum) * dt
    ecum = jnp.exp(cum)
    edec = jnp.exp(cum_end)

    ys = []
    for g in range(SSD_GROUPS):
        bm_g = bm[:, g * SSD_STATE:(g + 1) * SSD_STATE]
        cm_g = cm[:, g * SSD_STATE:(g + 1) * SSD_STATE]
        cb = _bdot_nt(cm_g, bm_g)
        bm_t = bm_g.T
        for hh in range(hpg):
            h = g * hpg + hh
            x_h = xs[:, h * p:(h + 1) * p]
            seg = cum[:, h:h + 1] - cum_t[h:h + 1, :]
            ldec = jnp.exp(jnp.where(causal, seg, -jnp.inf))
            wts = cb * ldec * dt_t[h:h + 1, :]
            y = _bdot(wts, x_h)
            h_prev = st_ref[h]
            y = y + _bdot(cm_g, h_prev) * ecum[:, h:h + 1]
            st_ref[h] = h_prev * edec[:, h:h + 1] + _bdot(bm_t, x_h * to_end[:, h:h + 1])
            ys.append(y + dskip_ref[:, h:h + 1] * x_h)
    yall = jnp.concatenate(ys, axis=1) * _silu(z)
    gw = SSD_DIM // SSD_GROUPS
    outs = []
    for g in range(SSD_GROUPS):
        yg = yall[:, g * gw:(g + 1) * gw]
        ms = jnp.mean(yg * yg, axis=-1, keepdims=True)
        outs.append(yg * lax.rsqrt(ms + NORM_EPS) * nrm_ref[:, g * gw:(g + 1) * gw])
    ys_ref[r0:r0 + q, :] = jnp.concatenate(outs, axis=1)


def _ssd(pb, bsz, seq, conv_w, conv_b, dt_bias, a_log, d_skip, ssd_norm):
    t = pb.shape[0]
    nck = 2 if seq % (2 * SSD_CHUNK) == 0 else 1
    rows = nck * SSD_CHUNK
    nc = seq // rows
    row = lambda b, i: (b * nc + i, 0)
    return pl.pallas_call(
        functools.partial(_ssd_kernel, nck=nck),
        grid=(bsz, nc),
        in_specs=[pl.BlockSpec((rows, SSD_COLS_PAD), row), _full(conv_w.shape), _full(conv_b.shape),
                  _full(dt_bias.shape), _full(a_log.shape), _full(d_skip.shape), _full(ssd_norm.shape)],
        out_specs=pl.BlockSpec((rows, SSD_DIM), row),
        out_shape=jax.ShapeDtypeStruct((t, SSD_DIM), F32),
        scratch_shapes=[pltpu.VMEM((8 + rows, SSD_XBC), F32),
                        pltpu.VMEM((SSD_HEADS, SSD_STATE, HEAD_DIM), F32)],
        compiler_params=_params("arbitrary", "arbitrary"),
        name="ssd",
    )(pb, conv_w, conv_b, dt_bias, a_log, d_skip, ssd_norm)


def _mix_ffn_kernel(h_ref, y_ref, bonus_ref, gate_ref, ys_ref, gnw_ref, gnb_ref, seg_ref, wa_ref, wb_ref,
                    g_ref, wg_ref, wu_ref, wd_ref, o_ref):
    y = y_ref[...]
    seg = seg_ref[...]
    mu = _head_sums(y, seg) * (1.0 / HEAD_DIM)
    yc = y - mu
    var = _head_sums(yc * yc, seg) * (1.0 / HEAD_DIM)
    yn = yc * lax.rsqrt(var + RWKV_GN_EPS) * gnw_ref[...] + gnb_ref[...]
    ya = (yn + bonus_ref[...]) * gate_ref[...]
    h = h_ref[...] + _bdot(ya, wa_ref[...]) + _bdot(ys_ref[...], wb_ref[...])

    xn = _rms(h, g_ref[...]).astype(BF16)
    act = _silu(jnp.dot(xn, wg_ref[...], preferred_element_type=F32)) * jnp.dot(
        xn, wu_ref[...], preferred_element_type=F32)
    o_ref[...] = h + jnp.dot(act.astype(BF16), wd_ref[...], preferred_element_type=F32)


def _mix_ffn(h2, y, bonus, gate, ys, gn_w, gn_b, seg, wa, wb, g, wg, wu, wd):
    t = h2.shape[0]
    tm = _tile(t, 512)
    row = lambda i: (i, 0)
    half = pl.BlockSpec((tm, RWKV_DIM), row)
    return pl.pallas_call(
        _mix_ffn_kernel,
        grid=(t // tm,),
        in_specs=[pl.BlockSpec((tm, D_MODEL), row), half, half, half, half, _full(gn_w.shape),
                  _full(gn_b.shape), _resident(seg.shape), _resident(wa.shape), _resident(wb.shape),
                  _full((1, D_MODEL)), _resident(wg.shape), _resident(wu.shape), _resident(wd.shape)],
        out_specs=pl.BlockSpec((tm, D_MODEL), row),
        out_shape=jax.ShapeDtypeStruct((t, D_MODEL), F32),
        compiler_params=_params("parallel"),
        name="mix_ffn",
    )(h2, y, bonus, gate, ys, gn_w, gn_b, seg, wa, wb, g, wg, wu, wd)


def _rope_table_kernel(pos_ref, freq_ref, cos_ref, sin_ref):
    ang = pos_ref[...] * freq_ref[...]
    cos_ref[...] = jnp.cos(ang)
    sin_ref[...] = jnp.sin(ang)


def _rope_tables(positions):
    t = positions.size
    half = ROPE_DIM // 2
    rows = t * half // LANES
    inv_freq = ROPE_THETA ** (-jnp.arange(0, ROPE_DIM, 2, dtype=F32) / ROPE_DIM)
    pos_rep = jnp.repeat(positions.reshape(-1).astype(F32), half).reshape(rows, LANES)
    freq = jnp.tile(inv_freq, LANES // half).reshape(1, LANES)
    tr = _tile(rows, 256)
    cos, sin = pl.pallas_call(
        _rope_table_kernel,
        grid=(rows // tr,),
        in_specs=[pl.BlockSpec((tr, LANES), lambda i: (i, 0)), _full((1, LANES))],
        out_specs=[pl.BlockSpec((tr, LANES), lambda i: (i, 0))] * 2,
        out_shape=[jax.ShapeDtypeStruct((rows, LANES), F32)] * 2,
        compiler_params=_params("parallel"),
        name="rope_tables",
    )(pos_rep, freq)
    return jnp.concatenate([cos.reshape(t, half) - 1.0, sin.reshape(t, half)], axis=1)


def _rope_expansion(width):
    half = ROPE_DIM // 2
    d = jnp.arange(width) % HEAD_DIM
    j = jnp.arange(half)[:, None]
    zero = jnp.zeros((half, width), F32)
    cos_rows = jnp.where((d[None, :] < ROPE_DIM) & (d[None, :] % half == j), 1.0, 0.0)
    sin_a = jnp.where((d[None, :] < half) & (d[None, :] == j), -1.0, 0.0)
    sin_b = jnp.where((d[None, :] >= half) & (d[None, :] < ROPE_DIM) & (d[None, :] - half == j), 1.0, 0.0)
    one = jnp.concatenate([jnp.concatenate([cos_rows, zero, zero], axis=1),
                           jnp.concatenate([zero, sin_a, sin_b], axis=1)], axis=0)
    return jnp.concatenate([one, one], axis=0).astype(BF16)


def _qkv_kernel(h_ref, g_ref, w_ref, b_ref, qn_ref, kn_ref, cs_ref, exp_ref, seg_ref, q_ref, k_ref, v_ref):
    xn = _rms(h_ref[...], g_ref[...]).astype(BF16)
    qkv = jnp.dot(xn, w_ref[...], preferred_element_type=F32) + b_ref[...]
    seg = seg_ref[...]
    wide = seg.shape[0]
    hi, lo = _split(cs_ref[...])
    tab = jnp.dot(jnp.concatenate([hi, lo], axis=1), exp_ref[...], preferred_element_type=F32)
    cos = 1.0 + tab[:, 0:wide]
    sa = tab[:, wide:2 * wide]
    sb = tab[:, 2 * wide:]

    def norm_rope(x, gain):
        ms = _head_sums(x * x, seg) * (1.0 / HEAD_DIM)
        xn_ = x * lax.rsqrt(ms + NORM_EPS) * gain
        return (xn_ * cos + pltpu.roll(xn_, wide - ROPE_DIM // 2, axis=1) * sa
                + pltpu.roll(xn_, ROPE_DIM // 2, axis=1) * sb)

    for c in range(Q_DIM // wide):
        x = qkv[:, c * wide:(c + 1) * wide]
        q_ref[:, c * wide:(c + 1) * wide] = (norm_rope(x, qn_ref[...]) * (HEAD_DIM ** -0.5)).astype(BF16)
    for c in range(KV_DIM // wide):
        x = qkv[:, Q_DIM + c * wide:Q_DIM + (c + 1) * wide]
        k_ref[:, c * wide:(c + 1) * wide] = norm_rope(x, kn_ref[...]).astype(BF16)
    v_ref[...] = qkv[:, Q_DIM + KV_DIM:].astype(BF16)


def _qkv(h2, g, w, b, qn, kn, cs, expand, seg):
    t = h2.shape[0]
    tm = _tile(t, 512)
    row = lambda i: (i, 0)
    return pl.pallas_call(
        _qkv_kernel,
        grid=(t // tm,),
        in_specs=[pl.BlockSpec((tm, D_MODEL), row), _full((1, D_MODEL)), _resident(w.shape), _full(b.shape),
                  _full(qn.shape), _full(kn.shape), pl.BlockSpec((tm, cs.shape[1]), row), _full(expand.shape),
                  _full(seg.shape)],
        out_specs=[pl.BlockSpec((tm, Q_DIM), row), pl.BlockSpec((tm, KV_DIM), row),
                   pl.BlockSpec((tm, KV_DIM), row)],
        out_shape=[jax.ShapeDtypeStruct((t, Q_DIM), BF16), jax.ShapeDtypeStruct((t, KV_DIM), BF16),
                   jax.ShapeDtypeStruct((t, KV_DIM), BF16)],
        compiler_params=_params("parallel"),
        name="qkv",
    )(h2, g, w, b, qn, kn, cs, expand, seg)


def _attn_kernel(h_ref, q_ref, kc_ref, kp_ref, vc_ref, vp_ref, sink_ref, wo_ref, bo_ref, o_ref, *, nqb):
    w = WINDOW
    n = HEAD_DIM
    rows = GQA_GROUP * w
    qi = lax.broadcasted_iota(jnp.int32, (rows, 2 * w), 0) % w
    kj = lax.broadcasted_iota(jnp.int32, (rows, 2 * w), 1)
    diff = w + qi - kj
    band = (diff >= 0) & (diff < w)
    band0 = band & (kj >= jnp.where(pl.program_id(1) == 0, w, 0))
    kall = jnp.concatenate([kp_ref[...], kc_ref[...]], axis=0)
    vall = jnp.concatenate([vp_ref[...], vc_ref[...]], axis=0)
    units = [(qb, g) for qb in range(nqb) for g in range(KV_HEADS)]
    s, e, den = {}, {}, {}
    for u in units:
        qb, g = u
        qs = jnp.concatenate(
            [q_ref[qb * w:(qb + 1) * w, (g * GQA_GROUP + i) * n:(g * GQA_GROUP + i + 1) * n]
             for i in range(GQA_GROUP)], axis=0)
        kk = kall[qb * w:(qb + 2) * w, g * n:(g + 1) * n]
        sc = lax.dot_general(qs, kk, (((1,), (1,)), ((), ())), preferred_element_type=F32)
        s[u] = jnp.where(band0 if qb == 0 else band, sc, -jnp.inf)
    sinks = [jnp.concatenate(
        [jnp.broadcast_to(sink_ref[:, g * GQA_GROUP + i:g * GQA_GROUP + i + 1], (w, LANES))
         for i in range(GQA_GROUP)], axis=0) for g in range(KV_HEADS)]
    ones = jnp.ones((2 * w, LANES), BF16)
    for u in units:
        sink = sinks[u[1]]
        m = jnp.maximum(jnp.broadcast_to(jnp.max(s[u], axis=-1, keepdims=True), (rows, LANES)), sink)
        e[u] = jnp.exp(s[u] - jnp.concatenate([m, m], axis=1)).astype(BF16)
        den[u] = jnp.exp(sink - m)
    outs = [[None] * Q_HEADS for _ in range(nqb)]
    for u in units:
        qb, g = u
        vv = vall[qb * w:(qb + 2) * w, g * n:(g + 1) * n]
        total = jnp.dot(e[u], ones, preferred_element_type=F32) + den[u]
        o = jnp.dot(e[u], vv, preferred_element_type=F32) / total[:, 0:n]
        for i in range(GQA_GROUP):
            outs[qb][g * GQA_GROUP + i] = o[i * w:(i + 1) * w]
    att = jnp.concatenate([jnp.concatenate(outs[qb], axis=1) for qb in range(nqb)], axis=0)
    o_ref[...] = h_ref[...] + _bdot(att, wo_ref[...]) + bo_ref[...]


def _attn(h2, q, k, v, sinks, wo, bo, bsz, seq):
    t = h2.shape[0]
    nqb = 4 if seq % (4 * WINDOW) == 0 else 1
    tq = nqb * WINDOW
    nb = seq // tq
    cur = lambda b, i: (b * nb + i, 0)
    prv = lambda b, i: (b * nb * nqb + jnp.maximum(i * nqb - 1, 0), 0)
    return pl.pallas_call(
        functools.partial(_attn_kernel, nqb=nqb),
        grid=(bsz, nb),
        in_specs=[pl.BlockSpec((tq, D_MODEL), cur), pl.BlockSpec((tq, Q_DIM), cur),
                  pl.BlockSpec((tq, KV_DIM), cur), pl.BlockSpec((WINDOW, KV_DIM), prv),
                  pl.BlockSpec((tq, KV_DIM), cur), pl.BlockSpec((WINDOW, KV_DIM), prv),
                  _full(sinks.shape), _full(wo.shape), _full(bo.shape)],
        out_specs=pl.BlockSpec((tq, D_MODEL), cur),
        out_shape=jax.ShapeDtypeStruct((t, D_MODEL), F32),
        compiler_params=_params("parallel", "parallel"),
        name="attn",
    )(h2, q, k, k, v, v, sinks, wo, bo)


def _router_kernel(h_ref, g_ref, wr_ref, aug_ref, cnt_ref, base_ref):
    @pl.when(pl.program_id(0) == 0)
    def _():
        base_ref[...] = jnp.zeros_like(base_ref)

    h = h_ref[...]
    xn = _rms(h, g_ref[...])
    hi, lo = _split(xn)
    whi = wr_ref[0]
    wlo = wr_ref[1]
    logits = (jnp.dot(hi, whi, preferred_element_type=F32) + jnp.dot(hi, wlo, preferred_element_type=F32)
              + jnp.dot(lo, whi, preferred_element_type=F32))
    tm = logits.shape[0]
    lane = lax.broadcasted_iota(jnp.int32, logits.shape, 1)
    logits = jnp.where(lane < N_EXPERTS, logits, -jnp.inf)
    m1 = jnp.max(logits, axis=-1, keepdims=True)
    i1 = jnp.min(jnp.where(logits == m1, lane, LANES), axis=-1, keepdims=True)
    rest = jnp.where(lane == i1, -jnp.inf, logits)
    m2 = jnp.max(rest, axis=-1, keepdims=True)
    i2 = jnp.min(jnp.where(rest == m2, lane, LANES), axis=-1, keepdims=True)
    e2 = jnp.exp(m2 - m1)
    w1 = 1.0 / (1.0 + e2)
    w2 = e2 / (1.0 + e2)

    first_low = i1 < i2
    pair = jnp.minimum(i1, i2) * N_EXPERTS + jnp.maximum(i1, i2)
    sel = lane == pair
    onehot = jnp.where(sel, 1.0, 0.0)
    ri = lax.broadcasted_iota(jnp.int32, (tm, tm), 0)
    ci = lax.broadcasted_iota(jnp.int32, (tm, tm), 1)
    before = jnp.dot(jnp.where(ci < ri, 1.0, 0.0).astype(BF16), onehot.astype(BF16),
                     preferred_element_type=F32)
    rank = jnp.sum(jnp.where(sel, base_ref[...] + before, 0.0), axis=-1, keepdims=True)
    total = base_ref[...] + jnp.sum(onehot, axis=0, keepdims=True)
    base_ref[...] = total
    cnt_ref[...] = total
    cols = (pair.astype(F32), jnp.where(first_low, w1, w2), jnp.where(first_low, w2, w1), rank)
    meta = jnp.zeros(logits.shape, F32)
    for c, val in enumerate(cols):
        meta = jnp.where(lane == c, val, meta)
    aug_ref[:, 0:D_MODEL] = h
    aug_ref[:, D_MODEL:] = meta


def _router(h2, g, wr):
    t = h2.shape[0]
    tm = _tile(t, 512)
    row = lambda i: (i, 0)
    return pl.pallas_call(
        _router_kernel,
        grid=(t // tm,),
        in_specs=[pl.BlockSpec((tm, D_MODEL), row), _full((1, D_MODEL)), _full(wr.shape)],
        out_specs=[pl.BlockSpec((tm, MOE_ROW), row), _full((1, LANES))],
        out_shape=[jax.ShapeDtypeStruct((t, MOE_ROW), F32), jax.ShapeDtypeStruct((1, LANES), F32)],
        scratch_shapes=[pltpu.VMEM((1, LANES), F32)],
        compiler_params=_params("arbitrary"),
        name="router",
    )(h2, g, wr)


def _moe_index_kernel(pair_ref, rank_ref, off_ref, pad_ref, src_ref, dst_ref, *, ntok, nrange):
    tm = MOE_TM
    nrow = src_ref.shape[0]

    def virtual(j, c):
        dst_ref[j] = nrow + j
        return c

    lax.fori_loop(0, tm, virtual, 0)

    def pad_range(e, c):
        lo = pad_ref[3 * e]
        base = pad_ref[3 * e + 2]

        def fill(p, c2):
            src_ref[p] = 0
            dst_ref[tm + p] = base + (p - lo)
            return c2

        lax.fori_loop(lo, pad_ref[3 * e + 1], fill, 0)
        return c

    lax.fori_loop(0, nrange, pad_range, 0)

    def body(t, c):
        p = off_ref[pair_ref[t]] + rank_ref[t]
        src_ref[p] = t
        dst_ref[tm + p] = t
        return c

    lax.fori_loop(0, ntok, body, 0, unroll=4)


def _moe_index(pair, rank, off, pad, rows):
    smem = pl.BlockSpec(memory_space=pltpu.SMEM)
    return pl.pallas_call(
        functools.partial(_moe_index_kernel, ntok=pair.shape[0], nrange=pad.shape[0] // 3),
        grid_spec=pltpu.PrefetchScalarGridSpec(
            num_scalar_prefetch=4, grid=(1,), in_specs=[], out_specs=[smem, smem]),
        out_shape=[jax.ShapeDtypeStruct((rows,), jnp.int32), jax.ShapeDtypeStruct((rows + MOE_TM,), jnp.int32)],
        compiler_params=_params("arbitrary"),
        name="moe_index",
    )(pair, rank, off, pad)


def _row_copy(src_ref, src_row, dst_ref, dst_row, sem):
    return pltpu.make_async_copy(src_ref.at[pl.ds(src_row, 1)], dst_ref.at[pl.ds(dst_row, 1)], sem)


def _experts_kernel(ta_ref, tb_ref, nused_ref, src_ref, dst_ref, x_ref, g_ref, wga_ref, wua_ref, wda_ref,
                    wgb_ref, wub_ref, wdb_ref, y_ref, xbuf, ybuf, gsem, ssem):
    del ta_ref, tb_ref
    i = pl.program_id(0)
    nused = nused_ref[0]
    tm = MOE_TM

    sub = MOE_SUB
    ngroup = tm // sub

    def gather_group(tile, slot, k):
        for u in range(sub):
            _row_copy(x_ref, src_ref[tile * tm + k * sub + u], xbuf.at[slot, k], u, gsem.at[slot]).start()

    def scatter_group(tile, slot, k):
        for u in range(sub):
            _row_copy(ybuf.at[slot, k], u, y_ref, dst_ref[(tile + 1) * tm + k * sub + u], ssem.at[slot]).start()

    def wait_gather(slot):
        def body(k, c):
            pltpu.make_async_copy(x_ref.at[pl.ds(0, sub)], xbuf.at[slot, k], gsem.at[slot]).wait()
            return c
        lax.fori_loop(0, ngroup, body, 0)

    def wait_scatter(slot):
        def body(k, c):
            pltpu.make_async_copy(ybuf.at[slot, k], y_ref.at[pl.ds(0, sub)], ssem.at[slot]).wait()
            return c
        lax.fori_loop(0, ngroup, body, 0)

    @pl.when(i == 0)
    def _():
        ybuf[1] = jnp.zeros(ybuf.shape[1:], F32)

        def body(k, c):
            gather_group(0, 0, k)
            return c
        lax.fori_loop(0, ngroup, body, 0)

    @pl.when(i < nused)
    def _():
        slot = i % 2
        other = 1 - slot
        nxt = jnp.minimum(i + 1, nused - 1)
        wait_gather(slot)

        @pl.when(i >= 1)
        def _():
            wait_scatter(slot)

        rows = xbuf[slot].reshape(tm, MOE_ROW)
        x = rows[:, 0:D_MODEL]
        meta = rows[:, D_MODEL:]
        xb = _rms(x, g_ref[...]).astype(BF16)
        bounds = list(range(0, FFN_EXPERT, MOE_FCHUNK)) + [FFN_EXPERT]
        nchunk = len(bounds) - 1
        experts = ((wga_ref, wua_ref, wda_ref, meta[:, 1:2]), (wgb_ref, wub_ref, wdb_ref, meta[:, 2:3]))
        y = x
        step = 0
        for wg_ref, wu_ref, wd_ref, weight in experts:
            acts = []
            for c in range(nchunk):
                c0, c1 = bounds[c], bounds[c + 1]
                gate = jnp.dot(xb, wg_ref[0, :, c0:c1], preferred_element_type=F32)
                up = jnp.dot(xb, wu_ref[0, :, c0:c1], preferred_element_type=F32)
                acts.append((_silu(gate) * up).astype(BF16))
                for k in range(step * ngroup // (2 * nchunk), (step + 1) * ngroup // (2 * nchunk)):
                    gather_group(nxt, other, k)
                    scatter_group(i - 1, other, k)
                step += 1
            y = y + weight * jnp.dot(jnp.concatenate(acts, axis=1), wd_ref[0], preferred_element_type=F32)
        ybuf[slot] = y.reshape(ngroup, sub, D_MODEL)

        @pl.when(i == nused - 1)
        def _():
            def body(k, c):
                scatter_group(i, slot, k)
                return c
            lax.fori_loop(0, ngroup, body, 0)
            wait_scatter(other)
            wait_scatter(slot)
            wait_gather(other)


def _experts(tile_a, tile_b, nused, src, dst, aug, g, wg, wu, wd, out_rows):
    ntile = tile_a.shape[0]
    tm = MOE_TM
    sel_a = lambda i, ta, tb, nu, s, d: (ta[i], 0, 0)
    sel_b = lambda i, ta, tb, nu, s, d: (tb[i], 0, 0)
    any_spec = pl.BlockSpec(memory_space=pl.ANY)
    up_shape = (1, D_MODEL, FFN_EXPERT)
    down_shape = (1, FFN_EXPERT, D_MODEL)
    return pl.pallas_call(
        _experts_kernel,
        grid_spec=pltpu.PrefetchScalarGridSpec(
            num_scalar_prefetch=5, grid=(ntile,),
            in_specs=[any_spec, pl.BlockSpec((1, D_MODEL), lambda i, *_: (0, 0)),
                      pl.BlockSpec(up_shape, sel_a), pl.BlockSpec(up_shape, sel_a), pl.BlockSpec(down_shape, sel_a),
                      pl.BlockSpec(up_shape, sel_b), pl.BlockSpec(up_shape, sel_b), pl.BlockSpec(down_shape, sel_b)],
            out_specs=any_spec,
            scratch_shapes=[pltpu.VMEM((2, tm // MOE_SUB, MOE_SUB, MOE_ROW), F32),
                            pltpu.VMEM((2, tm // MOE_SUB, MOE_SUB, D_MODEL), F32),
                            pltpu.SemaphoreType.DMA((2,)), pltpu.SemaphoreType.DMA((2,))]),
        out_shape=jax.ShapeDtypeStruct((out_rows, D_MODEL), F32),
        compiler_params=_params("arbitrary"),
        name="moe_experts",
    )(tile_a, tile_b, nused, src, dst, aug, g, wg, wu, wd, wg, wu, wd)


def _moe(h2, g, router, wg, wu, wd):
    t = h2.shape[0]
    tm = MOE_TM
    wr = jnp.pad(router, ((0, 0), (0, LANES - N_EXPERTS)))
    wr_hi = wr.astype(BF16)
    wr_lo = (wr - wr_hi.astype(F32)).astype(BF16)
    aug, cnt = _router(h2, g, jnp.stack([wr_hi, wr_lo]))

    cols = aug[:, D_MODEL:D_MODEL + 8].T.astype(jnp.int32)
    pair, rank = cols[0], cols[3]
    npair = N_EXPERTS * N_EXPERTS
    count = cnt[0, :npair].astype(jnp.int32)
    ntile = (count + tm - 1) // tm
    tile_end = jnp.cumsum(ntile)
    off = (tile_end - ntile) * tm
    rows = t + (N_EXPERTS * (N_EXPERTS - 1) // 2) * tm
    tile_id = jnp.arange(rows // tm)
    tile_pair = jnp.minimum(jnp.sum(tile_id[:, None] >= tile_end[None, :], axis=1), npair - 1).astype(jnp.int32)
    nused = tile_end[-1:].astype(jnp.int32)
    first = jnp.append(off + count, tile_end[-1] * tm)
    end = jnp.append(off + ntile * tm, rows)
    dump = t + first - jnp.append(jnp.cumsum(count), t)
    pad = jnp.stack([first, end, dump], axis=1).reshape(-1).astype(jnp.int32)
    src, dst = _moe_index(pair, rank, off, pad, rows)

    out = _experts(tile_pair // N_EXPERTS, tile_pair % N_EXPERTS, nused, src, dst, aug, g, wg, wu, wd, rows + tm)
    return out[:t]


def _segment_matrix(width):
    i = jnp.arange(width) // HEAD_DIM
    return (i[:, None] == i[None, :]).astype(BF16)


def _layer0(h2, bsz, seq, norm_mix, w_in, mu_shift, w0, w_decay_up, a0, w_iclr_up, w_gate_up, k_k, k_a, r_k,
            gn_w, gn_b, conv_w, conv_b, dt_bias, a_log, d_skip, ssd_norm, w_out, norm_ffn, ffn_gate, ffn_up,
            ffn_down):
    row = lambda a: a.reshape(1, -1)
    seg = _segment_matrix(SEG_WIDTH)
    wa = w_in[:, :RWKV_COLS].astype(BF16)
    wb = jnp.pad(w_in[:, RWKV_COLS:], ((0, 0), (0, SSD_COLS_PAD - SSD_COLS))).astype(BF16)
    lora = w_decay_up.shape[0]
    wd = jnp.concatenate([w_decay_up, jnp.zeros((LANES - lora, RWKV_DIM), F32)], axis=0).astype(BF16)
    wi = jnp.concatenate([jnp.zeros((LANES - lora, RWKV_DIM), F32), w_iclr_up], axis=0).astype(BF16)
    pb, r, k, v, lw, kkn, b, g, bonus = _in_proj(
        h2, bsz, seq, row(norm_mix), wa, wb, row(mu_shift), row(w0), wd, row(a0), wi, w_gate_up.astype(BF16),
        row(k_k), row(k_a), row(r_k), seg)
    y = _rwkv_chunk(r, k, v, lw, kkn, b, bsz, seq)

    lane_pad = lambda a: jnp.pad(row(a), ((0, 0), (0, LANES - a.size)))
    ys = _ssd(pb, bsz, seq, conv_w, row(conv_b), lane_pad(dt_bias), lane_pad(a_log), lane_pad(d_skip),
              row(ssd_norm))

    w_out = w_out.astype(BF16)
    return _mix_ffn(h2, y, bonus, g, ys, row(gn_w), row(gn_b), seg, w_out[:RWKV_DIM], w_out[RWKV_DIM:],
                    row(norm_ffn), ffn_gate.astype(BF16), ffn_up.astype(BF16), ffn_down.astype(BF16))


def _layer1(h2, bsz, seq, tables, norm_mix, w_qkv, b_qkv, q_norm, k_norm, sinks, w_o, b_o, norm_ffn, router,
            exp_gate, exp_up, exp_down):
    row = lambda a: a.reshape(1, -1)
    wide = lambda a: jnp.tile(a, SEG_WIDTH // HEAD_DIM).reshape(1, SEG_WIDTH)
    q, k, v = _qkv(h2, row(norm_mix), w_qkv.astype(BF16), row(b_qkv), wide(q_norm), wide(k_norm), tables,
                   _rope_expansion(SEG_WIDTH), _segment_matrix(SEG_WIDTH))
    h2 = _attn(h2, q, k, v, row(sinks), w_o.astype(BF16), row(b_o), bsz, seq)

    return _moe(h2, row(norm_ffn), router, exp_gate.astype(BF16), exp_up.astype(BF16), exp_down.astype(BF16))


def kernel(x, positions, ev_norm_mix, ev_w_in, ev_mu_shift, ev_w0, ev_w_decay_up, ev_a0, ev_w_iclr_up, ev_w_gate_up, ev_k_k, ev_k_a, ev_r_k, ev_gn_w, ev_gn_b, ev_conv_w, ev_conv_b, ev_dt_bias, ev_a_log, ev_d_skip, ev_ssd_norm, ev_w_out, ev_norm_ffn, ev_ffn_gate, ev_ffn_up, ev_ffn_down, od_norm_mix, od_w_qkv, od_b_qkv, od_q_norm, od_k_norm, od_sinks, od_w_o, od_b_o, od_norm_ffn, od_router, od_exp_gate, od_exp_up, od_exp_down):
    bsz, seq, d = x.shape
    depth = ev_norm_mix.shape[0] + od_norm_mix.shape[0]
    tables = _rope_tables(positions)
    h2 = x.reshape(bsz * seq, d)
    for layer in range(depth):
        i = layer // 2
        if layer % 2 == 0:
            h2 = _layer0(h2, bsz, seq, ev_norm_mix[i], ev_w_in[i], ev_mu_shift[i], ev_w0[i], ev_w_decay_up[i],
                         ev_a0[i], ev_w_iclr_up[i], ev_w_gate_up[i], ev_k_k[i], ev_k_a[i], ev_r_k[i],
                         ev_gn_w[i], ev_gn_b[i], ev_conv_w[i], ev_conv_b[i], ev_dt_bias[i], ev_a_log[i],
                         ev_d_skip[i], ev_ssd_norm[i], ev_w_out[i], ev_norm_ffn[i], ev_ffn_gate[i],
                         ev_ffn_up[i], ev_ffn_down[i])
        else:
            h2 = _layer1(h2, bsz, seq, tables, od_norm_mix[i], od_w_qkv[i], od_b_qkv[i], od_q_norm[i],
                         od_k_norm[i], od_sinks[i], od_w_o[i], od_b_o[i], od_norm_ffn[i], od_router[i],
                         od_exp_gate[i], od_exp_up[i], od_exp_down[i])
    return h2.reshape(bsz, seq, d)
```

```python
import functools
import math

import jax
import jax.numpy as jnp
from jax import lax
from jax.experimental import pallas as pl
from jax.experimental.pallas import tpu as pltpu

F32 = jnp.float32
BF16 = jnp.bfloat16

D_MODEL = 1024
HEAD_DIM = 64
NORM_EPS = 1e-6

RWKV_HEADS = 8
RWKV_DIM = 512
RWKV_COLS = 1792
RWKV_GN_EPS = 64e-5
RWKV_CHUNK = 64
RWKV_GROUP = 256

SSD_HEADS = 8
SSD_DIM = 512
SSD_GROUPS = 2
SSD_STATE = 128
SSD_CONV = 4
SSD_CHUNK = 128
SSD_XBC = 1024
SSD_COLS = 1544
SSD_COLS_PAD = 1664

Q_HEADS = 16
KV_HEADS = 4
GQA_GROUP = 4
Q_DIM = 1024
KV_DIM = 256
WINDOW = 128
ROPE_THETA = 500000.0
ROPE_DIM = 16

FFN_DENSE = 2816
N_EXPERTS = 8
FFN_EXPERT = 1408
MOE_TM = 256
MOE_SUB = 8
MOE_FCHUNK = 256

LANES = 128
SEG_WIDTH = 256
VMEM_LIMIT_BYTES = 56 * 1024 * 1024


def _params(*sem):
    return pltpu.CompilerParams(dimension_semantics=sem, vmem_limit_bytes=VMEM_LIMIT_BYTES)


def _bdot(a, b):
    return jnp.dot(a.astype(BF16), b.astype(BF16), preferred_element_type=F32)


def _bdot_nt(a, b):
    return lax.dot_general(a.astype(BF16), b.astype(BF16), (((1,), (1,)), ((), ())),
                           preferred_element_type=F32)


def _bdot_tn(a, b):
    return lax.dot_general(a.astype(BF16), b.astype(BF16), (((0,), (0,)), ((), ())),
                           preferred_element_type=F32)


def _split(x):
    hi = x.astype(BF16)
    lo = (x - hi.astype(F32)).astype(BF16)
    return hi, lo


def _dot_exact_lhs(m, x):
    hi, lo = _split(x)
    return (jnp.dot(m, hi, preferred_element_type=F32) + jnp.dot(m, lo, preferred_element_type=F32))


def _head_sums(x, seg):
    w = seg.shape[0]
    xb = x.astype(BF16)
    return jnp.concatenate(
        [jnp.dot(xb[:, c:c + w], seg, preferred_element_type=F32) for c in range(0, x.shape[1], w)], axis=1)


def _sigmoid(x):
    return 1.0 / (1.0 + jnp.exp(-x))


def _silu(x):
    return x * _sigmoid(x)


def _softplus(x):
    return jnp.maximum(x, 0.0) + jnp.log(1.0 + jnp.exp(-jnp.abs(x)))


def _rms(x, g):
    ms = jnp.mean(x * x, axis=-1, keepdims=True)
    return x * lax.rsqrt(ms + NORM_EPS) * g


def _tile(n, pref):
    t = min(n, pref)
    while n % t:
        t //= 2
    return t


def _full(shape):
    nd = len(shape)
    return pl.BlockSpec(shape, lambda *_: (0,) * nd)


def _resident(shape):
    nd = len(shape)
    return pl.BlockSpec(shape, lambda *_: (0,) * nd, pipeline_mode=pl.Buffered(1))


def _in_proj_kernel(x_ref, g_ref, wa_ref, wb_ref, mu_ref, w0_ref, wd_ref, a0_ref, wi_ref, wg_ref, kk_ref, ka_ref,
                    rk_ref, seg_ref, pb_ref, r_ref, k_ref, v_ref, lw_ref, kkn_ref, b_ref, gate_ref, bonus_ref,
                    carry_ref):
    @pl.when(pl.program_id(1) == 0)
    def _():
        carry_ref[...] = jnp.zeros_like(carry_ref)

    xn = _rms(x_ref[...], g_ref[...]).astype(BF16)
    pb_ref[...] = jnp.dot(xn, wb_ref[...], preferred_element_type=F32)
    pa = jnp.dot(xn, wa_ref[...], preferred_element_type=F32)
    tm = pa.shape[0]
    row = lax.broadcasted_iota(jnp.int32, pa.shape, 0)
    prev = jnp.where(row == 0, carry_ref[...], pltpu.roll(pa, 1, axis=0))
    carry_ref[...] = pa[tm - 1:tm, :]
    x = pa + (prev - pa) * mu_ref[...]

    r = x[:, 0:RWKV_DIM]
    k = x[:, RWKV_DIM:2 * RWKV_DIM]
    v = x[:, 2 * RWKV_DIM:3 * RWKV_DIM]
    lora = x[:, 3 * RWKV_DIM:3 * RWKV_DIM + LANES]
    gl = x[:, 3 * RWKV_DIM + LANES:]
    seg = seg_ref[...]

    w_raw = w0_ref[...] + _bdot(jnp.tanh(lora), wd_ref[...])
    lw_ref[...] = (-math.exp(-0.5)) * _sigmoid(w_raw)
    iclr = _sigmoid(a0_ref[...] + _bdot(lora, wi_ref[...]))
    gate_ref[...] = _bdot(_sigmoid(gl), wg_ref[...])

    kk = k * kk_ref[...]
    kkn = kk * lax.rsqrt(_head_sums(kk * kk, seg) + 1e-12)
    k2 = k * (1.0 + (iclr - 1.0) * ka_ref[...])
    r_ref[...] = r
    k_ref[...] = k2
    v_ref[...] = v
    kkn_ref[...] = kkn
    b_ref[...] = kkn * iclr
    bonus_ref[...] = _head_sums(r * k2 * rk_ref[...], seg) * v


def _in_proj(x2, bsz, seq, g, wa, wb, mu, w0, wd, a0, wi, wg, k_k, k_a, r_k, seg):
    t = x2.shape[0]
    tm = _tile(seq, 512)
    nt = seq // tm
    row = lambda b, i: (b * nt + i, 0)
    out = jax.ShapeDtypeStruct((t, RWKV_DIM), F32)
    small = [mu, w0, wd, a0, wi, wg, k_k, k_a, r_k, seg]
    return pl.pallas_call(
        _in_proj_kernel,
        grid=(bsz, nt),
        in_specs=[pl.BlockSpec((tm, D_MODEL), row), _full((1, D_MODEL)), _resident(wa.shape),
                  _resident(wb.shape)] + [_full(a.shape) for a in small],
        out_specs=[pl.BlockSpec((tm, SSD_COLS_PAD), row)] + [pl.BlockSpec((tm, RWKV_DIM), row)] * 8,
        out_shape=[jax.ShapeDtypeStruct((t, SSD_COLS_PAD), F32)] + [out] * 8,
        scratch_shapes=[pltpu.VMEM((1, RWKV_COLS), F32)],
        compiler_params=_params("arbitrary", "arbitrary"),
        name="in_proj",
    )(x2, g, wa, wb, *small)


def _rwkv_chunk_kernel(r_ref, k_ref, v_ref, lw_ref, kk_ref, b_ref, y_ref, z_ref, *, lt):
    c_len = RWKV_CHUNK
    n = HEAD_DIM
    gl = RWKV_GROUP
    nchunk = gl // c_len
    nheads = LANES // n
    shift = c_len.bit_length() - 1

    @pl.when(pl.program_id(2) == 0)
    def _():
        z_ref[...] = jnp.zeros_like(z_ref)

    ri = lax.broadcasted_iota(jnp.int32, (gl, gl), 0)
    ci = lax.broadcasted_iota(jnp.int32, (gl, gl), 1)
    tri_bd = jnp.where((ci <= ri) & (ci >= ((ri >> shift) << shift)), 1.0, 0.0).astype(BF16)
    re_ = lax.broadcasted_iota(jnp.int32, (c_len, c_len), 0)
    ce_ = lax.broadcasted_iota(jnp.int32, (c_len, c_len), 1)
    eye = re_ == ce_
    ri2 = lax.broadcasted_iota(jnp.int32, (2 * gl, gl), 0)
    ci2 = lax.broadcasted_iota(jnp.int32, (2 * gl, gl), 1)
    t2 = jnp.where(ri2 < gl, ri2, ri2 - gl)
    mask2 = (ci2 <= jnp.where(ri2 < gl, t2 - 1, t2)) & (ci2 >= ((t2 >> shift) << shift))
    zeros = jnp.zeros((c_len, n), F32)

    nsub = lt // gl
    sysid = [(s, h) for s in range(nsub) for h in range(nheads)]
    pre = []
    for s in range(nsub):
        sl = pl.ds(s * gl, gl)
        lw = lw_ref[sl, :]
        g_in = _dot_exact_lhs(tri_bd, lw)
        ends = [g_in[(c + 1) * c_len - 1:(c + 1) * c_len, :] for c in range(nchunk)]
        g_end = jnp.concatenate([jnp.broadcast_to(e, (c_len, LANES)) for e in ends], axis=0)
        e_end = jnp.exp(g_end - g_in)
        en = jnp.exp(-g_in)
        k = k_ref[sl, :]
        b = b_ref[sl, :]
        pre.append(dict(
            ends=ends, v=v_ref[sl, :], rt=r_ref[sl, :] * jnp.exp(g_in),
            at=-kk_ref[sl, :] * jnp.exp(g_in - lw), kt=k * en, bt=b * en, bend=b * e_end, kend=k * e_end))

    def hs(name, s, h):
        return pre[s][name][:, h * n:(h + 1) * n]

    xb, xk, xkv, p, x = {}, {}, {}, {}, {}
    for q in sysid:
        la = jnp.concatenate([hs("at", *q), hs("rt", *q)], axis=0)
        xb[q] = jnp.where(mask2, _bdot_nt(la, hs("bt", *q)), 0.0)
        xk[q] = jnp.where(mask2, _bdot_nt(la, hs("kt", *q)), 0.0)
    for q in sysid:
        xkv[q] = _bdot(xk[q], hs("v", *q))
        p[q] = xb[q][0:gl]
        x[q] = jnp.concatenate([hs("at", *q), xkv[q][0:gl]], axis=1)
    for i in range(6):
        for q in sysid:
            x[q] = x[q] + _bdot(p[q], x[q])
        if i < 5:
            for q in sysid:
                p[q] = _bdot(p[q], p[q])
    rq, y0, mn = {}, {}, {}
    for q in sysid:
        yy = _bdot(xb[q][gl:], x[q])
        rq[q] = yy[:, 0:n] + hs("rt", *q)
        y0[q] = yy[:, n:] + xkv[q][gl:]
        bend_h, kend_h, v_h = hs("bend", *q), hs("kend", *q), hs("v", *q)
        for c in range(nchunk):
            rows = slice(c * c_len, (c + 1) * c_len)
            lhs = jnp.concatenate([bend_h[rows], kend_h[rows]], axis=0)
            rhs = jnp.concatenate([x[q][rows], jnp.concatenate([zeros, v_h[rows]], axis=1)], axis=0)
            mn[q + (c,)] = _bdot_tn(lhs, rhs)
    zero_n = jnp.zeros((n, n), F32)
    comp = {q: [] for q in sysid}
    for c in range(nchunk):
        for q in sysid:
            s, h = q
            pend = jnp.exp(pre[s]["ends"][c][:, h * n:(h + 1) * n])
            m_mat = mn[q + (c,)][:, 0:n] + jnp.where(eye, pend, 0.0)
            n_mat = mn[q + (c,)][:, n:]
            if c == 0:
                comp[q].append(jnp.concatenate([m_mat, n_mat], axis=1))
            else:
                comp[q].append(_bdot(m_mat, comp[q][c - 1]) + jnp.concatenate([zero_n, n_mat], axis=1))
    zs = [z_ref[h] for h in range(nheads)]
    for s in range(nsub):
        yh = []
        for h in range(nheads):
            q = (s, h)
            pm = jnp.concatenate([comp[q][c][:, 0:n] for c in range(nchunk)], axis=0)
            pn = jnp.concatenate([comp[q][c][:, n:] for c in range(nchunk)], axis=0)
            after = _bdot(pm, zs[h]) + pn
            z_in = [zs[h]] + [after[c * c_len:(c + 1) * c_len] for c in range(nchunk - 1)]
            yh.append(jnp.concatenate(
                [_bdot(rq[q][c * c_len:(c + 1) * c_len], z_in[c]) + y0[q][c * c_len:(c + 1) * c_len]
                 for c in range(nchunk)], axis=0))
            zs[h] = after[(nchunk - 1) * c_len:]
        y_ref[pl.ds(s * gl, gl), :] = jnp.concatenate(yh, axis=1)
    for h in range(nheads):
        z_ref[h] = zs[h]


def _rwkv_chunk(r, k, v, lw, kkn, b, bsz, seq):
    t = r.shape[0]
    lt = _tile(seq, 1024)
    nt = seq // lt
    spec = pl.BlockSpec((lt, LANES), lambda bi, hp, i: (bi * nt + i, hp))
    return pl.pallas_call(
        functools.partial(_rwkv_chunk_kernel, lt=lt),
        grid=(bsz, RWKV_DIM // LANES, nt),
        in_specs=[spec] * 6,
        out_specs=spec,
        out_shape=jax.ShapeDtypeStruct((t, RWKV_DIM), F32),
        scratch_shapes=[pltpu.VMEM((2, HEAD_DIM, HEAD_DIM), F32)],
        compiler_params=_params("arbitrary", "arbitrary", "arbitrary"),
        name="rwkv_chunk",
    )(r, k, v, lw, kkn, b)


def _ssd_kernel(pb_ref, cw_ref, cb_ref, dtb_ref, alog_ref, dskip_ref, nrm_ref, ys_ref, ext_ref, st_ref, *, nck):
    q = SSD_CHUNK

    @pl.when(pl.program_id(1) == 0)
    def _():
        ext_ref[0:8, :] = jnp.zeros((8, SSD_XBC), F32)
        st_ref[...] = jnp.zeros_like(st_ref)

    ext_ref[8:8 + nck * q, :] = pb_ref[:, SSD_DIM:SSD_DIM + SSD_XBC]
    for c in range(nck):
        _ssd_chunk(pb_ref, cw_ref, cb_ref, dtb_ref, alog_ref, dskip_ref, nrm_ref, ys_ref, ext_ref, st_ref, c * q)
    ext_ref[0:8, :] = ext_ref[nck * q:nck * q + 8, :]


def _ssd_chunk(pb_ref, cw_ref, cb_ref, dtb_ref, alog_ref, dskip_ref, nrm_ref, ys_ref, ext_ref, st_ref, r0):
    q = SSD_CHUNK
    p = HEAD_DIM
    hpg = SSD_HEADS // SSD_GROUPS
    z = pb_ref[r0:r0 + q, 0:SSD_DIM]
    u = ext_ref[8 + r0:8 + r0 + q, :]
    dt_raw = pb_ref[r0:r0 + q, SSD_DIM + SSD_XBC:]

    conv = cb_ref[...] + cw_ref[SSD_CONV - 1:SSD_CONV, :] * u
    for j in range(SSD_CONV - 1):
        off = r0 + 8 - (SSD_CONV - 1) + j
        conv = conv + cw_ref[j:j + 1, :] * ext_ref[off:off + q, :]
    xbc = _silu(conv)
    xs = xbc[:, 0:SSD_DIM]
    bm = xbc[:, SSD_DIM:SSD_DIM + SSD_GROUPS * SSD_STATE]
    cm = xbc[:, SSD_DIM + SSD_GROUPS * SSD_STATE:]

    dt = _softplus(dt_raw + dtb_ref[...])
    a = -jnp.exp(alog_ref[...])
    ri = lax.broadcasted_iota(jnp.int32, (q, q), 0)
    ci = lax.broadcasted_iota(jnp.int32, (q, q), 1)
    causal = ri >= ci
    cum = _dot_exact_lhs(causal.astype(BF16), dt * a)
    cum_t = cum.T
    dt_t = dt.T
    cum_end = cum[q - 1:q, :]
    to_end = jnp.exp(cum_end - cum) * dt
    ecum = jnp.exp(cum)
    edec = jnp.exp(cum_end)

    ys = []
    for g in range(SSD_GROUPS):
        bm_g = bm[:, g * SSD_STATE:(g + 1) * SSD_STATE]
        cm_g = cm[:, g * SSD_STATE:(g + 1) * SSD_STATE]
        cb = _bdot_nt(cm_g, bm_g)
        bm_t = bm_g.T
        for hh in range(hpg):
            h = g * hpg + hh
            x_h = xs[:, h * p:(h + 1) * p]
            seg = cum[:, h:h + 1] - cum_t[h:h + 1, :]
            ldec = jnp.exp(jnp.where(causal, seg, -jnp.inf))
            wts = cb * ldec * dt_t[h:h + 1, :]
            y = _bdot(wts, x_h)
            h_prev = st_ref[h]
            y = y + _bdot(cm_g, h_prev) * ecum[:, h:h + 1]
            st_ref[h] = h_prev * edec[:, h:h + 1] + _bdot(bm_t, x_h * to_end[:, h:h + 1])
            ys.append(y + dskip_ref[:, h:h + 1] * x_h)
    yall = jnp.concatenate(ys, axis=1) * _silu(z)
    gw = SSD_DIM // SSD_GROUPS
    outs = []
    for g in range(SSD_GROUPS):
        yg = yall[:, g * gw:(g + 1) * gw]
        ms = jnp.mean(yg * yg, axis=-1, keepdims=True)
        outs.append(yg * lax.rsqrt(ms + NORM_EPS) * nrm_ref[:, g * gw:(g + 1) * gw])
    ys_ref[r0:r0 + q, :] = jnp.concatenate(outs, axis=1)


def _ssd(pb, bsz, seq, conv_w, conv_b, dt_bias, a_log, d_skip, ssd_norm):
    t = pb.shape[0]
    nck = 2 if seq % (2 * SSD_CHUNK) == 0 else 1
    rows = nck * SSD_CHUNK
    nc = seq // rows
    row = lambda b, i: (b * nc + i, 0)
    return pl.pallas_call(
        functools.partial(_ssd_kernel, nck=nck),
        grid=(bsz, nc),
        in_specs=[pl.BlockSpec((rows, SSD_COLS_PAD), row), _full(conv_w.shape), _full(conv_b.shape),
                  _full(dt_bias.shape), _full(a_log.shape), _full(d_skip.shape), _full(ssd_norm.shape)],
        out_specs=pl.BlockSpec((rows, SSD_DIM), row),
        out_shape=jax.ShapeDtypeStruct((t, SSD_DIM), F32),
        scratch_shapes=[pltpu.VMEM((8 + rows, SSD_XBC), F32),
                        pltpu.VMEM((SSD_HEADS, SSD_STATE, HEAD_DIM), F32)],
        compiler_params=_params("arbitrary", "arbitrary"),
        name="ssd",
    )(pb, conv_w, conv_b, dt_bias, a_log, d_skip, ssd_norm)


def _mix_ffn_kernel(h_ref, y_ref, bonus_ref, gate_ref, ys_ref, gnw_ref, gnb_ref, seg_ref, wa_ref, wb_ref,
                    g_ref, wg_ref, wu_ref, wd_ref, o_ref):
    y = y_ref[...]
    seg = seg_ref[...]
    mu = _head_sums(y, seg) * (1.0 / HEAD_DIM)
    yc = y - mu
    var = _head_sums(yc * yc, seg) * (1.0 / HEAD_DIM)
    yn = yc * lax.rsqrt(var + RWKV_GN_EPS) * gnw_ref[...] + gnb_ref[...]
    ya = (yn + bonus_ref[...]) * gate_ref[...]
    h = h_ref[...] + _bdot(ya, wa_ref[...]) + _bdot(ys_ref[...], wb_ref[...])

    xn = _rms(h, g_ref[...]).astype(BF16)
    act = _silu(jnp.dot(xn, wg_ref[...], preferred_element_type=F32)) * jnp.dot(
        xn, wu_ref[...], preferred_element_type=F32)
    o_ref[...] = h + jnp.dot(act.astype(BF16), wd_ref[...], preferred_element_type=F32)


def _mix_ffn(h2, y, bonus, gate, ys, gn_w, gn_b, seg, wa, wb, g, wg, wu, wd):
    t = h2.shape[0]
    tm = _tile(t, 512)
    row = lambda i: (i, 0)
    half = pl.BlockSpec((tm, RWKV_DIM), row)
    return pl.pallas_call(
        _mix_ffn_kernel,
        grid=(t // tm,),
        in_specs=[pl.BlockSpec((tm, D_MODEL), row), half, half, half, half, _full(gn_w.shape),
                  _full(gn_b.shape), _resident(seg.shape), _resident(wa.shape), _resident(wb.shape),
                  _full((1, D_MODEL)), _resident(wg.shape), _resident(wu.shape), _resident(wd.shape)],
        out_specs=pl.BlockSpec((tm, D_MODEL), row),
        out_shape=jax.ShapeDtypeStruct((t, D_MODEL), F32),
        compiler_params=_params("parallel"),
        name="mix_ffn",
    )(h2, y, bonus, gate, ys, gn_w, gn_b, seg, wa, wb, g, wg, wu, wd)


def _rope_table_kernel(pos_ref, freq_ref, cos_ref, sin_ref):
    ang = pos_ref[...] * freq_ref[...]
    cos_ref[...] = jnp.cos(ang)
    sin_ref[...] = jnp.sin(ang)


def _rope_tables(positions):
    t = positions.size
    half = ROPE_DIM // 2
    rows = t * half // LANES
    inv_freq = ROPE_THETA ** (-jnp.arange(0, ROPE_DIM, 2, dtype=F32) / ROPE_DIM)
    pos_rep = jnp.repeat(positions.reshape(-1).astype(F32), half).reshape(rows, LANES)
    freq = jnp.tile(inv_freq, LANES // half).reshape(1, LANES)
    tr = _tile(rows, 256)
    cos, sin = pl.pallas_call(
        _rope_table_kernel,
        grid=(rows // tr,),
        in_specs=[pl.BlockSpec((tr, LANES), lambda i: (i, 0)), _full((1, LANES))],
        out_specs=[pl.BlockSpec((tr, LANES), lambda i: (i, 0))] * 2,
        out_shape=[jax.ShapeDtypeStruct((rows, LANES), F32)] * 2,
        compiler_params=_params("parallel"),
        name="rope_tables",
    )(pos_rep, freq)
    return jnp.concatenate([cos.reshape(t, half) - 1.0, sin.reshape(t, half)], axis=1)


def _rope_expansion(width):
    half = ROPE_DIM // 2
    d = jnp.arange(width) % HEAD_DIM
    j = jnp.arange(half)[:, None]
    zero = jnp.zeros((half, width), F32)
    cos_rows = jnp.where((d[None, :] < ROPE_DIM) & (d[None, :] % half == j), 1.0, 0.0)
    sin_a = jnp.where((d[None, :] < half) & (d[None, :] == j), -1.0, 0.0)
    sin_b = jnp.where((d[None, :] >= half) & (d[None, :] < ROPE_DIM) & (d[None, :] - half == j), 1.0, 0.0)
    one = jnp.concatenate([jnp.concatenate([cos_rows, zero, zero], axis=1),
                           jnp.concatenate([zero, sin_a, sin_b], axis=1)], axis=0)
    return jnp.concatenate([one, one], axis=0).astype(BF16)


def _qkv_kernel(h_ref, g_ref, w_ref, b_ref, qn_ref, kn_ref, cs_ref, exp_ref, seg_ref, q_ref, k_ref, v_ref):
    xn = _rms(h_ref[...], g_ref[...]).astype(BF16)
    qkv = jnp.dot(xn, w_ref[...], preferred_element_type=F32) + b_ref[...]
    seg = seg_ref[...]
    wide = seg.shape[0]
    hi, lo = _split(cs_ref[...])
    tab = jnp.dot(jnp.concatenate([hi, lo], axis=1), exp_ref[...], preferred_element_type=F32)
    cos = 1.0 + tab[:, 0:wide]
    sa = tab[:, wide:2 * wide]
    sb = tab[:, 2 * wide:]

    def norm_rope(x, gain):
        ms = _head_sums(x * x, seg) * (1.0 / HEAD_DIM)
        xn_ = x * lax.rsqrt(ms + NORM_EPS) * gain
        return (xn_ * cos + pltpu.roll(xn_, wide - ROPE_DIM // 2, axis=1) * sa
                + pltpu.roll(xn_, ROPE_DIM // 2, axis=1) * sb)

    for c in range(Q_DIM // wide):
        x = qkv[:, c * wide:(c + 1) * wide]
        q_ref[:, c * wide:(c + 1) * wide] = (norm_rope(x, qn_ref[...]) * (HEAD_DIM ** -0.5)).astype(BF16)
    for c in range(KV_DIM // wide):
        x = qkv[:, Q_DIM + c * wide:Q_DIM + (c + 1) * wide]
        k_ref[:, c * wide:(c + 1) * wide] = norm_rope(x, kn_ref[...]).astype(BF16)
    v_ref[...] = qkv[:, Q_DIM + KV_DIM:].astype(BF16)


def _qkv(h2, g, w, b, qn, kn, cs, expand, seg):
    t = h2.shape[0]
    tm = _tile(t, 512)
    row = lambda i: (i, 0)
    return pl.pallas_call(
        _qkv_kernel,
        grid=(t // tm,),
        in_specs=[pl.BlockSpec((tm, D_MODEL), row), _full((1, D_MODEL)), _resident(w.shape), _full(b.shape),
                  _full(qn.shape), _full(kn.shape), pl.BlockSpec((tm, cs.shape[1]), row), _full(expand.shape),
                  _full(seg.shape)],
        out_specs=[pl.BlockSpec((tm, Q_DIM), row), pl.BlockSpec((tm, KV_DIM), row),
                   pl.BlockSpec((tm, KV_DIM), row)],
        out_shape=[jax.ShapeDtypeStruct((t, Q_DIM), BF16), jax.ShapeDtypeStruct((t, KV_DIM), BF16),
                   jax.ShapeDtypeStruct((t, KV_DIM), BF16)],
        compiler_params=_params("parallel"),
        name="qkv",
    )(h2, g, w, b, qn, kn, cs, expand, seg)


def _attn_kernel(h_ref, q_ref, kc_ref, kp_ref, vc_ref, vp_ref, sink_ref, wo_ref, bo_ref, o_ref, *, nqb):
    w = WINDOW
    n = HEAD_DIM
    rows = GQA_GROUP * w
    qi = lax.broadcasted_iota(jnp.int32, (rows, 2 * w), 0) % w
    kj = lax.broadcasted_iota(jnp.int32, (rows, 2 * w), 1)
    diff = w + qi - kj
    band = (diff >= 0) & (diff < w)
    band0 = band & (kj >= jnp.where(pl.program_id(1) == 0, w, 0))
    kall = jnp.concatenate([kp_ref[...], kc_ref[...]], axis=0)
    vall = jnp.concatenate([vp_ref[...], vc_ref[...]], axis=0)
    units = [(qb, g) for qb in range(nqb) for g in range(KV_HEADS)]
    s, e, den = {}, {}, {}
    for u in units:
        qb, g = u
        qs = jnp.concatenate(
            [q_ref[qb * w:(qb + 1) * w, (g * GQA_GROUP + i) * n:(g * GQA_GROUP + i + 1) * n]
             for i in range(GQA_GROUP)], axis=0)
        kk = kall[qb * w:(qb + 2) * w, g * n:(g + 1) * n]
        sc = lax.dot_general(qs, kk, (((1,), (1,)), ((), ())), preferred_element_type=F32)
        s[u] = jnp.where(band0 if qb == 0 else band, sc, -jnp.inf)
    sinks = [jnp.concatenate(
        [jnp.broadcast_to(sink_ref[:, g * GQA_GROUP + i:g * GQA_GROUP + i + 1], (w, LANES))
         for i in range(GQA_GROUP)], axis=0) for g in range(KV_HEADS)]
    ones = jnp.ones((2 * w, LANES - n), BF16)
    for u in units:
        sink = sinks[u[1]]
        m = jnp.maximum(jnp.broadcast_to(jnp.max(s[u], axis=-1, keepdims=True), (rows, LANES)), sink)
        e[u] = jnp.exp(s[u] - jnp.concatenate([m, m], axis=1)).astype(BF16)
        den[u] = jnp.exp(sink - m)
    outs = [[None] * Q_HEADS for _ in range(nqb)]
    for u in units:
        qb, g = u
        vv = jnp.concatenate([vall[qb * w:(qb + 2) * w, g * n:(g + 1) * n], ones], axis=1)
        both = jnp.dot(e[u], vv, preferred_element_type=F32)
        total = pltpu.roll(both, LANES - n, axis=1) + den[u]
        o = both[:, 0:n] / total[:, 0:n]
        for i in range(GQA_GROUP):
            outs[qb][g * GQA_GROUP + i] = o[i * w:(i + 1) * w]
    att = jnp.concatenate([jnp.concatenate(outs[qb], axis=1) for qb in range(nqb)], axis=0)
    o_ref[...] = h_ref[...] + _bdot(att, wo_ref[...]) + bo_ref[...]


def _attn(h2, q, k, v, sinks, wo, bo, bsz, seq):
    t = h2.shape[0]
    nqb = 4 if seq % (4 * WINDOW) == 0 else 1
    tq = nqb * WINDOW
    nb = seq // tq
    cur = lambda b, i: (b * nb + i, 0)
    prv = lambda b, i: (b * nb * nqb + jnp.maximum(i * nqb - 1, 0), 0)
    return pl.pallas_call(
        functools.partial(_attn_kernel, nqb=nqb),
        grid=(bsz, nb),
        in_specs=[pl.BlockSpec((tq, D_MODEL), cur), pl.BlockSpec((tq, Q_DIM), cur),
                  pl.BlockSpec((tq, KV_DIM), cur), pl.BlockSpec((WINDOW, KV_DIM), prv),
                  pl.BlockSpec((tq, KV_DIM), cur), pl.BlockSpec((WINDOW, KV_DIM), prv),
                  _full(sinks.shape), _full(wo.shape), _full(bo.shape)],
        out_specs=pl.BlockSpec((tq, D_MODEL), cur),
        out_shape=jax.ShapeDtypeStruct((t, D_MODEL), F32),
        compiler_params=_params("parallel", "parallel"),
        name="attn",
    )(h2, q, k, k, v, v, sinks, wo, bo)


def _router_kernel(h_ref, g_ref, wr_ref, xn_ref, meta_ref, cnt_ref, base_ref):
    @pl.when(pl.program_id(0) == 0)
    def _():
        base_ref[...] = jnp.zeros_like(base_ref)

    xn = _rms(h_ref[...], g_ref[...])
    xn_ref[...] = xn
    hi, lo = _split(xn)
    whi = wr_ref[0]
    wlo = wr_ref[1]
    logits = (jnp.dot(hi, whi, preferred_element_type=F32) + jnp.dot(hi, wlo, preferred_element_type=F32)
              + jnp.dot(lo, whi, preferred_element_type=F32))
    tm = logits.shape[0]
    lane = lax.broadcasted_iota(jnp.int32, logits.shape, 1)
    logits = jnp.where(lane < N_EXPERTS, logits, -jnp.inf)
    m1 = jnp.max(logits, axis=-1, keepdims=True)
    i1 = jnp.min(jnp.where(logits == m1, lane, LANES), axis=-1, keepdims=True)
    rest = jnp.where(lane == i1, -jnp.inf, logits)
    m2 = jnp.max(rest, axis=-1, keepdims=True)
    i2 = jnp.min(jnp.where(rest == m2, lane, LANES), axis=-1, keepdims=True)
    e2 = jnp.exp(m2 - m1)
    w1 = 1.0 / (1.0 + e2)
    w2 = e2 / (1.0 + e2)

    sel1 = lane == i1
    sel2 = lane == i2
    onehot = jnp.where(sel1, 1.0, 0.0) + jnp.where(sel2, 1.0, 0.0)
    ri = lax.broadcasted_iota(jnp.int32, (tm, tm), 0)
    ci = lax.broadcasted_iota(jnp.int32, (tm, tm), 1)
    before = jnp.dot(jnp.where(ci < ri, 1.0, 0.0).astype(BF16), onehot.astype(BF16),
                     preferred_element_type=F32)
    rank = base_ref[...] + before
    r1 = jnp.sum(jnp.where(sel1, rank, 0.0), axis=-1, keepdims=True)
    r2 = jnp.sum(jnp.where(sel2, rank, 0.0), axis=-1, keepdims=True)
    total = base_ref[...] + jnp.sum(onehot, axis=0, keepdims=True)
    base_ref[...] = total
    cnt_ref[...] = total
    cols = (i1.astype(F32), i2.astype(F32), w1, w2, r1, r2)
    meta = jnp.zeros(logits.shape, F32)
    for c, val in enumerate(cols):
        meta = jnp.where(lane == c, val, meta)
    meta_ref[...] = meta


def _router(h2, g, wr):
    t = h2.shape[0]
    tm = _tile(t, 512)
    row = lambda i: (i, 0)
    return pl.pallas_call(
        _router_kernel,
        grid=(t // tm,),
        in_specs=[pl.BlockSpec((tm, D_MODEL), row), _full((1, D_MODEL)), _full(wr.shape)],
        out_specs=[pl.BlockSpec((tm, D_MODEL), row), pl.BlockSpec((tm, LANES), row), _full((1, LANES))],
        out_shape=[jax.ShapeDtypeStruct((t, D_MODEL), F32), jax.ShapeDtypeStruct((t, LANES), F32),
                   jax.ShapeDtypeStruct((1, LANES), F32)],
        scratch_shapes=[pltpu.VMEM((1, LANES), F32)],
        compiler_params=_params("arbitrary"),
        name="router",
    )(h2, g, wr)


def _moe_index_kernel(p1_ref, p2_ref, pad_ref, src_ref, dst_ref, *, ntok):
    tm = MOE_TM
    nrow = src_ref.shape[0]

    def virtual(j, c):
        dst_ref[j] = nrow + j
        return c

    lax.fori_loop(0, tm, virtual, 0)

    for e in range(N_EXPERTS + 1):
        def fill(p, c, e=e):
            src_ref[p] = 0
            dst_ref[tm + p] = 2 * ntok + (e % N_EXPERTS) * tm + ((p - pad_ref[2 * e]) & (tm - 1))
            return c

        lax.fori_loop(pad_ref[2 * e], pad_ref[2 * e + 1], fill, 0)

    def body(t, c):
        p1 = p1_ref[t]
        p2 = p2_ref[t]
        src_ref[p1] = t
        dst_ref[tm + p1] = t
        src_ref[p2] = t
        dst_ref[tm + p2] = ntok + t
        return c

    lax.fori_loop(0, ntok, body, 0, unroll=4)


def _moe_index(pos1, pos2, pad, rows):
    smem = pl.BlockSpec(memory_space=pltpu.SMEM)
    return pl.pallas_call(
        functools.partial(_moe_index_kernel, ntok=pos1.shape[0]),
        grid_spec=pltpu.PrefetchScalarGridSpec(
            num_scalar_prefetch=3, grid=(1,), in_specs=[], out_specs=[smem, smem]),
        out_shape=[jax.ShapeDtypeStruct((rows,), jnp.int32), jax.ShapeDtypeStruct((rows + MOE_TM,), jnp.int32)],
        compiler_params=_params("arbitrary"),
        name="moe_index",
    )(pos1, pos2, pad)


def _row_copy(src_ref, src_row, dst_ref, dst_row, sem):
    return pltpu.make_async_copy(src_ref.at[pl.ds(src_row, 1)], dst_ref.at[pl.ds(dst_row, 1)], sem)


def _experts_kernel(te_ref, nused_ref, src_ref, dst_ref, x_ref, wg_ref, wu_ref, wd_ref, y_ref,
                    xbuf, ybuf, gsem, ssem):
    del te_ref
    i = pl.program_id(0)
    nused = nused_ref[0]
    tm = MOE_TM

    sub = MOE_SUB
    ngroup = tm // sub

    def gather_group(tile, slot, k):
        for u in range(sub):
            _row_copy(x_ref, src_ref[tile * tm + k * sub + u], xbuf.at[slot, k], u, gsem.at[slot]).start()

    def scatter_group(tile, slot, k):
        for u in range(sub):
            _row_copy(ybuf.at[slot, k], u, y_ref, dst_ref[(tile + 1) * tm + k * sub + u], ssem.at[slot]).start()

    def wait_gather(slot):
        def body(k, c):
            pltpu.make_async_copy(x_ref.at[pl.ds(0, sub)], xbuf.at[slot, k], gsem.at[slot]).wait()
            return c
        lax.fori_loop(0, ngroup, body, 0)

    def wait_scatter(slot):
        def body(k, c):
            pltpu.make_async_copy(ybuf.at[slot, k], y_ref.at[pl.ds(0, sub)], ssem.at[slot]).wait()
            return c
        lax.fori_loop(0, ngroup, body, 0)

    @pl.when(i == 0)
    def _():
        ybuf[1] = jnp.zeros(ybuf.shape[1:], F32)

        def body(k, c):
            gather_group(0, 0, k)
            return c
        lax.fori_loop(0, ngroup, body, 0)

    @pl.when(i < nused)
    def _():
        slot = i % 2
        other = 1 - slot
        nxt = jnp.minimum(i + 1, nused - 1)
        wait_gather(slot)

        @pl.when(i >= 1)
        def _():
            wait_scatter(slot)

        xb = xbuf[slot].reshape(tm, D_MODEL).astype(BF16)
        bounds = list(range(0, FFN_EXPERT, MOE_FCHUNK)) + [FFN_EXPERT]
        nchunk = len(bounds) - 1
        acts = []
        for c in range(nchunk):
            c0, c1 = bounds[c], bounds[c + 1]
            gate = jnp.dot(xb, wg_ref[0, :, c0:c1], preferred_element_type=F32)
            up = jnp.dot(xb, wu_ref[0, :, c0:c1], preferred_element_type=F32)
            acts.append((_silu(gate) * up).astype(BF16))
            for k in range(c * ngroup // nchunk, (c + 1) * ngroup // nchunk):
                gather_group(nxt, other, k)
                scatter_group(i - 1, other, k)
        y = jnp.dot(jnp.concatenate(acts, axis=1), wd_ref[0], preferred_element_type=F32)
        ybuf[slot] = y.reshape(ngroup, sub, D_MODEL)

        @pl.when(i == nused - 1)
        def _():
            def body(k, c):
                scatter_group(i, slot, k)
                return c
            lax.fori_loop(0, ngroup, body, 0)
            wait_scatter(other)
            wait_scatter(slot)
            wait_gather(other)


def _experts(tile_expert, nused, src, dst, xn, wg, wu, wd, out_rows):
    ntile = tile_expert.shape[0]
    tm = MOE_TM
    wsel = lambda i, te, nu, s, d: (te[i], 0, 0)
    any_spec = pl.BlockSpec(memory_space=pl.ANY)
    return pl.pallas_call(
        _experts_kernel,
        grid_spec=pltpu.PrefetchScalarGridSpec(
            num_scalar_prefetch=4, grid=(ntile,),
            in_specs=[any_spec,
                      pl.BlockSpec((1, D_MODEL, FFN_EXPERT), wsel),
                      pl.BlockSpec((1, D_MODEL, FFN_EXPERT), wsel),
                      pl.BlockSpec((1, FFN_EXPERT, D_MODEL), wsel)],
            out_specs=any_spec,
            scratch_shapes=[pltpu.VMEM((2, tm // MOE_SUB, MOE_SUB, D_MODEL), F32),
                            pltpu.VMEM((2, tm // MOE_SUB, MOE_SUB, D_MODEL), F32),
                            pltpu.SemaphoreType.DMA((2,)), pltpu.SemaphoreType.DMA((2,))]),
        out_shape=jax.ShapeDtypeStruct((out_rows, D_MODEL), F32),
        compiler_params=_params("arbitrary"),
        name="moe_experts",
    )(tile_expert, nused, src, dst, xn, wg, wu, wd)


def _combine_kernel(h_ref, meta_ref, y1_ref, y2_ref, o_ref):
    meta = meta_ref[...]
    o_ref[...] = h_ref[...] + meta[:, 2:3] * y1_ref[...] + meta[:, 3:4] * y2_ref[...]


def _combine(h2, meta, ys):
    t = h2.shape[0]
    tm = _tile(t, 512)
    nt = t // tm
    row = lambda i: (i, 0)
    return pl.pallas_call(
        _combine_kernel,
        grid=(nt,),
        in_specs=[pl.BlockSpec((tm, D_MODEL), row), pl.BlockSpec((tm, LANES), row),
                  pl.BlockSpec((tm, D_MODEL), row), pl.BlockSpec((tm, D_MODEL), lambda i: (nt + i, 0))],
        out_specs=pl.BlockSpec((tm, D_MODEL), row),
        out_shape=jax.ShapeDtypeStruct((t, D_MODEL), F32),
        compiler_params=_params("parallel"),
        name="moe_combine",
    )(h2, meta, ys, ys)


def _moe(h2, g, router, wg, wu, wd):
    t = h2.shape[0]
    tm = MOE_TM
    wr = jnp.pad(router, ((0, 0), (0, LANES - N_EXPERTS)))
    wr_hi = wr.astype(BF16)
    wr_lo = (wr - wr_hi.astype(F32)).astype(BF16)
    xn, meta, cnt = _router(h2, g, jnp.stack([wr_hi, wr_lo]))

    cols = meta[:, :8].T.astype(jnp.int32)
    e1, e2, r1, r2 = cols[0], cols[1], cols[4], cols[5]
    count = cnt[0, :N_EXPERTS].astype(jnp.int32)
    ntile = (count + tm - 1) // tm
    tile_end = jnp.cumsum(ntile)
    off = (tile_end - ntile) * tm
    rows = 2 * t + N_EXPERTS * tm
    tile_id = jnp.arange(rows // tm)
    tile_expert = jnp.minimum(
        jnp.sum(tile_id[:, None] >= tile_end[None, :], axis=1), N_EXPERTS - 1).astype(jnp.int32)
    nused = tile_end[-1:].astype(jnp.int32)
    expert = jnp.arange(N_EXPERTS, dtype=jnp.int32)[:, None]
    pos1 = r1 + jnp.sum(jnp.where(e1[None, :] == expert, off[:, None], 0), axis=0)
    pos2 = r2 + jnp.sum(jnp.where(e2[None, :] == expert, off[:, None], 0), axis=0)
    pad = jnp.stack([jnp.append(off + count, tile_end[-1] * tm),
                     jnp.append(off + ntile * tm, rows)], axis=1).reshape(-1).astype(jnp.int32)
    src, dst = _moe_index(pos1, pos2, pad, rows)

    ys = _experts(tile_expert, nused, src, dst, xn, wg, wu, wd, rows + tm)
    return _combine(h2, meta, ys)


def _segment_matrix(width):
    i = jnp.arange(width) // HEAD_DIM
    return (i[:, None] == i[None, :]).astype(BF16)


def _layer0(h2, bsz, seq, norm_mix, w_in, mu_shift, w0, w_decay_up, a0, w_iclr_up, w_gate_up, k_k, k_a, r_k,
            gn_w, gn_b, conv_w, conv_b, dt_bias, a_log, d_skip, ssd_norm, w_out, norm_ffn, ffn_gate, ffn_up,
            ffn_down):
    row = lambda a: a.reshape(1, -1)
    seg = _segment_matrix(SEG_WIDTH)
    wa = w_in[:, :RWKV_COLS].astype(BF16)
    wb = jnp.pad(w_in[:, RWKV_COLS:], ((0, 0), (0, SSD_COLS_PAD - SSD_COLS))).astype(BF16)
    lora = w_decay_up.shape[0]
    wd = jnp.concatenate([w_decay_up, jnp.zeros((LANES - lora, RWKV_DIM), F32)], axis=0).astype(BF16)
    wi = jnp.concatenate([jnp.zeros((LANES - lora, RWKV_DIM), F32), w_iclr_up], axis=0).astype(BF16)
    pb, r, k, v, lw, kkn, b, g, bonus = _in_proj(
        h2, bsz, seq, row(norm_mix), wa, wb, row(mu_shift), row(w0), wd, row(a0), wi, w_gate_up.astype(BF16),
        row(k_k), row(k_a), row(r_k), seg)
    y = _rwkv_chunk(r, k, v, lw, kkn, b, bsz, seq)

    lane_pad = lambda a: jnp.pad(row(a), ((0, 0), (0, LANES - a.size)))
    ys = _ssd(pb, bsz, seq, conv_w, row(conv_b), lane_pad(dt_bias), lane_pad(a_log), lane_pad(d_skip),
              row(ssd_norm))

    w_out = w_out.astype(BF16)
    return _mix_ffn(h2, y, bonus, g, ys, row(gn_w), row(gn_b), seg, w_out[:RWKV_DIM], w_out[RWKV_DIM:],
                    row(norm_ffn), ffn_gate.astype(BF16), ffn_up.astype(BF16), ffn_down.astype(BF16))


def _layer1(h2, bsz, seq, tables, norm_mix, w_qkv, b_qkv, q_norm, k_norm, sinks, w_o, b_o, norm_ffn, router,
            exp_gate, exp_up, exp_down):
    row = lambda a: a.reshape(1, -1)
    wide = lambda a: jnp.tile(a, SEG_WIDTH // HEAD_DIM).reshape(1, SEG_WIDTH)
    q, k, v = _qkv(h2, row(norm_mix), w_qkv.astype(BF16), row(b_qkv), wide(q_norm), wide(k_norm), tables,
                   _rope_expansion(SEG_WIDTH), _segment_matrix(SEG_WIDTH))
    h2 = _attn(h2, q, k, v, row(sinks), w_o.astype(BF16), row(b_o), bsz, seq)

    return _moe(h2, row(norm_ffn), router, exp_gate.astype(BF16), exp_up.astype(BF16), exp_down.astype(BF16))


def kernel(x, positions, ev_norm_mix, ev_w_in, ev_mu_shift, ev_w0, ev_w_decay_up, ev_a0, ev_w_iclr_up, ev_w_gate_up, ev_k_k, ev_k_a, ev_r_k, ev_gn_w, ev_gn_b, ev_conv_w, ev_conv_b, ev_dt_bias, ev_a_log, ev_d_skip, ev_ssd_norm, ev_w_out, ev_norm_ffn, ev_ffn_gate, ev_ffn_up, ev_ffn_down, od_norm_mix, od_w_qkv, od_b_qkv, od_q_norm, od_k_norm, od_sinks, od_w_o, od_b_o, od_norm_ffn, od_router, od_exp_gate, od_exp_up, od_exp_down):
    bsz, seq, d = x.shape
    depth = ev_norm_mix.shape[0] + od_norm_mix.shape[0]
    tables = _rope_tables(positions)
    h2 = x.reshape(bsz * seq, d)
    for layer in range(depth):
        i = layer // 2
        if layer % 2 == 0:
            h2 = _layer0(h2, bsz, seq, ev_norm_mix[i], ev_w_in[i], ev_mu_shift[i], ev_w0[i], ev_w_decay_up[i],
                         ev_a0[i], ev_w_iclr_up[i], ev_w_gate_up[i], ev_k_k[i], ev_k_a[i], ev_r_k[i],
                         ev_gn_w[i], ev_gn_b[i], ev_conv_w[i], ev_conv_b[i], ev_dt_bias[i], ev_a_log[i],
                         ev_d_skip[i], ev_ssd_norm[i], ev_w_out[i], ev_norm_ffn[i], ev_ffn_gate[i],
                         ev_ffn_up[i], ev_ffn_down[i])
        else:
            h2 = _layer1(h2, bsz, seq, tables, od_norm_mix[i], od_w_qkv[i], od_b_qkv[i], od_q_norm[i],
                         od_k_norm[i], od_sinks[i], od_w_o[i], od_b_o[i], od_norm_ffn[i], od_router[i],
                         od_exp_gate[i], od_exp_up[i], od_exp_down[i])
    return h2.reshape(bsz, seq, d)
```

```python
import functools
import math

import jax
import jax.numpy as jnp
from jax import lax
from jax.experimental import pallas as pl
from jax.experimental.pallas import tpu as pltpu

F32 = jnp.float32
BF16 = jnp.bfloat16

D_MODEL = 1024
HEAD_DIM = 64
NORM_EPS = 1e-6

RWKV_HEADS = 8
RWKV_DIM = 512
RWKV_COLS = 1792
RWKV_GN_EPS = 64e-5
RWKV_CHUNK = 64
RWKV_GROUP = 256

SSD_HEADS = 8
SSD_DIM = 512
SSD_GROUPS = 2
SSD_STATE = 128
SSD_CONV = 4
SSD_CHUNK = 128
SSD_XBC = 1024
SSD_COLS = 1544
SSD_COLS_PAD = 1664

Q_HEADS = 16
KV_HEADS = 4
GQA_GROUP = 4
Q_DIM = 1024
KV_DIM = 256
WINDOW = 128
ROPE_THETA = 500000.0
ROPE_DIM = 16

FFN_DENSE = 2816
N_EXPERTS = 8
FFN_EXPERT = 1408
MOE_TM = 256
MOE_SUB = 8
MOE_FCHUNK = 256

LANES = 128
SEG_WIDTH = 256
VMEM_LIMIT_BYTES = 56 * 1024 * 1024


def _params(*sem):
    return pltpu.CompilerParams(dimension_semantics=sem, vmem_limit_bytes=VMEM_LIMIT_BYTES)


def _bdot(a, b):
    return jnp.dot(a.astype(BF16), b.astype(BF16), preferred_element_type=F32)


def _bdot_nt(a, b):
    return lax.dot_general(a.astype(BF16), b.astype(BF16), (((1,), (1,)), ((), ())),
                           preferred_element_type=F32)


def _bdot_tn(a, b):
    return lax.dot_general(a.astype(BF16), b.astype(BF16), (((0,), (0,)), ((), ())),
                           preferred_element_type=F32)


def _split(x):
    hi = x.astype(BF16)
    lo = (x - hi.astype(F32)).astype(BF16)
    return hi, lo


def _dot_exact_lhs(m, x):
    hi, lo = _split(x)
    return (jnp.dot(m, hi, preferred_element_type=F32) + jnp.dot(m, lo, preferred_element_type=F32))


def _head_sums(x, seg):
    w = seg.shape[0]
    xb = x.astype(BF16)
    return jnp.concatenate(
        [jnp.dot(xb[:, c:c + w], seg, preferred_element_type=F32) for c in range(0, x.shape[1], w)], axis=1)


def _sigmoid(x):
    return 1.0 / (1.0 + jnp.exp(-x))


def _silu(x):
    return x * _sigmoid(x)


def _softplus(x):
    return jnp.maximum(x, 0.0) + jnp.log(1.0 + jnp.exp(-jnp.abs(x)))


def _rms(x, g):
    ms = jnp.mean(x * x, axis=-1, keepdims=True)
    return x * lax.rsqrt(ms + NORM_EPS) * g


def _tile(n, pref):
    t = min(n, pref)
    while n % t:
        t //= 2
    return t


def _full(shape):
    nd = len(shape)
    return pl.BlockSpec(shape, lambda *_: (0,) * nd)


def _resident(shape):
    nd = len(shape)
    return pl.BlockSpec(shape, lambda *_: (0,) * nd, pipeline_mode=pl.Buffered(1))


def _in_proj_kernel(x_ref, g_ref, wa_ref, wb_ref, mu_ref, w0_ref, wd_ref, a0_ref, wi_ref, wg_ref, kk_ref, ka_ref,
                    rk_ref, seg_ref, pb_ref, r_ref, k_ref, v_ref, lw_ref, kkn_ref, b_ref, gate_ref, bonus_ref,
                    carry_ref):
    @pl.when(pl.program_id(1) == 0)
    def _():
        carry_ref[...] = jnp.zeros_like(carry_ref)

    xn = _rms(x_ref[...], g_ref[...]).astype(BF16)
    pb_ref[...] = jnp.dot(xn, wb_ref[...], preferred_element_type=F32)
    pa = jnp.dot(xn, wa_ref[...], preferred_element_type=F32)
    tm = pa.shape[0]
    row = lax.broadcasted_iota(jnp.int32, pa.shape, 0)
    prev = jnp.where(row == 0, carry_ref[...], pltpu.roll(pa, 1, axis=0))
    carry_ref[...] = pa[tm - 1:tm, :]
    x = pa + (prev - pa) * mu_ref[...]

    r = x[:, 0:RWKV_DIM]
    k = x[:, RWKV_DIM:2 * RWKV_DIM]
    v = x[:, 2 * RWKV_DIM:3 * RWKV_DIM]
    lora = x[:, 3 * RWKV_DIM:3 * RWKV_DIM + LANES]
    gl = x[:, 3 * RWKV_DIM + LANES:]
    seg = seg_ref[...]

    w_raw = w0_ref[...] + _bdot(jnp.tanh(lora), wd_ref[...])
    lw_ref[...] = (-math.exp(-0.5)) * _sigmoid(w_raw)
    iclr = _sigmoid(a0_ref[...] + _bdot(lora, wi_ref[...]))
    gate_ref[...] = _bdot(_sigmoid(gl), wg_ref[...])

    kk = k * kk_ref[...]
    kkn = kk * lax.rsqrt(_head_sums(kk * kk, seg) + 1e-12)
    k2 = k * (1.0 + (iclr - 1.0) * ka_ref[...])
    r_ref[...] = r
    k_ref[...] = k2
    v_ref[...] = v
    kkn_ref[...] = kkn
    b_ref[...] = kkn * iclr
    bonus_ref[...] = _head_sums(r * k2 * rk_ref[...], seg) * v


def _in_proj(x2, bsz, seq, g, wa, wb, mu, w0, wd, a0, wi, wg, k_k, k_a, r_k, seg):
    t = x2.shape[0]
    tm = _tile(seq, 512)
    nt = seq // tm
    row = lambda b, i: (b * nt + i, 0)
    out = jax.ShapeDtypeStruct((t, RWKV_DIM), F32)
    small = [mu, w0, wd, a0, wi, wg, k_k, k_a, r_k, seg]
    return pl.pallas_call(
        _in_proj_kernel,
        grid=(bsz, nt),
        in_specs=[pl.BlockSpec((tm, D_MODEL), row), _full((1, D_MODEL)), _resident(wa.shape),
                  _resident(wb.shape)] + [_full(a.shape) for a in small],
        out_specs=[pl.BlockSpec((tm, SSD_COLS_PAD), row)] + [pl.BlockSpec((tm, RWKV_DIM), row)] * 8,
        out_shape=[jax.ShapeDtypeStruct((t, SSD_COLS_PAD), F32)] + [out] * 8,
        scratch_shapes=[pltpu.VMEM((1, RWKV_COLS), F32)],
        compiler_params=_params("arbitrary", "arbitrary"),
        name="in_proj",
    )(x2, g, wa, wb, *small)


def _rwkv_chunk_kernel(r_ref, k_ref, v_ref, lw_ref, kk_ref, b_ref, y_ref, z_ref, *, lt):
    c_len = RWKV_CHUNK
    n = HEAD_DIM
    gl = RWKV_GROUP
    nchunk = gl // c_len
    nheads = LANES // n
    shift = c_len.bit_length() - 1

    @pl.when(pl.program_id(2) == 0)
    def _():
        z_ref[...] = jnp.zeros_like(z_ref)

    ri = lax.broadcasted_iota(jnp.int32, (gl, gl), 0)
    ci = lax.broadcasted_iota(jnp.int32, (gl, gl), 1)
    tri_bd = jnp.where((ci <= ri) & (ci >= ((ri >> shift) << shift)), 1.0, 0.0).astype(BF16)
    re_ = lax.broadcasted_iota(jnp.int32, (c_len, c_len), 0)
    ce_ = lax.broadcasted_iota(jnp.int32, (c_len, c_len), 1)
    eye = re_ == ce_
    ri2 = lax.broadcasted_iota(jnp.int32, (2 * gl, gl), 0)
    ci2 = lax.broadcasted_iota(jnp.int32, (2 * gl, gl), 1)
    t2 = jnp.where(ri2 < gl, ri2, ri2 - gl)
    mask2 = (ci2 <= jnp.where(ri2 < gl, t2 - 1, t2)) & (ci2 >= ((t2 >> shift) << shift))
    zeros = jnp.zeros((c_len, n), F32)

    nsub = lt // gl
    sysid = [(s, h) for s in range(nsub) for h in range(nheads)]
    pre = []
    for s in range(nsub):
        sl = pl.ds(s * gl, gl)
        lw = lw_ref[sl, :]
        g_in = _dot_exact_lhs(tri_bd, lw)
        ends = [g_in[(c + 1) * c_len - 1:(c + 1) * c_len, :] for c in range(nchunk)]
        g_end = jnp.concatenate([jnp.broadcast_to(e, (c_len, LANES)) for e in ends], axis=0)
        e_end = jnp.exp(g_end - g_in)
        en = jnp.exp(-g_in)
        k = k_ref[sl, :]
        b = b_ref[sl, :]
        pre.append(dict(
            ends=ends, v=v_ref[sl, :], rt=r_ref[sl, :] * jnp.exp(g_in),
            at=-kk_ref[sl, :] * jnp.exp(g_in - lw), kt=k * en, bt=b * en, bend=b * e_end, kend=k * e_end))

    def hs(name, s, h):
        return pre[s][name][:, h * n:(h + 1) * n]

    xb, xk, xkv, p, x = {}, {}, {}, {}, {}
    for q in sysid:
        la = jnp.concatenate([hs("at", *q), hs("rt", *q)], axis=0)
        xb[q] = jnp.where(mask2, _bdot_nt(la, hs("bt", *q)), 0.0)
        xk[q] = jnp.where(mask2, _bdot_nt(la, hs("kt", *q)), 0.0)
    for q in sysid:
        xkv[q] = _bdot(xk[q], hs("v", *q))
        p[q] = xb[q][0:gl]
        x[q] = jnp.concatenate([hs("at", *q), xkv[q][0:gl]], axis=1)
    for i in range(6):
        for q in sysid:
            x[q] = x[q] + _bdot(p[q], x[q])
        if i < 5:
            for q in sysid:
                p[q] = _bdot(p[q], p[q])
    rq, y0, mn = {}, {}, {}
    for q in sysid:
        yy = _bdot(xb[q][gl:], x[q])
        rq[q] = yy[:, 0:n] + hs("rt", *q)
        y0[q] = yy[:, n:] + xkv[q][gl:]
        bend_h, kend_h, v_h = hs("bend", *q), hs("kend", *q), hs("v", *q)
        for c in range(nchunk):
            rows = slice(c * c_len, (c + 1) * c_len)
            lhs = jnp.concatenate([bend_h[rows], kend_h[rows]], axis=0)
            rhs = jnp.concatenate([x[q][rows], jnp.concatenate([zeros, v_h[rows]], axis=1)], axis=0)
            mn[q + (c,)] = _bdot_tn(lhs, rhs)
    zero_n = jnp.zeros((n, n), F32)
    comp = {q: [] for q in sysid}
    for c in range(nchunk):
        for q in sysid:
            s, h = q
            pend = jnp.exp(pre[s]["ends"][c][:, h * n:(h + 1) * n])
            m_mat = mn[q + (c,)][:, 0:n] + jnp.where(eye, pend, 0.0)
            n_mat = mn[q + (c,)][:, n:]
            if c == 0:
                comp[q].append(jnp.concatenate([m_mat, n_mat], axis=1))
            else:
                comp[q].append(_bdot(m_mat, comp[q][c - 1]) + jnp.concatenate([zero_n, n_mat], axis=1))
    zs = [z_ref[h] for h in range(nheads)]
    for s in range(nsub):
        yh = []
        for h in range(nheads):
            q = (s, h)
            pm = jnp.concatenate([comp[q][c][:, 0:n] for c in range(nchunk)], axis=0)
            pn = jnp.concatenate([comp[q][c][:, n:] for c in range(nchunk)], axis=0)
            after = _bdot(pm, zs[h]) + pn
            z_in = [zs[h]] + [after[c * c_len:(c + 1) * c_len] for c in range(nchunk - 1)]
            yh.append(jnp.concatenate(
                [_bdot(rq[q][c * c_len:(c + 1) * c_len], z_in[c]) + y0[q][c * c_len:(c + 1) * c_len]
                 for c in range(nchunk)], axis=0))
            zs[h] = after[(nchunk - 1) * c_len:]
        y_ref[pl.ds(s * gl, gl), :] = jnp.concatenate(yh, axis=1)
    for h in range(nheads):
        z_ref[h] = zs[h]


def _rwkv_chunk(r, k, v, lw, kkn, b, bsz, seq):
    t = r.shape[0]
    lt = _tile(seq, 1024)
    nt = seq // lt
    spec = pl.BlockSpec((lt, LANES), lambda bi, hp, i: (bi * nt + i, hp))
    return pl.pallas_call(
        functools.partial(_rwkv_chunk_kernel, lt=lt),
        grid=(bsz, RWKV_DIM // LANES, nt),
        in_specs=[spec] * 6,
        out_specs=spec,
        out_shape=jax.ShapeDtypeStruct((t, RWKV_DIM), F32),
        scratch_shapes=[pltpu.VMEM((2, HEAD_DIM, HEAD_DIM), F32)],
        compiler_params=_params("arbitrary", "arbitrary", "arbitrary"),
        name="rwkv_chunk",
    )(r, k, v, lw, kkn, b)


def _ssd_kernel(pb_ref, cw_ref, cb_ref, dtb_ref, alog_ref, dskip_ref, nrm_ref, ys_ref, ext_ref, st_ref, *, nck):
    q = SSD_CHUNK

    @pl.when(pl.program_id(1) == 0)
    def _():
        ext_ref[0:8, :] = jnp.zeros((8, SSD_XBC), F32)
        st_ref[...] = jnp.zeros_like(st_ref)

    ext_ref[8:8 + nck * q, :] = pb_ref[:, SSD_DIM:SSD_DIM + SSD_XBC]
    for c in range(nck):
        _ssd_chunk(pb_ref, cw_ref, cb_ref, dtb_ref, alog_ref, dskip_ref, nrm_ref, ys_ref, ext_ref, st_ref, c * q)
    ext_ref[0:8, :] = ext_ref[nck * q:nck * q + 8, :]


def _ssd_chunk(pb_ref, cw_ref, cb_ref, dtb_ref, alog_ref, dskip_ref, nrm_ref, ys_ref, ext_ref, st_ref, r0):
    q = SSD_CHUNK
    p = HEAD_DIM
    hpg = SSD_HEADS // SSD_GROUPS
    z = pb_ref[r0:r0 + q, 0:SSD_DIM]
    u = ext_ref[8 + r0:8 + r0 + q, :]
    dt_raw = pb_ref[r0:r0 + q, SSD_DIM + SSD_XBC:]

    conv = cb_ref[...] + cw_ref[SSD_CONV - 1:SSD_CONV, :] * u
    for j in range(SSD_CONV - 1):
        off = r0 + 8 - (SSD_CONV - 1) + j
        conv = conv + cw_ref[j:j + 1, :] * ext_ref[off:off + q, :]
    xbc = _silu(conv)
    xs = xbc[:, 0:SSD_DIM]
    bm = xbc[:, SSD_DIM:SSD_DIM + SSD_GROUPS * SSD_STATE]
    cm = xbc[:, SSD_DIM + SSD_GROUPS * SSD_STATE:]

    dt = _softplus(dt_raw + dtb_ref[...])
    a = -jnp.exp(alog_ref[...])
    ri = lax.broadcasted_iota(jnp.int32, (q, q), 0)
    ci = lax.broadcasted_iota(jnp.int32, (q, q), 1)
    causal = ri >= ci
    cum = _dot_exact_lhs(causal.astype(BF16), dt * a)
    cum_t = cum.T
    dt_t = dt.T
    cum_end = cum[q - 1:q, :]
    to_end = jnp.exp(cum_end - cum) * dt
    ecum = jnp.exp(cum)
    edec = jnp.exp(cum_end)

    ys = []
    for g in range(SSD_GROUPS):
        bm_g = bm[:, g * SSD_STATE:(g + 1) * SSD_STATE]
        cm_g = cm[:, g * SSD_STATE:(g + 1) * SSD_STATE]
        cb = _bdot_nt(cm_g, bm_g)
        bm_t = bm_g.T
        for hh in range(hpg):
            h = g * hpg + hh
            x_h = xs[:, h * p:(h + 1) * p]
            seg = cum[:, h:h + 1] - cum_t[h:h + 1, :]
            ldec = jnp.exp(jnp.where(causal, seg, -jnp.inf))
            wts = cb * ldec * dt_t[h:h + 1, :]
            y = _bdot(wts, x_h)
            h_prev = st_ref[h]
            y = y + _bdot(cm_g, h_prev) * ecum[:, h:h + 1]
            st_ref[h] = h_prev * edec[:, h:h + 1] + _bdot(bm_t, x_h * to_end[:, h:h + 1])
            ys.append(y + dskip_ref[:, h:h + 1] * x_h)
    yall = jnp.concatenate(ys, axis=1) * _silu(z)
    gw = SSD_DIM // SSD_GROUPS
    outs = []
    for g in range(SSD_GROUPS):
        yg = yall[:, g * gw:(g + 1) * gw]
        ms = jnp.mean(yg * yg, axis=-1, keepdims=True)
        outs.append(yg * lax.rsqrt(ms + NORM_EPS) * nrm_ref[:, g * gw:(g + 1) * gw])
    ys_ref[r0:r0 + q, :] = jnp.concatenate(outs, axis=1)


def _ssd(pb, bsz, seq, conv_w, conv_b, dt_bias, a_log, d_skip, ssd_norm):
    t = pb.shape[0]
    nck = 2 if seq % (2 * SSD_CHUNK) == 0 else 1
    rows = nck * SSD_CHUNK
    nc = seq // rows
    row = lambda b, i: (b * nc + i, 0)
    return pl.pallas_call(
        functools.partial(_ssd_kernel, nck=nck),
        grid=(bsz, nc),
        in_specs=[pl.BlockSpec((rows, SSD_COLS_PAD), row), _full(conv_w.shape), _full(conv_b.shape),
                  _full(dt_bias.shape), _full(a_log.shape), _full(d_skip.shape), _full(ssd_norm.shape)],
        out_specs=pl.BlockSpec((rows, SSD_DIM), row),
        out_shape=jax.ShapeDtypeStruct((t, SSD_DIM), F32),
        scratch_shapes=[pltpu.VMEM((8 + rows, SSD_XBC), F32),
                        pltpu.VMEM((SSD_HEADS, SSD_STATE, HEAD_DIM), F32)],
        compiler_params=_params("arbitrary", "arbitrary"),
        name="ssd",
    )(pb, conv_w, conv_b, dt_bias, a_log, d_skip, ssd_norm)


def _mix_ffn_kernel(h_ref, y_ref, bonus_ref, gate_ref, ys_ref, gnw_ref, gnb_ref, seg_ref, wa_ref, wb_ref,
                    g_ref, wg_ref, wu_ref, wd_ref, o_ref):
    y = y_ref[...]
    seg = seg_ref[...]
    mu = _head_sums(y, seg) * (1.0 / HEAD_DIM)
    yc = y - mu
    var = _head_sums(yc * yc, seg) * (1.0 / HEAD_DIM)
    yn = yc * lax.rsqrt(var + RWKV_GN_EPS) * gnw_ref[...] + gnb_ref[...]
    ya = (yn + bonus_ref[...]) * gate_ref[...]
    h = h_ref[...] + _bdot(ya, wa_ref[...]) + _bdot(ys_ref[...], wb_ref[...])

    xn = _rms(h, g_ref[...]).astype(BF16)
    act = _silu(jnp.dot(xn, wg_ref[...], preferred_element_type=F32)) * jnp.dot(
        xn, wu_ref[...], preferred_element_type=F32)
    o_ref[...] = h + jnp.dot(act.astype(BF16), wd_ref[...], preferred_element_type=F32)


def _mix_ffn(h2, y, bonus, gate, ys, gn_w, gn_b, seg, wa, wb, g, wg, wu, wd):
    t = h2.shape[0]
    tm = _tile(t, 512)
    row = lambda i: (i, 0)
    half = pl.BlockSpec((tm, RWKV_DIM), row)
    return pl.pallas_call(
        _mix_ffn_kernel,
        grid=(t // tm,),
        in_specs=[pl.BlockSpec((tm, D_MODEL), row), half, half, half, half, _full(gn_w.shape),
                  _full(gn_b.shape), _resident(seg.shape), _resident(wa.shape), _resident(wb.shape),
                  _full((1, D_MODEL)), _resident(wg.shape), _resident(wu.shape), _resident(wd.shape)],
        out_specs=pl.BlockSpec((tm, D_MODEL), row),
        out_shape=jax.ShapeDtypeStruct((t, D_MODEL), F32),
        compiler_params=_params("parallel"),
        name="mix_ffn",
    )(h2, y, bonus, gate, ys, gn_w, gn_b, seg, wa, wb, g, wg, wu, wd)


def _rope_table_kernel(pos_ref, freq_ref, cos_ref, sin_ref):
    ang = pos_ref[...] * freq_ref[...]
    cos_ref[...] = jnp.cos(ang)
    sin_ref[...] = jnp.sin(ang)


def _rope_tables(positions):
    t = positions.size
    half = ROPE_DIM // 2
    rows = t * half // LANES
    inv_freq = ROPE_THETA ** (-jnp.arange(0, ROPE_DIM, 2, dtype=F32) / ROPE_DIM)
    pos_rep = jnp.repeat(positions.reshape(-1).astype(F32), half).reshape(rows, LANES)
    freq = jnp.tile(inv_freq, LANES // half).reshape(1, LANES)
    tr = _tile(rows, 256)
    cos, sin = pl.pallas_call(
        _rope_table_kernel,
        grid=(rows // tr,),
        in_specs=[pl.BlockSpec((tr, LANES), lambda i: (i, 0)), _full((1, LANES))],
        out_specs=[pl.BlockSpec((tr, LANES), lambda i: (i, 0))] * 2,
        out_shape=[jax.ShapeDtypeStruct((rows, LANES), F32)] * 2,
        compiler_params=_params("parallel"),
        name="rope_tables",
    )(pos_rep, freq)
    return jnp.concatenate([cos.reshape(t, half) - 1.0, sin.reshape(t, half)], axis=1)


def _rope_expansion(width):
    half = ROPE_DIM // 2
    d = jnp.arange(width) % HEAD_DIM
    j = jnp.arange(half)[:, None]
    zero = jnp.zeros((half, width), F32)
    cos_rows = jnp.where((d[None, :] < ROPE_DIM) & (d[None, :] % half == j), 1.0, 0.0)
    sin_a = jnp.where((d[None, :] < half) & (d[None, :] == j), -1.0, 0.0)
    sin_b = jnp.where((d[None, :] >= half) & (d[None, :] < ROPE_DIM) & (d[None, :] - half == j), 1.0, 0.0)
    one = jnp.concatenate([jnp.concatenate([cos_rows, zero, zero], axis=1),
                           jnp.concatenate([zero, sin_a, sin_b], axis=1)], axis=0)
    return jnp.concatenate([one, one], axis=0).astype(BF16)


def _qkv_kernel(h_ref, g_ref, w_ref, b_ref, qn_ref, kn_ref, cs_ref, exp_ref, seg_ref, q_ref, k_ref, v_ref):
    xn = _rms(h_ref[...], g_ref[...]).astype(BF16)
    qkv = jnp.dot(xn, w_ref[...], preferred_element_type=F32) + b_ref[...]
    seg = seg_ref[...]
    wide = seg.shape[0]
    hi, lo = _split(cs_ref[...])
    tab = jnp.dot(jnp.concatenate([hi, lo], axis=1), exp_ref[...], preferred_element_type=F32)
    cos = 1.0 + tab[:, 0:wide]
    sa = tab[:, wide:2 * wide]
    sb = tab[:, 2 * wide:]

    def norm_rope(x, gain):
        ms = _head_sums(x * x, seg) * (1.0 / HEAD_DIM)
        xn_ = x * lax.rsqrt(ms + NORM_EPS) * gain
        return (xn_ * cos + pltpu.roll(xn_, wide - ROPE_DIM // 2, axis=1) * sa
                + pltpu.roll(xn_, ROPE_DIM // 2, axis=1) * sb)

    for c in range(Q_DIM // wide):
        x = qkv[:, c * wide:(c + 1) * wide]
        q_ref[:, c * wide:(c + 1) * wide] = (norm_rope(x, qn_ref[...]) * (HEAD_DIM ** -0.5)).astype(BF16)
    for c in range(KV_DIM // wide):
        x = qkv[:, Q_DIM + c * wide:Q_DIM + (c + 1) * wide]
        k_ref[:, c * wide:(c + 1) * wide] = norm_rope(x, kn_ref[...]).astype(BF16)
    v_ref[...] = qkv[:, Q_DIM + KV_DIM:].astype(BF16)


def _qkv(h2, g, w, b, qn, kn, cs, expand, seg):
    t = h2.shape[0]
    tm = _tile(t, 512)
    row = lambda i: (i, 0)
    return pl.pallas_call(
        _qkv_kernel,
        grid=(t // tm,),
        in_specs=[pl.BlockSpec((tm, D_MODEL), row), _full((1, D_MODEL)), _resident(w.shape), _full(b.shape),
                  _full(qn.shape), _full(kn.shape), pl.BlockSpec((tm, cs.shape[1]), row), _full(expand.shape),
                  _full(seg.shape)],
        out_specs=[pl.BlockSpec((tm, Q_DIM), row), pl.BlockSpec((tm, KV_DIM), row),
                   pl.BlockSpec((tm, KV_DIM), row)],
        out_shape=[jax.ShapeDtypeStruct((t, Q_DIM), BF16), jax.ShapeDtypeStruct((t, KV_DIM), BF16),
                   jax.ShapeDtypeStruct((t, KV_DIM), BF16)],
        compiler_params=_params("parallel"),
        name="qkv",
    )(h2, g, w, b, qn, kn, cs, expand, seg)


def _attn_kernel(h_ref, q_ref, kc_ref, kp_ref, vc_ref, vp_ref, sink_ref, wo_ref, bo_ref, o_ref, *, nqb):
    w = WINDOW
    n = HEAD_DIM
    rows = GQA_GROUP * w
    qi = lax.broadcasted_iota(jnp.int32, (rows, 2 * w), 0) % w
    kj = lax.broadcasted_iota(jnp.int32, (rows, 2 * w), 1)
    diff = w + qi - kj
    band = (diff >= 0) & (diff < w)
    band0 = band & (kj >= jnp.where(pl.program_id(1) == 0, w, 0))
    kall = jnp.concatenate([kp_ref[...], kc_ref[...]], axis=0)
    vall = jnp.concatenate([vp_ref[...], vc_ref[...]], axis=0)
    units = [(qb, g) for qb in range(nqb) for g in range(KV_HEADS)]
    s, e, den = {}, {}, {}
    for u in units:
        qb, g = u
        qs = jnp.concatenate(
            [q_ref[qb * w:(qb + 1) * w, (g * GQA_GROUP + i) * n:(g * GQA_GROUP + i + 1) * n]
             for i in range(GQA_GROUP)], axis=0)
        kk = kall[qb * w:(qb + 2) * w, g * n:(g + 1) * n]
        sc = lax.dot_general(qs, kk, (((1,), (1,)), ((), ())), preferred_element_type=F32)
        s[u] = jnp.where(band0 if qb == 0 else band, sc, -jnp.inf)
    sinks = [jnp.concatenate(
        [jnp.broadcast_to(sink_ref[:, g * GQA_GROUP + i:g * GQA_GROUP + i + 1], (w, LANES))
         for i in range(GQA_GROUP)], axis=0) for g in range(KV_HEADS)]
    ones = jnp.ones((2 * w, LANES - n), BF16)
    for u in units:
        sink = sinks[u[1]]
        m = jnp.maximum(jnp.broadcast_to(jnp.max(s[u], axis=-1, keepdims=True), (rows, LANES)), sink)
        e[u] = jnp.exp(s[u] - jnp.concatenate([m, m], axis=1)).astype(BF16)
        den[u] = jnp.exp(sink - m)
    outs = [[None] * Q_HEADS for _ in range(nqb)]
    for u in units:
        qb, g = u
        vv = jnp.concatenate([vall[qb * w:(qb + 2) * w, g * n:(g + 1) * n], ones], axis=1)
        both = jnp.dot(e[u], vv, preferred_element_type=F32)
        total = pltpu.roll(both, LANES - n, axis=1) + den[u]
        o = both[:, 0:n] / total[:, 0:n]
        for i in range(GQA_GROUP):
            outs[qb][g * GQA_GROUP + i] = o[i * w:(i + 1) * w]
    att = jnp.concatenate([jnp.concatenate(outs[qb], axis=1) for qb in range(nqb)], axis=0)
    o_ref[...] = h_ref[...] + _bdot(att, wo_ref[...]) + bo_ref[...]


def _attn(h2, q, k, v, sinks, wo, bo, bsz, seq):
    t = h2.shape[0]
    nqb = 4 if seq % (4 * WINDOW) == 0 else 1
    tq = nqb * WINDOW
    nb = seq // tq
    cur = lambda b, i: (b * nb + i, 0)
    prv = lambda b, i: (b * nb * nqb + jnp.maximum(i * nqb - 1, 0), 0)
    return pl.pallas_call(
        functools.partial(_attn_kernel, nqb=nqb),
        grid=(bsz, nb),
        in_specs=[pl.BlockSpec((tq, D_MODEL), cur), pl.BlockSpec((tq, Q_DIM), cur),
                  pl.BlockSpec((tq, KV_DIM), cur), pl.BlockSpec((WINDOW, KV_DIM), prv),
                  pl.BlockSpec((tq, KV_DIM), cur), pl.BlockSpec((WINDOW, KV_DIM), prv),
                  _full(sinks.shape), _full(wo.shape), _full(bo.shape)],
        out_specs=pl.BlockSpec((tq, D_MODEL), cur),
        out_shape=jax.ShapeDtypeStruct((t, D_MODEL), F32),
        compiler_params=_params("parallel", "parallel"),
        name="attn",
    )(h2, q, k, k, v, v, sinks, wo, bo)


def _router_kernel(h_ref, g_ref, wr_ref, xn_ref, meta_ref, cnt_ref, base_ref):
    @pl.when(pl.program_id(0) == 0)
    def _():
        base_ref[...] = jnp.zeros_like(base_ref)

    xn = _rms(h_ref[...], g_ref[...])
    xn_ref[...] = xn
    hi, lo = _split(xn)
    whi = wr_ref[0]
    wlo = wr_ref[1]
    logits = (jnp.dot(hi, whi, preferred_element_type=F32) + jnp.dot(hi, wlo, preferred_element_type=F32)
              + jnp.dot(lo, whi, preferred_element_type=F32))
    tm = logits.shape[0]
    lane = lax.broadcasted_iota(jnp.int32, logits.shape, 1)
    logits = jnp.where(lane < N_EXPERTS, logits, -jnp.inf)
    m1 = jnp.max(logits, axis=-1, keepdims=True)
    i1 = jnp.min(jnp.where(logits == m1, lane, LANES), axis=-1, keepdims=True)
    rest = jnp.where(lane == i1, -jnp.inf, logits)
    m2 = jnp.max(rest, axis=-1, keepdims=True)
    i2 = jnp.min(jnp.where(rest == m2, lane, LANES), axis=-1, keepdims=True)
    e2 = jnp.exp(m2 - m1)
    w1 = 1.0 / (1.0 + e2)
    w2 = e2 / (1.0 + e2)

    sel1 = lane == i1
    sel2 = lane == i2
    onehot = jnp.where(sel1, 1.0, 0.0) + jnp.where(sel2, 1.0, 0.0)
    ri = lax.broadcasted_iota(jnp.int32, (tm, tm), 0)
    ci = lax.broadcasted_iota(jnp.int32, (tm, tm), 1)
    before = jnp.dot(jnp.where(ci < ri, 1.0, 0.0).astype(BF16), onehot.astype(BF16),
                     preferred_element_type=F32)
    rank = base_ref[...] + before
    r1 = jnp.sum(jnp.where(sel1, rank, 0.0), axis=-1, keepdims=True)
    r2 = jnp.sum(jnp.where(sel2, rank, 0.0), axis=-1, keepdims=True)
    total = base_ref[...] + jnp.sum(onehot, axis=0, keepdims=True)
    base_ref[...] = total
    cnt_ref[...] = total
    cols = (i1.astype(F32), i2.astype(F32), w1, w2, r1, r2)
    meta = jnp.zeros(logits.shape, F32)
    for c, val in enumerate(cols):
        meta = jnp.where(lane == c, val, meta)
    meta_ref[...] = meta


def _router(h2, g, wr):
    t = h2.shape[0]
    tm = _tile(t, 512)
    row = lambda i: (i, 0)
    return pl.pallas_call(
        _router_kernel,
        grid=(t // tm,),
        in_specs=[pl.BlockSpec((tm, D_MODEL), row), _full((1, D_MODEL)), _full(wr.shape)],
        out_specs=[pl.BlockSpec((tm, D_MODEL), row), pl.BlockSpec((tm, LANES), row), _full((1, LANES))],
        out_shape=[jax.ShapeDtypeStruct((t, D_MODEL), F32), jax.ShapeDtypeStruct((t, LANES), F32),
                   jax.ShapeDtypeStruct((1, LANES), F32)],
        scratch_shapes=[pltpu.VMEM((1, LANES), F32)],
        compiler_params=_params("arbitrary"),
        name="router",
    )(h2, g, wr)


def _moe_index_kernel(p1_ref, p2_ref, pad_ref, src_ref, dst_ref, *, ntok):
    tm = MOE_TM
    nrow = src_ref.shape[0]

    def virtual(j, c):
        dst_ref[j] = nrow + j
        return c

    lax.fori_loop(0, tm, virtual, 0)

    for e in range(N_EXPERTS + 1):
        def fill(p, c, e=e):
            src_ref[p] = 0
            dst_ref[tm + p] = 2 * ntok + (e % N_EXPERTS) * tm + ((p - pad_ref[2 * e]) & (tm - 1))
            return c

        lax.fori_loop(pad_ref[2 * e], pad_ref[2 * e + 1], fill, 0)

    def body(t, c):
        p1 = p1_ref[t]
        p2 = p2_ref[t]
        src_ref[p1] = t
        dst_ref[tm + p1] = t
        src_ref[p2] = t
        dst_ref[tm + p2] = ntok + t
        return c

    lax.fori_loop(0, ntok, body, 0, unroll=4)


def _moe_index(pos1, pos2, pad, rows):
    smem = pl.BlockSpec(memory_space=pltpu.SMEM)
    return pl.pallas_call(
        functools.partial(_moe_index_kernel, ntok=pos1.shape[0]),
        grid_spec=pltpu.PrefetchScalarGridSpec(
            num_scalar_prefetch=3, grid=(1,), in_specs=[], out_specs=[smem, smem]),
        out_shape=[jax.ShapeDtypeStruct((rows,), jnp.int32), jax.ShapeDtypeStruct((rows + MOE_TM,), jnp.int32)],
        compiler_params=_params("arbitrary"),
        name="moe_index",
    )(pos1, pos2, pad)


def _row_copy(src_ref, src_row, dst_ref, dst_row, sem):
    return pltpu.make_async_copy(src_ref.at[pl.ds(src_row, 1)], dst_ref.at[pl.ds(dst_row, 1)], sem)


def _experts_kernel(te_ref, nused_ref, src_ref, dst_ref, x_ref, wg_ref, wu_ref, wd_ref, y_ref,
                    xbuf, ybuf, gsem, ssem):
    del te_ref
    i = pl.program_id(0)
    nused = nused_ref[0]
    tm = MOE_TM

    sub = MOE_SUB
    ngroup = tm // sub

    def gather_group(tile, slot, k):
        for u in range(sub):
            _row_copy(x_ref, src_ref[tile * tm + k * sub + u], xbuf.at[slot, k], u, gsem.at[slot]).start()

    def scatter_group(tile, slot, k):
        for u in range(sub):
            _row_copy(ybuf.at[slot, k], u, y_ref, dst_ref[(tile + 1) * tm + k * sub + u], ssem.at[slot]).start()

    def wait_gather(slot):
        def body(k, c):
            pltpu.make_async_copy(x_ref.at[pl.ds(0, sub)], xbuf.at[slot, k], gsem.at[slot]).wait()
            return c
        lax.fori_loop(0, ngroup, body, 0)

    def wait_scatter(slot):
        def body(k, c):
            pltpu.make_async_copy(ybuf.at[slot, k], y_ref.at[pl.ds(0, sub)], ssem.at[slot]).wait()
            return c
        lax.fori_loop(0, ngroup, body, 0)

    @pl.when(i == 0)
    def _():
        ybuf[1] = jnp.zeros(ybuf.shape[1:], F32)

        def body(k, c):
            gather_group(0, 0, k)
            return c
        lax.fori_loop(0, ngroup, body, 0)

    @pl.when(i < nused)
    def _():
        slot = i % 2
        other = 1 - slot
        nxt = jnp.minimum(i + 1, nused - 1)
        wait_gather(slot)

        @pl.when(i >= 1)
        def _():
            wait_scatter(slot)

        xb = xbuf[slot].reshape(tm, D_MODEL).astype(BF16)
        bounds = list(range(0, FFN_EXPERT, MOE_FCHUNK)) + [FFN_EXPERT]
        nchunk = len(bounds) - 1
        acts = []
        for c in range(nchunk):
            c0, c1 = bounds[c], bounds[c + 1]
            gate = jnp.dot(xb, wg_ref[0, :, c0:c1].astype(BF16), preferred_element_type=F32)
            up = jnp.dot(xb, wu_ref[0, :, c0:c1].astype(BF16), preferred_element_type=F32)
            acts.append((_silu(gate) * up).astype(BF16))
            for k in range(c * ngroup // nchunk, (c + 1) * ngroup // nchunk):
                gather_group(nxt, other, k)
                scatter_group(i - 1, other, k)
        y = jnp.dot(jnp.concatenate(acts, axis=1), wd_ref[0].astype(BF16), preferred_element_type=F32)
        ybuf[slot] = y.reshape(ngroup, sub, D_MODEL)

        @pl.when(i == nused - 1)
        def _():
            def body(k, c):
                scatter_group(i, slot, k)
                return c
            lax.fori_loop(0, ngroup, body, 0)
            wait_scatter(other)
            wait_scatter(slot)
            wait_gather(other)


def _experts(tile_expert, nused, src, dst, xn, wg, wu, wd, out_rows):
    ntile = tile_expert.shape[0]
    tm = MOE_TM
    wsel = lambda i, te, nu, s, d: (te[i], 0, 0)
    any_spec = pl.BlockSpec(memory_space=pl.ANY)
    return pl.pallas_call(
        _experts_kernel,
        grid_spec=pltpu.PrefetchScalarGridSpec(
            num_scalar_prefetch=4, grid=(ntile,),
            in_specs=[any_spec,
                      pl.BlockSpec((1, D_MODEL, FFN_EXPERT), wsel),
                      pl.BlockSpec((1, D_MODEL, FFN_EXPERT), wsel),
                      pl.BlockSpec((1, FFN_EXPERT, D_MODEL), wsel)],
            out_specs=any_spec,
            scratch_shapes=[pltpu.VMEM((2, tm // MOE_SUB, MOE_SUB, D_MODEL), F32),
                            pltpu.VMEM((2, tm // MOE_SUB, MOE_SUB, D_MODEL), F32),
                            pltpu.SemaphoreType.DMA((2,)), pltpu.SemaphoreType.DMA((2,))]),
        out_shape=jax.ShapeDtypeStruct((out_rows, D_MODEL), F32),
        compiler_params=_params("arbitrary"),
        name="moe_experts",
    )(tile_expert, nused, src, dst, xn, wg, wu, wd)


def _combine_kernel(h_ref, meta_ref, y1_ref, y2_ref, o_ref):
    meta = meta_ref[...]
    o_ref[...] = h_ref[...] + meta[:, 2:3] * y1_ref[...] + meta[:, 3:4] * y2_ref[...]


def _combine(h2, meta, ys):
    t = h2.shape[0]
    tm = _tile(t, 512)
    nt = t // tm
    row = lambda i: (i, 0)
    return pl.pallas_call(
        _combine_kernel,
        grid=(nt,),
        in_specs=[pl.BlockSpec((tm, D_MODEL), row), pl.BlockSpec((tm, LANES), row),
                  pl.BlockSpec((tm, D_MODEL), row), pl.BlockSpec((tm, D_MODEL), lambda i: (nt + i, 0))],
        out_specs=pl.BlockSpec((tm, D_MODEL), row),
        out_shape=jax.ShapeDtypeStruct((t, D_MODEL), F32),
        compiler_params=_params("parallel"),
        name="moe_combine",
    )(h2, meta, ys, ys)


def _moe(h2, g, router, wg, wu, wd):
    t = h2.shape[0]
    tm = MOE_TM
    wr = jnp.pad(router, ((0, 0), (0, LANES - N_EXPERTS)))
    wr_hi = wr.astype(BF16)
    wr_lo = (wr - wr_hi.astype(F32)).astype(BF16)
    xn, meta, cnt = _router(h2, g, jnp.stack([wr_hi, wr_lo]))

    cols = meta[:, :8].T.astype(jnp.int32)
    e1, e2, r1, r2 = cols[0], cols[1], cols[4], cols[5]
    count = cnt[0, :N_EXPERTS].astype(jnp.int32)
    ntile = (count + tm - 1) // tm
    tile_end = jnp.cumsum(ntile)
    off = (tile_end - ntile) * tm
    rows = 2 * t + N_EXPERTS * tm
    tile_id = jnp.arange(rows // tm)
    tile_expert = jnp.minimum(
        jnp.sum(tile_id[:, None] >= tile_end[None, :], axis=1), N_EXPERTS - 1).astype(jnp.int32)
    nused = tile_end[-1:].astype(jnp.int32)
    expert = jnp.arange(N_EXPERTS, dtype=jnp.int32)[:, None]
    pos1 = r1 + jnp.sum(jnp.where(e1[None, :] == expert, off[:, None], 0), axis=0)
    pos2 = r2 + jnp.sum(jnp.where(e2[None, :] == expert, off[:, None], 0), axis=0)
    pad = jnp.stack([jnp.append(off + count, tile_end[-1] * tm),
                     jnp.append(off + ntile * tm, rows)], axis=1).reshape(-1).astype(jnp.int32)
    src, dst = _moe_index(pos1, pos2, pad, rows)

    ys = _experts(tile_expert, nused, src, dst, xn, wg, wu, wd, rows + tm)
    return _combine(h2, meta, ys)


def _segment_matrix(width):
    i = jnp.arange(width) // HEAD_DIM
    return (i[:, None] == i[None, :]).astype(BF16)


def _layer0(h2, bsz, seq, norm_mix, w_in, mu_shift, w0, w_decay_up, a0, w_iclr_up, w_gate_up, k_k, k_a, r_k,
            gn_w, gn_b, conv_w, conv_b, dt_bias, a_log, d_skip, ssd_norm, w_out, norm_ffn, ffn_gate, ffn_up,
            ffn_down):
    row = lambda a: a.reshape(1, -1)
    seg = _segment_matrix(SEG_WIDTH)
    wa = w_in[:, :RWKV_COLS].astype(BF16)
    wb = jnp.pad(w_in[:, RWKV_COLS:], ((0, 0), (0, SSD_COLS_PAD - SSD_COLS))).astype(BF16)
    lora = w_decay_up.shape[0]
    wd = jnp.concatenate([w_decay_up, jnp.zeros((LANES - lora, RWKV_DIM), F32)], axis=0).astype(BF16)
    wi = jnp.concatenate([jnp.zeros((LANES - lora, RWKV_DIM), F32), w_iclr_up], axis=0).astype(BF16)
    pb, r, k, v, lw, kkn, b, g, bonus = _in_proj(
        h2, bsz, seq, row(norm_mix), wa, wb, row(mu_shift), row(w0), wd, row(a0), wi, w_gate_up.astype(BF16),
        row(k_k), row(k_a), row(r_k), seg)
    y = _rwkv_chunk(r, k, v, lw, kkn, b, bsz, seq)

    lane_pad = lambda a: jnp.pad(row(a), ((0, 0), (0, LANES - a.size)))
    ys = _ssd(pb, bsz, seq, conv_w, row(conv_b), lane_pad(dt_bias), lane_pad(a_log), lane_pad(d_skip),
              row(ssd_norm))

    w_out = w_out.astype(BF16)
    return _mix_ffn(h2, y, bonus, g, ys, row(gn_w), row(gn_b), seg, w_out[:RWKV_DIM], w_out[RWKV_DIM:],
                    row(norm_ffn), ffn_gate.astype(BF16), ffn_up.astype(BF16), ffn_down.astype(BF16))


def _layer1(h2, bsz, seq, tables, norm_mix, w_qkv, b_qkv, q_norm, k_norm, sinks, w_o, b_o, norm_ffn, router,
            exp_gate, exp_up, exp_down):
    row = lambda a: a.reshape(1, -1)
    wide = lambda a: jnp.tile(a, SEG_WIDTH // HEAD_DIM).reshape(1, SEG_WIDTH)
    q, k, v = _qkv(h2, row(norm_mix), w_qkv.astype(BF16), row(b_qkv), wide(q_norm), wide(k_norm), tables,
                   _rope_expansion(SEG_WIDTH), _segment_matrix(SEG_WIDTH))
    h2 = _attn(h2, q, k, v, row(sinks), w_o.astype(BF16), row(b_o), bsz, seq)

    return _moe(h2, row(norm_ffn), router, exp_gate, exp_up, exp_down)


def kernel(x, positions, ev_norm_mix, ev_w_in, ev_mu_shift, ev_w0, ev_w_decay_up, ev_a0, ev_w_iclr_up, ev_w_gate_up, ev_k_k, ev_k_a, ev_r_k, ev_gn_w, ev_gn_b, ev_conv_w, ev_conv_b, ev_dt_bias, ev_a_log, ev_d_skip, ev_ssd_norm, ev_w_out, ev_norm_ffn, ev_ffn_gate, ev_ffn_up, ev_ffn_down, od_norm_mix, od_w_qkv, od_b_qkv, od_q_norm, od_k_norm, od_sinks, od_w_o, od_b_o, od_norm_ffn, od_router, od_exp_gate, od_exp_up, od_exp_down):
    bsz, seq, d = x.shape
    depth = ev_norm_mix.shape[0] + od_norm_mix.shape[0]
    tables = _rope_tables(positions)
    h2 = x.reshape(bsz * seq, d)
    for layer in range(depth):
        i = layer // 2
        if layer % 2 == 0:
            h2 = _layer0(h2, bsz, seq, ev_norm_mix[i], ev_w_in[i], ev_mu_shift[i], ev_w0[i], ev_w_decay_up[i],
                         ev_a0[i], ev_w_iclr_up[i], ev_w_gate_up[i], ev_k_k[i], ev_k_a[i], ev_r_k[i],
                         ev_gn_w[i], ev_gn_b[i], ev_conv_w[i], ev_conv_b[i], ev_dt_bias[i], ev_a_log[i],
                         ev_d_skip[i], ev_ssd_norm[i], ev_w_out[i], ev_norm_ffn[i], ev_ffn_gate[i],
                         ev_ffn_up[i], ev_ffn_down[i])
        else:
            h2 = _layer1(h2, bsz, seq, tables, od_norm_mix[i], od_w_qkv[i], od_b_qkv[i], od_q_norm[i],
                         od_k_norm[i], od_sinks[i], od_w_o[i], od_b_o[i], od_norm_ffn[i], od_router[i],
                         od_exp_gate[i], od_exp_up[i], od_exp_down[i])
    return h2.reshape(bsz, seq, d)
```

```python
import functools
import math

import jax
import jax.numpy as jnp
from jax import lax
from jax.experimental import pallas as pl
from jax.experimental.pallas import tpu as pltpu

F32 = jnp.float32
BF16 = jnp.bfloat16

D_MODEL = 1024
HEAD_DIM = 64
NORM_EPS = 1e-6

RWKV_HEADS = 8
RWKV_DIM = 512
RWKV_COLS = 1792
RWKV_GN_EPS = 64e-5
RWKV_CHUNK = 64
RWKV_GROUP = 256

SSD_HEADS = 8
SSD_DIM = 512
SSD_GROUPS = 2
SSD_STATE = 128
SSD_CONV = 4
SSD_CHUNK = 128
SSD_XBC = 1024
SSD_COLS = 1544
SSD_COLS_PAD = 1664

Q_HEADS = 16
KV_HEADS = 4
GQA_GROUP = 4
Q_DIM = 1024
KV_DIM = 256
WINDOW = 128
ROPE_THETA = 500000.0
ROPE_DIM = 16

FFN_DENSE = 2816
N_EXPERTS = 8
FFN_EXPERT = 1408
MOE_TM = 256
MOE_SUB = 8
MOE_FCHUNK = 256

LANES = 128
SEG_WIDTH = 256
VMEM_LIMIT_BYTES = 56 * 1024 * 1024


def _params(*sem):
    return pltpu.CompilerParams(dimension_semantics=sem, vmem_limit_bytes=VMEM_LIMIT_BYTES)


def _bdot(a, b):
    return jnp.dot(a.astype(BF16), b.astype(BF16), preferred_element_type=F32)


def _bdot_nt(a, b):
    return lax.dot_general(a.astype(BF16), b.astype(BF16), (((1,), (1,)), ((), ())),
                           preferred_element_type=F32)


def _bdot_tn(a, b):
    return lax.dot_general(a.astype(BF16), b.astype(BF16), (((0,), (0,)), ((), ())),
                           preferred_element_type=F32)


def _split(x):
    hi = x.astype(BF16)
    lo = (x - hi.astype(F32)).astype(BF16)
    return hi, lo


def _dot_exact_lhs(m, x):
    hi, lo = _split(x)
    return (jnp.dot(m, hi, preferred_element_type=F32) + jnp.dot(m, lo, preferred_element_type=F32))


def _head_sums(x, seg):
    w = seg.shape[0]
    xb = x.astype(BF16)
    return jnp.concatenate(
        [jnp.dot(xb[:, c:c + w], seg, preferred_element_type=F32) for c in range(0, x.shape[1], w)], axis=1)


def _sigmoid(x):
    return 1.0 / (1.0 + jnp.exp(-x))


def _silu(x):
    return x * _sigmoid(x)


def _softplus(x):
    return jnp.maximum(x, 0.0) + jnp.log(1.0 + jnp.exp(-jnp.abs(x)))


def _rms(x, g):
    ms = jnp.mean(x * x, axis=-1, keepdims=True)
    return x * lax.rsqrt(ms + NORM_EPS) * g


def _tile(n, pref):
    t = min(n, pref)
    while n % t:
        t //= 2
    return t


def _full(shape):
    nd = len(shape)
    return pl.BlockSpec(shape, lambda *_: (0,) * nd)


def _resident(shape):
    nd = len(shape)
    return pl.BlockSpec(shape, lambda *_: (0,) * nd, pipeline_mode=pl.Buffered(1))


def _in_proj_kernel(x_ref, g_ref, wa_ref, wb_ref, mu_ref, w0_ref, wd_ref, a0_ref, wi_ref, wg_ref, kk_ref, ka_ref,
                    rk_ref, seg_ref, pb_ref, r_ref, k_ref, v_ref, lw_ref, kkn_ref, b_ref, gate_ref, bonus_ref,
                    carry_ref):
    @pl.when(pl.program_id(1) == 0)
    def _():
        carry_ref[...] = jnp.zeros_like(carry_ref)

    xn = _rms(x_ref[...], g_ref[...]).astype(BF16)
    pb_ref[...] = jnp.dot(xn, wb_ref[...], preferred_element_type=F32)
    pa = jnp.dot(xn, wa_ref[...], preferred_element_type=F32)
    tm = pa.shape[0]
    row = lax.broadcasted_iota(jnp.int32, pa.shape, 0)
    prev = jnp.where(row == 0, carry_ref[...], pltpu.roll(pa, 1, axis=0))
    carry_ref[...] = pa[tm - 1:tm, :]
    x = pa + (prev - pa) * mu_ref[...]

    r = x[:, 0:RWKV_DIM]
    k = x[:, RWKV_DIM:2 * RWKV_DIM]
    v = x[:, 2 * RWKV_DIM:3 * RWKV_DIM]
    lora = x[:, 3 * RWKV_DIM:3 * RWKV_DIM + LANES]
    gl = x[:, 3 * RWKV_DIM + LANES:]
    seg = seg_ref[...]

    w_raw = w0_ref[...] + _bdot(jnp.tanh(lora), wd_ref[...])
    lw_ref[...] = (-math.exp(-0.5)) * _sigmoid(w_raw)
    iclr = _sigmoid(a0_ref[...] + _bdot(lora, wi_ref[...]))
    gate_ref[...] = _bdot(_sigmoid(gl), wg_ref[...])

    kk = k * kk_ref[...]
    kkn = kk * lax.rsqrt(_head_sums(kk * kk, seg) + 1e-12)
    k2 = k * (1.0 + (iclr - 1.0) * ka_ref[...])
    r_ref[...] = r
    k_ref[...] = k2
    v_ref[...] = v
    kkn_ref[...] = kkn
    b_ref[...] = kkn * iclr
    bonus_ref[...] = _head_sums(r * k2 * rk_ref[...], seg) * v


def _in_proj(x2, bsz, seq, g, wa, wb, mu, w0, wd, a0, wi, wg, k_k, k_a, r_k, seg):
    t = x2.shape[0]
    tm = _tile(seq, 512)
    nt = seq // tm
    row = lambda b, i: (b * nt + i, 0)
    out = jax.ShapeDtypeStruct((t, RWKV_DIM), F32)
    small = [mu, w0, wd, a0, wi, wg, k_k, k_a, r_k, seg]
    return pl.pallas_call(
        _in_proj_kernel,
        grid=(bsz, nt),
        in_specs=[pl.BlockSpec((tm, D_MODEL), row), _full((1, D_MODEL)), _resident(wa.shape),
                  _resident(wb.shape)] + [_full(a.shape) for a in small],
        out_specs=[pl.BlockSpec((tm, SSD_COLS_PAD), row)] + [pl.BlockSpec((tm, RWKV_DIM), row)] * 8,
        out_shape=[jax.ShapeDtypeStruct((t, SSD_COLS_PAD), F32)] + [out] * 8,
        scratch_shapes=[pltpu.VMEM((1, RWKV_COLS), F32)],
        compiler_params=_params("arbitrary", "arbitrary"),
        name="in_proj",
    )(x2, g, wa, wb, *small)


def _rwkv_chunk_kernel(r_ref, k_ref, v_ref, lw_ref, kk_ref, b_ref, y_ref, z_ref, *, lt):
    c_len = RWKV_CHUNK
    n = HEAD_DIM
    gl = RWKV_GROUP
    nchunk = gl // c_len
    nheads = LANES // n
    shift = c_len.bit_length() - 1

    @pl.when(pl.program_id(2) == 0)
    def _():
        z_ref[...] = jnp.zeros_like(z_ref)

    ri = lax.broadcasted_iota(jnp.int32, (gl, gl), 0)
    ci = lax.broadcasted_iota(jnp.int32, (gl, gl), 1)
    tri_bd = jnp.where((ci <= ri) & (ci >= ((ri >> shift) << shift)), 1.0, 0.0).astype(BF16)
    re_ = lax.broadcasted_iota(jnp.int32, (c_len, c_len), 0)
    ce_ = lax.broadcasted_iota(jnp.int32, (c_len, c_len), 1)
    eye = re_ == ce_
    ri2 = lax.broadcasted_iota(jnp.int32, (2 * gl, gl), 0)
    ci2 = lax.broadcasted_iota(jnp.int32, (2 * gl, gl), 1)
    t2 = jnp.where(ri2 < gl, ri2, ri2 - gl)
    mask2 = (ci2 <= jnp.where(ri2 < gl, t2 - 1, t2)) & (ci2 >= ((t2 >> shift) << shift))
    zeros = jnp.zeros((c_len, n), F32)

    nsub = lt // gl
    sysid = [(s, h) for s in range(nsub) for h in range(nheads)]
    pre = []
    for s in range(nsub):
        sl = pl.ds(s * gl, gl)
        lw = lw_ref[sl, :]
        g_in = _dot_exact_lhs(tri_bd, lw)
        ends = [g_in[(c + 1) * c_len - 1:(c + 1) * c_len, :] for c in range(nchunk)]
        g_end = jnp.concatenate([jnp.broadcast_to(e, (c_len, LANES)) for e in ends], axis=0)
        e_end = jnp.exp(g_end - g_in)
        en = jnp.exp(-g_in)
        k = k_ref[sl, :]
        b = b_ref[sl, :]
        pre.append(dict(
            ends=ends, v=v_ref[sl, :], rt=r_ref[sl, :] * jnp.exp(g_in),
            at=-kk_ref[sl, :] * jnp.exp(g_in - lw), kt=k * en, bt=b * en, bend=b * e_end, kend=k * e_end))

    def hs(name, s, h):
        return pre[s][name][:, h * n:(h + 1) * n]

    xb, xk, xkv, p, x = {}, {}, {}, {}, {}
    for q in sysid:
        la = jnp.concatenate([hs("at", *q), hs("rt", *q)], axis=0)
        xb[q] = jnp.where(mask2, _bdot_nt(la, hs("bt", *q)), 0.0)
        xk[q] = jnp.where(mask2, _bdot_nt(la, hs("kt", *q)), 0.0)
    for q in sysid:
        xkv[q] = _bdot(xk[q], hs("v", *q))
        p[q] = xb[q][0:gl]
        x[q] = jnp.concatenate([hs("at", *q), xkv[q][0:gl]], axis=1)
    for i in range(6):
        for q in sysid:
            x[q] = x[q] + _bdot(p[q], x[q])
        if i < 5:
            for q in sysid:
                p[q] = _bdot(p[q], p[q])
    rq, y0, mn = {}, {}, {}
    for q in sysid:
        yy = _bdot(xb[q][gl:], x[q])
        rq[q] = yy[:, 0:n] + hs("rt", *q)
        y0[q] = yy[:, n:] + xkv[q][gl:]
        bend_h, kend_h, v_h = hs("bend", *q), hs("kend", *q), hs("v", *q)
        for c in range(nchunk):
            rows = slice(c * c_len, (c + 1) * c_len)
            lhs = jnp.concatenate([bend_h[rows], kend_h[rows]], axis=0)
            rhs = jnp.concatenate([x[q][rows], jnp.concatenate([zeros, v_h[rows]], axis=1)], axis=0)
            mn[q + (c,)] = _bdot_tn(lhs, rhs)
    zero_n = jnp.zeros((n, n), F32)
    comp = {q: [] for q in sysid}
    for c in range(nchunk):
        for q in sysid:
            s, h = q
            pend = jnp.exp(pre[s]["ends"][c][:, h * n:(h + 1) * n])
            m_mat = mn[q + (c,)][:, 0:n] + jnp.where(eye, pend, 0.0)
            n_mat = mn[q + (c,)][:, n:]
            if c == 0:
                comp[q].append(jnp.concatenate([m_mat, n_mat], axis=1))
            else:
                comp[q].append(_bdot(m_mat, comp[q][c - 1]) + jnp.concatenate([zero_n, n_mat], axis=1))
    zs = [z_ref[h] for h in range(nheads)]
    for s in range(nsub):
        yh = []
        for h in range(nheads):
            q = (s, h)
            pm = jnp.concatenate([comp[q][c][:, 0:n] for c in range(nchunk)], axis=0)
            pn = jnp.concatenate([comp[q][c][:, n:] for c in range(nchunk)], axis=0)
            after = _bdot(pm, zs[h]) + pn
            z_in = [zs[h]] + [after[c * c_len:(c + 1) * c_len] for c in range(nchunk - 1)]
            yh.append(jnp.concatenate(
                [_bdot(rq[q][c * c_len:(c + 1) * c_len], z_in[c]) + y0[q][c * c_len:(c + 1) * c_len]
                 for c in range(nchunk)], axis=0))
            zs[h] = after[(nchunk - 1) * c_len:]
        y_ref[pl.ds(s * gl, gl), :] = jnp.concatenate(yh, axis=1)
    for h in range(nheads):
        z_ref[h] = zs[h]


def _rwkv_chunk(r, k, v, lw, kkn, b, bsz, seq):
    t = r.shape[0]
    lt = _tile(seq, 1024)
    nt = seq // lt
    spec = pl.BlockSpec((lt, LANES), lambda bi, hp, i: (bi * nt + i, hp))
    return pl.pallas_call(
        functools.partial(_rwkv_chunk_kernel, lt=lt),
        grid=(bsz, RWKV_DIM // LANES, nt),
        in_specs=[spec] * 6,
        out_specs=spec,
        out_shape=jax.ShapeDtypeStruct((t, RWKV_DIM), F32),
        scratch_shapes=[pltpu.VMEM((2, HEAD_DIM, HEAD_DIM), F32)],
        compiler_params=_params("arbitrary", "arbitrary", "arbitrary"),
        name="rwkv_chunk",
    )(r, k, v, lw, kkn, b)


def _ssd_kernel(pb_ref, cw_ref, cb_ref, dtb_ref, alog_ref, dskip_ref, nrm_ref, ys_ref, ext_ref, st_ref, *, nck):
    q = SSD_CHUNK

    @pl.when(pl.program_id(1) == 0)
    def _():
        ext_ref[0:8, :] = jnp.zeros((8, SSD_XBC), F32)
        st_ref[...] = jnp.zeros_like(st_ref)

    ext_ref[8:8 + nck * q, :] = pb_ref[:, SSD_DIM:SSD_DIM + SSD_XBC]
    for c in range(nck):
        _ssd_chunk(pb_ref, cw_ref, cb_ref, dtb_ref, alog_ref, dskip_ref, nrm_ref, ys_ref, ext_ref, st_ref, c * q)
    ext_ref[0:8, :] = ext_ref[nck * q:nck * q + 8, :]


def _ssd_chunk(pb_ref, cw_ref, cb_ref, dtb_ref, alog_ref, dskip_ref, nrm_ref, ys_ref, ext_ref, st_ref, r0):
    q = SSD_CHUNK
    p = HEAD_DIM
    hpg = SSD_HEADS // SSD_GROUPS
    z = pb_ref[r0:r0 + q, 0:SSD_DIM]
    u = ext_ref[8 + r0:8 + r0 + q, :]
    dt_raw = pb_ref[r0:r0 + q, SSD_DIM + SSD_XBC:]

    conv = cb_ref[...] + cw_ref[SSD_CONV - 1:SSD_CONV, :] * u
    for j in range(SSD_CONV - 1):
        off = r0 + 8 - (SSD_CONV - 1) + j
        conv = conv + cw_ref[j:j + 1, :] * ext_ref[off:off + q, :]
    xbc = _silu(conv)
    xs = xbc[:, 0:SSD_DIM]
    bm = xbc[:, SSD_DIM:SSD_DIM + SSD_GROUPS * SSD_STATE]
    cm = xbc[:, SSD_DIM + SSD_GROUPS * SSD_STATE:]

    dt = _softplus(dt_raw + dtb_ref[...])
    a = -jnp.exp(alog_ref[...])
    ri = lax.broadcasted_iota(jnp.int32, (q, q), 0)
    ci = lax.broadcasted_iota(jnp.int32, (q, q), 1)
    causal = ri >= ci
    cum = _dot_exact_lhs(causal.astype(BF16), dt * a)
    cum_t = cum.T
    dt_t = dt.T
    cum_end = cum[q - 1:q, :]
    to_end = jnp.exp(cum_end - cum) * dt
    ecum = jnp.exp(cum)
    edec = jnp.exp(cum_end)

    ys = []
    for g in range(SSD_GROUPS):
        bm_g = bm[:, g * SSD_STATE:(g + 1) * SSD_STATE]
        cm_g = cm[:, g * SSD_STATE:(g + 1) * SSD_STATE]
        cb = _bdot_nt(cm_g, bm_g)
        bm_t = bm_g.T
        for hh in range(hpg):
            h = g * hpg + hh
            x_h = xs[:, h * p:(h + 1) * p]
            seg = cum[:, h:h + 1] - cum_t[h:h + 1, :]
            ldec = jnp.exp(jnp.where(causal, seg, -jnp.inf))
            wts = cb * ldec * dt_t[h:h + 1, :]
            y = _bdot(wts, x_h)
            h_prev = st_ref[h]
            y = y + _bdot(cm_g, h_prev) * ecum[:, h:h + 1]
            st_ref[h] = h_prev * edec[:, h:h + 1] + _bdot(bm_t, x_h * to_end[:, h:h + 1])
            ys.append(y + dskip_ref[:, h:h + 1] * x_h)
    yall = jnp.concatenate(ys, axis=1) * _silu(z)
    gw = SSD_DIM // SSD_GROUPS
    outs = []
    for g in range(SSD_GROUPS):
        yg = yall[:, g * gw:(g + 1) * gw]
        ms = jnp.mean(yg * yg, axis=-1, keepdims=True)
        outs.append(yg * lax.rsqrt(ms + NORM_EPS) * nrm_ref[:, g * gw:(g + 1) * gw])
    ys_ref[r0:r0 + q, :] = jnp.concatenate(outs, axis=1)


def _ssd(pb, bsz, seq, conv_w, conv_b, dt_bias, a_log, d_skip, ssd_norm):
    t = pb.shape[0]
    nck = 2 if seq % (2 * SSD_CHUNK) == 0 else 1
    rows = nck * SSD_CHUNK
    nc = seq // rows
    row = lambda b, i: (b * nc + i, 0)
    return pl.pallas_call(
        functools.partial(_ssd_kernel, nck=nck),
        grid=(bsz, nc),
        in_specs=[pl.BlockSpec((rows, SSD_COLS_PAD), row), _full(conv_w.shape), _full(conv_b.shape),
                  _full(dt_bias.shape), _full(a_log.shape), _full(d_skip.shape), _full(ssd_norm.shape)],
        out_specs=pl.BlockSpec((rows, SSD_DIM), row),
        out_shape=jax.ShapeDtypeStruct((t, SSD_DIM), F32),
        scratch_shapes=[pltpu.VMEM((8 + rows, SSD_XBC), F32),
                        pltpu.VMEM((SSD_HEADS, SSD_STATE, HEAD_DIM), F32)],
        compiler_params=_params("arbitrary", "arbitrary"),
        name="ssd",
    )(pb, conv_w, conv_b, dt_bias, a_log, d_skip, ssd_norm)


def _mix_ffn_kernel(h_ref, y_ref, bonus_ref, gate_ref, ys_ref, gnw_ref, gnb_ref, seg_ref, wa_ref, wb_ref,
                    g_ref, wg_ref, wu_ref, wd_ref, o_ref):
    y = y_ref[...]
    seg = seg_ref[...]
    mu = _head_sums(y, seg) * (1.0 / HEAD_DIM)
    yc = y - mu
    var = _head_sums(yc * yc, seg) * (1.0 / HEAD_DIM)
    yn = yc * lax.rsqrt(var + RWKV_GN_EPS) * gnw_ref[...] + gnb_ref[...]
    ya = (yn + bonus_ref[...]) * gate_ref[...]
    h = h_ref[...] + _bdot(ya, wa_ref[...]) + _bdot(ys_ref[...], wb_ref[...])

    xn = _rms(h, g_ref[...]).astype(BF16)
    act = _silu(jnp.dot(xn, wg_ref[...], preferred_element_type=F32)) * jnp.dot(
        xn, wu_ref[...], preferred_element_type=F32)
    o_ref[...] = h + jnp.dot(act.astype(BF16), wd_ref[...], preferred_element_type=F32)


def _mix_ffn(h2, y, bonus, gate, ys, gn_w, gn_b, seg, wa, wb, g, wg, wu, wd):
    t = h2.shape[0]
    tm = _tile(t, 512)
    row = lambda i: (i, 0)
    half = pl.BlockSpec((tm, RWKV_DIM), row)
    return pl.pallas_call(
        _mix_ffn_kernel,
        grid=(t // tm,),
        in_specs=[pl.BlockSpec((tm, D_MODEL), row), half, half, half, half, _full(gn_w.shape),
                  _full(gn_b.shape), _resident(seg.shape), _resident(wa.shape), _resident(wb.shape),
                  _full((1, D_MODEL)), _resident(wg.shape), _resident(wu.shape), _resident(wd.shape)],
        out_specs=pl.BlockSpec((tm, D_MODEL), row),
        out_shape=jax.ShapeDtypeStruct((t, D_MODEL), F32),
        compiler_params=_params("parallel"),
        name="mix_ffn",
    )(h2, y, bonus, gate, ys, gn_w, gn_b, seg, wa, wb, g, wg, wu, wd)


def _rope_table_kernel(pos_ref, freq_ref, cos_ref, sin_ref):
    ang = pos_ref[...] * freq_ref[...]
    cos_ref[...] = jnp.cos(ang)
    sin_ref[...] = jnp.sin(ang)


def _rope_tables(positions):
    t = positions.size
    half = ROPE_DIM // 2
    rows = t * half // LANES
    inv_freq = ROPE_THETA ** (-jnp.arange(0, ROPE_DIM, 2, dtype=F32) / ROPE_DIM)
    pos_rep = jnp.repeat(positions.reshape(-1).astype(F32), half).reshape(rows, LANES)
    freq = jnp.tile(inv_freq, LANES // half).reshape(1, LANES)
    tr = _tile(rows, 256)
    cos, sin = pl.pallas_call(
        _rope_table_kernel,
        grid=(rows // tr,),
        in_specs=[pl.BlockSpec((tr, LANES), lambda i: (i, 0)), _full((1, LANES))],
        out_specs=[pl.BlockSpec((tr, LANES), lambda i: (i, 0))] * 2,
        out_shape=[jax.ShapeDtypeStruct((rows, LANES), F32)] * 2,
        compiler_params=_params("parallel"),
        name="rope_tables",
    )(pos_rep, freq)
    return jnp.concatenate([cos.reshape(t, half) - 1.0, sin.reshape(t, half)], axis=1)


def _rope_expansion(width):
    half = ROPE_DIM // 2
    d = jnp.arange(width) % HEAD_DIM
    j = jnp.arange(half)[:, None]
    zero = jnp.zeros((half, width), F32)
    cos_rows = jnp.where((d[None, :] < ROPE_DIM) & (d[None, :] % half == j), 1.0, 0.0)
    sin_a = jnp.where((d[None, :] < half) & (d[None, :] == j), -1.0, 0.0)
    sin_b = jnp.where((d[None, :] >= half) & (d[None, :] < ROPE_DIM) & (d[None, :] - half == j), 1.0, 0.0)
    one = jnp.concatenate([jnp.concatenate([cos_rows, zero, zero], axis=1),
                           jnp.concatenate([zero, sin_a, sin_b], axis=1)], axis=0)
    return jnp.concatenate([one, one], axis=0).astype(BF16)


def _qkv_kernel(h_ref, g_ref, w_ref, b_ref, qn_ref, kn_ref, cs_ref, exp_ref, seg_ref, q_ref, k_ref, v_ref):
    xn = _rms(h_ref[...], g_ref[...]).astype(BF16)
    qkv = jnp.dot(xn, w_ref[...], preferred_element_type=F32) + b_ref[...]
    seg = seg_ref[...]
    wide = seg.shape[0]
    hi, lo = _split(cs_ref[...])
    tab = jnp.dot(jnp.concatenate([hi, lo], axis=1), exp_ref[...], preferred_element_type=F32)
    cos = 1.0 + tab[:, 0:wide]
    sa = tab[:, wide:2 * wide]
    sb = tab[:, 2 * wide:]

    def norm_rope(x, gain):
        ms = _head_sums(x * x, seg) * (1.0 / HEAD_DIM)
        xn_ = x * lax.rsqrt(ms + NORM_EPS) * gain
        return (xn_ * cos + pltpu.roll(xn_, wide - ROPE_DIM // 2, axis=1) * sa
                + pltpu.roll(xn_, ROPE_DIM // 2, axis=1) * sb)

    for c in range(Q_DIM // wide):
        x = qkv[:, c * wide:(c + 1) * wide]
        q_ref[:, c * wide:(c + 1) * wide] = (norm_rope(x, qn_ref[...]) * (HEAD_DIM ** -0.5)).astype(BF16)
    for c in range(KV_DIM // wide):
        x = qkv[:, Q_DIM + c * wide:Q_DIM + (c + 1) * wide]
        k_ref[:, c * wide:(c + 1) * wide] = norm_rope(x, kn_ref[...]).astype(BF16)
    v_ref[...] = qkv[:, Q_DIM + KV_DIM:].astype(BF16)


def _qkv(h2, g, w, b, qn, kn, cs, expand, seg):
    t = h2.shape[0]
    tm = _tile(t, 512)
    row = lambda i: (i, 0)
    return pl.pallas_call(
        _qkv_kernel,
        grid=(t // tm,),
        in_specs=[pl.BlockSpec((tm, D_MODEL), row), _full((1, D_MODEL)), _resident(w.shape), _full(b.shape),
                  _full(qn.shape), _full(kn.shape), pl.BlockSpec((tm, cs.shape[1]), row), _full(expand.shape),
                  _full(seg.shape)],
        out_specs=[pl.BlockSpec((tm, Q_DIM), row), pl.BlockSpec((tm, KV_DIM), row),
                   pl.BlockSpec((tm, KV_DIM), row)],
        out_shape=[jax.ShapeDtypeStruct((t, Q_DIM), BF16), jax.ShapeDtypeStruct((t, KV_DIM), BF16),
                   jax.ShapeDtypeStruct((t, KV_DIM), BF16)],
        compiler_params=_params("parallel"),
        name="qkv",
    )(h2, g, w, b, qn, kn, cs, expand, seg)


def _attn_kernel(h_ref, q_ref, kc_ref, kp_ref, vc_ref, vp_ref, sink_ref, wo_ref, bo_ref, o_ref, *, nqb):
    w = WINDOW
    n = HEAD_DIM
    rows = GQA_GROUP * w
    qi = lax.broadcasted_iota(jnp.int32, (rows, 2 * w), 0) % w
    kj = lax.broadcasted_iota(jnp.int32, (rows, 2 * w), 1)
    diff = w + qi - kj
    band = (diff >= 0) & (diff < w)
    band0 = band & (kj >= jnp.where(pl.program_id(1) == 0, w, 0))
    kall = jnp.concatenate([kp_ref[...], kc_ref[...]], axis=0)
    vall = jnp.concatenate([vp_ref[...], vc_ref[...]], axis=0)
    units = [(qb, g) for qb in range(nqb) for g in range(KV_HEADS)]
    s, e, den = {}, {}, {}
    for u in units:
        qb, g = u
        qs = jnp.concatenate(
            [q_ref[qb * w:(qb + 1) * w, (g * GQA_GROUP + i) * n:(g * GQA_GROUP + i + 1) * n]
             for i in range(GQA_GROUP)], axis=0)
        kk = kall[qb * w:(qb + 2) * w, g * n:(g + 1) * n]
        sc = lax.dot_general(qs, kk, (((1,), (1,)), ((), ())), preferred_element_type=F32)
        s[u] = jnp.where(band0 if qb == 0 else band, sc, -jnp.inf)
    sinks = [jnp.concatenate(
        [jnp.broadcast_to(sink_ref[:, g * GQA_GROUP + i:g * GQA_GROUP + i + 1], (w, LANES))
         for i in range(GQA_GROUP)], axis=0) for g in range(KV_HEADS)]
    ones = jnp.ones((2 * w, LANES - n), BF16)
    for u in units:
        sink = sinks[u[1]]
        m = jnp.maximum(jnp.broadcast_to(jnp.max(s[u], axis=-1, keepdims=True), (rows, LANES)), sink)
        e[u] = jnp.exp(s[u] - jnp.concatenate([m, m], axis=1)).astype(BF16)
        den[u] = jnp.exp(sink - m)
    outs = [[None] * Q_HEADS for _ in range(nqb)]
    for u in units:
        qb, g = u
        vv = jnp.concatenate([vall[qb * w:(qb + 2) * w, g * n:(g + 1) * n], ones], axis=1)
        both = jnp.dot(e[u], vv, preferred_element_type=F32)
        total = pltpu.roll(both, LANES - n, axis=1) + den[u]
        o = both[:, 0:n] / total[:, 0:n]
        for i in range(GQA_GROUP):
            outs[qb][g * GQA_GROUP + i] = o[i * w:(i + 1) * w]
    att = jnp.concatenate([jnp.concatenate(outs[qb], axis=1) for qb in range(nqb)], axis=0)
    o_ref[...] = h_ref[...] + _bdot(att, wo_ref[...]) + bo_ref[...]


def _attn(h2, q, k, v, sinks, wo, bo, bsz, seq):
    t = h2.shape[0]
    nqb = 4 if seq % (4 * WINDOW) == 0 else 1
    tq = nqb * WINDOW
    nb = seq // tq
    cur = lambda b, i: (b * nb + i, 0)
    prv = lambda b, i: (b * nb * nqb + jnp.maximum(i * nqb - 1, 0), 0)
    return pl.pallas_call(
        functools.partial(_attn_kernel, nqb=nqb),
        grid=(bsz, nb),
        in_specs=[pl.BlockSpec((tq, D_MODEL), cur), pl.BlockSpec((tq, Q_DIM), cur),
                  pl.BlockSpec((tq, KV_DIM), cur), pl.BlockSpec((WINDOW, KV_DIM), prv),
                  pl.BlockSpec((tq, KV_DIM), cur), pl.BlockSpec((WINDOW, KV_DIM), prv),
                  _full(sinks.shape), _full(wo.shape), _full(bo.shape)],
        out_specs=pl.BlockSpec((tq, D_MODEL), cur),
        out_shape=jax.ShapeDtypeStruct((t, D_MODEL), F32),
        compiler_params=_params("parallel", "parallel"),
        name="attn",
    )(h2, q, k, k, v, v, sinks, wo, bo)


def _router_kernel(h_ref, g_ref, wr_ref, xn_ref, meta_ref, cnt_ref, base_ref):
    @pl.when(pl.program_id(0) == 0)
    def _():
        base_ref[...] = jnp.zeros_like(base_ref)

    xn = _rms(h_ref[...], g_ref[...])
    xn_ref[...] = xn
    hi, lo = _split(xn)
    whi = wr_ref[0]
    wlo = wr_ref[1]
    logits = (jnp.dot(hi, whi, preferred_element_type=F32) + jnp.dot(hi, wlo, preferred_element_type=F32)
              + jnp.dot(lo, whi, preferred_element_type=F32))
    tm = logits.shape[0]
    lane = lax.broadcasted_iota(jnp.int32, logits.shape, 1)
    logits = jnp.where(lane < N_EXPERTS, logits, -jnp.inf)
    m1 = jnp.max(logits, axis=-1, keepdims=True)
    i1 = jnp.min(jnp.where(logits == m1, lane, LANES), axis=-1, keepdims=True)
    rest = jnp.where(lane == i1, -jnp.inf, logits)
    m2 = jnp.max(rest, axis=-1, keepdims=True)
    i2 = jnp.min(jnp.where(rest == m2, lane, LANES), axis=-1, keepdims=True)
    e2 = jnp.exp(m2 - m1)
    w1 = 1.0 / (1.0 + e2)
    w2 = e2 / (1.0 + e2)

    sel1 = lane == i1
    sel2 = lane == i2
    onehot = jnp.where(sel1, 1.0, 0.0) + jnp.where(sel2, 1.0, 0.0)
    ri = lax.broadcasted_iota(jnp.int32, (tm, tm), 0)
    ci = lax.broadcasted_iota(jnp.int32, (tm, tm), 1)
    before = jnp.dot(jnp.where(ci < ri, 1.0, 0.0).astype(BF16), onehot.astype(BF16),
                     preferred_element_type=F32)
    rank = base_ref[...] + before
    r1 = jnp.sum(jnp.where(sel1, rank, 0.0), axis=-1, keepdims=True)
    r2 = jnp.sum(jnp.where(sel2, rank, 0.0), axis=-1, keepdims=True)
    total = base_ref[...] + jnp.sum(onehot, axis=0, keepdims=True)
    base_ref[...] = total
    cnt_ref[...] = total
    cols = (i1.astype(F32), i2.astype(F32), w1, w2, r1, r2)
    meta = jnp.zeros(logits.shape, F32)
    for c, val in enumerate(cols):
        meta = jnp.where(lane == c, val, meta)
    meta_ref[...] = meta


def _router(h2, g, wr):
    t = h2.shape[0]
    tm = _tile(t, 512)
    row = lambda i: (i, 0)
    return pl.pallas_call(
        _router_kernel,
        grid=(t // tm,),
        in_specs=[pl.BlockSpec((tm, D_MODEL), row), _full((1, D_MODEL)), _full(wr.shape)],
        out_specs=[pl.BlockSpec((tm, D_MODEL), row), pl.BlockSpec((tm, LANES), row), _full((1, LANES))],
        out_shape=[jax.ShapeDtypeStruct((t, D_MODEL), F32), jax.ShapeDtypeStruct((t, LANES), F32),
                   jax.ShapeDtypeStruct((1, LANES), F32)],
        scratch_shapes=[pltpu.VMEM((1, LANES), F32)],
        compiler_params=_params("arbitrary"),
        name="router",
    )(h2, g, wr)


def _moe_index_kernel(p1_ref, p2_ref, pad_ref, src_ref, dst_ref, *, ntok):
    tm = MOE_TM
    nrow = src_ref.shape[0]

    def virtual(j, c):
        dst_ref[j] = nrow + j
        return c

    lax.fori_loop(0, tm, virtual, 0)

    for e in range(N_EXPERTS + 1):
        def fill(p, c, e=e):
            src_ref[p] = 0
            dst_ref[tm + p] = 2 * ntok + (e % N_EXPERTS) * tm + ((p - pad_ref[2 * e]) & (tm - 1))
            return c

        lax.fori_loop(pad_ref[2 * e], pad_ref[2 * e + 1], fill, 0)

    def body(t, c):
        p1 = p1_ref[t]
        p2 = p2_ref[t]
        src_ref[p1] = t
        dst_ref[tm + p1] = t
        src_ref[p2] = t
        dst_ref[tm + p2] = ntok + t
        return c

    lax.fori_loop(0, ntok, body, 0, unroll=4)


def _moe_index(pos1, pos2, pad, rows):
    smem = pl.BlockSpec(memory_space=pltpu.SMEM)
    return pl.pallas_call(
        functools.partial(_moe_index_kernel, ntok=pos1.shape[0]),
        grid_spec=pltpu.PrefetchScalarGridSpec(
            num_scalar_prefetch=3, grid=(1,), in_specs=[], out_specs=[smem, smem]),
        out_shape=[jax.ShapeDtypeStruct((rows,), jnp.int32), jax.ShapeDtypeStruct((rows + MOE_TM,), jnp.int32)],
        compiler_params=_params("arbitrary"),
        name="moe_index",
    )(pos1, pos2, pad)


def _row_copy(src_ref, src_row, dst_ref, dst_row, sem):
    return pltpu.make_async_copy(src_ref.at[pl.ds(src_row, 1)], dst_ref.at[pl.ds(dst_row, 1)], sem)


def _experts_kernel(te_ref, nused_ref, src_ref, dst_ref, x_ref, wg_ref, wu_ref, wd_ref, y_ref,
                    xbuf, ybuf, wgb, wub, wdb, gsem, ssem):
    i = pl.program_id(0)
    nused = nused_ref[0]
    tm = MOE_TM

    sub = MOE_SUB
    ngroup = tm // sub

    def gather_group(tile, slot, k):
        for u in range(sub):
            _row_copy(x_ref, src_ref[tile * tm + k * sub + u], xbuf.at[slot, k], u, gsem.at[slot]).start()

    def scatter_group(tile, slot, k):
        for u in range(sub):
            _row_copy(ybuf.at[slot, k], u, y_ref, dst_ref[(tile + 1) * tm + k * sub + u], ssem.at[slot]).start()

    def wait_gather(slot):
        def body(k, c):
            pltpu.make_async_copy(x_ref.at[pl.ds(0, sub)], xbuf.at[slot, k], gsem.at[slot]).wait()
            return c
        lax.fori_loop(0, ngroup, body, 0)

    def wait_scatter(slot):
        def body(k, c):
            pltpu.make_async_copy(ybuf.at[slot, k], y_ref.at[pl.ds(0, sub)], ssem.at[slot]).wait()
            return c
        lax.fori_loop(0, ngroup, body, 0)

    @pl.when(i == 0)
    def _():
        ybuf[1] = jnp.zeros(ybuf.shape[1:], F32)

        def body(k, c):
            gather_group(0, 0, k)
            return c
        lax.fori_loop(0, ngroup, body, 0)

    @pl.when(i < nused)
    def _():
        slot = i % 2
        other = 1 - slot
        nxt = jnp.minimum(i + 1, nused - 1)
        wait_gather(slot)

        @pl.when(i >= 1)
        def _():
            wait_scatter(slot)

        @pl.when((i == 0) | (te_ref[i] != te_ref[jnp.maximum(i - 1, 0)]))
        def _():
            wgb[...] = wg_ref[0].astype(BF16)
            wub[...] = wu_ref[0].astype(BF16)
            wdb[...] = wd_ref[0].astype(BF16)

        xb = xbuf[slot].reshape(tm, D_MODEL).astype(BF16)
        bounds = list(range(0, FFN_EXPERT, MOE_FCHUNK)) + [FFN_EXPERT]
        nchunk = len(bounds) - 1
        acts = []
        for c in range(nchunk):
            c0, c1 = bounds[c], bounds[c + 1]
            gate = jnp.dot(xb, wgb[:, c0:c1], preferred_element_type=F32)
            up = jnp.dot(xb, wub[:, c0:c1], preferred_element_type=F32)
            acts.append((_silu(gate) * up).astype(BF16))
            for k in range(c * ngroup // nchunk, (c + 1) * ngroup // nchunk):
                gather_group(nxt, other, k)
                scatter_group(i - 1, other, k)
        y = jnp.dot(jnp.concatenate(acts, axis=1), wdb[...], preferred_element_type=F32)
        ybuf[slot] = y.reshape(ngroup, sub, D_MODEL)

        @pl.when(i == nused - 1)
        def _():
            def body(k, c):
                scatter_group(i, slot, k)
                return c
            lax.fori_loop(0, ngroup, body, 0)
            wait_scatter(other)
            wait_scatter(slot)
            wait_gather(other)


def _experts(tile_expert, nused, src, dst, xn, wg, wu, wd, out_rows):
    ntile = tile_expert.shape[0]
    tm = MOE_TM
    wsel = lambda i, te, nu, s, d: (te[i], 0, 0)
    any_spec = pl.BlockSpec(memory_space=pl.ANY)
    return pl.pallas_call(
        _experts_kernel,
        grid_spec=pltpu.PrefetchScalarGridSpec(
            num_scalar_prefetch=4, grid=(ntile,),
            in_specs=[any_spec,
                      pl.BlockSpec((1, D_MODEL, FFN_EXPERT), wsel),
                      pl.BlockSpec((1, D_MODEL, FFN_EXPERT), wsel),
                      pl.BlockSpec((1, FFN_EXPERT, D_MODEL), wsel)],
            out_specs=any_spec,
            scratch_shapes=[pltpu.VMEM((2, tm // MOE_SUB, MOE_SUB, D_MODEL), F32),
                            pltpu.VMEM((2, tm // MOE_SUB, MOE_SUB, D_MODEL), F32),
                            pltpu.VMEM((D_MODEL, FFN_EXPERT), BF16), pltpu.VMEM((D_MODEL, FFN_EXPERT), BF16),
                            pltpu.VMEM((FFN_EXPERT, D_MODEL), BF16),
                            pltpu.SemaphoreType.DMA((2,)), pltpu.SemaphoreType.DMA((2,))]),
        out_shape=jax.ShapeDtypeStruct((out_rows, D_MODEL), F32),
        compiler_params=_params("arbitrary"),
        name="moe_experts",
    )(tile_expert, nused, src, dst, xn, wg, wu, wd)


def _combine_kernel(h_ref, meta_ref, y1_ref, y2_ref, o_ref):
    meta = meta_ref[...]
    o_ref[...] = h_ref[...] + meta[:, 2:3] * y1_ref[...] + meta[:, 3:4] * y2_ref[...]


def _combine(h2, meta, ys):
    t = h2.shape[0]
    tm = _tile(t, 512)
    nt = t // tm
    row = lambda i: (i, 0)
    return pl.pallas_call(
        _combine_kernel,
        grid=(nt,),
        in_specs=[pl.BlockSpec((tm, D_MODEL), row), pl.BlockSpec((tm, LANES), row),
                  pl.BlockSpec((tm, D_MODEL), row), pl.BlockSpec((tm, D_MODEL), lambda i: (nt + i, 0))],
        out_specs=pl.BlockSpec((tm, D_MODEL), row),
        out_shape=jax.ShapeDtypeStruct((t, D_MODEL), F32),
        compiler_params=_params("parallel"),
        name="moe_combine",
    )(h2, meta, ys, ys)


def _moe(h2, g, router, wg, wu, wd):
    t = h2.shape[0]
    tm = MOE_TM
    wr = jnp.pad(router, ((0, 0), (0, LANES - N_EXPERTS)))
    wr_hi = wr.astype(BF16)
    wr_lo = (wr - wr_hi.astype(F32)).astype(BF16)
    xn, meta, cnt = _router(h2, g, jnp.stack([wr_hi, wr_lo]))

    cols = meta[:, :8].T.astype(jnp.int32)
    e1, e2, r1, r2 = cols[0], cols[1], cols[4], cols[5]
    count = cnt[0, :N_EXPERTS].astype(jnp.int32)
    ntile = (count + tm - 1) // tm
    tile_end = jnp.cumsum(ntile)
    off = (tile_end - ntile) * tm
    rows = 2 * t + N_EXPERTS * tm
    tile_id = jnp.arange(rows // tm)
    tile_expert = jnp.minimum(
        jnp.sum(tile_id[:, None] >= tile_end[None, :], axis=1), N_EXPERTS - 1).astype(jnp.int32)
    nused = tile_end[-1:].astype(jnp.int32)
    expert = jnp.arange(N_EXPERTS, dtype=jnp.int32)[:, None]
    pos1 = r1 + jnp.sum(jnp.where(e1[None, :] == expert, off[:, None], 0), axis=0)
    pos2 = r2 + jnp.sum(jnp.where(e2[None, :] == expert, off[:, None], 0), axis=0)
    pad = jnp.stack([jnp.append(off + count, tile_end[-1] * tm),
                     jnp.append(off + ntile * tm, rows)], axis=1).reshape(-1).astype(jnp.int32)
    src, dst = _moe_index(pos1, pos2, pad, rows)

    ys = _experts(tile_expert, nused, src, dst, xn, wg, wu, wd, rows + tm)
    return _combine(h2, meta, ys)


def _segment_matrix(width):
    i = jnp.arange(width) // HEAD_DIM
    return (i[:, None] == i[None, :]).astype(BF16)


def _layer0(h2, bsz, seq, norm_mix, w_in, mu_shift, w0, w_decay_up, a0, w_iclr_up, w_gate_up, k_k, k_a, r_k,
            gn_w, gn_b, conv_w, conv_b, dt_bias, a_log, d_skip, ssd_norm, w_out, norm_ffn, ffn_gate, ffn_up,
            ffn_down):
    row = lambda a: a.reshape(1, -1)
    seg = _segment_matrix(SEG_WIDTH)
    wa = w_in[:, :RWKV_COLS].astype(BF16)
    wb = jnp.pad(w_in[:, RWKV_COLS:], ((0, 0), (0, SSD_COLS_PAD - SSD_COLS))).astype(BF16)
    lora = w_decay_up.shape[0]
    wd = jnp.concatenate([w_decay_up, jnp.zeros((LANES - lora, RWKV_DIM), F32)], axis=0).astype(BF16)
    wi = jnp.concatenate([jnp.zeros((LANES - lora, RWKV_DIM), F32), w_iclr_up], axis=0).astype(BF16)
    pb, r, k, v, lw, kkn, b, g, bonus = _in_proj(
        h2, bsz, seq, row(norm_mix), wa, wb, row(mu_shift), row(w0), wd, row(a0), wi, w_gate_up.astype(BF16),
        row(k_k), row(k_a), row(r_k), seg)
    y = _rwkv_chunk(r, k, v, lw, kkn, b, bsz, seq)

    lane_pad = lambda a: jnp.pad(row(a), ((0, 0), (0, LANES - a.size)))
    ys = _ssd(pb, bsz, seq, conv_w, row(conv_b), lane_pad(dt_bias), lane_pad(a_log), lane_pad(d_skip),
              row(ssd_norm))

    w_out = w_out.astype(BF16)
    return _mix_ffn(h2, y, bonus, g, ys, row(gn_w), row(gn_b), seg, w_out[:RWKV_DIM], w_out[RWKV_DIM:],
                    row(norm_ffn), ffn_gate.astype(BF16), ffn_up.astype(BF16), ffn_down.astype(BF16))


def _layer1(h2, bsz, seq, tables, norm_mix, w_qkv, b_qkv, q_norm, k_norm, sinks, w_o, b_o, norm_ffn, router,
            exp_gate, exp_up, exp_down):
    row = lambda a: a.reshape(1, -1)
    wide = lambda a: jnp.tile(a, SEG_WIDTH // HEAD_DIM).reshape(1, SEG_WIDTH)
    q, k, v = _qkv(h2, row(norm_mix), w_qkv.astype(BF16), row(b_qkv), wide(q_norm), wide(k_norm), tables,
                   _rope_expansion(SEG_WIDTH), _segment_matrix(SEG_WIDTH))
    h2 = _attn(h2, q, k, v, row(sinks), w_o.astype(BF16), row(b_o), bsz, seq)

    return _moe(h2, row(norm_ffn), router, exp_gate, exp_up, exp_down)


def kernel(x, positions, ev_norm_mix, ev_w_in, ev_mu_shift, ev_w0, ev_w_decay_up, ev_a0, ev_w_iclr_up, ev_w_gate_up, ev_k_k, ev_k_a, ev_r_k, ev_gn_w, ev_gn_b, ev_conv_w, ev_conv_b, ev_dt_bias, ev_a_log, ev_d_skip, ev_ssd_norm, ev_w_out, ev_norm_ffn, ev_ffn_gate, ev_ffn_up, ev_ffn_down, od_norm_mix, od_w_qkv, od_b_qkv, od_q_norm, od_k_norm, od_sinks, od_w_o, od_b_o, od_norm_ffn, od_router, od_exp_gate, od_exp_up, od_exp_down):
    bsz, seq, d = x.shape
    depth = ev_norm_mix.shape[0] + od_norm_mix.shape[0]
    tables = _rope_tables(positions)
    h2 = x.reshape(bsz * seq, d)
    for layer in range(depth):
        i = layer // 2
        if layer % 2 == 0:
            h2 = _layer0(h2, bsz, seq, ev_norm_mix[i], ev_w_in[i], ev_mu_shift[i], ev_w0[i], ev_w_decay_up[i],
                         ev_a0[i], ev_w_iclr_up[i], ev_w_gate_up[i], ev_k_k[i], ev_k_a[i], ev_r_k[i],
                         ev_gn_w[i], ev_gn_b[i], ev_conv_w[i], ev_conv_b[i], ev_dt_bias[i], ev_a_log[i],
                         ev_d_skip[i], ev_ssd_norm[i], ev_w_out[i], ev_norm_ffn[i], ev_ffn_gate[i],
                         ev_ffn_up[i], ev_ffn_down[i])
        else:
            h2 = _layer1(h2, bsz, seq, tables, od_norm_mix[i], od_w_qkv[i], od_b_qkv[i], od_q_norm[i],
                         od_k_norm[i], od_sinks[i], od_w_o[i], od_b_o[i], od_norm_ffn[i], od_router[i],
                         od_exp_gate[i], od_exp_up[i], od_exp_down[i])
    return h2.reshape(bsz, seq, d)
```

```python
import functools
import math

import jax
import jax.numpy as jnp
from jax import lax
from jax.experimental import pallas as pl
from jax.experimental.pallas import tpu as pltpu

F32 = jnp.float32
BF16 = jnp.bfloat16

D_MODEL = 1024
HEAD_DIM = 64
NORM_EPS = 1e-6

RWKV_HEADS = 8
RWKV_DIM = 512
RWKV_COLS = 1792
RWKV_GN_EPS = 64e-5
RWKV_CHUNK = 64
RWKV_GROUP = 256

SSD_HEADS = 8
SSD_DIM = 512
SSD_GROUPS = 2
SSD_STATE = 128
SSD_CONV = 4
SSD_CHUNK = 128
SSD_XBC = 1024
SSD_COLS = 1544
SSD_COLS_PAD = 1664

Q_HEADS = 16
KV_HEADS = 4
GQA_GROUP = 4
Q_DIM = 1024
KV_DIM = 256
WINDOW = 128
ROPE_THETA = 500000.0
ROPE_DIM = 16

FFN_DENSE = 2816
N_EXPERTS = 8
FFN_EXPERT = 1408
MOE_TM = 512
MOE_SUB = 8
MOE_FCHUNK = 256

LANES = 128
SEG_WIDTH = 256
VMEM_LIMIT_BYTES = 56 * 1024 * 1024


def _params(*sem):
    return pltpu.CompilerParams(dimension_semantics=sem, vmem_limit_bytes=VMEM_LIMIT_BYTES)


def _bdot(a, b):
    return jnp.dot(a.astype(BF16), b.astype(BF16), preferred_element_type=F32)


def _bdot_nt(a, b):
    return lax.dot_general(a.astype(BF16), b.astype(BF16), (((1,), (1,)), ((), ())),
                           preferred_element_type=F32)


def _bdot_tn(a, b):
    return lax.dot_general(a.astype(BF16), b.astype(BF16), (((0,), (0,)), ((), ())),
                           preferred_element_type=F32)


def _split(x):
    hi = x.astype(BF16)
    lo = (x - hi.astype(F32)).astype(BF16)
    return hi, lo


def _dot_exact_lhs(m, x):
    hi, lo = _split(x)
    return (jnp.dot(m, hi, preferred_element_type=F32) + jnp.dot(m, lo, preferred_element_type=F32))


def _head_sums(x, seg):
    w = seg.shape[0]
    xb = x.astype(BF16)
    return jnp.concatenate(
        [jnp.dot(xb[:, c:c + w], seg, preferred_element_type=F32) for c in range(0, x.shape[1], w)], axis=1)


def _sigmoid(x):
    return 1.0 / (1.0 + jnp.exp(-x))


def _silu(x):
    return x * _sigmoid(x)


def _softplus(x):
    return jnp.maximum(x, 0.0) + jnp.log(1.0 + jnp.exp(-jnp.abs(x)))


def _rms(x, g):
    ms = jnp.mean(x * x, axis=-1, keepdims=True)
    return x * lax.rsqrt(ms + NORM_EPS) * g


def _tile(n, pref):
    t = min(n, pref)
    while n % t:
        t //= 2
    return t


def _full(shape):
    nd = len(shape)
    return pl.BlockSpec(shape, lambda *_: (0,) * nd)


def _resident(shape):
    nd = len(shape)
    return pl.BlockSpec(shape, lambda *_: (0,) * nd, pipeline_mode=pl.Buffered(1))


def _in_proj_kernel(x_ref, g_ref, wa_ref, wb_ref, mu_ref, w0_ref, wd_ref, a0_ref, wi_ref, wg_ref, kk_ref, ka_ref,
                    rk_ref, seg_ref, pb_ref, r_ref, k_ref, v_ref, lw_ref, kkn_ref, b_ref, gate_ref, bonus_ref,
                    carry_ref):
    @pl.when(pl.program_id(1) == 0)
    def _():
        carry_ref[...] = jnp.zeros_like(carry_ref)

    xn = _rms(x_ref[...], g_ref[...]).astype(BF16)
    pb_ref[...] = jnp.dot(xn, wb_ref[...], preferred_element_type=F32)
    pa = jnp.dot(xn, wa_ref[...], preferred_element_type=F32)
    tm = pa.shape[0]
    row = lax.broadcasted_iota(jnp.int32, pa.shape, 0)
    prev = jnp.where(row == 0, carry_ref[...], pltpu.roll(pa, 1, axis=0))
    carry_ref[...] = pa[tm - 1:tm, :]
    x = pa + (prev - pa) * mu_ref[...]

    r = x[:, 0:RWKV_DIM]
    k = x[:, RWKV_DIM:2 * RWKV_DIM]
    v = x[:, 2 * RWKV_DIM:3 * RWKV_DIM]
    lora = x[:, 3 * RWKV_DIM:3 * RWKV_DIM + LANES]
    gl = x[:, 3 * RWKV_DIM + LANES:]
    seg = seg_ref[...]

    w_raw = w0_ref[...] + _bdot(jnp.tanh(lora), wd_ref[...])
    lw_ref[...] = (-math.exp(-0.5)) * _sigmoid(w_raw)
    iclr = _sigmoid(a0_ref[...] + _bdot(lora, wi_ref[...]))
    gate_ref[...] = _bdot(_sigmoid(gl), wg_ref[...])

    kk = k * kk_ref[...]
    kkn = kk * lax.rsqrt(_head_sums(kk * kk, seg) + 1e-12)
    k2 = k * (1.0 + (iclr - 1.0) * ka_ref[...])
    r_ref[...] = r
    k_ref[...] = k2
    v_ref[...] = v
    kkn_ref[...] = kkn
    b_ref[...] = kkn * iclr
    bonus_ref[...] = _head_sums(r * k2 * rk_ref[...], seg) * v


def _in_proj(x2, bsz, seq, g, wa, wb, mu, w0, wd, a0, wi, wg, k_k, k_a, r_k, seg):
    t = x2.shape[0]
    tm = _tile(seq, 512)
    nt = seq // tm
    row = lambda b, i: (b * nt + i, 0)
    out = jax.ShapeDtypeStruct((t, RWKV_DIM), F32)
    small = [mu, w0, wd, a0, wi, wg, k_k, k_a, r_k, seg]
    return pl.pallas_call(
        _in_proj_kernel,
        grid=(bsz, nt),
        in_specs=[pl.BlockSpec((tm, D_MODEL), row), _full((1, D_MODEL)), _resident(wa.shape),
                  _resident(wb.shape)] + [_full(a.shape) for a in small],
        out_specs=[pl.BlockSpec((tm, SSD_COLS_PAD), row)] + [pl.BlockSpec((tm, RWKV_DIM), row)] * 8,
        out_shape=[jax.ShapeDtypeStruct((t, SSD_COLS_PAD), F32)] + [out] * 8,
        scratch_shapes=[pltpu.VMEM((1, RWKV_COLS), F32)],
        compiler_params=_params("arbitrary", "arbitrary"),
        name="in_proj",
    )(x2, g, wa, wb, *small)


def _rwkv_chunk_kernel(r_ref, k_ref, v_ref, lw_ref, kk_ref, b_ref, y_ref, z_ref, *, lt):
    c_len = RWKV_CHUNK
    n = HEAD_DIM
    gl = RWKV_GROUP
    nchunk = gl // c_len
    nheads = LANES // n
    shift = c_len.bit_length() - 1

    @pl.when(pl.program_id(2) == 0)
    def _():
        z_ref[...] = jnp.zeros_like(z_ref)

    ri = lax.broadcasted_iota(jnp.int32, (gl, gl), 0)
    ci = lax.broadcasted_iota(jnp.int32, (gl, gl), 1)
    tri_bd = jnp.where((ci <= ri) & (ci >= ((ri >> shift) << shift)), 1.0, 0.0).astype(BF16)
    re_ = lax.broadcasted_iota(jnp.int32, (c_len, c_len), 0)
    ce_ = lax.broadcasted_iota(jnp.int32, (c_len, c_len), 1)
    eye = re_ == ce_
    ri2 = lax.broadcasted_iota(jnp.int32, (2 * gl, gl), 0)
    ci2 = lax.broadcasted_iota(jnp.int32, (2 * gl, gl), 1)
    t2 = jnp.where(ri2 < gl, ri2, ri2 - gl)
    mask2 = (ci2 <= jnp.where(ri2 < gl, t2 - 1, t2)) & (ci2 >= ((t2 >> shift) << shift))
    zeros = jnp.zeros((c_len, n), F32)

    nsub = lt // gl
    sysid = [(s, h) for s in range(nsub) for h in range(nheads)]
    pre = []
    for s in range(nsub):
        sl = pl.ds(s * gl, gl)
        lw = lw_ref[sl, :]
        g_in = _dot_exact_lhs(tri_bd, lw)
        ends = [g_in[(c + 1) * c_len - 1:(c + 1) * c_len, :] for c in range(nchunk)]
        g_end = jnp.concatenate([jnp.broadcast_to(e, (c_len, LANES)) for e in ends], axis=0)
        e_end = jnp.exp(g_end - g_in)
        en = jnp.exp(-g_in)
        k = k_ref[sl, :]
        b = b_ref[sl, :]
        pre.append(dict(
            ends=ends, v=v_ref[sl, :], rt=r_ref[sl, :] * jnp.exp(g_in),
            at=-kk_ref[sl, :] * jnp.exp(g_in - lw), kt=k * en, bt=b * en, bend=b * e_end, kend=k * e_end))

    def hs(name, s, h):
        return pre[s][name][:, h * n:(h + 1) * n]

    xb, xk, xkv, p, x = {}, {}, {}, {}, {}
    for q in sysid:
        la = jnp.concatenate([hs("at", *q), hs("rt", *q)], axis=0)
        xb[q] = jnp.where(mask2, _bdot_nt(la, hs("bt", *q)), 0.0)
        xk[q] = jnp.where(mask2, _bdot_nt(la, hs("kt", *q)), 0.0)
    for q in sysid:
        xkv[q] = _bdot(xk[q], hs("v", *q))
        p[q] = xb[q][0:gl]
        x[q] = jnp.concatenate([hs("at", *q), xkv[q][0:gl]], axis=1)
    for i in range(6):
        for q in sysid:
            x[q] = x[q] + _bdot(p[q], x[q])
        if i < 5:
            for q in sysid:
                p[q] = _bdot(p[q], p[q])
    rq, y0, mn = {}, {}, {}
    for q in sysid:
        yy = _bdot(xb[q][gl:], x[q])
        rq[q] = yy[:, 0:n] + hs("rt", *q)
        y0[q] = yy[:, n:] + xkv[q][gl:]
        bend_h, kend_h, v_h = hs("bend", *q), hs("kend", *q), hs("v", *q)
        for c in range(nchunk):
            rows = slice(c * c_len, (c + 1) * c_len)
            lhs = jnp.concatenate([bend_h[rows], kend_h[rows]], axis=0)
            rhs = jnp.concatenate([x[q][rows], jnp.concatenate([zeros, v_h[rows]], axis=1)], axis=0)
            mn[q + (c,)] = _bdot_tn(lhs, rhs)
    zero_n = jnp.zeros((n, n), F32)
    comp = {q: [] for q in sysid}
    for c in range(nchunk):
        for q in sysid:
            s, h = q
            pend = jnp.exp(pre[s]["ends"][c][:, h * n:(h + 1) * n])
            m_mat = mn[q + (c,)][:, 0:n] + jnp.where(eye, pend, 0.0)
            n_mat = mn[q + (c,)][:, n:]
            if c == 0:
                comp[q].append(jnp.concatenate([m_mat, n_mat], axis=1))
            else:
                comp[q].append(_bdot(m_mat, comp[q][c - 1]) + jnp.concatenate([zero_n, n_mat], axis=1))
    zs = [z_ref[h] for h in range(nheads)]
    for s in range(nsub):
        yh = []
        for h in range(nheads):
            q = (s, h)
            pm = jnp.concatenate([comp[q][c][:, 0:n] for c in range(nchunk)], axis=0)
            pn = jnp.concatenate([comp[q][c][:, n:] for c in range(nchunk)], axis=0)
            after = _bdot(pm, zs[h]) + pn
            z_in = [zs[h]] + [after[c * c_len:(c + 1) * c_len] for c in range(nchunk - 1)]
            yh.append(jnp.concatenate(
                [_bdot(rq[q][c * c_len:(c + 1) * c_len], z_in[c]) + y0[q][c * c_len:(c + 1) * c_len]
                 for c in range(nchunk)], axis=0))
            zs[h] = after[(nchunk - 1) * c_len:]
        y_ref[pl.ds(s * gl, gl), :] = jnp.concatenate(yh, axis=1)
    for h in range(nheads):
        z_ref[h] = zs[h]


def _rwkv_chunk(r, k, v, lw, kkn, b, bsz, seq):
    t = r.shape[0]
    lt = _tile(seq, 1024)
    nt = seq // lt
    spec = pl.BlockSpec((lt, LANES), lambda bi, hp, i: (bi * nt + i, hp))
    return pl.pallas_call(
        functools.partial(_rwkv_chunk_kernel, lt=lt),
        grid=(bsz, RWKV_DIM // LANES, nt),
        in_specs=[spec] * 6,
        out_specs=spec,
        out_shape=jax.ShapeDtypeStruct((t, RWKV_DIM), F32),
        scratch_shapes=[pltpu.VMEM((2, HEAD_DIM, HEAD_DIM), F32)],
        compiler_params=_params("arbitrary", "arbitrary", "arbitrary"),
        name="rwkv_chunk",
    )(r, k, v, lw, kkn, b)


def _ssd_kernel(pb_ref, cw_ref, cb_ref, dtb_ref, alog_ref, dskip_ref, nrm_ref, ys_ref, ext_ref, st_ref, *, nck):
    q = SSD_CHUNK

    @pl.when(pl.program_id(1) == 0)
    def _():
        ext_ref[0:8, :] = jnp.zeros((8, SSD_XBC), F32)
        st_ref[...] = jnp.zeros_like(st_ref)

    ext_ref[8:8 + nck * q, :] = pb_ref[:, SSD_DIM:SSD_DIM + SSD_XBC]
    for c in range(nck):
        _ssd_chunk(pb_ref, cw_ref, cb_ref, dtb_ref, alog_ref, dskip_ref, nrm_ref, ys_ref, ext_ref, st_ref, c * q)
    ext_ref[0:8, :] = ext_ref[nck * q:nck * q + 8, :]


def _ssd_chunk(pb_ref, cw_ref, cb_ref, dtb_ref, alog_ref, dskip_ref, nrm_ref, ys_ref, ext_ref, st_ref, r0):
    q = SSD_CHUNK
    p = HEAD_DIM
    hpg = SSD_HEADS // SSD_GROUPS
    z = pb_ref[r0:r0 + q, 0:SSD_DIM]
    u = ext_ref[8 + r0:8 + r0 + q, :]
    dt_raw = pb_ref[r0:r0 + q, SSD_DIM + SSD_XBC:]

    conv = cb_ref[...] + cw_ref[SSD_CONV - 1:SSD_CONV, :] * u
    for j in range(SSD_CONV - 1):
        off = r0 + 8 - (SSD_CONV - 1) + j
        conv = conv + cw_ref[j:j + 1, :] * ext_ref[off:off + q, :]
    xbc = _silu(conv)
    xs = xbc[:, 0:SSD_DIM]
    bm = xbc[:, SSD_DIM:SSD_DIM + SSD_GROUPS * SSD_STATE]
    cm = xbc[:, SSD_DIM + SSD_GROUPS * SSD_STATE:]

    dt = _softplus(dt_raw + dtb_ref[...])
    a = -jnp.exp(alog_ref[...])
    ri = lax.broadcasted_iota(jnp.int32, (q, q), 0)
    ci = lax.broadcasted_iota(jnp.int32, (q, q), 1)
    causal = ri >= ci
    cum = _dot_exact_lhs(causal.astype(BF16), dt * a)
    cum_t = cum.T
    dt_t = dt.T
    cum_end = cum[q - 1:q, :]
    to_end = jnp.exp(cum_end - cum) * dt
    ecum = jnp.exp(cum)
    edec = jnp.exp(cum_end)

    ys = []
    for g in range(SSD_GROUPS):
        bm_g = bm[:, g * SSD_STATE:(g + 1) * SSD_STATE]
        cm_g = cm[:, g * SSD_STATE:(g + 1) * SSD_STATE]
        cb = _bdot_nt(cm_g, bm_g)
        bm_t = bm_g.T
        for hh in range(hpg):
            h = g * hpg + hh
            x_h = xs[:, h * p:(h + 1) * p]
            seg = cum[:, h:h + 1] - cum_t[h:h + 1, :]
            ldec = jnp.exp(jnp.where(causal, seg, -jnp.inf))
            wts = cb * ldec * dt_t[h:h + 1, :]
            y = _bdot(wts, x_h)
            h_prev = st_ref[h]
            y = y + _bdot(cm_g, h_prev) * ecum[:, h:h + 1]
            st_ref[h] = h_prev * edec[:, h:h + 1] + _bdot(bm_t, x_h * to_end[:, h:h + 1])
            ys.append(y + dskip_ref[:, h:h + 1] * x_h)
    yall = jnp.concatenate(ys, axis=1) * _silu(z)
    gw = SSD_DIM // SSD_GROUPS
    outs = []
    for g in range(SSD_GROUPS):
        yg = yall[:, g * gw:(g + 1) * gw]
        ms = jnp.mean(yg * yg, axis=-1, keepdims=True)
        outs.append(yg * lax.rsqrt(ms + NORM_EPS) * nrm_ref[:, g * gw:(g + 1) * gw])
    ys_ref[r0:r0 + q, :] = jnp.concatenate(outs, axis=1)


def _ssd(pb, bsz, seq, conv_w, conv_b, dt_bias, a_log, d_skip, ssd_norm):
    t = pb.shape[0]
    nck = 2 if seq % (2 * SSD_CHUNK) == 0 else 1
    rows = nck * SSD_CHUNK
    nc = seq // rows
    row = lambda b, i: (b * nc + i, 0)
    return pl.pallas_call(
        functools.partial(_ssd_kernel, nck=nck),
        grid=(bsz, nc),
        in_specs=[pl.BlockSpec((rows, SSD_COLS_PAD), row), _full(conv_w.shape), _full(conv_b.shape),
                  _full(dt_bias.shape), _full(a_log.shape), _full(d_skip.shape), _full(ssd_norm.shape)],
        out_specs=pl.BlockSpec((rows, SSD_DIM), row),
        out_shape=jax.ShapeDtypeStruct((t, SSD_DIM), F32),
        scratch_shapes=[pltpu.VMEM((8 + rows, SSD_XBC), F32),
                        pltpu.VMEM((SSD_HEADS, SSD_STATE, HEAD_DIM), F32)],
        compiler_params=_params("arbitrary", "arbitrary"),
        name="ssd",
    )(pb, conv_w, conv_b, dt_bias, a_log, d_skip, ssd_norm)


def _mix_ffn_kernel(h_ref, y_ref, bonus_ref, gate_ref, ys_ref, gnw_ref, gnb_ref, seg_ref, wa_ref, wb_ref,
                    g_ref, wg_ref, wu_ref, wd_ref, o_ref):
    y = y_ref[...]
    seg = seg_ref[...]
    mu = _head_sums(y, seg) * (1.0 / HEAD_DIM)
    yc = y - mu
    var = _head_sums(yc * yc, seg) * (1.0 / HEAD_DIM)
    yn = yc * lax.rsqrt(var + RWKV_GN_EPS) * gnw_ref[...] + gnb_ref[...]
    ya = (yn + bonus_ref[...]) * gate_ref[...]
    h = h_ref[...] + _bdot(ya, wa_ref[...]) + _bdot(ys_ref[...], wb_ref[...])

    xn = _rms(h, g_ref[...]).astype(BF16)
    act = _silu(jnp.dot(xn, wg_ref[...], preferred_element_type=F32)) * jnp.dot(
        xn, wu_ref[...], preferred_element_type=F32)
    o_ref[...] = h + jnp.dot(act.astype(BF16), wd_ref[...], preferred_element_type=F32)


def _mix_ffn(h2, y, bonus, gate, ys, gn_w, gn_b, seg, wa, wb, g, wg, wu, wd):
    t = h2.shape[0]
    tm = _tile(t, 512)
    row = lambda i: (i, 0)
    half = pl.BlockSpec((tm, RWKV_DIM), row)
    return pl.pallas_call(
        _mix_ffn_kernel,
        grid=(t // tm,),
        in_specs=[pl.BlockSpec((tm, D_MODEL), row), half, half, half, half, _full(gn_w.shape),
                  _full(gn_b.shape), _resident(seg.shape), _resident(wa.shape), _resident(wb.shape),
                  _full((1, D_MODEL)), _resident(wg.shape), _resident(wu.shape), _resident(wd.shape)],
        out_specs=pl.BlockSpec((tm, D_MODEL), row),
        out_shape=jax.ShapeDtypeStruct((t, D_MODEL), F32),
        compiler_params=_params("parallel"),
        name="mix_ffn",
    )(h2, y, bonus, gate, ys, gn_w, gn_b, seg, wa, wb, g, wg, wu, wd)


def _rope_table_kernel(pos_ref, freq_ref, cos_ref, sin_ref):
    ang = pos_ref[...] * freq_ref[...]
    cos_ref[...] = jnp.cos(ang)
    sin_ref[...] = jnp.sin(ang)


def _rope_tables(positions):
    t = positions.size
    half = ROPE_DIM // 2
    rows = t * half // LANES
    inv_freq = ROPE_THETA ** (-jnp.arange(0, ROPE_DIM, 2, dtype=F32) / ROPE_DIM)
    pos_rep = jnp.repeat(positions.reshape(-1).astype(F32), half).reshape(rows, LANES)
    freq = jnp.tile(inv_freq, LANES // half).reshape(1, LANES)
    tr = _tile(rows, 256)
    cos, sin = pl.pallas_call(
        _rope_table_kernel,
        grid=(rows // tr,),
        in_specs=[pl.BlockSpec((tr, LANES), lambda i: (i, 0)), _full((1, LANES))],
        out_specs=[pl.BlockSpec((tr, LANES), lambda i: (i, 0))] * 2,
        out_shape=[jax.ShapeDtypeStruct((rows, LANES), F32)] * 2,
        compiler_params=_params("parallel"),
        name="rope_tables",
    )(pos_rep, freq)
    return jnp.concatenate([cos.reshape(t, half) - 1.0, sin.reshape(t, half)], axis=1)


def _rope_expansion(width):
    half = ROPE_DIM // 2
    d = jnp.arange(width) % HEAD_DIM
    j = jnp.arange(half)[:, None]
    zero = jnp.zeros((half, width), F32)
    cos_rows = jnp.where((d[None, :] < ROPE_DIM) & (d[None, :] % half == j), 1.0, 0.0)
    sin_a = jnp.where((d[None, :] < half) & (d[None, :] == j), -1.0, 0.0)
    sin_b = jnp.where((d[None, :] >= half) & (d[None, :] < ROPE_DIM) & (d[None, :] - half == j), 1.0, 0.0)
    one = jnp.concatenate([jnp.concatenate([cos_rows, zero, zero], axis=1),
                           jnp.concatenate([zero, sin_a, sin_b], axis=1)], axis=0)
    return jnp.concatenate([one, one], axis=0).astype(BF16)


def _qkv_kernel(h_ref, g_ref, w_ref, b_ref, qn_ref, kn_ref, cs_ref, exp_ref, seg_ref, q_ref, k_ref, v_ref):
    xn = _rms(h_ref[...], g_ref[...]).astype(BF16)
    qkv = jnp.dot(xn, w_ref[...], preferred_element_type=F32) + b_ref[...]
    seg = seg_ref[...]
    wide = seg.shape[0]
    hi, lo = _split(cs_ref[...])
    tab = jnp.dot(jnp.concatenate([hi, lo], axis=1), exp_ref[...], preferred_element_type=F32)
    cos = 1.0 + tab[:, 0:wide]
    sa = tab[:, wide:2 * wide]
    sb = tab[:, 2 * wide:]

    def norm_rope(x, gain):
        ms = _head_sums(x * x, seg) * (1.0 / HEAD_DIM)
        xn_ = x * lax.rsqrt(ms + NORM_EPS) * gain
        return (xn_ * cos + pltpu.roll(xn_, wide - ROPE_DIM // 2, axis=1) * sa
                + pltpu.roll(xn_, ROPE_DIM // 2, axis=1) * sb)

    for c in range(Q_DIM // wide):
        x = qkv[:, c * wide:(c + 1) * wide]
        q_ref[:, c * wide:(c + 1) * wide] = (norm_rope(x, qn_ref[...]) * (HEAD_DIM ** -0.5)).astype(BF16)
    for c in range(KV_DIM // wide):
        x = qkv[:, Q_DIM + c * wide:Q_DIM + (c + 1) * wide]
        k_ref[:, c * wide:(c + 1) * wide] = norm_rope(x, kn_ref[...]).astype(BF16)
    v_ref[...] = qkv[:, Q_DIM + KV_DIM:].astype(BF16)


def _qkv(h2, g, w, b, qn, kn, cs, expand, seg):
    t = h2.shape[0]
    tm = _tile(t, 512)
    row = lambda i: (i, 0)
    return pl.pallas_call(
        _qkv_kernel,
        grid=(t // tm,),
        in_specs=[pl.BlockSpec((tm, D_MODEL), row), _full((1, D_MODEL)), _resident(w.shape), _full(b.shape),
                  _full(qn.shape), _full(kn.shape), pl.BlockSpec((tm, cs.shape[1]), row), _full(expand.shape),
                  _full(seg.shape)],
        out_specs=[pl.BlockSpec((tm, Q_DIM), row), pl.BlockSpec((tm, KV_DIM), row),
                   pl.BlockSpec((tm, KV_DIM), row)],
        out_shape=[jax.ShapeDtypeStruct((t, Q_DIM), BF16), jax.ShapeDtypeStruct((t, KV_DIM), BF16),
                   jax.ShapeDtypeStruct((t, KV_DIM), BF16)],
        compiler_params=_params("parallel"),
        name="qkv",
    )(h2, g, w, b, qn, kn, cs, expand, seg)


def _attn_kernel(h_ref, q_ref, kc_ref, kp_ref, vc_ref, vp_ref, sink_ref, wo_ref, bo_ref, o_ref, *, nqb):
    w = WINDOW
    n = HEAD_DIM
    rows = GQA_GROUP * w
    qi = lax.broadcasted_iota(jnp.int32, (rows, 2 * w), 0) % w
    kj = lax.broadcasted_iota(jnp.int32, (rows, 2 * w), 1)
    diff = w + qi - kj
    band = (diff >= 0) & (diff < w)
    band0 = band & (kj >= jnp.where(pl.program_id(1) == 0, w, 0))
    kall = jnp.concatenate([kp_ref[...], kc_ref[...]], axis=0)
    vall = jnp.concatenate([vp_ref[...], vc_ref[...]], axis=0)
    units = [(qb, g) for qb in range(nqb) for g in range(KV_HEADS)]
    s, e, den = {}, {}, {}
    for u in units:
        qb, g = u
        qs = jnp.concatenate(
            [q_ref[qb * w:(qb + 1) * w, (g * GQA_GROUP + i) * n:(g * GQA_GROUP + i + 1) * n]
             for i in range(GQA_GROUP)], axis=0)
        kk = kall[qb * w:(qb + 2) * w, g * n:(g + 1) * n]
        sc = lax.dot_general(qs, kk, (((1,), (1,)), ((), ())), preferred_element_type=F32)
        s[u] = jnp.where(band0 if qb == 0 else band, sc, -jnp.inf)
    sinks = [jnp.concatenate(
        [jnp.broadcast_to(sink_ref[:, g * GQA_GROUP + i:g * GQA_GROUP + i + 1], (w, LANES))
         for i in range(GQA_GROUP)], axis=0) for g in range(KV_HEADS)]
    ones = jnp.ones((2 * w, LANES - n), BF16)
    for u in units:
        sink = sinks[u[1]]
        m = jnp.maximum(jnp.broadcast_to(jnp.max(s[u], axis=-1, keepdims=True), (rows, LANES)), sink)
        e[u] = jnp.exp(s[u] - jnp.concatenate([m, m], axis=1)).astype(BF16)
        den[u] = jnp.exp(sink - m)
    outs = [[None] * Q_HEADS for _ in range(nqb)]
    for u in units:
        qb, g = u
        vv = jnp.concatenate([vall[qb * w:(qb + 2) * w, g * n:(g + 1) * n], ones], axis=1)
        both = jnp.dot(e[u], vv, preferred_element_type=F32)
        total = pltpu.roll(both, LANES - n, axis=1) + den[u]
        o = both[:, 0:n] / total[:, 0:n]
        for i in range(GQA_GROUP):
            outs[qb][g * GQA_GROUP + i] = o[i * w:(i + 1) * w]
    att = jnp.concatenate([jnp.concatenate(outs[qb], axis=1) for qb in range(nqb)], axis=0)
    o_ref[...] = h_ref[...] + _bdot(att, wo_ref[...]) + bo_ref[...]


def _attn(h2, q, k, v, sinks, wo, bo, bsz, seq):
    t = h2.shape[0]
    nqb = 4 if seq % (4 * WINDOW) == 0 else 1
    tq = nqb * WINDOW
    nb = seq // tq
    cur = lambda b, i: (b * nb + i, 0)
    prv = lambda b, i: (b * nb * nqb + jnp.maximum(i * nqb - 1, 0), 0)
    return pl.pallas_call(
        functools.partial(_attn_kernel, nqb=nqb),
        grid=(bsz, nb),
        in_specs=[pl.BlockSpec((tq, D_MODEL), cur), pl.BlockSpec((tq, Q_DIM), cur),
                  pl.BlockSpec((tq, KV_DIM), cur), pl.BlockSpec((WINDOW, KV_DIM), prv),
                  pl.BlockSpec((tq, KV_DIM), cur), pl.BlockSpec((WINDOW, KV_DIM), prv),
                  _full(sinks.shape), _full(wo.shape), _full(bo.shape)],
        out_specs=pl.BlockSpec((tq, D_MODEL), cur),
        out_shape=jax.ShapeDtypeStruct((t, D_MODEL), F32),
        compiler_params=_params("parallel", "parallel"),
        name="attn",
    )(h2, q, k, k, v, v, sinks, wo, bo)


def _router_kernel(h_ref, g_ref, wr_ref, xn_ref, meta_ref, cnt_ref, base_ref):
    @pl.when(pl.program_id(0) == 0)
    def _():
        base_ref[...] = jnp.zeros_like(base_ref)

    xn = _rms(h_ref[...], g_ref[...])
    xn_ref[...] = xn
    hi, lo = _split(xn)
    whi = wr_ref[0]
    wlo = wr_ref[1]
    logits = (jnp.dot(hi, whi, preferred_element_type=F32) + jnp.dot(hi, wlo, preferred_element_type=F32)
              + jnp.dot(lo, whi, preferred_element_type=F32))
    tm = logits.shape[0]
    lane = lax.broadcasted_iota(jnp.int32, logits.shape, 1)
    logits = jnp.where(lane < N_EXPERTS, logits, -jnp.inf)
    m1 = jnp.max(logits, axis=-1, keepdims=True)
    i1 = jnp.min(jnp.where(logits == m1, lane, LANES), axis=-1, keepdims=True)
    rest = jnp.where(lane == i1, -jnp.inf, logits)
    m2 = jnp.max(rest, axis=-1, keepdims=True)
    i2 = jnp.min(jnp.where(rest == m2, lane, LANES), axis=-1, keepdims=True)
    e2 = jnp.exp(m2 - m1)
    w1 = 1.0 / (1.0 + e2)
    w2 = e2 / (1.0 + e2)

    sel1 = lane == i1
    sel2 = lane == i2
    onehot = jnp.where(sel1, 1.0, 0.0) + jnp.where(sel2, 1.0, 0.0)
    ri = lax.broadcasted_iota(jnp.int32, (tm, tm), 0)
    ci = lax.broadcasted_iota(jnp.int32, (tm, tm), 1)
    before = jnp.dot(jnp.where(ci < ri, 1.0, 0.0).astype(BF16), onehot.astype(BF16),
                     preferred_element_type=F32)
    rank = base_ref[...] + before
    r1 = jnp.sum(jnp.where(sel1, rank, 0.0), axis=-1, keepdims=True)
    r2 = jnp.sum(jnp.where(sel2, rank, 0.0), axis=-1, keepdims=True)
    total = base_ref[...] + jnp.sum(onehot, axis=0, keepdims=True)
    base_ref[...] = total
    cnt_ref[...] = total
    cols = (i1.astype(F32), i2.astype(F32), w1, w2, r1, r2)
    meta = jnp.zeros(logits.shape, F32)
    for c, val in enumerate(cols):
        meta = jnp.where(lane == c, val, meta)
    meta_ref[...] = meta


def _router(h2, g, wr):
    t = h2.shape[0]
    tm = _tile(t, 512)
    row = lambda i: (i, 0)
    return pl.pallas_call(
        _router_kernel,
        grid=(t // tm,),
        in_specs=[pl.BlockSpec((tm, D_MODEL), row), _full((1, D_MODEL)), _full(wr.shape)],
        out_specs=[pl.BlockSpec((tm, D_MODEL), row), pl.BlockSpec((tm, LANES), row), _full((1, LANES))],
        out_shape=[jax.ShapeDtypeStruct((t, D_MODEL), F32), jax.ShapeDtypeStruct((t, LANES), F32),
                   jax.ShapeDtypeStruct((1, LANES), F32)],
        scratch_shapes=[pltpu.VMEM((1, LANES), F32)],
        compiler_params=_params("arbitrary"),
        name="router",
    )(h2, g, wr)


def _moe_index_kernel(p1_ref, p2_ref, pad_ref, src_ref, dst_ref, *, ntok):
    tm = MOE_TM
    nrow = src_ref.shape[0]

    def virtual(j, c):
        dst_ref[j] = nrow + j
        return c

    lax.fori_loop(0, tm, virtual, 0, unroll=8)

    for e in range(N_EXPERTS):
        lo = pad_ref[2 * e]
        base = 2 * ntok + e * tm - lo

        def fill(p, c, base=base):
            src_ref[p] = 0
            dst_ref[tm + p] = base + p
            return c

        lax.fori_loop(lo, pad_ref[2 * e + 1], fill, 0)

    def unused(p, c):
        src_ref[p] = 0
        dst_ref[tm + p] = 2 * ntok + (p & (tm - 1))
        return c

    lax.fori_loop(pad_ref[2 * N_EXPERTS], pad_ref[2 * N_EXPERTS + 1], unused, 0)

    def body(t, c):
        p1 = p1_ref[t]
        p2 = p2_ref[t]
        src_ref[p1] = t
        dst_ref[tm + p1] = t
        src_ref[p2] = t
        dst_ref[tm + p2] = ntok + t
        return c

    lax.fori_loop(0, ntok, body, 0, unroll=4)


def _moe_index(pos1, pos2, pad, rows):
    smem = pl.BlockSpec(memory_space=pltpu.SMEM)
    return pl.pallas_call(
        functools.partial(_moe_index_kernel, ntok=pos1.shape[0]),
        grid_spec=pltpu.PrefetchScalarGridSpec(
            num_scalar_prefetch=3, grid=(1,), in_specs=[], out_specs=[smem, smem]),
        out_shape=[jax.ShapeDtypeStruct((rows,), jnp.int32), jax.ShapeDtypeStruct((rows + MOE_TM,), jnp.int32)],
        compiler_params=_params("arbitrary"),
        name="moe_index",
    )(pos1, pos2, pad)


def _row_copy(src_ref, src_row, dst_ref, dst_row, sem):
    return pltpu.make_async_copy(src_ref.at[pl.ds(src_row, 1)], dst_ref.at[pl.ds(dst_row, 1)], sem)


def _experts_kernel(te_ref, nused_ref, src_ref, dst_ref, x_ref, wg_ref, wu_ref, wd_ref, y_ref,
                    xbuf, ybuf, wgb, wub, wdb, gsem, ssem):
    i = pl.program_id(0)
    nused = nused_ref[0]
    tm = MOE_TM

    sub = MOE_SUB
    ngroup = tm // sub

    def gather_group(tile, slot, k):
        for u in range(sub):
            _row_copy(x_ref, src_ref[tile * tm + k * sub + u], xbuf.at[slot, k], u, gsem.at[slot]).start()

    def scatter_group(tile, slot, k):
        for u in range(sub):
            _row_copy(ybuf.at[slot, k], u, y_ref, dst_ref[(tile + 1) * tm + k * sub + u], ssem.at[slot]).start()

    def wait_gather(slot):
        def body(k, c):
            pltpu.make_async_copy(x_ref.at[pl.ds(0, sub)], xbuf.at[slot, k], gsem.at[slot]).wait()
            return c
        lax.fori_loop(0, ngroup, body, 0)

    def wait_scatter(slot):
        def body(k, c):
            pltpu.make_async_copy(ybuf.at[slot, k], y_ref.at[pl.ds(0, sub)], ssem.at[slot]).wait()
            return c
        lax.fori_loop(0, ngroup, body, 0)

    @pl.when(i == 0)
    def _():
        ybuf[1] = jnp.zeros(ybuf.shape[1:], F32)

        def body(k, c):
            gather_group(0, 0, k)
            return c
        lax.fori_loop(0, ngroup, body, 0)

    @pl.when(i < nused)
    def _():
        slot = i % 2
        other = 1 - slot
        nxt = jnp.minimum(i + 1, nused - 1)
        wait_gather(slot)

        @pl.when(i >= 1)
        def _():
            wait_scatter(slot)

        @pl.when((i == 0) | (te_ref[i] != te_ref[jnp.maximum(i - 1, 0)]))
        def _():
            wgb[...] = wg_ref[0].astype(BF16)
            wub[...] = wu_ref[0].astype(BF16)
            wdb[...] = wd_ref[0].astype(BF16)

        xb = xbuf[slot].reshape(tm, D_MODEL).astype(BF16)
        bounds = list(range(0, FFN_EXPERT, MOE_FCHUNK)) + [FFN_EXPERT]
        nchunk = len(bounds) - 1
        acts = []
        for c in range(nchunk):
            c0, c1 = bounds[c], bounds[c + 1]
            gate = jnp.dot(xb, wgb[:, c0:c1], preferred_element_type=F32)
            up = jnp.dot(xb, wub[:, c0:c1], preferred_element_type=F32)
            acts.append((_silu(gate) * up).astype(BF16))
            for k in range(c * ngroup // nchunk, (c + 1) * ngroup // nchunk):
                gather_group(nxt, other, k)
                scatter_group(i - 1, other, k)
        y = jnp.dot(jnp.concatenate(acts, axis=1), wdb[...], preferred_element_type=F32)
        ybuf[slot] = y.reshape(ngroup, sub, D_MODEL)

        @pl.when(i == nused - 1)
        def _():
            def body(k, c):
                scatter_group(i, slot, k)
                return c
            lax.fori_loop(0, ngroup, body, 0)
            wait_scatter(other)
            wait_scatter(slot)
            wait_gather(other)


def _experts(tile_expert, nused, src, dst, xn, wg, wu, wd, out_rows):
    ntile = tile_expert.shape[0]
    tm = MOE_TM
    wsel = lambda i, te, nu, s, d: (te[i], 0, 0)
    any_spec = pl.BlockSpec(memory_space=pl.ANY)
    return pl.pallas_call(
        _experts_kernel,
        grid_spec=pltpu.PrefetchScalarGridSpec(
            num_scalar_prefetch=4, grid=(ntile,),
            in_specs=[any_spec,
                      pl.BlockSpec((1, D_MODEL, FFN_EXPERT), wsel),
                      pl.BlockSpec((1, D_MODEL, FFN_EXPERT), wsel),
                      pl.BlockSpec((1, FFN_EXPERT, D_MODEL), wsel)],
            out_specs=any_spec,
            scratch_shapes=[pltpu.VMEM((2, tm // MOE_SUB, MOE_SUB, D_MODEL), F32),
                            pltpu.VMEM((2, tm // MOE_SUB, MOE_SUB, D_MODEL), F32),
                            pltpu.VMEM((D_MODEL, FFN_EXPERT), BF16), pltpu.VMEM((D_MODEL, FFN_EXPERT), BF16),
                            pltpu.VMEM((FFN_EXPERT, D_MODEL), BF16),
                            pltpu.SemaphoreType.DMA((2,)), pltpu.SemaphoreType.DMA((2,))]),
        out_shape=jax.ShapeDtypeStruct((out_rows, D_MODEL), F32),
        compiler_params=_params("arbitrary"),
        name="moe_experts",
    )(tile_expert, nused, src, dst, xn, wg, wu, wd)


def _combine_kernel(h_ref, meta_ref, y1_ref, y2_ref, o_ref):
    meta = meta_ref[...]
    o_ref[...] = h_ref[...] + meta[:, 2:3] * y1_ref[...] + meta[:, 3:4] * y2_ref[...]


def _combine(h2, meta, ys):
    t = h2.shape[0]
    tm = _tile(t, 512)
    nt = t // tm
    row = lambda i: (i, 0)
    return pl.pallas_call(
        _combine_kernel,
        grid=(nt,),
        in_specs=[pl.BlockSpec((tm, D_MODEL), row), pl.BlockSpec((tm, LANES), row),
                  pl.BlockSpec((tm, D_MODEL), row), pl.BlockSpec((tm, D_MODEL), lambda i: (nt + i, 0))],
        out_specs=pl.BlockSpec((tm, D_MODEL), row),
        out_shape=jax.ShapeDtypeStruct((t, D_MODEL), F32),
        compiler_params=_params("parallel"),
        name="moe_combine",
    )(h2, meta, ys, ys)


def _moe(h2, g, router, wg, wu, wd):
    t = h2.shape[0]
    tm = MOE_TM
    wr = jnp.pad(router, ((0, 0), (0, LANES - N_EXPERTS)))
    wr_hi = wr.astype(BF16)
    wr_lo = (wr - wr_hi.astype(F32)).astype(BF16)
    xn, meta, cnt = _router(h2, g, jnp.stack([wr_hi, wr_lo]))

    cols = meta[:, :8].T.astype(jnp.int32)
    e1, e2, r1, r2 = cols[0], cols[1], cols[4], cols[5]
    count = cnt[0, :N_EXPERTS].astype(jnp.int32)
    ntile = (count + tm - 1) // tm
    tile_end = jnp.cumsum(ntile)
    off = (tile_end - ntile) * tm
    rows = 2 * t + N_EXPERTS * tm
    tile_id = jnp.arange(rows // tm)
    tile_expert = jnp.minimum(
        jnp.sum(tile_id[:, None] >= tile_end[None, :], axis=1), N_EXPERTS - 1).astype(jnp.int32)
    nused = tile_end[-1:].astype(jnp.int32)
    expert = jnp.arange(N_EXPERTS, dtype=jnp.int32)[:, None]
    pos1 = r1 + jnp.sum(jnp.where(e1[None, :] == expert, off[:, None], 0), axis=0)
    pos2 = r2 + jnp.sum(jnp.where(e2[None, :] == expert, off[:, None], 0), axis=0)
    pad = jnp.stack([jnp.append(off + count, tile_end[-1] * tm),
                     jnp.append(off + ntile * tm, rows)], axis=1).reshape(-1).astype(jnp.int32)
    src, dst = _moe_index(pos1, pos2, pad, rows)

    ys = _experts(tile_expert, nused, src, dst, xn, wg, wu, wd, rows + tm)
    return _combine(h2, meta, ys)


def _segment_matrix(width):
    i = jnp.arange(width) // HEAD_DIM
    return (i[:, None] == i[None, :]).astype(BF16)


def _layer0(h2, bsz, seq, norm_mix, w_in, mu_shift, w0, w_decay_up, a0, w_iclr_up, w_gate_up, k_k, k_a, r_k,
            gn_w, gn_b, conv_w, conv_b, dt_bias, a_log, d_skip, ssd_norm, w_out, norm_ffn, ffn_gate, ffn_up,
            ffn_down):
    row = lambda a: a.reshape(1, -1)
    seg = _segment_matrix(SEG_WIDTH)
    wa = w_in[:, :RWKV_COLS].astype(BF16)
    wb = jnp.pad(w_in[:, RWKV_COLS:], ((0, 0), (0, SSD_COLS_PAD - SSD_COLS))).astype(BF16)
    lora = w_decay_up.shape[0]
    wd = jnp.concatenate([w_decay_up, jnp.zeros((LANES - lora, RWKV_DIM), F32)], axis=0).astype(BF16)
    wi = jnp.concatenate([jnp.zeros((LANES - lora, RWKV_DIM), F32), w_iclr_up], axis=0).astype(BF16)
    pb, r, k, v, lw, kkn, b, g, bonus = _in_proj(
        h2, bsz, seq, row(norm_mix), wa, wb, row(mu_shift), row(w0), wd, row(a0), wi, w_gate_up.astype(BF16),
        row(k_k), row(k_a), row(r_k), seg)
    y = _rwkv_chunk(r, k, v, lw, kkn, b, bsz, seq)

    lane_pad = lambda a: jnp.pad(row(a), ((0, 0), (0, LANES - a.size)))
    ys = _ssd(pb, bsz, seq, conv_w, row(conv_b), lane_pad(dt_bias), lane_pad(a_log), lane_pad(d_skip),
              row(ssd_norm))

    w_out = w_out.astype(BF16)
    return _mix_ffn(h2, y, bonus, g, ys, row(gn_w), row(gn_b), seg, w_out[:RWKV_DIM], w_out[RWKV_DIM:],
                    row(norm_ffn), ffn_gate.astype(BF16), ffn_up.astype(BF16), ffn_down.astype(BF16))


def _layer1(h2, bsz, seq, tables, norm_mix, w_qkv, b_qkv, q_norm, k_norm, sinks, w_o, b_o, norm_ffn, router,
            exp_gate, exp_up, exp_down):
    row = lambda a: a.reshape(1, -1)
    wide = lambda a: jnp.tile(a, SEG_WIDTH // HEAD_DIM).reshape(1, SEG_WIDTH)
    q, k, v = _qkv(h2, row(norm_mix), w_qkv.astype(BF16), row(b_qkv), wide(q_norm), wide(k_norm), tables,
                   _rope_expansion(SEG_WIDTH), _segment_matrix(SEG_WIDTH))
    h2 = _attn(h2, q, k, v, row(sinks), w_o.astype(BF16), row(b_o), bsz, seq)

    return _moe(h2, row(norm_ffn), router, exp_gate, exp_up, exp_down)


def kernel(x, positions, ev_norm_mix, ev_w_in, ev_mu_shift, ev_w0, ev_w_decay_up, ev_a0, ev_w_iclr_up, ev_w_gate_up, ev_k_k, ev_k_a, ev_r_k, ev_gn_w, ev_gn_b, ev_conv_w, ev_conv_b, ev_dt_bias, ev_a_log, ev_d_skip, ev_ssd_norm, ev_w_out, ev_norm_ffn, ev_ffn_gate, ev_ffn_up, ev_ffn_down, od_norm_mix, od_w_qkv, od_b_qkv, od_q_norm, od_k_norm, od_sinks, od_w_o, od_b_o, od_norm_ffn, od_router, od_exp_gate, od_exp_up, od_exp_down):
    bsz, seq, d = x.shape
    depth = ev_norm_mix.shape[0] + od_norm_mix.shape[0]
    tables = _rope_tables(positions)
    h2 = x.reshape(bsz * seq, d)
    for layer in range(depth):
        i = layer // 2
        if layer % 2 == 0:
            h2 = _layer0(h2, bsz, seq, ev_norm_mix[i], ev_w_in[i], ev_mu_shift[i], ev_w0[i], ev_w_decay_up[i],
                         ev_a0[i], ev_w_iclr_up[i], ev_w_gate_up[i], ev_k_k[i], ev_k_a[i], ev_r_k[i],
                         ev_gn_w[i], ev_gn_b[i], ev_conv_w[i], ev_conv_b[i], ev_dt_bias[i], ev_a_log[i],
                         ev_d_skip[i], ev_ssd_norm[i], ev_w_out[i], ev_norm_ffn[i], ev_ffn_gate[i],
                         ev_ffn_up[i], ev_ffn_down[i])
        else:
            h2 = _layer1(h2, bsz, seq, tables, od_norm_mix[i], od_w_qkv[i], od_b_qkv[i], od_q_norm[i],
                         od_k_norm[i], od_sinks[i], od_w_o[i], od_b_o[i], od_norm_ffn[i], od_router[i],
                         od_exp_gate[i], od_exp_up[i], od_exp_down[i])
    return h2.reshape(bsz, seq, d)
```

```python
import functools
import math

import jax
import jax.numpy as jnp
from jax import lax
from jax.experimental import pallas as pl
from jax.experimental.pallas import tpu as pltpu

F32 = jnp.float32
BF16 = jnp.bfloat16

D_MODEL = 1024
HEAD_DIM = 64
NORM_EPS = 1e-6

RWKV_HEADS = 8
RWKV_DIM = 512
RWKV_COLS = 1792
RWKV_GN_EPS = 64e-5
RWKV_CHUNK = 64
RWKV_GROUP = 256

SSD_HEADS = 8
SSD_DIM = 512
SSD_GROUPS = 2
SSD_STATE = 128
SSD_CONV = 4
SSD_CHUNK = 128
SSD_XBC = 1024
SSD_COLS = 1544
SSD_COLS_PAD = 1664

Q_HEADS = 16
KV_HEADS = 4
GQA_GROUP = 4
Q_DIM = 1024
KV_DIM = 256
WINDOW = 128
ROPE_THETA = 500000.0
ROPE_DIM = 16

FFN_DENSE = 2816
N_EXPERTS = 8
FFN_EXPERT = 1408
MOE_TM = 512
MOE_SUB = 8
MOE_FCHUNK = 256

LANES = 128
SEG_WIDTH = 256
VMEM_LIMIT_BYTES = 56 * 1024 * 1024


def _params(*sem):
    return pltpu.CompilerParams(dimension_semantics=sem, vmem_limit_bytes=VMEM_LIMIT_BYTES)


def _bdot(a, b):
    return jnp.dot(a.astype(BF16), b.astype(BF16), preferred_element_type=F32)


def _bdot_nt(a, b):
    return lax.dot_general(a.astype(BF16), b.astype(BF16), (((1,), (1,)), ((), ())),
                           preferred_element_type=F32)


def _bdot_tn(a, b):
    return lax.dot_general(a.astype(BF16), b.astype(BF16), (((0,), (0,)), ((), ())),
                           preferred_element_type=F32)


def _split(x):
    hi = x.astype(BF16)
    lo = (x - hi.astype(F32)).astype(BF16)
    return hi, lo


def _dot_exact_lhs(m, x):
    hi, lo = _split(x)
    return (jnp.dot(m, hi, preferred_element_type=F32) + jnp.dot(m, lo, preferred_element_type=F32))


def _head_sums(x, seg):
    w = seg.shape[0]
    xb = x.astype(BF16)
    return jnp.concatenate(
        [jnp.dot(xb[:, c:c + w], seg, preferred_element_type=F32) for c in range(0, x.shape[1], w)], axis=1)


def _sigmoid(x):
    return 1.0 / (1.0 + jnp.exp(-x))


def _silu(x):
    return x * _sigmoid(x)


def _softplus(x):
    return jnp.maximum(x, 0.0) + jnp.log(1.0 + jnp.exp(-jnp.abs(x)))


def _rms(x, g):
    ms = jnp.mean(x * x, axis=-1, keepdims=True)
    return x * lax.rsqrt(ms + NORM_EPS) * g


def _tile(n, pref):
    t = min(n, pref)
    while n % t:
        t //= 2
    return t


def _full(shape):
    nd = len(shape)
    return pl.BlockSpec(shape, lambda *_: (0,) * nd)


def _resident(shape):
    nd = len(shape)
    return pl.BlockSpec(shape, lambda *_: (0,) * nd, pipeline_mode=pl.Buffered(1))


def _in_proj_kernel(x_ref, g_ref, wa_ref, wb_ref, mu_ref, w0_ref, wd_ref, a0_ref, wi_ref, wg_ref, kk_ref, ka_ref,
                    rk_ref, seg_ref, pb_ref, r_ref, k_ref, v_ref, lw_ref, kkn_ref, b_ref, gate_ref, bonus_ref,
                    carry_ref):
    @pl.when(pl.program_id(1) == 0)
    def _():
        carry_ref[...] = jnp.zeros_like(carry_ref)

    xn = _rms(x_ref[...], g_ref[...]).astype(BF16)
    pb_ref[...] = jnp.dot(xn, wb_ref[...], preferred_element_type=F32)
    pa = jnp.dot(xn, wa_ref[...], preferred_element_type=F32)
    tm = pa.shape[0]
    row = lax.broadcasted_iota(jnp.int32, pa.shape, 0)
    prev = jnp.where(row == 0, carry_ref[...], pltpu.roll(pa, 1, axis=0))
    carry_ref[...] = pa[tm - 1:tm, :]
    x = pa + (prev - pa) * mu_ref[...]

    r = x[:, 0:RWKV_DIM]
    k = x[:, RWKV_DIM:2 * RWKV_DIM]
    v = x[:, 2 * RWKV_DIM:3 * RWKV_DIM]
    lora = x[:, 3 * RWKV_DIM:3 * RWKV_DIM + LANES]
    gl = x[:, 3 * RWKV_DIM + LANES:]
    seg = seg_ref[...]

    w_raw = w0_ref[...] + _bdot(jnp.tanh(lora), wd_ref[...])
    lw_ref[...] = (-math.exp(-0.5)) * _sigmoid(w_raw)
    iclr = _sigmoid(a0_ref[...] + _bdot(lora, wi_ref[...]))
    gate_ref[...] = _bdot(_sigmoid(gl), wg_ref[...])

    kk = k * kk_ref[...]
    kkn = kk * lax.rsqrt(_head_sums(kk * kk, seg) + 1e-12)
    k2 = k * (1.0 + (iclr - 1.0) * ka_ref[...])
    r_ref[...] = r
    k_ref[...] = k2
    v_ref[...] = v
    kkn_ref[...] = kkn
    b_ref[...] = kkn * iclr
    bonus_ref[...] = _head_sums(r * k2 * rk_ref[...], seg) * v


def _in_proj(x2, bsz, seq, g, wa, wb, mu, w0, wd, a0, wi, wg, k_k, k_a, r_k, seg):
    t = x2.shape[0]
    tm = _tile(seq, 512)
    nt = seq // tm
    row = lambda b, i: (b * nt + i, 0)
    out = jax.ShapeDtypeStruct((t, RWKV_DIM), F32)
    small = [mu, w0, wd, a0, wi, wg, k_k, k_a, r_k, seg]
    return pl.pallas_call(
        _in_proj_kernel,
        grid=(bsz, nt),
        in_specs=[pl.BlockSpec((tm, D_MODEL), row), _full((1, D_MODEL)), _resident(wa.shape),
                  _resident(wb.shape)] + [_full(a.shape) for a in small],
        out_specs=[pl.BlockSpec((tm, SSD_COLS_PAD), row)] + [pl.BlockSpec((tm, RWKV_DIM), row)] * 8,
        out_shape=[jax.ShapeDtypeStruct((t, SSD_COLS_PAD), F32)] + [out] * 8,
        scratch_shapes=[pltpu.VMEM((1, RWKV_COLS), F32)],
        compiler_params=_params("arbitrary", "arbitrary"),
        name="in_proj",
    )(x2, g, wa, wb, *small)


def _rwkv_chunk_kernel(r_ref, k_ref, v_ref, lw_ref, kk_ref, b_ref, y_ref, z_ref, *, lt):
    c_len = RWKV_CHUNK
    n = HEAD_DIM
    gl = RWKV_GROUP
    nchunk = gl // c_len
    nheads = LANES // n
    shift = c_len.bit_length() - 1

    @pl.when(pl.program_id(2) == 0)
    def _():
        z_ref[...] = jnp.zeros_like(z_ref)

    ri = lax.broadcasted_iota(jnp.int32, (gl, gl), 0)
    ci = lax.broadcasted_iota(jnp.int32, (gl, gl), 1)
    tri_bd = jnp.where((ci <= ri) & (ci >= ((ri >> shift) << shift)), 1.0, 0.0).astype(BF16)
    re_ = lax.broadcasted_iota(jnp.int32, (c_len, c_len), 0)
    ce_ = lax.broadcasted_iota(jnp.int32, (c_len, c_len), 1)
    eye = re_ == ce_
    ri2 = lax.broadcasted_iota(jnp.int32, (2 * gl, gl), 0)
    ci2 = lax.broadcasted_iota(jnp.int32, (2 * gl, gl), 1)
    t2 = jnp.where(ri2 < gl, ri2, ri2 - gl)
    mask2 = (ci2 <= jnp.where(ri2 < gl, t2 - 1, t2)) & (ci2 >= ((t2 >> shift) << shift))
    zeros = jnp.zeros((c_len, n), F32)

    nsub = lt // gl
    sysid = [(s, h) for s in range(nsub) for h in range(nheads)]
    pre = []
    for s in range(nsub):
        sl = pl.ds(s * gl, gl)
        lw = lw_ref[sl, :]
        g_in = _dot_exact_lhs(tri_bd, lw)
        ends = [g_in[(c + 1) * c_len - 1:(c + 1) * c_len, :] for c in range(nchunk)]
        g_end = jnp.concatenate([jnp.broadcast_to(e, (c_len, LANES)) for e in ends], axis=0)
        e_end = jnp.exp(g_end - g_in)
        en = jnp.exp(-g_in)
        k = k_ref[sl, :]
        b = b_ref[sl, :]
        pre.append(dict(
            ends=ends, v=v_ref[sl, :], rt=r_ref[sl, :] * jnp.exp(g_in),
            at=-kk_ref[sl, :] * jnp.exp(g_in - lw), kt=k * en, bt=b * en, bend=b * e_end, kend=k * e_end))

    def hs(name, s, h):
        return pre[s][name][:, h * n:(h + 1) * n]

    xb, xk, xkv, p, x = {}, {}, {}, {}, {}
    for q in sysid:
        la = jnp.concatenate([hs("at", *q), hs("rt", *q)], axis=0)
        xb[q] = jnp.where(mask2, _bdot_nt(la, hs("bt", *q)), 0.0)
        xk[q] = jnp.where(mask2, _bdot_nt(la, hs("kt", *q)), 0.0)
    for q in sysid:
        xkv[q] = _bdot(xk[q], hs("v", *q))
        p[q] = xb[q][0:gl]
        x[q] = jnp.concatenate([hs("at", *q), xkv[q][0:gl]], axis=1)
    for i in range(6):
        for q in sysid:
            x[q] = x[q] + _bdot(p[q], x[q])
        if i < 5:
            for q in sysid:
                p[q] = _bdot(p[q], p[q])
    rq, y0, mn = {}, {}, {}
    for q in sysid:
        yy = _bdot(xb[q][gl:], x[q])
        rq[q] = yy[:, 0:n] + hs("rt", *q)
        y0[q] = yy[:, n:] + xkv[q][gl:]
        bend_h, kend_h, v_h = hs("bend", *q), hs("kend", *q), hs("v", *q)
        for c in range(nchunk):
            rows = slice(c * c_len, (c + 1) * c_len)
            lhs = jnp.concatenate([bend_h[rows], kend_h[rows]], axis=0)
            rhs = jnp.concatenate([x[q][rows], jnp.concatenate([zeros, v_h[rows]], axis=1)], axis=0)
            mn[q + (c,)] = _bdot_tn(lhs, rhs)
    zero_n = jnp.zeros((n, n), F32)
    comp = {q: [] for q in sysid}
    for c in range(nchunk):
        for q in sysid:
            s, h = q
            pend = jnp.exp(pre[s]["ends"][c][:, h * n:(h + 1) * n])
            m_mat = mn[q + (c,)][:, 0:n] + jnp.where(eye, pend, 0.0)
            n_mat = mn[q + (c,)][:, n:]
            if c == 0:
                comp[q].append(jnp.concatenate([m_mat, n_mat], axis=1))
            else:
                comp[q].append(_bdot(m_mat, comp[q][c - 1]) + jnp.concatenate([zero_n, n_mat], axis=1))
    zs = [z_ref[h] for h in range(nheads)]
    for s in range(nsub):
        yh = []
        for h in range(nheads):
            q = (s, h)
            pm = jnp.concatenate([comp[q][c][:, 0:n] for c in range(nchunk)], axis=0)
            pn = jnp.concatenate([comp[q][c][:, n:] for c in range(nchunk)], axis=0)
            after = _bdot(pm, zs[h]) + pn
            z_in = [zs[h]] + [after[c * c_len:(c + 1) * c_len] for c in range(nchunk - 1)]
            yh.append(jnp.concatenate(
                [_bdot(rq[q][c * c_len:(c + 1) * c_len], z_in[c]) + y0[q][c * c_len:(c + 1) * c_len]
                 for c in range(nchunk)], axis=0))
            zs[h] = after[(nchunk - 1) * c_len:]
        y_ref[pl.ds(s * gl, gl), :] = jnp.concatenate(yh, axis=1)
    for h in range(nheads):
        z_ref[h] = zs[h]


def _rwkv_chunk(r, k, v, lw, kkn, b, bsz, seq):
    t = r.shape[0]
    lt = _tile(seq, 1024)
    nt = seq // lt
    spec = pl.BlockSpec((lt, LANES), lambda bi, hp, i: (bi * nt + i, hp))
    return pl.pallas_call(
        functools.partial(_rwkv_chunk_kernel, lt=lt),
        grid=(bsz, RWKV_DIM // LANES, nt),
        in_specs=[spec] * 6,
        out_specs=spec,
        out_shape=jax.ShapeDtypeStruct((t, RWKV_DIM), F32),
        scratch_shapes=[pltpu.VMEM((2, HEAD_DIM, HEAD_DIM), F32)],
        compiler_params=_params("arbitrary", "arbitrary", "arbitrary"),
        name="rwkv_chunk",
    )(r, k, v, lw, kkn, b)


def _ssd_kernel(pb_ref, cw_ref, cb_ref, dtb_ref, alog_ref, dskip_ref, nrm_ref, ys_ref, ext_ref, st_ref, *, nck):
    q = SSD_CHUNK

    @pl.when(pl.program_id(1) == 0)
    def _():
        ext_ref[0:8, :] = jnp.zeros((8, SSD_XBC), F32)
        st_ref[...] = jnp.zeros_like(st_ref)

    ext_ref[8:8 + nck * q, :] = pb_ref[:, SSD_DIM:SSD_DIM + SSD_XBC]
    for c in range(nck):
        _ssd_chunk(pb_ref, cw_ref, cb_ref, dtb_ref, alog_ref, dskip_ref, nrm_ref, ys_ref, ext_ref, st_ref, c * q)
    ext_ref[0:8, :] = ext_ref[nck * q:nck * q + 8, :]


def _ssd_chunk(pb_ref, cw_ref, cb_ref, dtb_ref, alog_ref, dskip_ref, nrm_ref, ys_ref, ext_ref, st_ref, r0):
    q = SSD_CHUNK
    p = HEAD_DIM
    hpg = SSD_HEADS // SSD_GROUPS
    z = pb_ref[r0:r0 + q, 0:SSD_DIM]
    u = ext_ref[8 + r0:8 + r0 + q, :]
    dt_raw = pb_ref[r0:r0 + q, SSD_DIM + SSD_XBC:]

    conv = cb_ref[...] + cw_ref[SSD_CONV - 1:SSD_CONV, :] * u
    for j in range(SSD_CONV - 1):
        off = r0 + 8 - (SSD_CONV - 1) + j
        conv = conv + cw_ref[j:j + 1, :] * ext_ref[off:off + q, :]
    xbc = _silu(conv)
    xs = xbc[:, 0:SSD_DIM]
    bm = xbc[:, SSD_DIM:SSD_DIM + SSD_GROUPS * SSD_STATE]
    cm = xbc[:, SSD_DIM + SSD_GROUPS * SSD_STATE:]

    dt = _softplus(dt_raw + dtb_ref[...])
    a = -jnp.exp(alog_ref[...])
    ri = lax.broadcasted_iota(jnp.int32, (q, q), 0)
    ci = lax.broadcasted_iota(jnp.int32, (q, q), 1)
    causal = ri >= ci
    cum = _dot_exact_lhs(causal.astype(BF16), dt * a)
    cum_t = cum.T
    dt_t = dt.T
    cum_end = cum[q - 1:q, :]
    to_end = jnp.exp(cum_end - cum) * dt
    ecum = jnp.exp(cum)
    edec = jnp.exp(cum_end)

    ys = []
    for g in range(SSD_GROUPS):
        bm_g = bm[:, g * SSD_STATE:(g + 1) * SSD_STATE]
        cm_g = cm[:, g * SSD_STATE:(g + 1) * SSD_STATE]
        cb = _bdot_nt(cm_g, bm_g)
        bm_t = bm_g.T
        for hh in range(hpg):
            h = g * hpg + hh
            x_h = xs[:, h * p:(h + 1) * p]
            seg = cum[:, h:h + 1] - cum_t[h:h + 1, :]
            ldec = jnp.exp(jnp.where(causal, seg, -jnp.inf))
            wts = cb * ldec * dt_t[h:h + 1, :]
            y = _bdot(wts, x_h)
            h_prev = st_ref[h]
            y = y + _bdot(cm_g, h_prev) * ecum[:, h:h + 1]
            st_ref[h] = h_prev * edec[:, h:h + 1] + _bdot(bm_t, x_h * to_end[:, h:h + 1])
            ys.append(y + dskip_ref[:, h:h + 1] * x_h)
    yall = jnp.concatenate(ys, axis=1) * _silu(z)
    gw = SSD_DIM // SSD_GROUPS
    outs = []
    for g in range(SSD_GROUPS):
        yg = yall[:, g * gw:(g + 1) * gw]
        ms = jnp.mean(yg * yg, axis=-1, keepdims=True)
        outs.append(yg * lax.rsqrt(ms + NORM_EPS) * nrm_ref[:, g * gw:(g + 1) * gw])
    ys_ref[r0:r0 + q, :] = jnp.concatenate(outs, axis=1)


def _ssd(pb, bsz, seq, conv_w, conv_b, dt_bias, a_log, d_skip, ssd_norm):
    t = pb.shape[0]
    nck = 2 if seq % (2 * SSD_CHUNK) == 0 else 1
    rows = nck * SSD_CHUNK
    nc = seq // rows
    row = lambda b, i: (b * nc + i, 0)
    return pl.pallas_call(
        functools.partial(_ssd_kernel, nck=nck),
        grid=(bsz, nc),
        in_specs=[pl.BlockSpec((rows, SSD_COLS_PAD), row), _full(conv_w.shape), _full(conv_b.shape),
                  _full(dt_bias.shape), _full(a_log.shape), _full(d_skip.shape), _full(ssd_norm.shape)],
        out_specs=pl.BlockSpec((rows, SSD_DIM), row),
        out_shape=jax.ShapeDtypeStruct((t, SSD_DIM), F32),
        scratch_shapes=[pltpu.VMEM((8 + rows, SSD_XBC), F32),
                        pltpu.VMEM((SSD_HEADS, SSD_STATE, HEAD_DIM), F32)],
        compiler_params=_params("arbitrary", "arbitrary"),
        name="ssd",
    )(pb, conv_w, conv_b, dt_bias, a_log, d_skip, ssd_norm)


def _mix_ffn_kernel(h_ref, y_ref, bonus_ref, gate_ref, ys_ref, gnw_ref, gnb_ref, seg_ref, wa_ref, wb_ref,
                    g_ref, wg_ref, wu_ref, wd_ref, o_ref):
    y = y_ref[...]
    seg = seg_ref[...]
    mu = _head_sums(y, seg) * (1.0 / HEAD_DIM)
    yc = y - mu
    var = _head_sums(yc * yc, seg) * (1.0 / HEAD_DIM)
    yn = yc * lax.rsqrt(var + RWKV_GN_EPS) * gnw_ref[...] + gnb_ref[...]
    ya = (yn + bonus_ref[...]) * gate_ref[...]
    h = h_ref[...] + _bdot(ya, wa_ref[...]) + _bdot(ys_ref[...], wb_ref[...])

    xn = _rms(h, g_ref[...]).astype(BF16)
    act = _silu(jnp.dot(xn, wg_ref[...], preferred_element_type=F32)) * jnp.dot(
        xn, wu_ref[...], preferred_element_type=F32)
    o_ref[...] = h + jnp.dot(act.astype(BF16), wd_ref[...], preferred_element_type=F32)


def _mix_ffn(h2, y, bonus, gate, ys, gn_w, gn_b, seg, wa, wb, g, wg, wu, wd):
    t = h2.shape[0]
    tm = _tile(t, 512)
    row = lambda i: (i, 0)
    half = pl.BlockSpec((tm, RWKV_DIM), row)
    return pl.pallas_call(
        _mix_ffn_kernel,
        grid=(t // tm,),
        in_specs=[pl.BlockSpec((tm, D_MODEL), row), half, half, half, half, _full(gn_w.shape),
                  _full(gn_b.shape), _resident(seg.shape), _resident(wa.shape), _resident(wb.shape),
                  _full((1, D_MODEL)), _resident(wg.shape), _resident(wu.shape), _resident(wd.shape)],
        out_specs=pl.BlockSpec((tm, D_MODEL), row),
        out_shape=jax.ShapeDtypeStruct((t, D_MODEL), F32),
        compiler_params=_params("parallel"),
        name="mix_ffn",
    )(h2, y, bonus, gate, ys, gn_w, gn_b, seg, wa, wb, g, wg, wu, wd)


def _rope_table_kernel(pos_ref, freq_ref, cos_ref, sin_ref):
    ang = pos_ref[...] * freq_ref[...]
    cos_ref[...] = jnp.cos(ang)
    sin_ref[...] = jnp.sin(ang)


def _rope_tables(positions):
    t = positions.size
    half = ROPE_DIM // 2
    rows = t * half // LANES
    inv_freq = ROPE_THETA ** (-jnp.arange(0, ROPE_DIM, 2, dtype=F32) / ROPE_DIM)
    pos_rep = jnp.repeat(positions.reshape(-1).astype(F32), half).reshape(rows, LANES)
    freq = jnp.tile(inv_freq, LANES // half).reshape(1, LANES)
    tr = _tile(rows, 256)
    cos, sin = pl.pallas_call(
        _rope_table_kernel,
        grid=(rows // tr,),
        in_specs=[pl.BlockSpec((tr, LANES), lambda i: (i, 0)), _full((1, LANES))],
        out_specs=[pl.BlockSpec((tr, LANES), lambda i: (i, 0))] * 2,
        out_shape=[jax.ShapeDtypeStruct((rows, LANES), F32)] * 2,
        compiler_params=_params("parallel"),
        name="rope_tables",
    )(pos_rep, freq)
    return jnp.concatenate([cos.reshape(t, half) - 1.0, sin.reshape(t, half)], axis=1)


def _rope_expansion(width):
    half = ROPE_DIM // 2
    d = jnp.arange(width) % HEAD_DIM
    j = jnp.arange(half)[:, None]
    zero = jnp.zeros((half, width), F32)
    cos_rows = jnp.where((d[None, :] < ROPE_DIM) & (d[None, :] % half == j), 1.0, 0.0)
    sin_a = jnp.where((d[None, :] < half) & (d[None, :] == j), -1.0, 0.0)
    sin_b = jnp.where((d[None, :] >= half) & (d[None, :] < ROPE_DIM) & (d[None, :] - half == j), 1.0, 0.0)
    one = jnp.concatenate([jnp.concatenate([cos_rows, zero, zero], axis=1),
                           jnp.concatenate([zero, sin_a, sin_b], axis=1)], axis=0)
    return jnp.concatenate([one, one], axis=0).astype(BF16)


def _qkv_kernel(h_ref, g_ref, w_ref, b_ref, qn_ref, kn_ref, cs_ref, exp_ref, seg_ref, q_ref, k_ref, v_ref):
    xn = _rms(h_ref[...], g_ref[...]).astype(BF16)
    qkv = jnp.dot(xn, w_ref[...], preferred_element_type=F32) + b_ref[...]
    seg = seg_ref[...]
    wide = seg.shape[0]
    hi, lo = _split(cs_ref[...])
    tab = jnp.dot(jnp.concatenate([hi, lo], axis=1), exp_ref[...], preferred_element_type=F32)
    cos = 1.0 + tab[:, 0:wide]
    sa = tab[:, wide:2 * wide]
    sb = tab[:, 2 * wide:]

    def norm_rope(x, gain):
        ms = _head_sums(x * x, seg) * (1.0 / HEAD_DIM)
        xn_ = x * lax.rsqrt(ms + NORM_EPS) * gain
        return (xn_ * cos + pltpu.roll(xn_, wide - ROPE_DIM // 2, axis=1) * sa
                + pltpu.roll(xn_, ROPE_DIM // 2, axis=1) * sb)

    for c in range(Q_DIM // wide):
        x = qkv[:, c * wide:(c + 1) * wide]
        q_ref[:, c * wide:(c + 1) * wide] = (norm_rope(x, qn_ref[...]) * (HEAD_DIM ** -0.5)).astype(BF16)
    for c in range(KV_DIM // wide):
        x = qkv[:, Q_DIM + c * wide:Q_DIM + (c + 1) * wide]
        k_ref[:, c * wide:(c + 1) * wide] = norm_rope(x, kn_ref[...]).astype(BF16)
    v_ref[...] = qkv[:, Q_DIM + KV_DIM:].astype(BF16)


def _qkv(h2, g, w, b, qn, kn, cs, expand, seg):
    t = h2.shape[0]
    tm = _tile(t, 512)
    row = lambda i: (i, 0)
    return pl.pallas_call(
        _qkv_kernel,
        grid=(t // tm,),
        in_specs=[pl.BlockSpec((tm, D_MODEL), row), _full((1, D_MODEL)), _resident(w.shape), _full(b.shape),
                  _full(qn.shape), _full(kn.shape), pl.BlockSpec((tm, cs.shape[1]), row), _full(expand.shape),
                  _full(seg.shape)],
        out_specs=[pl.BlockSpec((tm, Q_DIM), row), pl.BlockSpec((tm, KV_DIM), row),
                   pl.BlockSpec((tm, KV_DIM), row)],
        out_shape=[jax.ShapeDtypeStruct((t, Q_DIM), BF16), jax.ShapeDtypeStruct((t, KV_DIM), BF16),
                   jax.ShapeDtypeStruct((t, KV_DIM), BF16)],
        compiler_params=_params("parallel"),
        name="qkv",
    )(h2, g, w, b, qn, kn, cs, expand, seg)


def _attn_kernel(h_ref, q_ref, kc_ref, kp_ref, vc_ref, vp_ref, sink_ref, wo_ref, bo_ref, o_ref, *, nqb):
    w = WINDOW
    n = HEAD_DIM
    rows = GQA_GROUP * w
    qi = lax.broadcasted_iota(jnp.int32, (rows, 2 * w), 0) % w
    kj = lax.broadcasted_iota(jnp.int32, (rows, 2 * w), 1)
    diff = w + qi - kj
    band = (diff >= 0) & (diff < w)
    band0 = band & (kj >= jnp.where(pl.program_id(1) == 0, w, 0))
    kall = jnp.concatenate([kp_ref[...], kc_ref[...]], axis=0)
    vall = jnp.concatenate([vp_ref[...], vc_ref[...]], axis=0)
    units = [(qb, g) for qb in range(nqb) for g in range(KV_HEADS)]
    s, e, den = {}, {}, {}
    for u in units:
        qb, g = u
        qs = jnp.concatenate(
            [q_ref[qb * w:(qb + 1) * w, (g * GQA_GROUP + i) * n:(g * GQA_GROUP + i + 1) * n]
             for i in range(GQA_GROUP)], axis=0)
        kk = kall[qb * w:(qb + 2) * w, g * n:(g + 1) * n]
        sc = lax.dot_general(qs, kk, (((1,), (1,)), ((), ())), preferred_element_type=F32)
        s[u] = jnp.where(band0 if qb == 0 else band, sc, -jnp.inf)
    sinks = [jnp.concatenate(
        [jnp.broadcast_to(sink_ref[:, g * GQA_GROUP + i:g * GQA_GROUP + i + 1], (w, LANES))
         for i in range(GQA_GROUP)], axis=0) for g in range(KV_HEADS)]
    ones = jnp.ones((2 * w, LANES - n), BF16)
    for u in units:
        sink = sinks[u[1]]
        m = jnp.maximum(jnp.broadcast_to(jnp.max(s[u], axis=-1, keepdims=True), (rows, LANES)), sink)
        e[u] = jnp.exp(s[u] - jnp.concatenate([m, m], axis=1)).astype(BF16)
        den[u] = jnp.exp(sink - m)
    outs = [[None] * Q_HEADS for _ in range(nqb)]
    for u in units:
        qb, g = u
        vv = jnp.concatenate([vall[qb * w:(qb + 2) * w, g * n:(g + 1) * n], ones], axis=1)
        both = jnp.dot(e[u], vv, preferred_element_type=F32)
        total = pltpu.roll(both, LANES - n, axis=1) + den[u]
        o = both[:, 0:n] / total[:, 0:n]
        for i in range(GQA_GROUP):
            outs[qb][g * GQA_GROUP + i] = o[i * w:(i + 1) * w]
    att = jnp.concatenate([jnp.concatenate(outs[qb], axis=1) for qb in range(nqb)], axis=0)
    o_ref[...] = h_ref[...] + _bdot(att, wo_ref[...]) + bo_ref[...]


def _attn(h2, q, k, v, sinks, wo, bo, bsz, seq):
    t = h2.shape[0]
    nqb = 4 if seq % (4 * WINDOW) == 0 else 1
    tq = nqb * WINDOW
    nb = seq // tq
    cur = lambda b, i: (b * nb + i, 0)
    prv = lambda b, i: (b * nb * nqb + jnp.maximum(i * nqb - 1, 0), 0)
    return pl.pallas_call(
        functools.partial(_attn_kernel, nqb=nqb),
        grid=(bsz, nb),
        in_specs=[pl.BlockSpec((tq, D_MODEL), cur), pl.BlockSpec((tq, Q_DIM), cur),
                  pl.BlockSpec((tq, KV_DIM), cur), pl.BlockSpec((WINDOW, KV_DIM), prv),
                  pl.BlockSpec((tq, KV_DIM), cur), pl.BlockSpec((WINDOW, KV_DIM), prv),
                  _full(sinks.shape), _full(wo.shape), _full(bo.shape)],
        out_specs=pl.BlockSpec((tq, D_MODEL), cur),
        out_shape=jax.ShapeDtypeStruct((t, D_MODEL), F32),
        compiler_params=_params("parallel", "parallel"),
        name="attn",
    )(h2, q, k, k, v, v, sinks, wo, bo)


def _router_kernel(h_ref, g_ref, wr_ref, xn_ref, meta_ref, cnt_ref, base_ref):
    @pl.when(pl.program_id(0) == 0)
    def _():
        base_ref[...] = jnp.zeros_like(base_ref)

    xn = _rms(h_ref[...], g_ref[...])
    xn_ref[...] = xn
    hi, lo = _split(xn)
    whi = wr_ref[0]
    wlo = wr_ref[1]
    logits = (jnp.dot(hi, whi, preferred_element_type=F32) + jnp.dot(hi, wlo, preferred_element_type=F32)
              + jnp.dot(lo, whi, preferred_element_type=F32))
    tm = logits.shape[0]
    lane = lax.broadcasted_iota(jnp.int32, logits.shape, 1)
    logits = jnp.where(lane < N_EXPERTS, logits, -jnp.inf)
    m1 = jnp.max(logits, axis=-1, keepdims=True)
    i1 = jnp.min(jnp.where(logits == m1, lane, LANES), axis=-1, keepdims=True)
    rest = jnp.where(lane == i1, -jnp.inf, logits)
    m2 = jnp.max(rest, axis=-1, keepdims=True)
    i2 = jnp.min(jnp.where(rest == m2, lane, LANES), axis=-1, keepdims=True)
    e2 = jnp.exp(m2 - m1)
    w1 = 1.0 / (1.0 + e2)
    w2 = e2 / (1.0 + e2)

    sel1 = lane == i1
    sel2 = lane == i2
    onehot = jnp.where(sel1, 1.0, 0.0) + jnp.where(sel2, 1.0, 0.0)
    ri = lax.broadcasted_iota(jnp.int32, (tm, tm), 0)
    ci = lax.broadcasted_iota(jnp.int32, (tm, tm), 1)
    before = jnp.dot(jnp.where(ci < ri, 1.0, 0.0).astype(BF16), onehot.astype(BF16),
                     preferred_element_type=F32)
    rank = base_ref[...] + before
    r1 = jnp.sum(jnp.where(sel1, rank, 0.0), axis=-1, keepdims=True)
    r2 = jnp.sum(jnp.where(sel2, rank, 0.0), axis=-1, keepdims=True)
    total = base_ref[...] + jnp.sum(onehot, axis=0, keepdims=True)
    base_ref[...] = total
    cnt_ref[...] = total
    cols = (i1.astype(F32), i2.astype(F32), w1, w2, r1, r2)
    meta = jnp.zeros(logits.shape, F32)
    for c, val in enumerate(cols):
        meta = jnp.where(lane == c, val, meta)
    meta_ref[...] = meta


def _router(h2, g, wr):
    t = h2.shape[0]
    tm = _tile(t, 512)
    row = lambda i: (i, 0)
    return pl.pallas_call(
        _router_kernel,
        grid=(t // tm,),
        in_specs=[pl.BlockSpec((tm, D_MODEL), row), _full((1, D_MODEL)), _full(wr.shape)],
        out_specs=[pl.BlockSpec((tm, D_MODEL), row), pl.BlockSpec((tm, LANES), row), _full((1, LANES))],
        out_shape=[jax.ShapeDtypeStruct((t, D_MODEL), F32), jax.ShapeDtypeStruct((t, LANES), F32),
                   jax.ShapeDtypeStruct((1, LANES), F32)],
        scratch_shapes=[pltpu.VMEM((1, LANES), F32)],
        compiler_params=_params("arbitrary"),
        name="router",
    )(h2, g, wr)


def _moe_index_kernel(p1_ref, p2_ref, pad_ref, src_ref, dst_ref, *, ntok):
    tm = MOE_TM
    nrow = src_ref.shape[0]

    def virtual(j, c):
        dst_ref[j] = nrow + j
        return c

    lax.fori_loop(0, tm, virtual, 0, unroll=8)

    for e in range(N_EXPERTS):
        lo = pad_ref[2 * e]
        base = 2 * ntok + e * tm - lo

        def fill(p, c, base=base):
            src_ref[p] = 0
            dst_ref[tm + p] = base + p
            return c

        lax.fori_loop(lo, pad_ref[2 * e + 1], fill, 0)

    def unused(p, c):
        src_ref[p] = 0
        dst_ref[tm + p] = 2 * ntok + (p & (tm - 1))
        return c

    lax.fori_loop(pad_ref[2 * N_EXPERTS], pad_ref[2 * N_EXPERTS + 1], unused, 0)

    def body(t, c):
        p1 = p1_ref[t]
        p2 = p2_ref[t]
        src_ref[p1] = t
        dst_ref[tm + p1] = t
        src_ref[p2] = t
        dst_ref[tm + p2] = ntok + t
        return c

    lax.fori_loop(0, ntok, body, 0, unroll=4)


def _moe_index(pos1, pos2, pad, rows):
    smem = pl.BlockSpec(memory_space=pltpu.SMEM)
    return pl.pallas_call(
        functools.partial(_moe_index_kernel, ntok=pos1.shape[0]),
        grid_spec=pltpu.PrefetchScalarGridSpec(
            num_scalar_prefetch=3, grid=(1,), in_specs=[], out_specs=[smem, smem]),
        out_shape=[jax.ShapeDtypeStruct((rows,), jnp.int32), jax.ShapeDtypeStruct((rows + MOE_TM,), jnp.int32)],
        compiler_params=_params("arbitrary"),
        name="moe_index",
    )(pos1, pos2, pad)


def _row_copy(src_ref, src_row, dst_ref, dst_row, sem):
    return pltpu.make_async_copy(src_ref.at[pl.ds(src_row, 1)], dst_ref.at[pl.ds(dst_row, 1)], sem)


def _experts_kernel(te_ref, nused_ref, src_ref, dst_ref, x_ref, wg_ref, wu_ref, wd_ref, y_ref,
                    xbuf, ybuf, wgb, wub, wdb, gsem, ssem):
    i = pl.program_id(0)
    nused = nused_ref[0]
    tm = MOE_TM

    sub = MOE_SUB
    ngroup = tm // sub

    def gather_group(tile, slot, k):
        for u in range(sub):
            _row_copy(x_ref, src_ref[tile * tm + k * sub + u], xbuf.at[slot, k], u, gsem.at[slot]).start()

    def scatter_group(tile, slot, k):
        for u in range(sub):
            _row_copy(ybuf.at[slot, k], u, y_ref, dst_ref[(tile + 1) * tm + k * sub + u], ssem.at[slot]).start()

    def wait_gather(slot):
        def body(k, c):
            pltpu.make_async_copy(x_ref.at[pl.ds(0, sub)], xbuf.at[slot, k], gsem.at[slot]).wait()
            return c
        lax.fori_loop(0, ngroup, body, 0)

    def wait_scatter(slot):
        def body(k, c):
            pltpu.make_async_copy(ybuf.at[slot, k], y_ref.at[pl.ds(0, sub)], ssem.at[slot]).wait()
            return c
        lax.fori_loop(0, ngroup, body, 0)

    @pl.when(i == 0)
    def _():
        ybuf[1] = jnp.zeros(ybuf.shape[1:], F32)

        def body(k, c):
            gather_group(0, 0, k)
            return c
        lax.fori_loop(0, ngroup, body, 0)

    @pl.when(i < nused)
    def _():
        slot = i % 2
        other = 1 - slot
        nxt = jnp.minimum(i + 1, nused - 1)
        wait_gather(slot)

        @pl.when((i == 0) | (te_ref[i] != te_ref[jnp.maximum(i - 1, 0)]))
        def _():
            wgb[...] = wg_ref[0].astype(BF16)
            wub[...] = wu_ref[0].astype(BF16)
            wdb[...] = wd_ref[0].astype(BF16)

        xb = xbuf[slot].reshape(tm, D_MODEL).astype(BF16)
        bounds = list(range(0, FFN_EXPERT, MOE_FCHUNK)) + [FFN_EXPERT]
        nchunk = len(bounds) - 1
        acts = []
        for c in range(nchunk):
            c0, c1 = bounds[c], bounds[c + 1]
            gate = jnp.dot(xb, wgb[:, c0:c1], preferred_element_type=F32)
            up = jnp.dot(xb, wub[:, c0:c1], preferred_element_type=F32)
            acts.append((_silu(gate) * up).astype(BF16))
            half = nchunk // 2
            if c < half:
                for k in range(c * ngroup // half, (c + 1) * ngroup // half):
                    gather_group(nxt, other, k)
            else:
                for k in range((c - half) * ngroup // (nchunk - half), (c - half + 1) * ngroup // (nchunk - half)):
                    scatter_group(i - 1, other, k)
        y = jnp.dot(jnp.concatenate(acts, axis=1), wdb[...], preferred_element_type=F32)

        @pl.when(i >= 1)
        def _():
            wait_scatter(slot)

        ybuf[slot] = y.reshape(ngroup, sub, D_MODEL)

        @pl.when(i == nused - 1)
        def _():
            def body(k, c):
                scatter_group(i, slot, k)
                return c
            lax.fori_loop(0, ngroup, body, 0)
            wait_scatter(other)
            wait_scatter(slot)
            wait_gather(other)


def _experts(tile_expert, nused, src, dst, xn, wg, wu, wd, out_rows):
    ntile = tile_expert.shape[0]
    tm = MOE_TM
    wsel = lambda i, te, nu, s, d: (te[i], 0, 0)
    any_spec = pl.BlockSpec(memory_space=pl.ANY)
    return pl.pallas_call(
        _experts_kernel,
        grid_spec=pltpu.PrefetchScalarGridSpec(
            num_scalar_prefetch=4, grid=(ntile,),
            in_specs=[any_spec,
                      pl.BlockSpec((1, D_MODEL, FFN_EXPERT), wsel),
                      pl.BlockSpec((1, D_MODEL, FFN_EXPERT), wsel),
                      pl.BlockSpec((1, FFN_EXPERT, D_MODEL), wsel)],
            out_specs=any_spec,
            scratch_shapes=[pltpu.VMEM((2, tm // MOE_SUB, MOE_SUB, D_MODEL), F32),
                            pltpu.VMEM((2, tm // MOE_SUB, MOE_SUB, D_MODEL), F32),
                            pltpu.VMEM((D_MODEL, FFN_EXPERT), BF16), pltpu.VMEM((D_MODEL, FFN_EXPERT), BF16),
                            pltpu.VMEM((FFN_EXPERT, D_MODEL), BF16),
                            pltpu.SemaphoreType.DMA((2,)), pltpu.SemaphoreType.DMA((2,))]),
        out_shape=jax.ShapeDtypeStruct((out_rows, D_MODEL), F32),
        compiler_params=_params("arbitrary"),
        name="moe_experts",
    )(tile_expert, nused, src, dst, xn, wg, wu, wd)


def _combine_kernel(h_ref, meta_ref, y1_ref, y2_ref, o_ref):
    meta = meta_ref[...]
    o_ref[...] = h_ref[...] + meta[:, 2:3] * y1_ref[...] + meta[:, 3:4] * y2_ref[...]


def _combine(h2, meta, ys):
    t = h2.shape[0]
    tm = _tile(t, 512)
    nt = t // tm
    row = lambda i: (i, 0)
    return pl.pallas_call(
        _combine_kernel,
        grid=(nt,),
        in_specs=[pl.BlockSpec((tm, D_MODEL), row), pl.BlockSpec((tm, LANES), row),
                  pl.BlockSpec((tm, D_MODEL), row), pl.BlockSpec((tm, D_MODEL), lambda i: (nt + i, 0))],
        out_specs=pl.BlockSpec((tm, D_MODEL), row),
        out_shape=jax.ShapeDtypeStruct((t, D_MODEL), F32),
        compiler_params=_params("parallel"),
        name="moe_combine",
    )(h2, meta, ys, ys)


def _moe(h2, g, router, wg, wu, wd):
    t = h2.shape[0]
    tm = MOE_TM
    wr = jnp.pad(router, ((0, 0), (0, LANES - N_EXPERTS)))
    wr_hi = wr.astype(BF16)
    wr_lo = (wr - wr_hi.astype(F32)).astype(BF16)
    xn, meta, cnt = _router(h2, g, jnp.stack([wr_hi, wr_lo]))

    cols = meta[:, :8].T.astype(jnp.int32)
    e1, e2, r1, r2 = cols[0], cols[1], cols[4], cols[5]
    count = cnt[0, :N_EXPERTS].astype(jnp.int32)
    ntile = (count + tm - 1) // tm
    tile_end = jnp.cumsum(ntile)
    off = (tile_end - ntile) * tm
    rows = 2 * t + N_EXPERTS * tm
    tile_id = jnp.arange(rows // tm)
    tile_expert = jnp.minimum(
        jnp.sum(tile_id[:, None] >= tile_end[None, :], axis=1), N_EXPERTS - 1).astype(jnp.int32)
    nused = tile_end[-1:].astype(jnp.int32)
    expert = jnp.arange(N_EXPERTS, dtype=jnp.int32)[:, None]
    pos1 = r1 + jnp.sum(jnp.where(e1[None, :] == expert, off[:, None], 0), axis=0)
    pos2 = r2 + jnp.sum(jnp.where(e2[None, :] == expert, off[:, None], 0), axis=0)
    pad = jnp.stack([jnp.append(off + count, tile_end[-1] * tm),
                     jnp.append(off + ntile * tm, rows)], axis=1).reshape(-1).astype(jnp.int32)
    src, dst = _moe_index(pos1, pos2, pad, rows)

    ys = _experts(tile_expert, nused, src, dst, xn, wg, wu, wd, rows + tm)
    return _combine(h2, meta, ys)


def _segment_matrix(width):
    i = jnp.arange(width) // HEAD_DIM
    return (i[:, None] == i[None, :]).astype(BF16)


def _layer0(h2, bsz, seq, norm_mix, w_in, mu_shift, w0, w_decay_up, a0, w_iclr_up, w_gate_up, k_k, k_a, r_k,
            gn_w, gn_b, conv_w, conv_b, dt_bias, a_log, d_skip, ssd_norm, w_out, norm_ffn, ffn_gate, ffn_up,
            ffn_down):
    row = lambda a: a.reshape(1, -1)
    seg = _segment_matrix(SEG_WIDTH)
    wa = w_in[:, :RWKV_COLS].astype(BF16)
    wb = jnp.pad(w_in[:, RWKV_COLS:], ((0, 0), (0, SSD_COLS_PAD - SSD_COLS))).astype(BF16)
    lora = w_decay_up.shape[0]
    wd = jnp.concatenate([w_decay_up, jnp.zeros((LANES - lora, RWKV_DIM), F32)], axis=0).astype(BF16)
    wi = jnp.concatenate([jnp.zeros((LANES - lora, RWKV_DIM), F32), w_iclr_up], axis=0).astype(BF16)
    pb, r, k, v, lw, kkn, b, g, bonus = _in_proj(
        h2, bsz, seq, row(norm_mix), wa, wb, row(mu_shift), row(w0), wd, row(a0), wi, w_gate_up.astype(BF16),
        row(k_k), row(k_a), row(r_k), seg)
    y = _rwkv_chunk(r, k, v, lw, kkn, b, bsz, seq)

    lane_pad = lambda a: jnp.pad(row(a), ((0, 0), (0, LANES - a.size)))
    ys = _ssd(pb, bsz, seq, conv_w, row(conv_b), lane_pad(dt_bias), lane_pad(a_log), lane_pad(d_skip),
              row(ssd_norm))

    w_out = w_out.astype(BF16)
    return _mix_ffn(h2, y, bonus, g, ys, row(gn_w), row(gn_b), seg, w_out[:RWKV_DIM], w_out[RWKV_DIM:],
                    row(norm_ffn), ffn_gate.astype(BF16), ffn_up.astype(BF16), ffn_down.astype(BF16))


def _layer1(h2, bsz, seq, tables, norm_mix, w_qkv, b_qkv, q_norm, k_norm, sinks, w_o, b_o, norm_ffn, router,
            exp_gate, exp_up, exp_down):
    row = lambda a: a.reshape(1, -1)
    wide = lambda a: jnp.tile(a, SEG_WIDTH // HEAD_DIM).reshape(1, SEG_WIDTH)
    q, k, v = _qkv(h2, row(norm_mix), w_qkv.astype(BF16), row(b_qkv), wide(q_norm), wide(k_norm), tables,
                   _rope_expansion(SEG_WIDTH), _segment_matrix(SEG_WIDTH))
    h2 = _attn(h2, q, k, v, row(sinks), w_o.astype(BF16), row(b_o), bsz, seq)

    return _moe(h2, row(norm_ffn), router, exp_gate, exp_up, exp_down)


def kernel(x, positions, ev_norm_mix, ev_w_in, ev_mu_shift, ev_w0, ev_w_decay_up, ev_a0, ev_w_iclr_up, ev_w_gate_up, ev_k_k, ev_k_a, ev_r_k, ev_gn_w, ev_gn_b, ev_conv_w, ev_conv_b, ev_dt_bias, ev_a_log, ev_d_skip, ev_ssd_norm, ev_w_out, ev_norm_ffn, ev_ffn_gate, ev_ffn_up, ev_ffn_down, od_norm_mix, od_w_qkv, od_b_qkv, od_q_norm, od_k_norm, od_sinks, od_w_o, od_b_o, od_norm_ffn, od_router, od_exp_gate, od_exp_up, od_exp_down):
    bsz, seq, d = x.shape
    depth = ev_norm_mix.shape[0] + od_norm_mix.shape[0]
    tables = _rope_tables(positions)
    h2 = x.reshape(bsz * seq, d)
    for layer in range(depth):
        i = layer // 2
        if layer % 2 == 0:
            h2 = _layer0(h2, bsz, seq, ev_norm_mix[i], ev_w_in[i], ev_mu_shift[i], ev_w0[i], ev_w_decay_up[i],
                         ev_a0[i], ev_w_iclr_up[i], ev_w_gate_up[i], ev_k_k[i], ev_k_a[i], ev_r_k[i],
                         ev_gn_w[i], ev_gn_b[i], ev_conv_w[i], ev_conv_b[i], ev_dt_bias[i], ev_a_log[i],
                         ev_d_skip[i], ev_ssd_norm[i], ev_w_out[i], ev_norm_ffn[i], ev_ffn_gate[i],
                         ev_ffn_up[i], ev_ffn_down[i])
        else:
            h2 = _layer1(h2, bsz, seq, tables, od_norm_mix[i], od_w_qkv[i], od_b_qkv[i], od_q_norm[i],
                         od_k_norm[i], od_sinks[i], od_w_o[i], od_b_o[i], od_norm_ffn[i], od_router[i],
                         od_exp_gate[i], od_exp_up[i], od_exp_down[i])
    return h2.reshape(bsz, seq, d)
```

```python
import functools
import math

import jax
import jax.numpy as jnp
from jax import lax
from jax.experimental import pallas as pl
from jax.experimental.pallas import tpu as pltpu

F32 = jnp.float32
BF16 = jnp.bfloat16

D_MODEL = 1024
HEAD_DIM = 64
NORM_EPS = 1e-6

RWKV_HEADS = 8
RWKV_DIM = 512
RWKV_COLS = 1792
RWKV_GN_EPS = 64e-5
RWKV_CHUNK = 64
RWKV_GROUP = 256

SSD_HEADS = 8
SSD_DIM = 512
SSD_GROUPS = 2
SSD_STATE = 128
SSD_CONV = 4
SSD_CHUNK = 128
SSD_XBC = 1024
SSD_COLS = 1544
SSD_COLS_PAD = 1664

Q_HEADS = 16
KV_HEADS = 4
GQA_GROUP = 4
Q_DIM = 1024
KV_DIM = 256
WINDOW = 128
ROPE_THETA = 500000.0
ROPE_DIM = 16

FFN_DENSE = 2816
N_EXPERTS = 8
FFN_EXPERT = 1408
MOE_TM = 512
MOE_SUB = 8
MOE_FCHUNK = 256

LANES = 128
SEG_WIDTH = 256
VMEM_LIMIT_BYTES = 56 * 1024 * 1024


def _params(*sem):
    return pltpu.CompilerParams(dimension_semantics=sem, vmem_limit_bytes=VMEM_LIMIT_BYTES)


def _bdot(a, b):
    return jnp.dot(a.astype(BF16), b.astype(BF16), preferred_element_type=F32)


def _bdot_nt(a, b):
    return lax.dot_general(a.astype(BF16), b.astype(BF16), (((1,), (1,)), ((), ())),
                           preferred_element_type=F32)


def _bdot_tn(a, b):
    return lax.dot_general(a.astype(BF16), b.astype(BF16), (((0,), (0,)), ((), ())),
                           preferred_element_type=F32)


def _split(x):
    hi = x.astype(BF16)
    lo = (x - hi.astype(F32)).astype(BF16)
    return hi, lo


def _dot_exact_lhs(m, x):
    hi, lo = _split(x)
    return (jnp.dot(m, hi, preferred_element_type=F32) + jnp.dot(m, lo, preferred_element_type=F32))


def _head_sums(x, seg):
    w = seg.shape[0]
    xb = x.astype(BF16)
    return jnp.concatenate(
        [jnp.dot(xb[:, c:c + w], seg, preferred_element_type=F32) for c in range(0, x.shape[1], w)], axis=1)


def _sigmoid(x):
    return 1.0 / (1.0 + jnp.exp(-x))


def _silu(x):
    return x * _sigmoid(x)


def _softplus(x):
    return jnp.maximum(x, 0.0) + jnp.log(1.0 + jnp.exp(-jnp.abs(x)))


def _rms(x, g):
    ms = jnp.mean(x * x, axis=-1, keepdims=True)
    return x * lax.rsqrt(ms + NORM_EPS) * g


def _tile(n, pref):
    t = min(n, pref)
    while n % t:
        t //= 2
    return t


def _full(shape):
    nd = len(shape)
    return pl.BlockSpec(shape, lambda *_: (0,) * nd)


def _resident(shape):
    nd = len(shape)
    return pl.BlockSpec(shape, lambda *_: (0,) * nd, pipeline_mode=pl.Buffered(1))


def _in_proj_kernel(x_ref, g_ref, wa_ref, wb_ref, mu_ref, w0_ref, wd_ref, a0_ref, wi_ref, wg_ref, kk_ref, ka_ref,
                    rk_ref, seg_ref, pb_ref, r_ref, k_ref, v_ref, lw_ref, kkn_ref, b_ref, gate_ref, bonus_ref,
                    carry_ref):
    @pl.when(pl.program_id(1) == 0)
    def _():
        carry_ref[...] = jnp.zeros_like(carry_ref)

    xn = _rms(x_ref[...], g_ref[...]).astype(BF16)
    pb_ref[...] = jnp.dot(xn, wb_ref[...], preferred_element_type=F32)
    pa = jnp.dot(xn, wa_ref[...], preferred_element_type=F32)
    tm = pa.shape[0]
    row = lax.broadcasted_iota(jnp.int32, pa.shape, 0)
    prev = jnp.where(row == 0, carry_ref[...], pltpu.roll(pa, 1, axis=0))
    carry_ref[...] = pa[tm - 1:tm, :]
    x = pa + (prev - pa) * mu_ref[...]

    r = x[:, 0:RWKV_DIM]
    k = x[:, RWKV_DIM:2 * RWKV_DIM]
    v = x[:, 2 * RWKV_DIM:3 * RWKV_DIM]
    lora = x[:, 3 * RWKV_DIM:3 * RWKV_DIM + LANES]
    gl = x[:, 3 * RWKV_DIM + LANES:]
    seg = seg_ref[...]

    w_raw = w0_ref[...] + _bdot(jnp.tanh(lora), wd_ref[...])
    lw_ref[...] = (-math.exp(-0.5)) * _sigmoid(w_raw)
    iclr = _sigmoid(a0_ref[...] + _bdot(lora, wi_ref[...]))
    gate_ref[...] = _bdot(_sigmoid(gl), wg_ref[...])

    kk = k * kk_ref[...]
    kkn = kk * lax.rsqrt(_head_sums(kk * kk, seg) + 1e-12)
    k2 = k * (1.0 + (iclr - 1.0) * ka_ref[...])
    r_ref[...] = r
    k_ref[...] = k2
    v_ref[...] = v
    kkn_ref[...] = kkn
    b_ref[...] = kkn * iclr
    bonus_ref[...] = _head_sums(r * k2 * rk_ref[...], seg) * v


def _in_proj(x2, bsz, seq, g, wa, wb, mu, w0, wd, a0, wi, wg, k_k, k_a, r_k, seg):
    t = x2.shape[0]
    tm = _tile(seq, 512)
    nt = seq // tm
    row = lambda b, i: (b * nt + i, 0)
    out = jax.ShapeDtypeStruct((t, RWKV_DIM), F32)
    small = [mu, w0, wd, a0, wi, wg, k_k, k_a, r_k, seg]
    return pl.pallas_call(
        _in_proj_kernel,
        grid=(bsz, nt),
        in_specs=[pl.BlockSpec((tm, D_MODEL), row), _full((1, D_MODEL)), _resident(wa.shape),
                  _resident(wb.shape)] + [_full(a.shape) for a in small],
        out_specs=[pl.BlockSpec((tm, SSD_COLS_PAD), row)] + [pl.BlockSpec((tm, RWKV_DIM), row)] * 8,
        out_shape=[jax.ShapeDtypeStruct((t, SSD_COLS_PAD), F32)] + [out] * 8,
        scratch_shapes=[pltpu.VMEM((1, RWKV_COLS), F32)],
        compiler_params=_params("arbitrary", "arbitrary"),
        name="in_proj",
    )(x2, g, wa, wb, *small)


def _rwkv_chunk_kernel(r_ref, k_ref, v_ref, lw_ref, kk_ref, b_ref, y_ref, z_ref, *, lt):
    c_len = RWKV_CHUNK
    n = HEAD_DIM
    gl = RWKV_GROUP
    nchunk = gl // c_len
    nheads = LANES // n
    shift = c_len.bit_length() - 1

    @pl.when(pl.program_id(2) == 0)
    def _():
        z_ref[...] = jnp.zeros_like(z_ref)

    ri = lax.broadcasted_iota(jnp.int32, (gl, gl), 0)
    ci = lax.broadcasted_iota(jnp.int32, (gl, gl), 1)
    tri_bd = jnp.where((ci <= ri) & (ci >= ((ri >> shift) << shift)), 1.0, 0.0).astype(BF16)
    re_ = lax.broadcasted_iota(jnp.int32, (c_len, c_len), 0)
    ce_ = lax.broadcasted_iota(jnp.int32, (c_len, c_len), 1)
    eye = re_ == ce_
    ri2 = lax.broadcasted_iota(jnp.int32, (2 * gl, gl), 0)
    ci2 = lax.broadcasted_iota(jnp.int32, (2 * gl, gl), 1)
    t2 = jnp.where(ri2 < gl, ri2, ri2 - gl)
    mask2 = (ci2 <= jnp.where(ri2 < gl, t2 - 1, t2)) & (ci2 >= ((t2 >> shift) << shift))
    zeros = jnp.zeros((c_len, n), F32)

    nsub = lt // gl
    sysid = [(s, h) for s in range(nsub) for h in range(nheads)]
    pre = []
    for s in range(nsub):
        sl = pl.ds(s * gl, gl)
        lw = lw_ref[sl, :]
        g_in = _dot_exact_lhs(tri_bd, lw)
        ends = [g_in[(c + 1) * c_len - 1:(c + 1) * c_len, :] for c in range(nchunk)]
        g_end = jnp.concatenate([jnp.broadcast_to(e, (c_len, LANES)) for e in ends], axis=0)
        e_end = jnp.exp(g_end - g_in)
        en = jnp.exp(-g_in)
        k = k_ref[sl, :]
        b = b_ref[sl, :]
        pre.append(dict(
            ends=ends, v=v_ref[sl, :], rt=r_ref[sl, :] * jnp.exp(g_in),
            at=-kk_ref[sl, :] * jnp.exp(g_in - lw), kt=k * en, bt=b * en, bend=b * e_end, kend=k * e_end))

    def hs(name, s, h):
        return pre[s][name][:, h * n:(h + 1) * n]

    xb, xk, xkv, p, x = {}, {}, {}, {}, {}
    for q in sysid:
        la = jnp.concatenate([hs("at", *q), hs("rt", *q)], axis=0)
        xb[q] = jnp.where(mask2, _bdot_nt(la, hs("bt", *q)), 0.0)
        xk[q] = jnp.where(mask2, _bdot_nt(la, hs("kt", *q)), 0.0)
    for q in sysid:
        xkv[q] = _bdot(xk[q], hs("v", *q))
        p[q] = xb[q][0:gl]
        x[q] = jnp.concatenate([hs("at", *q), xkv[q][0:gl]], axis=1)
    for i in range(6):
        for q in sysid:
            x[q] = x[q] + _bdot(p[q], x[q])
        if i < 5:
            for q in sysid:
                p[q] = _bdot(p[q], p[q])
    rq, y0, mn = {}, {}, {}
    for q in sysid:
        yy = _bdot(xb[q][gl:], x[q])
        rq[q] = yy[:, 0:n] + hs("rt", *q)
        y0[q] = yy[:, n:] + xkv[q][gl:]
        bend_h, kend_h, v_h = hs("bend", *q), hs("kend", *q), hs("v", *q)
        for c in range(nchunk):
            rows = slice(c * c_len, (c + 1) * c_len)
            lhs = jnp.concatenate([bend_h[rows], kend_h[rows]], axis=0)
            rhs = jnp.concatenate([x[q][rows], jnp.concatenate([zeros, v_h[rows]], axis=1)], axis=0)
            mn[q + (c,)] = _bdot_tn(lhs, rhs)
    zero_n = jnp.zeros((n, n), F32)
    comp = {q: [] for q in sysid}
    for c in range(nchunk):
        for q in sysid:
            s, h = q
            pend = jnp.exp(pre[s]["ends"][c][:, h * n:(h + 1) * n])
            m_mat = mn[q + (c,)][:, 0:n] + jnp.where(eye, pend, 0.0)
            n_mat = mn[q + (c,)][:, n:]
            if c == 0:
                comp[q].append(jnp.concatenate([m_mat, n_mat], axis=1))
            else:
                comp[q].append(_bdot(m_mat, comp[q][c - 1]) + jnp.concatenate([zero_n, n_mat], axis=1))
    zs = [z_ref[h] for h in range(nheads)]
    for s in range(nsub):
        yh = []
        for h in range(nheads):
            q = (s, h)
            pm = jnp.concatenate([comp[q][c][:, 0:n] for c in range(nchunk)], axis=0)
            pn = jnp.concatenate([comp[q][c][:, n:] for c in range(nchunk)], axis=0)
            after = _bdot(pm, zs[h]) + pn
            z_in = [zs[h]] + [after[c * c_len:(c + 1) * c_len] for c in range(nchunk - 1)]
            yh.append(jnp.concatenate(
                [_bdot(rq[q][c * c_len:(c + 1) * c_len], z_in[c]) + y0[q][c * c_len:(c + 1) * c_len]
                 for c in range(nchunk)], axis=0))
            zs[h] = after[(nchunk - 1) * c_len:]
        y_ref[pl.ds(s * gl, gl), :] = jnp.concatenate(yh, axis=1)
    for h in range(nheads):
        z_ref[h] = zs[h]


def _rwkv_chunk(r, k, v, lw, kkn, b, bsz, seq):
    t = r.shape[0]
    lt = _tile(seq, 1024)
    nt = seq // lt
    spec = pl.BlockSpec((lt, LANES), lambda bi, hp, i: (bi * nt + i, hp))
    return pl.pallas_call(
        functools.partial(_rwkv_chunk_kernel, lt=lt),
        grid=(bsz, RWKV_DIM // LANES, nt),
        in_specs=[spec] * 6,
        out_specs=spec,
        out_shape=jax.ShapeDtypeStruct((t, RWKV_DIM), F32),
        scratch_shapes=[pltpu.VMEM((2, HEAD_DIM, HEAD_DIM), F32)],
        compiler_params=_params("arbitrary", "arbitrary", "arbitrary"),
        name="rwkv_chunk",
    )(r, k, v, lw, kkn, b)


def _ssd_kernel(pb_ref, cw_ref, cb_ref, dtb_ref, alog_ref, dskip_ref, nrm_ref, ys_ref, ext_ref, st_ref, *, nck):
    q = SSD_CHUNK

    @pl.when(pl.program_id(1) == 0)
    def _():
        ext_ref[0:8, :] = jnp.zeros((8, SSD_XBC), F32)
        st_ref[...] = jnp.zeros_like(st_ref)

    ext_ref[8:8 + nck * q, :] = pb_ref[:, SSD_DIM:SSD_DIM + SSD_XBC]
    for c in range(nck):
        _ssd_chunk(pb_ref, cw_ref, cb_ref, dtb_ref, alog_ref, dskip_ref, nrm_ref, ys_ref, ext_ref, st_ref, c * q)
    ext_ref[0:8, :] = ext_ref[nck * q:nck * q + 8, :]


def _ssd_chunk(pb_ref, cw_ref, cb_ref, dtb_ref, alog_ref, dskip_ref, nrm_ref, ys_ref, ext_ref, st_ref, r0):
    q = SSD_CHUNK
    p = HEAD_DIM
    hpg = SSD_HEADS // SSD_GROUPS
    z = pb_ref[r0:r0 + q, 0:SSD_DIM]
    u = ext_ref[8 + r0:8 + r0 + q, :]
    dt_raw = pb_ref[r0:r0 + q, SSD_DIM + SSD_XBC:]

    conv = cb_ref[...] + cw_ref[SSD_CONV - 1:SSD_CONV, :] * u
    for j in range(SSD_CONV - 1):
        off = r0 + 8 - (SSD_CONV - 1) + j
        conv = conv + cw_ref[j:j + 1, :] * ext_ref[off:off + q, :]
    xbc = _silu(conv)
    xs = xbc[:, 0:SSD_DIM]
    bm = xbc[:, SSD_DIM:SSD_DIM + SSD_GROUPS * SSD_STATE]
    cm = xbc[:, SSD_DIM + SSD_GROUPS * SSD_STATE:]

    dt = _softplus(dt_raw + dtb_ref[...])
    a = -jnp.exp(alog_ref[...])
    ri = lax.broadcasted_iota(jnp.int32, (q, q), 0)
    ci = lax.broadcasted_iota(jnp.int32, (q, q), 1)
    causal = ri >= ci
    cum = _dot_exact_lhs(causal.astype(BF16), dt * a)
    cum_t = cum.T
    dt_t = dt.T
    cum_end = cum[q - 1:q, :]
    to_end = jnp.exp(cum_end - cum) * dt
    ecum = jnp.exp(cum)
    edec = jnp.exp(cum_end)

    ys = []
    for g in range(SSD_GROUPS):
        bm_g = bm[:, g * SSD_STATE:(g + 1) * SSD_STATE]
        cm_g = cm[:, g * SSD_STATE:(g + 1) * SSD_STATE]
        cb = _bdot_nt(cm_g, bm_g)
        bm_t = bm_g.T
        for hh in range(hpg):
            h = g * hpg + hh
            x_h = xs[:, h * p:(h + 1) * p]
            seg = cum[:, h:h + 1] - cum_t[h:h + 1, :]
            ldec = jnp.exp(jnp.where(causal, seg, -jnp.inf))
            wts = cb * ldec * dt_t[h:h + 1, :]
            y = _bdot(wts, x_h)
            h_prev = st_ref[h]
            y = y + _bdot(cm_g, h_prev) * ecum[:, h:h + 1]
            st_ref[h] = h_prev * edec[:, h:h + 1] + _bdot(bm_t, x_h * to_end[:, h:h + 1])
            ys.append(y + dskip_ref[:, h:h + 1] * x_h)
    yall = jnp.concatenate(ys, axis=1) * _silu(z)
    gw = SSD_DIM // SSD_GROUPS
    outs = []
    for g in range(SSD_GROUPS):
        yg = yall[:, g * gw:(g + 1) * gw]
        ms = jnp.mean(yg * yg, axis=-1, keepdims=True)
        outs.append(yg * lax.rsqrt(ms + NORM_EPS) * nrm_ref[:, g * gw:(g + 1) * gw])
    ys_ref[r0:r0 + q, :] = jnp.concatenate(outs, axis=1)


def _ssd(pb, bsz, seq, conv_w, conv_b, dt_bias, a_log, d_skip, ssd_norm):
    t = pb.shape[0]
    nck = 2 if seq % (2 * SSD_CHUNK) == 0 else 1
    rows = nck * SSD_CHUNK
    nc = seq // rows
    row = lambda b, i: (b * nc + i, 0)
    return pl.pallas_call(
        functools.partial(_ssd_kernel, nck=nck),
        grid=(bsz, nc),
        in_specs=[pl.BlockSpec((rows, SSD_COLS_PAD), row), _full(conv_w.shape), _full(conv_b.shape),
                  _full(dt_bias.shape), _full(a_log.shape), _full(d_skip.shape), _full(ssd_norm.shape)],
        out_specs=pl.BlockSpec((rows, SSD_DIM), row),
        out_shape=jax.ShapeDtypeStruct((t, SSD_DIM), F32),
        scratch_shapes=[pltpu.VMEM((8 + rows, SSD_XBC), F32),
                        pltpu.VMEM((SSD_HEADS, SSD_STATE, HEAD_DIM), F32)],
        compiler_params=_params("arbitrary", "arbitrary"),
        name="ssd",
    )(pb, conv_w, conv_b, dt_bias, a_log, d_skip, ssd_norm)


def _mix_ffn_kernel(h_ref, y_ref, bonus_ref, gate_ref, ys_ref, gnw_ref, gnb_ref, seg_ref, wa_ref, wb_ref,
                    g_ref, wg_ref, wu_ref, wd_ref, o_ref):
    y = y_ref[...]
    seg = seg_ref[...]
    mu = _head_sums(y, seg) * (1.0 / HEAD_DIM)
    yc = y - mu
    var = _head_sums(yc * yc, seg) * (1.0 / HEAD_DIM)
    yn = yc * lax.rsqrt(var + RWKV_GN_EPS) * gnw_ref[...] + gnb_ref[...]
    ya = (yn + bonus_ref[...]) * gate_ref[...]
    h = h_ref[...] + _bdot(ya, wa_ref[...]) + _bdot(ys_ref[...], wb_ref[...])

    xn = _rms(h, g_ref[...]).astype(BF16)
    act = _silu(jnp.dot(xn, wg_ref[...], preferred_element_type=F32)) * jnp.dot(
        xn, wu_ref[...], preferred_element_type=F32)
    o_ref[...] = h + jnp.dot(act.astype(BF16), wd_ref[...], preferred_element_type=F32)


def _mix_ffn(h2, y, bonus, gate, ys, gn_w, gn_b, seg, wa, wb, g, wg, wu, wd):
    t = h2.shape[0]
    tm = _tile(t, 512)
    row = lambda i: (i, 0)
    half = pl.BlockSpec((tm, RWKV_DIM), row)
    return pl.pallas_call(
        _mix_ffn_kernel,
        grid=(t // tm,),
        in_specs=[pl.BlockSpec((tm, D_MODEL), row), half, half, half, half, _full(gn_w.shape),
                  _full(gn_b.shape), _resident(seg.shape), _resident(wa.shape), _resident(wb.shape),
                  _full((1, D_MODEL)), _resident(wg.shape), _resident(wu.shape), _resident(wd.shape)],
        out_specs=pl.BlockSpec((tm, D_MODEL), row),
        out_shape=jax.ShapeDtypeStruct((t, D_MODEL), F32),
        compiler_params=_params("parallel"),
        name="mix_ffn",
    )(h2, y, bonus, gate, ys, gn_w, gn_b, seg, wa, wb, g, wg, wu, wd)


def _rope_table_kernel(pos_ref, freq_ref, cos_ref, sin_ref):
    ang = pos_ref[...] * freq_ref[...]
    cos_ref[...] = jnp.cos(ang)
    sin_ref[...] = jnp.sin(ang)


def _rope_tables(positions):
    t = positions.size
    half = ROPE_DIM // 2
    rows = t * half // LANES
    inv_freq = ROPE_THETA ** (-jnp.arange(0, ROPE_DIM, 2, dtype=F32) / ROPE_DIM)
    pos_rep = jnp.repeat(positions.reshape(-1).astype(F32), half).reshape(rows, LANES)
    freq = jnp.tile(inv_freq, LANES // half).reshape(1, LANES)
    tr = _tile(rows, 256)
    cos, sin = pl.pallas_call(
        _rope_table_kernel,
        grid=(rows // tr,),
        in_specs=[pl.BlockSpec((tr, LANES), lambda i: (i, 0)), _full((1, LANES))],
        out_specs=[pl.BlockSpec((tr, LANES), lambda i: (i, 0))] * 2,
        out_shape=[jax.ShapeDtypeStruct((rows, LANES), F32)] * 2,
        compiler_params=_params("parallel"),
        name="rope_tables",
    )(pos_rep, freq)
    return jnp.concatenate([cos.reshape(t, half) - 1.0, sin.reshape(t, half)], axis=1)


def _rope_expansion(width):
    half = ROPE_DIM // 2
    d = jnp.arange(width) % HEAD_DIM
    j = jnp.arange(half)[:, None]
    zero = jnp.zeros((half, width), F32)
    cos_rows = jnp.where((d[None, :] < ROPE_DIM) & (d[None, :] % half == j), 1.0, 0.0)
    sin_a = jnp.where((d[None, :] < half) & (d[None, :] == j), -1.0, 0.0)
    sin_b = jnp.where((d[None, :] >= half) & (d[None, :] < ROPE_DIM) & (d[None, :] - half == j), 1.0, 0.0)
    one = jnp.concatenate([jnp.concatenate([cos_rows, zero, zero], axis=1),
                           jnp.concatenate([zero, sin_a, sin_b], axis=1)], axis=0)
    return jnp.concatenate([one, one], axis=0).astype(BF16)


def _qkv_kernel(h_ref, g_ref, w_ref, b_ref, qn_ref, kn_ref, cs_ref, exp_ref, seg_ref, q_ref, k_ref, v_ref):
    xn = _rms(h_ref[...], g_ref[...]).astype(BF16)
    qkv = jnp.dot(xn, w_ref[...], preferred_element_type=F32) + b_ref[...]
    seg = seg_ref[...]
    wide = seg.shape[0]
    hi, lo = _split(cs_ref[...])
    tab = jnp.dot(jnp.concatenate([hi, lo], axis=1), exp_ref[...], preferred_element_type=F32)
    cos = 1.0 + tab[:, 0:wide]
    sa = tab[:, wide:2 * wide]
    sb = tab[:, 2 * wide:]

    def norm_rope(x, gain):
        ms = _head_sums(x * x, seg) * (1.0 / HEAD_DIM)
        xn_ = x * lax.rsqrt(ms + NORM_EPS) * gain
        return (xn_ * cos + pltpu.roll(xn_, wide - ROPE_DIM // 2, axis=1) * sa
                + pltpu.roll(xn_, ROPE_DIM // 2, axis=1) * sb)

    for c in range(Q_DIM // wide):
        x = qkv[:, c * wide:(c + 1) * wide]
        q_ref[:, c * wide:(c + 1) * wide] = (norm_rope(x, qn_ref[...]) * (HEAD_DIM ** -0.5)).astype(BF16)
    for c in range(KV_DIM // wide):
        x = qkv[:, Q_DIM + c * wide:Q_DIM + (c + 1) * wide]
        k_ref[:, c * wide:(c + 1) * wide] = norm_rope(x, kn_ref[...]).astype(BF16)
    v_ref[...] = qkv[:, Q_DIM + KV_DIM:].astype(BF16)


def _qkv(h2, g, w, b, qn, kn, cs, expand, seg):
    t = h2.shape[0]
    tm = _tile(t, 512)
    row = lambda i: (i, 0)
    return pl.pallas_call(
        _qkv_kernel,
        grid=(t // tm,),
        in_specs=[pl.BlockSpec((tm, D_MODEL), row), _full((1, D_MODEL)), _resident(w.shape), _full(b.shape),
                  _full(qn.shape), _full(kn.shape), pl.BlockSpec((tm, cs.shape[1]), row), _full(expand.shape),
                  _full(seg.shape)],
        out_specs=[pl.BlockSpec((tm, Q_DIM), row), pl.BlockSpec((tm, KV_DIM), row),
                   pl.BlockSpec((tm, KV_DIM), row)],
        out_shape=[jax.ShapeDtypeStruct((t, Q_DIM), BF16), jax.ShapeDtypeStruct((t, KV_DIM), BF16),
                   jax.ShapeDtypeStruct((t, KV_DIM), BF16)],
        compiler_params=_params("parallel"),
        name="qkv",
    )(h2, g, w, b, qn, kn, cs, expand, seg)


def _attn_kernel(h_ref, q_ref, kc_ref, kp_ref, vc_ref, vp_ref, sink_ref, wo_ref, bo_ref, o_ref, *, nqb):
    w = WINDOW
    n = HEAD_DIM
    rows = GQA_GROUP * w
    qi = lax.broadcasted_iota(jnp.int32, (rows, 2 * w), 0) % w
    kj = lax.broadcasted_iota(jnp.int32, (rows, 2 * w), 1)
    diff = w + qi - kj
    band = (diff >= 0) & (diff < w)
    band0 = band & (kj >= jnp.where(pl.program_id(1) == 0, w, 0))
    kall = jnp.concatenate([kp_ref[...], kc_ref[...]], axis=0)
    vall = jnp.concatenate([vp_ref[...], vc_ref[...]], axis=0)
    units = [(qb, g) for qb in range(nqb) for g in range(KV_HEADS)]
    s, e, den = {}, {}, {}
    for u in units:
        qb, g = u
        qs = jnp.concatenate(
            [q_ref[qb * w:(qb + 1) * w, (g * GQA_GROUP + i) * n:(g * GQA_GROUP + i + 1) * n]
             for i in range(GQA_GROUP)], axis=0)
        kk = kall[qb * w:(qb + 2) * w, g * n:(g + 1) * n]
        sc = lax.dot_general(qs, kk, (((1,), (1,)), ((), ())), preferred_element_type=F32)
        s[u] = jnp.where(band0 if qb == 0 else band, sc, -jnp.inf)
    sinks = [jnp.concatenate(
        [jnp.broadcast_to(sink_ref[:, g * GQA_GROUP + i:g * GQA_GROUP + i + 1], (w, LANES))
         for i in range(GQA_GROUP)], axis=0) for g in range(KV_HEADS)]
    ones = jnp.ones((2 * w, LANES - n), BF16)
    for u in units:
        sink = sinks[u[1]]
        m = jnp.maximum(jnp.broadcast_to(jnp.max(s[u], axis=-1, keepdims=True), (rows, LANES)), sink)
        e[u] = jnp.exp(s[u] - jnp.concatenate([m, m], axis=1)).astype(BF16)
        den[u] = jnp.exp(sink - m)
    outs = [[None] * Q_HEADS for _ in range(nqb)]
    for u in units:
        qb, g = u
        vv = jnp.concatenate([vall[qb * w:(qb + 2) * w, g * n:(g + 1) * n], ones], axis=1)
        both = jnp.dot(e[u], vv, preferred_element_type=F32)
        total = pltpu.roll(both, LANES - n, axis=1) + den[u]
        o = both[:, 0:n] / total[:, 0:n]
        for i in range(GQA_GROUP):
            outs[qb][g * GQA_GROUP + i] = o[i * w:(i + 1) * w]
    att = jnp.concatenate([jnp.concatenate(outs[qb], axis=1) for qb in range(nqb)], axis=0)
    o_ref[...] = h_ref[...] + _bdot(att, wo_ref[...]) + bo_ref[...]


def _attn(h2, q, k, v, sinks, wo, bo, bsz, seq):
    t = h2.shape[0]
    nqb = 4 if seq % (4 * WINDOW) == 0 else 1
    tq = nqb * WINDOW
    nb = seq // tq
    cur = lambda b, i: (b * nb + i, 0)
    prv = lambda b, i: (b * nb * nqb + jnp.maximum(i * nqb - 1, 0), 0)
    return pl.pallas_call(
        functools.partial(_attn_kernel, nqb=nqb),
        grid=(bsz, nb),
        in_specs=[pl.BlockSpec((tq, D_MODEL), cur), pl.BlockSpec((tq, Q_DIM), cur),
                  pl.BlockSpec((tq, KV_DIM), cur), pl.BlockSpec((WINDOW, KV_DIM), prv),
                  pl.BlockSpec((tq, KV_DIM), cur), pl.BlockSpec((WINDOW, KV_DIM), prv),
                  _full(sinks.shape), _full(wo.shape), _full(bo.shape)],
        out_specs=pl.BlockSpec((tq, D_MODEL), cur),
        out_shape=jax.ShapeDtypeStruct((t, D_MODEL), F32),
        compiler_params=_params("parallel", "parallel"),
        name="attn",
    )(h2, q, k, k, v, v, sinks, wo, bo)


def _router_kernel(h_ref, g_ref, wr_ref, xn_ref, meta_ref, cnt_ref, base_ref):
    @pl.when(pl.program_id(0) == 0)
    def _():
        base_ref[...] = jnp.zeros_like(base_ref)

    xn = _rms(h_ref[...], g_ref[...])
    xn_ref[...] = xn
    hi, lo = _split(xn)
    whi = wr_ref[0]
    wlo = wr_ref[1]
    logits = (jnp.dot(hi, whi, preferred_element_type=F32) + jnp.dot(hi, wlo, preferred_element_type=F32)
              + jnp.dot(lo, whi, preferred_element_type=F32))
    tm = logits.shape[0]
    lane = lax.broadcasted_iota(jnp.int32, logits.shape, 1)
    logits = jnp.where(lane < N_EXPERTS, logits, -jnp.inf)
    m1 = jnp.max(logits, axis=-1, keepdims=True)
    i1 = jnp.min(jnp.where(logits == m1, lane, LANES), axis=-1, keepdims=True)
    rest = jnp.where(lane == i1, -jnp.inf, logits)
    m2 = jnp.max(rest, axis=-1, keepdims=True)
    i2 = jnp.min(jnp.where(rest == m2, lane, LANES), axis=-1, keepdims=True)
    e2 = jnp.exp(m2 - m1)
    w1 = 1.0 / (1.0 + e2)
    w2 = e2 / (1.0 + e2)

    sel1 = lane == i1
    sel2 = lane == i2
    onehot = jnp.where(sel1, 1.0, 0.0) + jnp.where(sel2, 1.0, 0.0)
    ri = lax.broadcasted_iota(jnp.int32, (tm, tm), 0)
    ci = lax.broadcasted_iota(jnp.int32, (tm, tm), 1)
    before = jnp.dot(jnp.where(ci < ri, 1.0, 0.0).astype(BF16), onehot.astype(BF16),
                     preferred_element_type=F32)
    rank = base_ref[...] + before
    r1 = jnp.sum(jnp.where(sel1, rank, 0.0), axis=-1, keepdims=True)
    r2 = jnp.sum(jnp.where(sel2, rank, 0.0), axis=-1, keepdims=True)
    total = base_ref[...] + jnp.sum(onehot, axis=0, keepdims=True)
    base_ref[...] = total
    cnt_ref[...] = total
    cols = (i1.astype(F32), i2.astype(F32), w1, w2, r1, r2)
    meta = jnp.zeros(logits.shape, F32)
    for c, val in enumerate(cols):
        meta = jnp.where(lane == c, val, meta)
    meta_ref[...] = meta


def _router(h2, g, wr):
    t = h2.shape[0]
    tm = _tile(t, 512)
    row = lambda i: (i, 0)
    return pl.pallas_call(
        _router_kernel,
        grid=(t // tm,),
        in_specs=[pl.BlockSpec((tm, D_MODEL), row), _full((1, D_MODEL)), _full(wr.shape)],
        out_specs=[pl.BlockSpec((tm, D_MODEL), row), pl.BlockSpec((tm, LANES), row), _full((1, LANES))],
        out_shape=[jax.ShapeDtypeStruct((t, D_MODEL), F32), jax.ShapeDtypeStruct((t, LANES), F32),
                   jax.ShapeDtypeStruct((1, LANES), F32)],
        scratch_shapes=[pltpu.VMEM((1, LANES), F32)],
        compiler_params=_params("arbitrary"),
        name="router",
    )(h2, g, wr)


def _moe_index_kernel(p1_ref, p2_ref, pad_ref, src_ref, dst_ref, *, ntok):
    tm = MOE_TM
    nrow = src_ref.shape[0]

    def virtual(j, c):
        dst_ref[j] = nrow + j
        return c

    lax.fori_loop(0, tm, virtual, 0, unroll=8)

    for e in range(N_EXPERTS):
        lo = pad_ref[2 * e]
        base = 2 * ntok + e * tm - lo

        def fill(p, c, base=base):
            src_ref[p] = 0
            dst_ref[tm + p] = base + p
            return c

        lax.fori_loop(lo, pad_ref[2 * e + 1], fill, 0)

    def unused(p, c):
        src_ref[p] = 0
        dst_ref[tm + p] = 2 * ntok + (p & (tm - 1))
        return c

    lax.fori_loop(pad_ref[2 * N_EXPERTS], pad_ref[2 * N_EXPERTS + 1], unused, 0)

    def body(t, c):
        p1 = p1_ref[t]
        p2 = p2_ref[t]
        src_ref[p1] = t
        dst_ref[tm + p1] = t
        src_ref[p2] = t
        dst_ref[tm + p2] = ntok + t
        return c

    lax.fori_loop(0, ntok, body, 0, unroll=4)


def _moe_index(pos1, pos2, pad, rows):
    smem = pl.BlockSpec(memory_space=pltpu.SMEM)
    return pl.pallas_call(
        functools.partial(_moe_index_kernel, ntok=pos1.shape[0]),
        grid_spec=pltpu.PrefetchScalarGridSpec(
            num_scalar_prefetch=3, grid=(1,), in_specs=[], out_specs=[smem, smem]),
        out_shape=[jax.ShapeDtypeStruct((rows,), jnp.int32), jax.ShapeDtypeStruct((rows + MOE_TM,), jnp.int32)],
        compiler_params=_params("arbitrary"),
        name="moe_index",
    )(pos1, pos2, pad)


def _row_copy(src_ref, src_row, dst_ref, dst_row, sem):
    return pltpu.make_async_copy(src_ref.at[pl.ds(src_row, 1)], dst_ref.at[pl.ds(dst_row, 1)], sem)


def _experts_kernel(te_ref, nused_ref, src_ref, dst_ref, x_ref, wg_ref, wu_ref, wd_ref, y_ref,
                    xbuf, ybuf, wgb, wub, wdb, gsem, ssem):
    i = pl.program_id(0)
    nused = nused_ref[0]
    tm = MOE_TM

    sub = MOE_SUB
    ngroup = tm // sub

    def gather_group(tile, slot, k):
        for u in range(sub):
            _row_copy(x_ref, src_ref[tile * tm + k * sub + u], xbuf.at[slot, k], u, gsem.at[slot]).start()

    def scatter_group(tile, slot, k):
        for u in range(sub):
            _row_copy(ybuf.at[slot, k], u, y_ref, dst_ref[(tile + 1) * tm + k * sub + u], ssem.at[slot]).start()

    def wait_gather(slot):
        def body(k, c):
            pltpu.make_async_copy(x_ref.at[pl.ds(0, sub)], xbuf.at[slot, k], gsem.at[slot]).wait()
            return c
        lax.fori_loop(0, ngroup, body, 0)

    def wait_scatter(slot):
        def body(k, c):
            pltpu.make_async_copy(ybuf.at[slot, k], y_ref.at[pl.ds(0, sub)], ssem.at[slot]).wait()
            return c
        lax.fori_loop(0, ngroup, body, 0)

    @pl.when(i == 0)
    def _():
        ybuf[1] = jnp.zeros(ybuf.shape[1:], F32)

        ndump = (N_EXPERTS + 1) * tm
        dump0 = y_ref.shape[0] - ndump

        def zero(g, c):
            pltpu.make_async_copy(ybuf.at[1, 0], y_ref.at[pl.ds(dump0 + g * sub, sub)], ssem.at[0]).start()
            return c
        lax.fori_loop(0, ndump // sub, zero, 0)

        def zero_wait(g, c):
            pltpu.make_async_copy(ybuf.at[1, 0], y_ref.at[pl.ds(0, sub)], ssem.at[0]).wait()
            return c
        lax.fori_loop(0, ndump // sub, zero_wait, 0)

        def body(k, c):
            gather_group(0, 0, k)
            return c
        lax.fori_loop(0, ngroup, body, 0)

    @pl.when(i < nused)
    def _():
        slot = i % 2
        other = 1 - slot
        nxt = jnp.minimum(i + 1, nused - 1)
        wait_gather(slot)

        @pl.when((i == 0) | (te_ref[i] != te_ref[jnp.maximum(i - 1, 0)]))
        def _():
            wgb[...] = wg_ref[0].astype(BF16)
            wub[...] = wu_ref[0].astype(BF16)
            wdb[...] = wd_ref[0].astype(BF16)

        xb = xbuf[slot].reshape(tm, D_MODEL).astype(BF16)
        bounds = list(range(0, FFN_EXPERT, MOE_FCHUNK)) + [FFN_EXPERT]
        nchunk = len(bounds) - 1
        acts = []
        for c in range(nchunk):
            c0, c1 = bounds[c], bounds[c + 1]
            gate = jnp.dot(xb, wgb[:, c0:c1], preferred_element_type=F32)
            up = jnp.dot(xb, wub[:, c0:c1], preferred_element_type=F32)
            acts.append((_silu(gate) * up).astype(BF16))
            half = nchunk // 2
            if c < half:
                for k in range(c * ngroup // half, (c + 1) * ngroup // half):
                    gather_group(nxt, other, k)
            else:
                for k in range((c - half) * ngroup // (nchunk - half), (c - half + 1) * ngroup // (nchunk - half)):
                    scatter_group(i - 1, other, k)
        y = jnp.dot(jnp.concatenate(acts, axis=1), wdb[...], preferred_element_type=F32)

        @pl.when(i >= 1)
        def _():
            wait_scatter(slot)

        ybuf[slot] = y.reshape(ngroup, sub, D_MODEL)

        @pl.when(i == nused - 1)
        def _():
            def body(k, c):
                scatter_group(i, slot, k)
                return c
            lax.fori_loop(0, ngroup, body, 0)
            wait_scatter(other)
            wait_scatter(slot)
            wait_gather(other)


def _experts(tile_expert, nused, src, dst, xn, wg, wu, wd, out_rows):
    ntile = tile_expert.shape[0]
    tm = MOE_TM
    wsel = lambda i, te, nu, s, d: (te[i], 0, 0)
    any_spec = pl.BlockSpec(memory_space=pl.ANY)
    return pl.pallas_call(
        _experts_kernel,
        grid_spec=pltpu.PrefetchScalarGridSpec(
            num_scalar_prefetch=4, grid=(ntile,),
            in_specs=[any_spec,
                      pl.BlockSpec((1, D_MODEL, FFN_EXPERT), wsel),
                      pl.BlockSpec((1, D_MODEL, FFN_EXPERT), wsel),
                      pl.BlockSpec((1, FFN_EXPERT, D_MODEL), wsel)],
            out_specs=any_spec,
            scratch_shapes=[pltpu.VMEM((2, tm // MOE_SUB, MOE_SUB, D_MODEL), F32),
                            pltpu.VMEM((2, tm // MOE_SUB, MOE_SUB, D_MODEL), F32),
                            pltpu.VMEM((D_MODEL, FFN_EXPERT), BF16), pltpu.VMEM((D_MODEL, FFN_EXPERT), BF16),
                            pltpu.VMEM((FFN_EXPERT, D_MODEL), BF16),
                            pltpu.SemaphoreType.DMA((2,)), pltpu.SemaphoreType.DMA((2,))]),
        out_shape=jax.ShapeDtypeStruct((out_rows, D_MODEL), F32),
        compiler_params=_params("arbitrary"),
        name="moe_experts",
    )(tile_expert, nused, src, dst, xn, wg, wu, wd)


def _combine_kernel(h_ref, meta_ref, y1_ref, y2_ref, o_ref):
    meta = meta_ref[...]
    o_ref[...] = h_ref[...] + meta[:, 2:3] * y1_ref[...] + meta[:, 3:4] * y2_ref[...]


def _combine(h2, meta, ys):
    t = h2.shape[0]
    tm = _tile(t, 512)
    nt = t // tm
    row = lambda i: (i, 0)
    return pl.pallas_call(
        _combine_kernel,
        grid=(nt,),
        in_specs=[pl.BlockSpec((tm, D_MODEL), row), pl.BlockSpec((tm, LANES), row),
                  pl.BlockSpec((tm, D_MODEL), row), pl.BlockSpec((tm, D_MODEL), lambda i: (nt + i, 0))],
        out_specs=pl.BlockSpec((tm, D_MODEL), row),
        out_shape=jax.ShapeDtypeStruct((t, D_MODEL), F32),
        compiler_params=_params("parallel"),
        name="moe_combine",
    )(h2, meta, ys, ys)


def _moe(h2, g, router, wg, wu, wd):
    t = h2.shape[0]
    tm = MOE_TM
    wr = jnp.pad(router, ((0, 0), (0, LANES - N_EXPERTS)))
    wr_hi = wr.astype(BF16)
    wr_lo = (wr - wr_hi.astype(F32)).astype(BF16)
    xn, meta, cnt = _router(h2, g, jnp.stack([wr_hi, wr_lo]))

    cols = meta[:, :8].T.astype(jnp.int32)
    e1, e2, r1, r2 = cols[0], cols[1], cols[4], cols[5]
    count = cnt[0, :N_EXPERTS].astype(jnp.int32)
    ntile = (count + tm - 1) // tm
    tile_end = jnp.cumsum(ntile)
    off = (tile_end - ntile) * tm
    rows = 2 * t + N_EXPERTS * tm
    tile_id = jnp.arange(rows // tm)
    tile_expert = jnp.minimum(
        jnp.sum(tile_id[:, None] >= tile_end[None, :], axis=1), N_EXPERTS - 1).astype(jnp.int32)
    nused = tile_end[-1:].astype(jnp.int32)
    expert = jnp.arange(N_EXPERTS, dtype=jnp.int32)[:, None]
    pos1 = r1 + jnp.sum(jnp.where(e1[None, :] == expert, off[:, None], 0), axis=0)
    pos2 = r2 + jnp.sum(jnp.where(e2[None, :] == expert, off[:, None], 0), axis=0)
    pad = jnp.stack([jnp.append(off + count, tile_end[-1] * tm),
                     jnp.append(off + ntile * tm, rows)], axis=1).reshape(-1).astype(jnp.int32)
    src, dst = _moe_index(pos1, pos2, pad, rows)

    ys = _experts(tile_expert, nused, src, dst, xn, wg, wu, wd, rows + tm)
    return _combine(h2, meta, ys)


def _segment_matrix(width):
    i = jnp.arange(width) // HEAD_DIM
    return (i[:, None] == i[None, :]).astype(BF16)


def _layer0(h2, bsz, seq, norm_mix, w_in, mu_shift, w0, w_decay_up, a0, w_iclr_up, w_gate_up, k_k, k_a, r_k,
            gn_w, gn_b, conv_w, conv_b, dt_bias, a_log, d_skip, ssd_norm, w_out, norm_ffn, ffn_gate, ffn_up,
            ffn_down):
    row = lambda a: a.reshape(1, -1)
    seg = _segment_matrix(SEG_WIDTH)
    wa = w_in[:, :RWKV_COLS].astype(BF16)
    wb = jnp.pad(w_in[:, RWKV_COLS:], ((0, 0), (0, SSD_COLS_PAD - SSD_COLS))).astype(BF16)
    lora = w_decay_up.shape[0]
    wd = jnp.concatenate([w_decay_up, jnp.zeros((LANES - lora, RWKV_DIM), F32)], axis=0).astype(BF16)
    wi = jnp.concatenate([jnp.zeros((LANES - lora, RWKV_DIM), F32), w_iclr_up], axis=0).astype(BF16)
    pb, r, k, v, lw, kkn, b, g, bonus = _in_proj(
        h2, bsz, seq, row(norm_mix), wa, wb, row(mu_shift), row(w0), wd, row(a0), wi, w_gate_up.astype(BF16),
        row(k_k), row(k_a), row(r_k), seg)
    y = _rwkv_chunk(r, k, v, lw, kkn, b, bsz, seq)

    lane_pad = lambda a: jnp.pad(row(a), ((0, 0), (0, LANES - a.size)))
    ys = _ssd(pb, bsz, seq, conv_w, row(conv_b), lane_pad(dt_bias), lane_pad(a_log), lane_pad(d_skip),
              row(ssd_norm))

    w_out = w_out.astype(BF16)
    return _mix_ffn(h2, y, bonus, g, ys, row(gn_w), row(gn_b), seg, w_out[:RWKV_DIM], w_out[RWKV_DIM:],
                    row(norm_ffn), ffn_gate.astype(BF16), ffn_up.astype(BF16), ffn_down.astype(BF16))


def _layer1(h2, bsz, seq, tables, norm_mix, w_qkv, b_qkv, q_norm, k_norm, sinks, w_o, b_o, norm_ffn, router,
            exp_gate, exp_up, exp_down):
    row = lambda a: a.reshape(1, -1)
    wide = lambda a: jnp.tile(a, SEG_WIDTH // HEAD_DIM).reshape(1, SEG_WIDTH)
    q, k, v = _qkv(h2, row(norm_mix), w_qkv.astype(BF16), row(b_qkv), wide(q_norm), wide(k_norm), tables,
                   _rope_expansion(SEG_WIDTH), _segment_matrix(SEG_WIDTH))
    h2 = _attn(h2, q, k, v, row(sinks), w_o.astype(BF16), row(b_o), bsz, seq)

    return _moe(h2, row(norm_ffn), router, exp_gate, exp_up, exp_down)


def kernel(x, positions, ev_norm_mix, ev_w_in, ev_mu_shift, ev_w0, ev_w_decay_up, ev_a0, ev_w_iclr_up, ev_w_gate_up, ev_k_k, ev_k_a, ev_r_k, ev_gn_w, ev_gn_b, ev_conv_w, ev_conv_b, ev_dt_bias, ev_a_log, ev_d_skip, ev_ssd_norm, ev_w_out, ev_norm_ffn, ev_ffn_gate, ev_ffn_up, ev_ffn_down, od_norm_mix, od_w_qkv, od_b_qkv, od_q_norm, od_k_norm, od_sinks, od_w_o, od_b_o, od_norm_ffn, od_router, od_exp_gate, od_exp_up, od_exp_down):
    bsz, seq, d = x.shape
    depth = ev_norm_mix.shape[0] + od_norm_mix.shape[0]
    tables = _rope_tables(positions)
    h2 = x.reshape(bsz * seq, d)
    for layer in range(depth):
        i = layer // 2
        if layer % 2 == 0:
            h2 = _layer0(h2, bsz, seq, ev_norm_mix[i], ev_w_in[i], ev_mu_shift[i], ev_w0[i], ev_w_decay_up[i],
                         ev_a0[i], ev_w_iclr_up[i], ev_w_gate_up[i], ev_k_k[i], ev_k_a[i], ev_r_k[i],
                         ev_gn_w[i], ev_gn_b[i], ev_conv_w[i], ev_conv_b[i], ev_dt_bias[i], ev_a_log[i],
                         ev_d_skip[i], ev_ssd_norm[i], ev_w_out[i], ev_norm_ffn[i], ev_ffn_gate[i],
                         ev_ffn_up[i], ev_ffn_down[i])
        else:
            h2 = _layer1(h2, bsz, seq, tables, od_norm_mix[i], od_w_qkv[i], od_b_qkv[i], od_q_norm[i],
                         od_k_norm[i], od_sinks[i], od_w_o[i], od_b_o[i], od_norm_ffn[i], od_router[i],
                         od_exp_gate[i], od_exp_up[i], od_exp_down[i])
    return h2.reshape(bsz, seq, d)
```
